```python
import math
import jax, jax.numpy as jnp
from jax import lax
import numpy as np

D_MODEL = 1024
BATCH = 16
SEQ = 4096
DEPTH = 2

PLE_DIM = 256
EPS = 1e-6
NEG_INF = -1e30
MAX_POS_OFFSET = 1024

DIL_GROUPS = ((128, 1), (512, 4), (2048, 16))
N_GROUPS = len(DIL_GROUPS)
A_HEADS = 4
A_HEAD_DIM = 128
A_WIDTH = A_HEADS * A_HEAD_DIM
BLK = 64
ROT_DIM = A_HEAD_DIM // 4
ROPE_THETA = 500000.0

B_HEADS = 4
B_KEY = 128
B_VAL = 128
B_WIDTH = B_HEADS * B_KEY
CHUNK = 64

C_WIDTH = 512
CONV_W = 3

N_BRANCH = 3
BRANCH_WIDTH = 512

D_FF = 2816
N_EXPERTS = 8
TOP_K = 2
D_FF_EXPERT = 3584
N_DENSE = (DEPTH + 1) // 2
N_MOE = DEPTH // 2

A_COLS = N_GROUPS * 3 * A_WIDTH
B_COLS = 5 * B_WIDTH
C_COLS = 3 * C_WIDTH
G_COLS = N_BRANCH * D_MODEL
IN_COLS = A_COLS + B_COLS + C_COLS + G_COLS

kernel_name = "hybrid_dilated_hgrn2_shortconv_moe_encoder"


def rms_norm(x, g):
    xf = x.astype(jnp.float32)
    y = xf * lax.rsqrt(jnp.mean(xf * xf, axis=-1, keepdims=True) + EPS) * g.astype(jnp.float32)
    return y.astype(x.dtype)


def rope_tables(positions, dtype):
    inv_freq = ROPE_THETA ** (-jnp.arange(0, ROT_DIM, 2, dtype=jnp.float32) / ROT_DIM)
    ang = positions.astype(jnp.float32)[..., None] * inv_freq
    return jnp.cos(ang)[:, :, None, :].astype(dtype), jnp.sin(ang)[:, :, None, :].astype(dtype)


def apply_partial_rope(t, cos, sin):
    half = ROT_DIM // 2
    t1 = t[..., :half]
    t2 = t[..., half:ROT_DIM]
    return jnp.concatenate([t1 * cos - t2 * sin, t2 * cos + t1 * sin, t[..., ROT_DIM:]], axis=-1)


def dilated_window_attention(q, k, v, dil, half_span):
    b, s, h, dh = q.shape
    L = s // dil
    nb = -(-L // BLK)
    Lp = nb * BLK

    def to_sub(t):
        t = t.reshape(b, L, dil, h, dh).transpose(0, 2, 3, 1, 4)
        return jnp.pad(t, ((0, 0), (0, 0), (0, 0), (0, Lp - L), (0, 0)))

    def band(t):
        tp = jnp.pad(t, ((0, 0), (0, 0), (0, 0), (BLK, BLK), (0, 0))).reshape(b, dil, h, nb + 2, BLK, dh)
        return jnp.concatenate([tp[:, :, :, :-2], tp[:, :, :, 1:-1], tp[:, :, :, 2:]], axis=4)

    qb = to_sub(q).reshape(b, dil, h, nb, BLK, dh)
    kb = band(to_sub(k))
    vb = band(to_sub(v))
    qi = jnp.arange(nb)[:, None, None] * BLK + jnp.arange(BLK)[None, :, None]
    ki = (jnp.arange(nb)[:, None, None] - 1) * BLK + jnp.arange(3 * BLK)[None, None, :]
    valid = (jnp.abs(ki - qi) <= half_span) & (ki >= 0) & (ki < L)
    scores = jnp.einsum('brhnqd,brhnkd->brhnqk', qb, kb,
                        preferred_element_type=jnp.float32) * (1.0 / math.sqrt(dh))
    scores = jnp.where(valid, scores, NEG_INF)
    lse = jax.nn.logsumexp(scores, axis=-1)
    probs = jnp.exp(scores - lse[..., None]).astype(v.dtype)
    out = jnp.einsum('brhnqk,brhnkd->brhnqd', probs, vb)
    out = out.reshape(b, dil, h, Lp, dh)[:, :, :, :L].transpose(0, 3, 1, 2, 4).reshape(b, s, h, dh)
    lse = lse.reshape(b, dil, h, Lp)[..., :L].transpose(0, 3, 1, 2).reshape(b, s, h)
    return out, lse


def mixer_a(cols, cos, sin):
    b, s, _ = cols.shape
    outs, lses = [], []
    for g, (win, dil) in enumerate(DIL_GROUPS):
        blk = cols[..., g * 3 * A_WIDTH:(g + 1) * 3 * A_WIDTH].reshape(b, s, 3, A_HEADS, A_HEAD_DIM)
        q = apply_partial_rope(blk[:, :, 0], cos, sin)
        k = apply_partial_rope(blk[:, :, 1], cos, sin)
        o, l = dilated_window_attention(q, k, blk[:, :, 2], dil, win // (2 * dil))
        outs.append(o)
        lses.append(l)
    wts = jax.nn.softmax(jnp.stack(lses, axis=0), axis=0).astype(cols.dtype)
    out = jnp.einsum('gbsh,gbshd->bshd', wts, jnp.stack(outs, axis=0))
    return out.reshape(b, s, A_WIDTH)


def hgrn_lower_bounds(logits):
    sm = jax.nn.softmax(logits.astype(jnp.float32), axis=1)
    return jnp.cumsum(sm, axis=1) - sm[:, :1]


def hgrn2_scan(q, z, v, lb):
    b, s, h, dk = q.shape
    dv = v.shape[-1]
    nc = s // CHUNK
    f = lb + (1.0 - lb) * jax.nn.sigmoid(z)
    logf = jnp.log(f)
    k = (1.0 - lb) * jax.nn.sigmoid(-z)

    def chunks(t):
        return t.reshape(b, nc, CHUNK, h, t.shape[-1]).transpose(1, 0, 3, 2, 4)

    q, k, v, logf = chunks(q), chunks(k), chunks(v), chunks(logf)
    cum = jnp.cumsum(logf, axis=3)
    total = cum[:, :, :, -1:, :]
    mid = cum[:, :, :, CHUNK // 2 - 1:CHUNK // 2, :]
    att = jnp.einsum('nbhtc,nbhsc->nbhts', q * jnp.exp(cum - mid), k * jnp.exp(mid - cum))
    tril = jnp.tril(jnp.ones((CHUNK, CHUNK), dtype=bool))
    att = jnp.where(tril, att, 0.0)
    o_intra = jnp.einsum('nbhts,nbhsv->nbhtv', att, v)
    q_dec = q * jnp.exp(cum)
    k_dec = k * jnp.exp(total - cum)
    chunk_decay = jnp.exp(total[:, :, :, 0, :])

    def step(state, inp):
        q_c, k_c, v_c, d_c = inp
        o_c = jnp.einsum('bhtc,bhcv->bhtv', q_c, state)
        state = d_c[..., None] * state + jnp.einsum('bhsc,bhsv->bhcv', k_c, v_c)
        return state, o_c

    s0 = jnp.zeros((b, h, dk, dv), jnp.float32)
    _, o_inter = lax.scan(step, s0, (q_dec, k_dec, v, chunk_decay))
    o = o_intra + o_inter
    return o.transpose(1, 0, 3, 2, 4).reshape(b, s, h, dv)


def mixer_b(cols, lb_fwd, lb_bwd, norm_g):
    b, s, _ = cols.shape
    parts = [cols[..., n * B_WIDTH:(n + 1) * B_WIDTH] for n in range(5)]
    q, z_f, z_b, inp = [t.astype(jnp.float32).reshape(b, s, B_HEADS, B_KEY) for t in parts[:4]]
    gate = parts[4]
    lf = lb_fwd.reshape(B_HEADS, B_KEY)
    lbk = lb_bwd.reshape(B_HEADS, B_KEY)
    o_f = hgrn2_scan(q, z_f, inp, lf)
    o_b = jnp.flip(hgrn2_scan(jnp.flip(q, 1), jnp.flip(z_b, 1), jnp.flip(inp, 1), lbk), 1)
    o = o_f + o_b
    o = o * lax.rsqrt(jnp.mean(o * o, axis=-1, keepdims=True) + EPS) * norm_g.astype(jnp.float32).reshape(B_HEADS, B_VAL)
    return o.reshape(b, s, B_WIDTH).astype(cols.dtype) * jax.nn.silu(gate)


def mixer_c(cols, conv_w):
    u = cols[..., :C_WIDTH]
    b_gate = cols[..., C_WIDTH:2 * C_WIDTH]
    c_gate = cols[..., 2 * C_WIDTH:]
    y = lax.conv_general_dilated(c_gate * u, conv_w[:, None, :].astype(cols.dtype), window_strides=(1,),
                                 padding=((CONV_W // 2, CONV_W // 2),),
                                 dimension_numbers=('NWC', 'WIO', 'NWC'), feature_group_count=C_WIDTH)
    return b_gate * y


def swiglu(h, wg, wu, wd):
    return (jax.nn.silu(h @ wg) * (h @ wu)) @ wd


def moe_swiglu(h, router, wg, wu, wd):
    logits = jnp.einsum('bsd,de->bse', h, router).astype(jnp.float32)
    top_val, top_idx = lax.top_k(logits, TOP_K)
    top_w = jax.nn.softmax(top_val, axis=-1)
    combine = jnp.einsum('bsk,bske->bse', top_w,
                         jax.nn.one_hot(top_idx, N_EXPERTS, dtype=jnp.float32)).astype(h.dtype)
    out = jnp.zeros_like(h)
    for e in range(N_EXPERTS):
        out = out + combine[..., e:e + 1] * swiglu(h, wg[e], wu[e], wd[e])
    return out


def setup_inputs(seed: int = 0) -> dict:
    key = jax.random.key(seed)
    ks = jax.random.split(key, 26)
    f32 = jnp.float32

    def nrm(k, shape, fan_in):
        return jax.random.normal(k, shape, f32) * (fan_in ** -0.5)

    def gain(k, shape):
        return 1.0 + 0.05 * jax.random.normal(k, shape, f32)

    positions = (jnp.arange(SEQ, dtype=jnp.int32)[None, :]
                 + jax.random.randint(ks[2], (BATCH, 1), 0, MAX_POS_OFFSET, dtype=jnp.int32))
    return {
        "x": jax.random.normal(ks[0], (BATCH, SEQ, D_MODEL), f32),
        "p": jax.random.normal(ks[1], (DEPTH, BATCH, SEQ, PLE_DIM), f32),
        "positions": positions,
        "w_in": nrm(ks[3], (DEPTH, D_MODEL, IN_COLS), D_MODEL),
        "conv_w": nrm(ks[4], (DEPTH, CONV_W, C_WIDTH), CONV_W),
        "hgrn_lb_logits": jax.random.normal(ks[5], (2, DEPTH, B_WIDTH), f32),
        "hgrn_norm_g": gain(ks[6], (DEPTH, B_WIDTH)),
        "w_branch": nrm(ks[7], (DEPTH, N_BRANCH, BRANCH_WIDTH, D_MODEL), BRANCH_WIDTH),
        "w_out": nrm(ks[8], (DEPTH, D_MODEL, D_MODEL), D_MODEL),
        "g_mix_pre": gain(ks[9], (DEPTH, D_MODEL)),
        "g_mix_post": gain(ks[10], (DEPTH, D_MODEL)),
        "g_ffn_pre": gain(ks[11], (DEPTH, D_MODEL)),
        "g_ffn_post": gain(ks[12], (DEPTH, D_MODEL)),
        "dense_w_gate": nrm(ks[13], (N_DENSE, D_MODEL, D_FF), D_MODEL),
        "dense_w_up": nrm(ks[14], (N_DENSE, D_MODEL, D_FF), D_MODEL),
        "dense_w_down": nrm(ks[15], (N_DENSE, D_FF, D_MODEL), D_FF),
        "moe_router": nrm(ks[16], (N_MOE, D_MODEL, N_EXPERTS), D_MODEL),
        "moe_w_gate": nrm(ks[17], (N_MOE, N_EXPERTS, D_MODEL, D_FF_EXPERT), D_MODEL),
        "moe_w_up": nrm(ks[18], (N_MOE, N_EXPERTS, D_MODEL, D_FF_EXPERT), D_MODEL),
        "moe_w_down": nrm(ks[19], (N_MOE, N_EXPERTS, D_FF_EXPERT, D_MODEL), D_FF_EXPERT),
        "ple_w_proj": nrm(ks[20], (DEPTH, PLE_DIM, D_MODEL), PLE_DIM),
        "ple_w_gate": nrm(ks[21], (DEPTH, D_MODEL, D_MODEL), D_MODEL),
        "ple_g_in": gain(ks[22], (DEPTH, D_MODEL)),
        "ple_g_post": gain(ks[23], (DEPTH, D_MODEL)),
    }


def reference(x, p, positions, w_in, conv_w, hgrn_lb_logits, hgrn_norm_g, w_branch, w_out,
              g_mix_pre, g_mix_post, g_ffn_pre, g_ffn_post,
              dense_w_gate, dense_w_up, dense_w_down,
              moe_router, moe_w_gate, moe_w_up, moe_w_down,
              ple_w_proj, ple_w_gate, ple_g_in, ple_g_post):
    b, s, _ = x.shape
    cos, sin = rope_tables(positions, x.dtype)
    lower_bounds = hgrn_lower_bounds(hgrn_lb_logits)
    for i in range(DEPTH):
        h = rms_norm(x, g_mix_pre[i])
        proj = jnp.einsum('bsd,dc->bsc', h, w_in[i])
        o_a = mixer_a(proj[..., :A_COLS], cos, sin)
        o_b = mixer_b(proj[..., A_COLS:A_COLS + B_COLS], lower_bounds[0, i], lower_bounds[1, i], hgrn_norm_g[i])
        o_c = mixer_c(proj[..., A_COLS + B_COLS:A_COLS + B_COLS + C_COLS], conv_w[i])
        gates = jax.nn.sigmoid(proj[..., A_COLS + B_COLS + C_COLS:].reshape(b, s, N_BRANCH, D_MODEL))
        branches = jnp.stack([o_a, o_b, o_c], axis=2)
        up = jnp.einsum('bsnw,nwd->bsnd', branches, w_branch[i])
        merged = jnp.sum(gates * up, axis=2)
        y = merged @ w_out[i]
        x = x + rms_norm(y, g_mix_post[i])
        h = rms_norm(x, g_ffn_pre[i])
        j = i // 2
        if i % 2 == 0:
            y = swiglu(h, dense_w_gate[j], dense_w_up[j], dense_w_down[j])
        else:
            y = moe_swiglu(h, moe_router[j], moe_w_gate[j], moe_w_up[j], moe_w_down[j])
        x = x + rms_norm(y, g_ffn_post[i])
        gate = jax.nn.sigmoid(rms_norm(x, ple_g_in[i]) @ ple_w_gate[i])
        e = (p[i] @ ple_w_proj[i]) * gate
        x = x + rms_norm(e, ple_g_post[i])
    return x
```

```python
import functools
import math

import jax
import jax.numpy as jnp
from jax import lax
from jax.experimental import pallas as pl
from jax.experimental.pallas import tpu as pltpu

D_MODEL = 1024
EPS = 1e-6
NEG_INF = -1e30

HEAD_DIM = 128
N_HEADS = 4
BRANCH = N_HEADS * HEAD_DIM
DIL_GROUPS = ((128, 1), (512, 4), (2048, 16))
N_GROUPS = len(DIL_GROUPS)
BLK = 64
ROT_DIM = HEAD_DIM // 4
ROPE_THETA = 500000.0
TOP_K = 2

A_COLS = N_GROUPS * 3 * BRANCH
B_COLS = 5 * BRANCH
C_COLS = 3 * BRANCH
G_COLS = 3 * D_MODEL

BF16 = jnp.bfloat16
F32 = jnp.float32

VMEM_LIMIT = 56 * 1024 * 1024


def _params(sem):
    return pltpu.CompilerParams(dimension_semantics=sem, vmem_limit_bytes=VMEM_LIMIT)


def _rms(xf, g):
    return xf * lax.rsqrt(jnp.mean(xf * xf, axis=-1, keepdims=True) + EPS) * g


def _sigmoid(z):
    return 1.0 / (1.0 + jnp.exp(-z))


def _dot(a, b):
    return jnp.dot(a, b, preferred_element_type=F32)


def _dot_nt(a, b):
    return lax.dot_general(a, b, (((1,), (1,)), ((), ())), preferred_element_type=F32)


def _dot_tn(a, b):
    return lax.dot_general(a, b, (((0,), (0,)), ((), ())), preferred_element_type=F32)


def _rope_table_kernel(pos_ref, freq_ref, sign_ref, cos_ref, sin_ref):
    ang = pos_ref[...] * freq_ref[...]
    cos_ref[...] = jnp.cos(ang)
    sin_ref[...] = jnp.sin(ang) * sign_ref[...]


def _rope_tables(pos_col, tm):
    t = pos_col.shape[0]
    half = ROT_DIM // 2
    inv_freq = ROPE_THETA ** (-jnp.arange(0, ROT_DIM, 2, dtype=F32) / ROT_DIM)
    freq = jnp.concatenate([inv_freq, inv_freq, jnp.zeros((HEAD_DIM - ROT_DIM,), F32)])[None, :]
    sign = jnp.concatenate([-jnp.ones((half,), F32), jnp.ones((HEAD_DIM - half,), F32)])[None, :]
    row = pl.BlockSpec((1, HEAD_DIM), lambda i: (0, 0))
    tab = pl.BlockSpec((tm, HEAD_DIM), lambda i: (i, 0))
    return pl.pallas_call(
        _rope_table_kernel,
        out_shape=(jax.ShapeDtypeStruct((t, HEAD_DIM), F32),) * 2,
        grid=(t // tm,),
        in_specs=[pl.BlockSpec((tm, 1), lambda i: (i, 0)), row, row],
        out_specs=(tab, tab),
        compiler_params=_params(("parallel",)),
        name="rope_tables",
    )(pos_col, freq, sign)


def _rope_head(th, cos, sin):
    half = ROT_DIM // 2
    lane = lax.broadcasted_iota(jnp.int32, th.shape, 1)
    swapped = jnp.where(lane < half, pltpu.roll(th, HEAD_DIM - half, 1), pltpu.roll(th, half, 1))
    return th * cos + swapped * sin


def _inproj_attn_kernel(x_ref, g_ref, w_ref, cos_ref, sin_ref, o_ref):
    h = _rms(x_ref[...], g_ref[...]).astype(BF16)
    cos = cos_ref[...]
    sin = sin_ref[...]
    scale = 1.0 / math.sqrt(HEAD_DIM)
    for c in range(A_COLS // BRANCH):
        cols = slice(c * BRANCH, (c + 1) * BRANCH)
        t = _dot(h, w_ref[:, cols])
        part = c % 3
        if part == 2:
            o_ref[:, cols] = t.astype(BF16)
        else:
            for hd in range(N_HEADS):
                hs = slice(hd * HEAD_DIM, (hd + 1) * HEAD_DIM)
                r = _rope_head(t[:, hs], cos, sin)
                if part == 0:
                    r = r * scale
                o_ref[:, c * BRANCH + hd * HEAD_DIM:c * BRANCH + (hd + 1) * HEAD_DIM] = r.astype(BF16)


def _inproj_plain_kernel(x_ref, g_ref, w_ref, o_ref):
    h = _rms(x_ref[...], g_ref[...]).astype(BF16)
    for c in range(o_ref.shape[1] // BRANCH):
        cols = slice(c * BRANCH, (c + 1) * BRANCH)
        o_ref[:, cols] = _dot(h, w_ref[:, cols]).astype(BF16)


def _inproj_conv_gate_kernel(x_ref, g_ref, w_ref, cb_ref, gate_ref):
    h = _rms(x_ref[...], g_ref[...]).astype(BF16)
    u = _dot(h, w_ref[:, 0:BRANCH])
    b_gate = _dot(h, w_ref[:, BRANCH:2 * BRANCH])
    c_gate = _dot(h, w_ref[:, 2 * BRANCH:3 * BRANCH])
    cb_ref[:, 0:BRANCH] = (c_gate * u).astype(BF16)
    cb_ref[:, BRANCH:2 * BRANCH] = b_gate.astype(BF16)
    for c in range(G_COLS // BRANCH):
        t = _dot(h, w_ref[:, C_COLS + c * BRANCH:C_COLS + (c + 1) * BRANCH])
        gate_ref[:, c * BRANCH:(c + 1) * BRANCH] = _sigmoid(t).astype(BF16)


def _inproj(kernel, x, g, w, extra, out_widths, tm, name):
    t = x.shape[0]
    n = w.shape[1]
    in_specs = [pl.BlockSpec((tm, D_MODEL), lambda i: (i, 0)),
                pl.BlockSpec((1, D_MODEL), lambda i: (0, 0)),
                pl.BlockSpec((D_MODEL, n), lambda i: (0, 0))]
    in_specs += [pl.BlockSpec((tm, HEAD_DIM), lambda i: (i, 0)) for _ in extra]
    out_shape = tuple(jax.ShapeDtypeStruct((t, wd), BF16) for wd in out_widths)
    out_specs = tuple(pl.BlockSpec((tm, wd), lambda i: (i, 0)) for wd in out_widths)
    return pl.pallas_call(
        kernel, out_shape=out_shape, grid=(t // tm,), in_specs=in_specs, out_specs=out_specs,
        compiler_params=_params(("parallel",)), name=name,
    )(x, g, w, *extra)


def _attn_kernel(q_ref, k_ref, kp_ref, kn_ref, v_ref, vp_ref, vn_ref, o_ref, lse_ref, *, n_super, nj):
    i = pl.program_id(2)
    col_lo = jnp.where(i > 0, 0, BLK)
    col_hi = jnp.where(i < n_super - 1, 3 * BLK, 2 * BLK)
    row = lax.broadcasted_iota(jnp.int32, (BLK, 3 * BLK), 0)
    col = lax.broadcasted_iota(jnp.int32, (BLK, 3 * BLK), 1)
    band = jnp.abs(col - BLK - row) <= BLK
    lane = lax.broadcasted_iota(jnp.int32, (BLK, HEAD_DIM), 1)

    def window(own_ref, prev_ref, next_ref, j, hs):
        parts = []
        for w in (j - 1, j, j + 1):
            if w < 0:
                parts.append(prev_ref[0, :, hs])
            elif w >= nj:
                parts.append(next_ref[0, :, hs])
            else:
                parts.append(own_ref[0, w * BLK:(w + 1) * BLK, hs])
        return jnp.concatenate(parts, axis=0)

    for j in range(nj):
        mask = band
        if j == 0:
            mask = mask & (col >= col_lo)
        if j == nj - 1:
            mask = mask & (col < col_hi)
        rows = slice(j * BLK, (j + 1) * BLK)
        lse_tile = jnp.zeros((BLK, HEAD_DIM), F32)
        for hd in range(N_HEADS):
            hs = slice(hd * HEAD_DIM, (hd + 1) * HEAD_DIM)
            s = _dot_nt(q_ref[0, rows, hs], window(k_ref, kp_ref, kn_ref, j, hs))
            s = jnp.where(mask, s, NEG_INF)
            m = jnp.max(s, axis=1, keepdims=True)
            p = jnp.exp(s - m)
            l = jnp.sum(p, axis=1, keepdims=True)
            o = _dot(p.astype(BF16), window(v_ref, vp_ref, vn_ref, j, hs)) * (1.0 / l)
            o_ref[0, rows, hs] = o.astype(o_ref.dtype)
            lse_tile = jnp.where(lane == hd, m + jnp.log(l), lse_tile)
        lse_ref[0, rows, :] = lse_tile


def _attention_group(qkv, batch, seq, group):
    _, dil = DIL_GROUPS[group]
    t = qkv.shape[0]
    sub_len = seq // dil
    sup = min(4 * BLK, sub_len)
    nj = sup // BLK
    n_super = sub_len // sup
    n_blk = sub_len // BLK
    per_res = A_COLS // BRANCH
    base = group * 3
    view = qkv.reshape(batch, sub_len, dil * A_COLS)

    def own(off):
        return pl.BlockSpec((1, sup, BRANCH), lambda b, r, i: (b, i, r * per_res + base + off))

    def prev(off):
        return pl.BlockSpec((1, BLK, BRANCH),
                            lambda b, r, i: (b, jnp.maximum(i * nj - 1, 0), r * per_res + base + off))

    def nxt(off):
        return pl.BlockSpec((1, BLK, BRANCH),
                            lambda b, r, i: (b, jnp.minimum((i + 1) * nj, n_blk - 1), r * per_res + base + off))

    out, lse = pl.pallas_call(
        functools.partial(_attn_kernel, n_super=n_super, nj=nj),
        out_shape=(jax.ShapeDtypeStruct((batch, sub_len, dil * BRANCH), BF16),
                   jax.ShapeDtypeStruct((batch, sub_len, dil * HEAD_DIM), F32)),
        grid=(batch, dil, n_super),
        in_specs=[own(0), own(1), prev(1), nxt(1), own(2), prev(2), nxt(2)],
        out_specs=(pl.BlockSpec((1, sup, BRANCH), lambda b, r, i: (b, i, r)),
                   pl.BlockSpec((1, sup, HEAD_DIM), lambda b, r, i: (b, i, r))),
        compiler_params=_params(("parallel", "parallel", "parallel")),
        name=f"dilated_attention_g{group}",
    )(view, view, view, view, view, view, view)
    return out.reshape(t, BRANCH), lse.reshape(t, HEAD_DIM)


def _hgrn_kernel(logit_ref, qf_ref, zf_ref, vf_ref, qb_ref, zb_ref, vb_ref, of_ref, ob_ref, state_ref,
                 *, layer):
    c = pl.program_id(1)

    @pl.when(c == 0)
    def _():
        state_ref[...] = jnp.zeros_like(state_ref)

    depth = logit_ref.shape[1]
    lbs = []
    for dirn in range(2):
        rows = [logit_ref[dirn, j:j + 1, :] for j in range(depth)]
        top = functools.reduce(jnp.maximum, rows)
        exps = [jnp.exp(rw - top) for rw in rows]
        lbs.append(sum(exps[1:layer + 1], jnp.zeros_like(top)) / sum(exps[1:], exps[0]))

    r = lax.broadcasted_iota(jnp.int32, (BLK, BLK), 0)
    cc = lax.broadcasted_iota(jnp.int32, (BLK, BLK), 1)

    for dirn, (q_ref, z_ref, v_ref, o_ref) in enumerate(
            ((qf_ref, zf_ref, vf_ref, of_ref), (qb_ref, zb_ref, vb_ref, ob_ref))):
        tri = (cc <= r) if dirn == 0 else (cc >= r)
        tri_b = jnp.where(tri, 1.0, 0.0).astype(BF16)
        last, mid_row = (BLK - 1, BLK // 2 - 1) if dirn == 0 else (0, BLK // 2)
        for hd in range(N_HEADS):
            hs = slice(hd * HEAD_DIM, (hd + 1) * HEAD_DIM)
            q = q_ref[:, hs].astype(F32)
            z = z_ref[:, hs].astype(F32)
            v = v_ref[:, hs]
            lb = lbs[dirn][:, hs]
            sg = _sigmoid(z)
            logf = jnp.log(lb + (1.0 - lb) * sg)
            kk = (1.0 - lb) * (1.0 - sg)
            hi = logf.astype(BF16)
            lo = (logf - hi.astype(F32)).astype(BF16)
            cum = _dot(tri_b, hi) + _dot(tri_b, lo)
            total = cum[last:last + 1, :]
            mid = cum[mid_row:mid_row + 1, :]
            qa = q * jnp.exp(cum - mid)
            ka = kk * jnp.exp(mid - cum)
            att = jnp.where(tri, _dot_nt(qa.astype(BF16), ka.astype(BF16)), 0.0)
            o = _dot(att.astype(BF16), v)
            q_dec = (qa * jnp.exp(mid)).astype(BF16)
            k_dec = (ka * jnp.exp(total - mid)).astype(BF16)
            state_t = state_ref[dirn, hd]
            o = o + _dot_nt(q_dec, state_t.astype(BF16))
            state_ref[dirn, hd] = state_t * jnp.exp(total) + _dot_tn(v, k_dec)
            o_ref[:, hs] = o


def _hgrn(proj_b, logits, batch, seq, layer):
    t = proj_b.shape[0]
    nc = seq // BLK
    depth = logits.shape[1]

    def fwd(colblk):
        return pl.BlockSpec((BLK, BRANCH), lambda b, c: (b * nc + c, colblk))

    def bwd(colblk):
        return pl.BlockSpec((BLK, BRANCH), lambda b, c: (b * nc + nc - 1 - c, colblk))

    return pl.pallas_call(
        functools.partial(_hgrn_kernel, layer=layer),
        out_shape=(jax.ShapeDtypeStruct((t, BRANCH), F32),) * 2,
        grid=(batch, nc),
        in_specs=[pl.BlockSpec((2, depth, BRANCH), lambda b, c: (0, 0, 0)),
                  fwd(0), fwd(1), fwd(3), bwd(0), bwd(2), bwd(3)],
        out_specs=(pl.BlockSpec((BLK, BRANCH), lambda b, c: (b * nc + c, 0)),
                   pl.BlockSpec((BLK, BRANCH), lambda b, c: (b * nc + nc - 1 - c, 0))),
        scratch_shapes=[pltpu.VMEM((2, N_HEADS, HEAD_DIM, HEAD_DIM), F32)],
        compiler_params=_params(("parallel", "arbitrary")),
        name="hgrn2_scan",
    )(logits, proj_b, proj_b, proj_b, proj_b, proj_b, proj_b)


def _merge_kernel(x_ref, ao0_ref, ao1_ref, ao2_ref, l0_ref, l1_ref, l2_ref, of_ref, ob_ref, bg_ref,
                  cb_ref, cbp_ref, cbn_ref, gates_ref, convw_ref, ng_ref, wbr_ref, wout_ref, gpost_ref,
                  o_ref, *, tm, seq):
    i = pl.program_id(0)

    ls = (l0_ref[...], l1_ref[...], l2_ref[...])
    aos = (ao0_ref, ao1_ref, ao2_ref)
    lmax = jnp.maximum(jnp.maximum(ls[0], ls[1]), ls[2])
    es = [jnp.exp(l - lmax) for l in ls]
    inv = 1.0 / (es[0] + es[1] + es[2])
    a_parts = []
    for hd in range(N_HEADS):
        hs = slice(hd * HEAD_DIM, (hd + 1) * HEAD_DIM)
        acc = None
        for g in range(N_GROUPS):
            w = (es[g] * inv)[:, hd:hd + 1]
            term = w * aos[g][:, hs].astype(F32)
            acc = term if acc is None else acc + term
        a_parts.append(acc)
    o_a = jnp.concatenate(a_parts, axis=1).astype(BF16)

    b_parts = []
    for hd in range(N_HEADS):
        hs = slice(hd * HEAD_DIM, (hd + 1) * HEAD_DIM)
        o = of_ref[:, hs] + ob_ref[:, hs]
        b_parts.append(_rms(o, ng_ref[:, hs]))
    gate = bg_ref[...].astype(F32)
    o_b = (jnp.concatenate(b_parts, axis=1) * (gate * _sigmoid(gate))).astype(BF16)

    cu = cb_ref[:, 0:BRANCH].astype(F32)
    keep_prev = jnp.where((i * tm) % seq == 0, 0.0, 1.0)
    keep_next = jnp.where(((i + 1) * tm) % seq == 0, 0.0, 1.0)
    halo_rows = cbp_ref.shape[0]
    prev_row = keep_prev * cbp_ref[halo_rows - 1:halo_rows, 0:BRANCH].astype(F32)
    next_row = keep_next * cbn_ref[0:1, 0:BRANCH].astype(F32)
    rid = lax.broadcasted_iota(jnp.int32, cu.shape, 0)
    before = jnp.where(rid == 0, prev_row, pltpu.roll(cu, 1, 0))
    after = jnp.where(rid == tm - 1, next_row, pltpu.roll(cu, tm - 1, 0))
    conv = convw_ref[0:1, :] * before + convw_ref[1:2, :] * cu + convw_ref[2:3, :] * after
    o_c = (cb_ref[:, BRANCH:2 * BRANCH].astype(F32) * conv).astype(BF16)

    merged = None
    for n, o_n in enumerate((o_a, o_b, o_c)):
        up = _dot(o_n, wbr_ref[n])
        term = gates_ref[:, n * D_MODEL:(n + 1) * D_MODEL].astype(F32) * up
        merged = term if merged is None else merged + term
    y = _dot(merged.astype(BF16), wout_ref[...])
    o_ref[...] = x_ref[...] + _rms(y, gpost_ref[...])


def _merge(x, aos, lses, o_f, o_b, proj_b, cb, gates, conv_w, norm_g, w_branch, w_out, g_post, seq, tm):
    t = x.shape[0]
    halo = 16
    n_halo = t // halo
    per = tm // halo

    def rows(width, colblk=0):
        return pl.BlockSpec((tm, width), lambda i: (i, colblk))

    def full(shape):
        return pl.BlockSpec(shape, lambda i: (0,) * len(shape))

    in_specs = ([rows(D_MODEL)] + [rows(BRANCH)] * 3 + [rows(HEAD_DIM)] * 3 + [rows(BRANCH)] * 2
                + [rows(BRANCH, 4), rows(2 * BRANCH),
                   pl.BlockSpec((halo, 2 * BRANCH), lambda i: (jnp.maximum(i * per - 1, 0), 0)),
                   pl.BlockSpec((halo, 2 * BRANCH), lambda i: (jnp.minimum((i + 1) * per, n_halo - 1), 0)),
                   rows(G_COLS), full((3, BRANCH)), full((1, BRANCH)),
                   full((3, BRANCH, D_MODEL)), full((D_MODEL, D_MODEL)), full((1, D_MODEL))])
    return pl.pallas_call(
        functools.partial(_merge_kernel, tm=tm, seq=seq),
        out_shape=jax.ShapeDtypeStruct((t, D_MODEL), F32),
        grid=(t // tm,),
        in_specs=in_specs,
        out_specs=rows(D_MODEL),
        compiler_params=_params(("parallel",)),
        name="merge_branches",
    )(x, *aos, *lses, o_f, o_b, proj_b, cb, cb, cb, gates, conv_w, norm_g, w_branch, w_out, g_post)


def _ffn_kernel(x_ref, gpre_ref, gpost_ref, rhi_ref, rlo_ref, wg_ref, wu_ref, wd_ref, o_ref,
                h_ref, acc_ref, comb_ref, *, n_experts, routed):
    e = pl.program_id(1)
    c = pl.program_id(2)
    first = (e == 0) & (c == 0)
    last = (e == pl.num_programs(1) - 1) & (c == pl.num_programs(2) - 1)

    @pl.when(first)
    def _():
        hf = _rms(x_ref[...], gpre_ref[...])
        h_ref[...] = hf.astype(BF16)
        acc_ref[...] = jnp.zeros_like(acc_ref)
        if routed:
            h_hi = hf.astype(BF16)
            h_lo = (hf - h_hi.astype(F32)).astype(BF16)
            logits = _dot(h_hi, rhi_ref[...]) + _dot(h_hi, rlo_ref[...]) + _dot(h_lo, rhi_ref[...])
            lane = lax.broadcasted_iota(jnp.int32, logits.shape, 1).astype(F32)
            logits = jnp.where(lane < n_experts, logits, NEG_INF)
            m1 = jnp.max(logits, axis=1, keepdims=True)
            i1 = jnp.min(jnp.where(logits == m1, lane, 1e9), axis=1, keepdims=True)
            rest = jnp.where(lane == i1, NEG_INF, logits)
            m2 = jnp.max(rest, axis=1, keepdims=True)
            i2 = jnp.min(jnp.where(rest == m2, lane, 1e9), axis=1, keepdims=True)
            ex = jnp.exp(m2 - m1)
            w1 = 1.0 / (1.0 + ex)
            comb_ref[...] = jnp.where(lane == i1, w1, 0.0) + jnp.where(lane == i2, ex * w1, 0.0)

    h = h_ref[...]
    g = _dot(h, wg_ref[0])
    u = _dot(h, wu_ref[0])
    y = _dot((g * _sigmoid(g) * u).astype(BF16), wd_ref[0])
    if routed:
        lane = lax.broadcasted_iota(jnp.int32, comb_ref.shape, 1)
        y = y * jnp.sum(jnp.where(lane == e, comb_ref[...], 0.0), axis=1, keepdims=True)
    acc_ref[...] += y

    @pl.when(last)
    def _():
        o_ref[...] = x_ref[...] + _rms(acc_ref[...], gpost_ref[...])


def _ffn(x, g_pre, g_post, router_hi, router_lo, wg, wu, wd, tm, fc, routed):
    t = x.shape[0]
    n_experts, _, ff = wg.shape
    grid = (t // tm, n_experts, ff // fc)
    return pl.pallas_call(
        functools.partial(_ffn_kernel, n_experts=n_experts, routed=routed),
        out_shape=jax.ShapeDtypeStruct((t, D_MODEL), F32),
        grid=grid,
        in_specs=[pl.BlockSpec((tm, D_MODEL), lambda i, e, c: (i, 0)),
                  pl.BlockSpec((1, D_MODEL), lambda i, e, c: (0, 0)),
                  pl.BlockSpec((1, D_MODEL), lambda i, e, c: (0, 0)),
                  pl.BlockSpec((D_MODEL, HEAD_DIM), lambda i, e, c: (0, 0)),
                  pl.BlockSpec((D_MODEL, HEAD_DIM), lambda i, e, c: (0, 0)),
                  pl.BlockSpec((1, D_MODEL, fc), lambda i, e, c: (e, 0, c)),
                  pl.BlockSpec((1, D_MODEL, fc), lambda i, e, c: (e, 0, c)),
                  pl.BlockSpec((1, fc, D_MODEL), lambda i, e, c: (e, c, 0))],
        out_specs=pl.BlockSpec((tm, D_MODEL), lambda i, e, c: (i, 0)),
        scratch_shapes=[pltpu.VMEM((tm, D_MODEL), BF16), pltpu.VMEM((tm, D_MODEL), F32),
                        pltpu.VMEM((tm, HEAD_DIM), F32)],
        compiler_params=_params(("parallel", "arbitrary", "arbitrary")),
        name="moe_swiglu" if routed else "dense_swiglu",
    )(x, g_pre, g_post, router_hi, router_lo, wg, wu, wd)


def _ple_kernel(x_ref, p_ref, gin_ref, gpost_ref, wgate_ref, wproj_ref, o_ref):
    x = x_ref[...]
    gate = _sigmoid(_dot(_rms(x, gin_ref[...]).astype(BF16), wgate_ref[...]))
    e = _dot(p_ref[...].astype(BF16), wproj_ref[...]) * gate
    o_ref[...] = x + _rms(e, gpost_ref[...])


def _ple(x, p, g_in, g_post, w_gate, w_proj, tm):
    t = x.shape[0]
    pd = p.shape[1]
    return pl.pallas_call(
        _ple_kernel,
        out_shape=jax.ShapeDtypeStruct((t, D_MODEL), F32),
        grid=(t // tm,),
        in_specs=[pl.BlockSpec((tm, D_MODEL), lambda i: (i, 0)),
                  pl.BlockSpec((tm, pd), lambda i: (i, 0)),
                  pl.BlockSpec((1, D_MODEL), lambda i: (0, 0)),
                  pl.BlockSpec((1, D_MODEL), lambda i: (0, 0)),
                  pl.BlockSpec((D_MODEL, D_MODEL), lambda i: (0, 0)),
                  pl.BlockSpec((pd, D_MODEL), lambda i: (0, 0))],
        out_specs=pl.BlockSpec((tm, D_MODEL), lambda i: (i, 0)),
        compiler_params=_params(("parallel",)),
        name="layer_embedding",
    )(x, p, g_in, g_post, w_gate, w_proj)


def _largest_divisor(n, cap, multiple):
    best = multiple
    for cand in range(multiple, cap + 1, multiple):
        if n % cand == 0:
            best = cand
    return best


def kernel(x, p, positions, w_in, conv_w, hgrn_lb_logits, hgrn_norm_g, w_branch, w_out, g_mix_pre, g_mix_post, g_ffn_pre, g_ffn_post, dense_w_gate, dense_w_up, dense_w_down, moe_router, moe_w_gate, moe_w_up, moe_w_down, ple_w_proj, ple_w_gate, ple_g_in, ple_g_post):
    batch, seq, _ = x.shape
    depth = w_in.shape[0]
    t = batch * seq
    tm = min(512, seq)
    tm_merge = min(256, seq)
    tm_ffn = min(1024, seq)

    xt = x.reshape(t, D_MODEL)
    cos, sin = _rope_tables(positions.astype(F32).reshape(t, 1), tm)
    row = lambda a: a.reshape(1, -1)

    for i in range(depth):
        w_i = w_in[i].astype(BF16)
        g_pre = row(g_mix_pre[i])
        (qkv,) = _inproj(_inproj_attn_kernel, xt, g_pre, w_i[:, :A_COLS], (cos, sin), (A_COLS,), tm,
                         "inproj_attention")
        (proj_b,) = _inproj(_inproj_plain_kernel, xt, g_pre, w_i[:, A_COLS:A_COLS + B_COLS], (), (B_COLS,), tm,
                            "inproj_hgrn")
        cb, gates = _inproj(_inproj_conv_gate_kernel, xt, g_pre, w_i[:, A_COLS + B_COLS:], (),
                            (2 * BRANCH, G_COLS), tm, "inproj_conv_gates")

        aos, lses = zip(*[_attention_group(qkv, batch, seq, g) for g in range(N_GROUPS)])
        o_f, o_b = _hgrn(proj_b, hgrn_lb_logits, batch, seq, i)
        xt = _merge(xt, aos, lses, o_f, o_b, proj_b, cb, gates, conv_w[i], row(hgrn_norm_g[i]),
                    w_branch[i].astype(BF16), w_out[i].astype(BF16), row(g_mix_post[i]), seq, tm_merge)

        j = i // 2
        if i % 2 == 0:
            ff = dense_w_gate.shape[2]
            zeros_r = jnp.zeros((D_MODEL, HEAD_DIM), BF16)
            xt = _ffn(xt, row(g_ffn_pre[i]), row(g_ffn_post[i]), zeros_r, zeros_r,
                      dense_w_gate[j].astype(BF16)[None], dense_w_up[j].astype(BF16)[None],
                      dense_w_down[j].astype(BF16)[None], tm_ffn, _largest_divisor(ff, 1536, HEAD_DIM),
                      routed=False)
        else:
            n_experts = moe_router.shape[2]
            ff = moe_w_gate.shape[3]
            router = jnp.pad(moe_router[j], ((0, 0), (0, HEAD_DIM - n_experts)))
            r_hi = router.astype(BF16)
            r_lo = (router - r_hi.astype(F32)).astype(BF16)
            xt = _ffn(xt, row(g_ffn_pre[i]), row(g_ffn_post[i]), r_hi, r_lo,
                      moe_w_gate[j].astype(BF16), moe_w_up[j].astype(BF16), moe_w_down[j].astype(BF16),
                      tm_ffn, _largest_divisor(ff, 1024, HEAD_DIM), routed=True)

        xt = _ple(xt, p[i].reshape(t, -1), row(ple_g_in[i]), row(ple_g_post[i]),
                  ple_w_gate[i].astype(BF16), ple_w_proj[i].astype(BF16), tm)

    return xt.reshape(batch, seq, D_MODEL)
```

```python
import functools
import math

import jax
import jax.numpy as jnp
from jax import lax
from jax.experimental import pallas as pl
from jax.experimental.pallas import tpu as pltpu

D_MODEL = 1024
EPS = 1e-6
NEG_INF = -1e30

HEAD_DIM = 128
N_HEADS = 4
BRANCH = N_HEADS * HEAD_DIM
DIL_GROUPS = ((128, 1), (512, 4), (2048, 16))
N_GROUPS = len(DIL_GROUPS)
BLK = 64
ROT_DIM = HEAD_DIM // 4
ROPE_THETA = 500000.0
TOP_K = 2

A_COLS = N_GROUPS * 3 * BRANCH
B_COLS = 5 * BRANCH
C_COLS = 3 * BRANCH
G_COLS = 3 * D_MODEL

BF16 = jnp.bfloat16
F32 = jnp.float32

VMEM_LIMIT = 56 * 1024 * 1024


def _params(sem):
    return pltpu.CompilerParams(dimension_semantics=sem, vmem_limit_bytes=VMEM_LIMIT)


def _rms(xf, g):
    return xf * lax.rsqrt(jnp.mean(xf * xf, axis=-1, keepdims=True) + EPS) * g


def _sigmoid(z):
    return 1.0 / (1.0 + jnp.exp(-z))


def _dot(a, b):
    return jnp.dot(a, b, preferred_element_type=F32)


def _dot_nt(a, b):
    return lax.dot_general(a, b, (((1,), (1,)), ((), ())), preferred_element_type=F32)


def _dot_tn(a, b):
    return lax.dot_general(a, b, (((0,), (0,)), ((), ())), preferred_element_type=F32)


def _rope_table_kernel(pos_ref, freq_ref, sign_ref, cos_ref, sin_ref):
    ang = pos_ref[...] * freq_ref[...]
    cos_ref[...] = jnp.cos(ang)
    sin_ref[...] = jnp.sin(ang) * sign_ref[...]


def _rope_tables(pos_col, tm):
    t = pos_col.shape[0]
    half = ROT_DIM // 2
    inv_freq = ROPE_THETA ** (-jnp.arange(0, ROT_DIM, 2, dtype=F32) / ROT_DIM)
    freq = jnp.concatenate([inv_freq, inv_freq, jnp.zeros((HEAD_DIM - ROT_DIM,), F32)])[None, :]
    sign = jnp.concatenate([-jnp.ones((half,), F32), jnp.ones((HEAD_DIM - half,), F32)])[None, :]
    row = pl.BlockSpec((1, HEAD_DIM), lambda i: (0, 0))
    tab = pl.BlockSpec((tm, HEAD_DIM), lambda i: (i, 0))
    return pl.pallas_call(
        _rope_table_kernel,
        out_shape=(jax.ShapeDtypeStruct((t, HEAD_DIM), F32),) * 2,
        grid=(t // tm,),
        in_specs=[pl.BlockSpec((tm, 1), lambda i: (i, 0)), row, row],
        out_specs=(tab, tab),
        compiler_params=_params(("parallel",)),
        name="rope_tables",
    )(pos_col, freq, sign)


def _rope_head(th, cos, sin):
    half = ROT_DIM // 2
    lane = lax.broadcasted_iota(jnp.int32, th.shape, 1)
    swapped = jnp.where(lane < half, pltpu.roll(th, HEAD_DIM - half, 1), pltpu.roll(th, half, 1))
    return th * cos + swapped * sin


def _inproj_attn_kernel(x_ref, g_ref, w_ref, cos_ref, sin_ref, o0_ref, o1_ref, o2_ref, stage_ref):
    h = _rms(x_ref[...], g_ref[...]).astype(BF16)
    cos = cos_ref[...]
    sin = sin_ref[...]
    scale = 1.0 / math.sqrt(HEAD_DIM)
    tm = x_ref.shape[0]
    for grp, o_ref in enumerate((o0_ref, o1_ref, o2_ref)):
        dil = DIL_GROUPS[grp][1]
        for part in range(3):
            c = grp * 3 + part
            t = _dot(h, w_ref[:, c * BRANCH:(c + 1) * BRANCH])
            if part < 2:
                heads = []
                for hd in range(N_HEADS):
                    r = _rope_head(t[:, hd * HEAD_DIM:(hd + 1) * HEAD_DIM], cos, sin)
                    heads.append(r * scale if part == 0 else r)
                t = jnp.concatenate(heads, axis=1)
            cols = slice(part * BRANCH, (part + 1) * BRANCH)
            if dil == 1:
                o_ref[0, 0, :, cols] = t.astype(BF16)
            else:
                for hd in range(N_HEADS):
                    stage_ref[hd] = t[:, hd * HEAD_DIM:(hd + 1) * HEAD_DIM]
                for r in range(dil):
                    for hd in range(N_HEADS):
                        lo = part * BRANCH + hd * HEAD_DIM
                        o_ref[0, r, :, lo:lo + HEAD_DIM] = (
                            stage_ref[hd, pl.ds(r, tm // dil, stride=dil), :].astype(BF16))


def _inproj_attention(x, g, w, cos, sin, batch, seq, tm):
    per_seq = seq // tm
    out_shape = tuple(jax.ShapeDtypeStruct((batch, dil, seq // dil, 3 * BRANCH), BF16) for _, dil in DIL_GROUPS)
    out_specs = tuple(pl.BlockSpec((1, dil, tm // dil, 3 * BRANCH), lambda b, j: (b, 0, j, 0))
                      for _, dil in DIL_GROUPS)
    rows = lambda width: pl.BlockSpec((tm, width), lambda b, j: (b * per_seq + j, 0))
    return pl.pallas_call(
        _inproj_attn_kernel, out_shape=out_shape, grid=(batch, per_seq),
        in_specs=[rows(D_MODEL), pl.BlockSpec((1, D_MODEL), lambda b, j: (0, 0)),
                  pl.BlockSpec((D_MODEL, A_COLS), lambda b, j: (0, 0)), rows(HEAD_DIM), rows(HEAD_DIM)],
        out_specs=out_specs,
        scratch_shapes=[pltpu.VMEM((N_HEADS, tm, HEAD_DIM), F32)],
        compiler_params=_params(("parallel", "parallel")), name="inproj_attention",
    )(x, g, w, cos, sin)


def _inproj_plain_kernel(x_ref, g_ref, w_ref, o_ref):
    h = _rms(x_ref[...], g_ref[...]).astype(BF16)
    for c in range(o_ref.shape[1] // BRANCH):
        cols = slice(c * BRANCH, (c + 1) * BRANCH)
        o_ref[:, cols] = _dot(h, w_ref[:, cols]).astype(BF16)


def _inproj_conv_gate_kernel(x_ref, g_ref, w_ref, cb_ref, gate_ref):
    h = _rms(x_ref[...], g_ref[...]).astype(BF16)
    u = _dot(h, w_ref[:, 0:BRANCH])
    b_gate = _dot(h, w_ref[:, BRANCH:2 * BRANCH])
    c_gate = _dot(h, w_ref[:, 2 * BRANCH:3 * BRANCH])
    cb_ref[:, 0:BRANCH] = (c_gate * u).astype(BF16)
    cb_ref[:, BRANCH:2 * BRANCH] = b_gate.astype(BF16)
    for c in range(G_COLS // BRANCH):
        t = _dot(h, w_ref[:, C_COLS + c * BRANCH:C_COLS + (c + 1) * BRANCH])
        gate_ref[:, c * BRANCH:(c + 1) * BRANCH] = _sigmoid(t).astype(BF16)


def _inproj(kernel, x, g, w, out_widths, tm, name):
    t = x.shape[0]
    n = w.shape[1]
    in_specs = [pl.BlockSpec((tm, D_MODEL), lambda i: (i, 0)),
                pl.BlockSpec((1, D_MODEL), lambda i: (0, 0)),
                pl.BlockSpec((D_MODEL, n), lambda i: (0, 0))]
    out_shape = tuple(jax.ShapeDtypeStruct((t, wd), BF16) for wd in out_widths)
    out_specs = tuple(pl.BlockSpec((tm, wd), lambda i: (i, 0)) for wd in out_widths)
    return pl.pallas_call(
        kernel, out_shape=out_shape, grid=(t // tm,), in_specs=in_specs, out_specs=out_specs,
        compiler_params=_params(("parallel",)), name=name,
    )(x, g, w)


def _attn_kernel(q_ref, k_ref, kp_ref, kn_ref, v_ref, vp_ref, vn_ref, o_ref, lse_ref, o_stage, lse_stage,
                 *, dil, nj, n_steps):
    n = pl.program_id(1)
    col_lo = jnp.where(n > 0, 0, BLK)
    col_hi = jnp.where(n < n_steps - 1, 3 * BLK, 2 * BLK)
    row = lax.broadcasted_iota(jnp.int32, (BLK, 3 * BLK), 0)
    col = lax.broadcasted_iota(jnp.int32, (BLK, 3 * BLK), 1)
    band = jnp.abs(col - BLK - row) <= BLK
    masks = {}
    for j in range(nj):
        m = band
        if j == 0:
            m = m & (col >= col_lo)
        if j == nj - 1:
            m = m & (col < col_hi)
        masks[j] = m
    pairs = [(r, j) for r in range(dil) for j in range(nj)]
    lane = lax.broadcasted_iota(jnp.int32, (len(pairs), BLK, HEAD_DIM), 2)

    def window(own_ref, prev_ref, next_ref, r, j, hs):
        parts = []
        for w in (j - 1, j, j + 1):
            if w < 0:
                parts.append(prev_ref[0, r, :, hs])
            elif w >= nj:
                parts.append(next_ref[0, r, :, hs])
            else:
                parts.append(own_ref[0, r, w * BLK:(w + 1) * BLK, hs])
        return jnp.concatenate(parts, axis=0)

    lse_tiles = jnp.zeros((len(pairs), BLK, HEAD_DIM), F32)
    for hd in range(N_HEADS):
        hs = slice(hd * HEAD_DIM, (hd + 1) * HEAD_DIM)
        scores = [_dot_nt(q_ref[0, r, j * BLK:(j + 1) * BLK, hs], window(k_ref, kp_ref, kn_ref, r, j, hs))
                  for r, j in pairs]
        s = jnp.stack([jnp.where(masks[j], sc, NEG_INF) for (r, j), sc in zip(pairs, scores)])
        m = jnp.max(s, axis=2, keepdims=True)
        p = jnp.exp(s - m)
        l = jnp.sum(p, axis=2, keepdims=True)
        pb = p.astype(BF16)
        outs = jnp.stack([_dot(pb[i], window(v_ref, vp_ref, vn_ref, r, j, hs)) for i, (r, j) in enumerate(pairs)])
        outs = outs * (1.0 / l)
        lse_tiles = jnp.where(lane == hd, m + jnp.log(l), lse_tiles)
        for i, (r, j) in enumerate(pairs):
            o_stage[hd, pl.ds(j * BLK * dil + r, BLK, stride=dil), :] = outs[i]
        o_ref[:, hs] = o_stage[hd].astype(o_ref.dtype)
    for i, (r, j) in enumerate(pairs):
        lse_stage[pl.ds(j * BLK * dil + r, BLK, stride=dil), :] = lse_tiles[i]
    lse_ref[...] = lse_stage[...]


def _attention_group(qkv, batch, seq, group):
    _, dil = DIL_GROUPS[group]
    t = batch * seq
    sub_len = seq // dil
    rows_per_step = min(seq, 16 * BLK)
    nj = rows_per_step // (dil * BLK)
    n_steps = seq // rows_per_step
    n_blk = sub_len // BLK
    sub_rows = rows_per_step // dil

    def own(part):
        return pl.BlockSpec((1, dil, sub_rows, BRANCH), lambda b, n: (b, 0, n, part))

    def prev(part):
        return pl.BlockSpec((1, dil, BLK, BRANCH), lambda b, n: (b, 0, jnp.maximum(n * nj - 1, 0), part))

    def nxt(part):
        return pl.BlockSpec((1, dil, BLK, BRANCH), lambda b, n: (b, 0, jnp.minimum((n + 1) * nj, n_blk - 1), part))

    return pl.pallas_call(
        functools.partial(_attn_kernel, dil=dil, nj=nj, n_steps=n_steps),
        out_shape=(jax.ShapeDtypeStruct((t, BRANCH), BF16), jax.ShapeDtypeStruct((t, HEAD_DIM), F32)),
        grid=(batch, n_steps),
        in_specs=[own(0), own(1), prev(1), nxt(1), own(2), prev(2), nxt(2)],
        out_specs=(pl.BlockSpec((rows_per_step, BRANCH), lambda b, n: (b * n_steps + n, 0)),
                   pl.BlockSpec((rows_per_step, HEAD_DIM), lambda b, n: (b * n_steps + n, 0))),
        scratch_shapes=[pltpu.VMEM((N_HEADS, rows_per_step, HEAD_DIM), F32), pltpu.VMEM((rows_per_step, HEAD_DIM), F32)],
        compiler_params=_params(("parallel", "parallel")),
        name=f"dilated_attention_g{group}",
    )(qkv, qkv, qkv, qkv, qkv, qkv, qkv)


def _hgrn_kernel(logit_ref, qf_ref, zf_ref, vf_ref, qb_ref, zb_ref, vb_ref, of_ref, ob_ref, state_ref,
                 *, layer, n_chunks):
    c = pl.program_id(1)

    @pl.when(c == 0)
    def _():
        state_ref[...] = jnp.zeros_like(state_ref)

    depth = logit_ref.shape[1]
    lbs = []
    for dirn in range(2):
        rows = [logit_ref[dirn, j:j + 1, :] for j in range(depth)]
        top = functools.reduce(jnp.maximum, rows)
        exps = [jnp.exp(rw - top) for rw in rows]
        lbs.append(sum(exps[1:layer + 1], jnp.zeros_like(top)) / sum(exps[1:], exps[0]))

    r = lax.broadcasted_iota(jnp.int32, (BLK, BLK), 0)
    cc = lax.broadcasted_iota(jnp.int32, (BLK, BLK), 1)

    def bcast_rows(per_chunk):
        return jnp.concatenate([jnp.broadcast_to(v, (BLK, BRANCH)) for v in per_chunk], axis=0)

    for dirn, (q_ref, z_ref, v_ref, o_ref) in enumerate(
            ((qf_ref, zf_ref, vf_ref, of_ref), (qb_ref, zb_ref, vb_ref, ob_ref))):
        tri = (cc <= r) if dirn == 0 else (cc >= r)
        tri_b = jnp.where(tri, 1.0, 0.0).astype(BF16)
        last, mid_row = (BLK - 1, BLK // 2 - 1) if dirn == 0 else (0, BLK // 2)
        lb = lbs[dirn]
        q = q_ref[...].astype(F32)
        z = z_ref[...].astype(F32)
        v = v_ref[...]
        sg = _sigmoid(z)
        logf = jnp.log(lb + (1.0 - lb) * sg)
        kk = (1.0 - lb) * (1.0 - sg)
        hi = logf.astype(BF16)
        lo = (logf - hi.astype(F32)).astype(BF16)
        chunks = [slice(g * BLK, (g + 1) * BLK) for g in range(n_chunks)]
        cums = [_dot(tri_b, hi[ch]) + _dot(tri_b, lo[ch]) for ch in chunks]
        totals = [cm[last:last + 1, :] for cm in cums]
        mids = [cm[mid_row:mid_row + 1, :] for cm in cums]
        cum = jnp.concatenate(cums, axis=0)
        mid = bcast_rows(mids)
        qa = q * jnp.exp(cum - mid)
        ka = kk * jnp.exp(mid - cum)
        q_dec = (qa * bcast_rows([jnp.exp(md) for md in mids])).astype(BF16)
        k_dec = (ka * bcast_rows([jnp.exp(tt - md) for tt, md in zip(totals, mids)])).astype(BF16)
        qa = qa.astype(BF16)
        ka = ka.astype(BF16)
        tiles = [(g, hd) for g in range(n_chunks) for hd in range(N_HEADS)]
        sl = lambda g, hd: (slice(g * BLK, (g + 1) * BLK), slice(hd * HEAD_DIM, (hd + 1) * HEAD_DIM))
        atts = [jnp.where(tri, _dot_nt(qa[sl(g, hd)], ka[sl(g, hd)]), 0.0).astype(BF16) for g, hd in tiles]
        intra = {t: _dot(att, v[sl(*t)]) for t, att in zip(tiles, atts)}
        kvs = {t: _dot_tn(v[sl(*t)], k_dec[sl(*t)]) for t in tiles}
        decays = [jnp.exp(tt) for tt in totals]
        order = range(n_chunks) if dirn == 0 else range(n_chunks - 1, -1, -1)
        for hd in range(N_HEADS):
            hs = slice(hd * HEAD_DIM, (hd + 1) * HEAD_DIM)
            state_t = state_ref[dirn, hd]
            for g in order:
                o_ref[g * BLK:(g + 1) * BLK, hs] = intra[(g, hd)] + _dot_nt(q_dec[sl(g, hd)], state_t.astype(BF16))
                state_t = state_t * decays[g][:, hs] + kvs[(g, hd)]
            state_ref[dirn, hd] = state_t


def _hgrn(proj_b, logits, batch, seq, layer):
    t = proj_b.shape[0]
    n_chunks = 4
    rows = n_chunks * BLK
    nc = seq // rows
    depth = logits.shape[1]

    def fwd(colblk):
        return pl.BlockSpec((rows, BRANCH), lambda b, c: (b * nc + c, colblk))

    def bwd(colblk):
        return pl.BlockSpec((rows, BRANCH), lambda b, c: (b * nc + nc - 1 - c, colblk))

    return pl.pallas_call(
        functools.partial(_hgrn_kernel, layer=layer, n_chunks=n_chunks),
        out_shape=(jax.ShapeDtypeStruct((t, BRANCH), F32),) * 2,
        grid=(batch, nc),
        in_specs=[pl.BlockSpec((2, depth, BRANCH), lambda b, c: (0, 0, 0)),
                  fwd(0), fwd(1), fwd(3), bwd(0), bwd(2), bwd(3)],
        out_specs=(pl.BlockSpec((rows, BRANCH), lambda b, c: (b * nc + c, 0)),
                   pl.BlockSpec((rows, BRANCH), lambda b, c: (b * nc + nc - 1 - c, 0))),
        scratch_shapes=[pltpu.VMEM((2, N_HEADS, HEAD_DIM, HEAD_DIM), F32)],
        compiler_params=_params(("parallel", "arbitrary")),
        name="hgrn2_scan",
    )(logits, proj_b, proj_b, proj_b, proj_b, proj_b, proj_b)


def _merge_kernel(x_ref, ao0_ref, ao1_ref, ao2_ref, l0_ref, l1_ref, l2_ref, of_ref, ob_ref, bg_ref,
                  cb_ref, cbp_ref, cbn_ref, gates_ref, convw_ref, ng_ref, wbr_ref, wout_ref, gpost_ref,
                  o_ref, *, tm, seq):
    i = pl.program_id(0)

    ls = (l0_ref[...], l1_ref[...], l2_ref[...])
    aos = (ao0_ref, ao1_ref, ao2_ref)
    lmax = jnp.maximum(jnp.maximum(ls[0], ls[1]), ls[2])
    es = [jnp.exp(l - lmax) for l in ls]
    inv = 1.0 / (es[0] + es[1] + es[2])
    a_parts = []
    for hd in range(N_HEADS):
        hs = slice(hd * HEAD_DIM, (hd + 1) * HEAD_DIM)
        acc = None
        for g in range(N_GROUPS):
            w = (es[g] * inv)[:, hd:hd + 1]
            term = w * aos[g][:, hs].astype(F32)
            acc = term if acc is None else acc + term
        a_parts.append(acc)
    o_a = jnp.concatenate(a_parts, axis=1).astype(BF16)

    b_parts = []
    for hd in range(N_HEADS):
        hs = slice(hd * HEAD_DIM, (hd + 1) * HEAD_DIM)
        o = of_ref[:, hs] + ob_ref[:, hs]
        b_parts.append(_rms(o, ng_ref[:, hs]))
    gate = bg_ref[...].astype(F32)
    o_b = (jnp.concatenate(b_parts, axis=1) * (gate * _sigmoid(gate))).astype(BF16)

    cu = cb_ref[:, 0:BRANCH].astype(F32)
    keep_prev = jnp.where((i * tm) % seq == 0, 0.0, 1.0)
    keep_next = jnp.where(((i + 1) * tm) % seq == 0, 0.0, 1.0)
    halo_rows = cbp_ref.shape[0]
    prev_row = keep_prev * cbp_ref[halo_rows - 1:halo_rows, 0:BRANCH].astype(F32)
    next_row = keep_next * cbn_ref[0:1, 0:BRANCH].astype(F32)
    rid = lax.broadcasted_iota(jnp.int32, cu.shape, 0)
    before = jnp.where(rid == 0, prev_row, pltpu.roll(cu, 1, 0))
    after = jnp.where(rid == tm - 1, next_row, pltpu.roll(cu, tm - 1, 0))
    conv = convw_ref[0:1, :] * before + convw_ref[1:2, :] * cu + convw_ref[2:3, :] * after
    o_c = (cb_ref[:, BRANCH:2 * BRANCH].astype(F32) * conv).astype(BF16)

    merged = None
    for n, o_n in enumerate((o_a, o_b, o_c)):
        up = _dot(o_n, wbr_ref[n])
        term = gates_ref[:, n * D_MODEL:(n + 1) * D_MODEL].astype(F32) * up
        merged = term if merged is None else merged + term
    y = _dot(merged.astype(BF16), wout_ref[...])
    o_ref[...] = x_ref[...] + _rms(y, gpost_ref[...])


def _merge(x, aos, lses, o_f, o_b, proj_b, cb, gates, conv_w, norm_g, w_branch, w_out, g_post, seq, tm):
    t = x.shape[0]
    halo = 16
    n_halo = t // halo
    per = tm // halo

    def rows(width, colblk=0):
        return pl.BlockSpec((tm, width), lambda i: (i, colblk))

    def full(shape):
        return pl.BlockSpec(shape, lambda i: (0,) * len(shape))

    in_specs = ([rows(D_MODEL)] + [rows(BRANCH)] * 3 + [rows(HEAD_DIM)] * 3 + [rows(BRANCH)] * 2
                + [rows(BRANCH, 4), rows(2 * BRANCH),
                   pl.BlockSpec((halo, 2 * BRANCH), lambda i: (jnp.maximum(i * per - 1, 0), 0)),
                   pl.BlockSpec((halo, 2 * BRANCH), lambda i: (jnp.minimum((i + 1) * per, n_halo - 1), 0)),
                   rows(G_COLS), full((3, BRANCH)), full((1, BRANCH)),
                   full((3, BRANCH, D_MODEL)), full((D_MODEL, D_MODEL)), full((1, D_MODEL))])
    return pl.pallas_call(
        functools.partial(_merge_kernel, tm=tm, seq=seq),
        out_shape=jax.ShapeDtypeStruct((t, D_MODEL), F32),
        grid=(t // tm,),
        in_specs=in_specs,
        out_specs=rows(D_MODEL),
        compiler_params=_params(("parallel",)),
        name="merge_branches",
    )(x, *aos, *lses, o_f, o_b, proj_b, cb, cb, cb, gates, conv_w, norm_g, w_branch, w_out, g_post)


def _ffn_kernel(x_ref, gpre_ref, gpost_ref, rhi_ref, rlo_ref, wg_ref, wu_ref, wd_ref, o_ref,
                h_ref, acc_ref, comb_ref, *, n_experts, routed):
    e = pl.program_id(1)
    c = pl.program_id(2)
    first = (e == 0) & (c == 0)
    last = (e == pl.num_programs(1) - 1) & (c == pl.num_programs(2) - 1)

    @pl.when(first)
    def _():
        hf = _rms(x_ref[...], gpre_ref[...])
        h_ref[...] = hf.astype(BF16)
        acc_ref[...] = jnp.zeros_like(acc_ref)
        if routed:
            h_hi = hf.astype(BF16)
            h_lo = (hf - h_hi.astype(F32)).astype(BF16)
            logits = _dot(h_hi, rhi_ref[...]) + _dot(h_hi, rlo_ref[...]) + _dot(h_lo, rhi_ref[...])
            lane = lax.broadcasted_iota(jnp.int32, logits.shape, 1).astype(F32)
            logits = jnp.where(lane < n_experts, logits, NEG_INF)
            m1 = jnp.max(logits, axis=1, keepdims=True)
            i1 = jnp.min(jnp.where(logits == m1, lane, 1e9), axis=1, keepdims=True)
            rest = jnp.where(lane == i1, NEG_INF, logits)
            m2 = jnp.max(rest, axis=1, keepdims=True)
            i2 = jnp.min(jnp.where(rest == m2, lane, 1e9), axis=1, keepdims=True)
            ex = jnp.exp(m2 - m1)
            w1 = 1.0 / (1.0 + ex)
            comb_ref[...] = jnp.where(lane == i1, w1, 0.0) + jnp.where(lane == i2, ex * w1, 0.0)

    h = h_ref[...]
    g = _dot(h, wg_ref[0])
    u = _dot(h, wu_ref[0])
    y = _dot((g * _sigmoid(g) * u).astype(BF16), wd_ref[0])
    if routed:
        lane = lax.broadcasted_iota(jnp.int32, comb_ref.shape, 1)
        y = y * jnp.sum(jnp.where(lane == e, comb_ref[...], 0.0), axis=1, keepdims=True)
    acc_ref[...] += y

    @pl.when(last)
    def _():
        o_ref[...] = x_ref[...] + _rms(acc_ref[...], gpost_ref[...])


def _ffn(x, g_pre, g_post, router_hi, router_lo, wg, wu, wd, tm, fc, routed):
    t = x.shape[0]
    n_experts, _, ff = wg.shape
    grid = (t // tm, n_experts, ff // fc)
    return pl.pallas_call(
        functools.partial(_ffn_kernel, n_experts=n_experts, routed=routed),
        out_shape=jax.ShapeDtypeStruct((t, D_MODEL), F32),
        grid=grid,
        in_specs=[pl.BlockSpec((tm, D_MODEL), lambda i, e, c: (i, 0)),
                  pl.BlockSpec((1, D_MODEL), lambda i, e, c: (0, 0)),
                  pl.BlockSpec((1, D_MODEL), lambda i, e, c: (0, 0)),
                  pl.BlockSpec((D_MODEL, HEAD_DIM), lambda i, e, c: (0, 0)),
                  pl.BlockSpec((D_MODEL, HEAD_DIM), lambda i, e, c: (0, 0)),
                  pl.BlockSpec((1, D_MODEL, fc), lambda i, e, c: (e, 0, c)),
                  pl.BlockSpec((1, D_MODEL, fc), lambda i, e, c: (e, 0, c)),
                  pl.BlockSpec((1, fc, D_MODEL), lambda i, e, c: (e, c, 0))],
        out_specs=pl.BlockSpec((tm, D_MODEL), lambda i, e, c: (i, 0)),
        scratch_shapes=[pltpu.VMEM((tm, D_MODEL), BF16), pltpu.VMEM((tm, D_MODEL), F32),
                        pltpu.VMEM((tm, HEAD_DIM), F32)],
        compiler_params=_params(("parallel", "arbitrary", "arbitrary")),
        name="moe_swiglu" if routed else "dense_swiglu",
    )(x, g_pre, g_post, router_hi, router_lo, wg, wu, wd)


def _ple_kernel(x_ref, p_ref, gin_ref, gpost_ref, wgate_ref, wproj_ref, o_ref):
    x = x_ref[...]
    gate = _sigmoid(_dot(_rms(x, gin_ref[...]).astype(BF16), wgate_ref[...]))
    e = _dot(p_ref[...].astype(BF16), wproj_ref[...]) * gate
    o_ref[...] = x + _rms(e, gpost_ref[...])


def _ple(x, p, g_in, g_post, w_gate, w_proj, tm):
    t = x.shape[0]
    pd = p.shape[1]
    return pl.pallas_call(
        _ple_kernel,
        out_shape=jax.ShapeDtypeStruct((t, D_MODEL), F32),
        grid=(t // tm,),
        in_specs=[pl.BlockSpec((tm, D_MODEL), lambda i: (i, 0)),
                  pl.BlockSpec((tm, pd), lambda i: (i, 0)),
                  pl.BlockSpec((1, D_MODEL), lambda i: (0, 0)),
                  pl.BlockSpec((1, D_MODEL), lambda i: (0, 0)),
                  pl.BlockSpec((D_MODEL, D_MODEL), lambda i: (0, 0)),
                  pl.BlockSpec((pd, D_MODEL), lambda i: (0, 0))],
        out_specs=pl.BlockSpec((tm, D_MODEL), lambda i: (i, 0)),
        compiler_params=_params(("parallel",)),
        name="layer_embedding",
    )(x, p, g_in, g_post, w_gate, w_proj)


def _largest_divisor(n, cap, multiple):
    best = multiple
    for cand in range(multiple, cap + 1, multiple):
        if n % cand == 0:
            best = cand
    return best


def kernel(x, p, positions, w_in, conv_w, hgrn_lb_logits, hgrn_norm_g, w_branch, w_out, g_mix_pre, g_mix_post, g_ffn_pre, g_ffn_post, dense_w_gate, dense_w_up, dense_w_down, moe_router, moe_w_gate, moe_w_up, moe_w_down, ple_w_proj, ple_w_gate, ple_g_in, ple_g_post):
    batch, seq, _ = x.shape
    depth = w_in.shape[0]
    t = batch * seq
    tm = min(512, seq)
    tm_merge = min(256, seq)
    tm_ffn = min(1024, seq)

    xt = x.reshape(t, D_MODEL)
    cos, sin = _rope_tables(positions.astype(F32).reshape(t, 1), tm)
    row = lambda a: a.reshape(1, -1)

    for i in range(depth):
        w_i = w_in[i].astype(BF16)
        g_pre = row(g_mix_pre[i])
        qkvs = _inproj_attention(xt, g_pre, w_i[:, :A_COLS], cos, sin, batch, seq, tm)
        (proj_b,) = _inproj(_inproj_plain_kernel, xt, g_pre, w_i[:, A_COLS:A_COLS + B_COLS], (B_COLS,), tm,
                            "inproj_hgrn")
        cb, gates = _inproj(_inproj_conv_gate_kernel, xt, g_pre, w_i[:, A_COLS + B_COLS:],
                            (2 * BRANCH, G_COLS), tm, "inproj_conv_gates")

        aos, lses = zip(*[_attention_group(qkvs[g], batch, seq, g) for g in range(N_GROUPS)])
        o_f, o_b = _hgrn(proj_b, hgrn_lb_logits, batch, seq, i)
        xt = _merge(xt, aos, lses, o_f, o_b, proj_b, cb, gates, conv_w[i], row(hgrn_norm_g[i]),
                    w_branch[i].astype(BF16), w_out[i].astype(BF16), row(g_mix_post[i]), seq, tm_merge)

        j = i // 2
        if i % 2 == 0:
            ff = dense_w_gate.shape[2]
            zeros_r = jnp.zeros((D_MODEL, HEAD_DIM), BF16)
            xt = _ffn(xt, row(g_ffn_pre[i]), row(g_ffn_post[i]), zeros_r, zeros_r,
                      dense_w_gate[j].astype(BF16)[None], dense_w_up[j].astype(BF16)[None],
                      dense_w_down[j].astype(BF16)[None], tm_ffn, _largest_divisor(ff, 1536, HEAD_DIM),
                      routed=False)
        else:
            n_experts = moe_router.shape[2]
            ff = moe_w_gate.shape[3]
            router = jnp.pad(moe_router[j], ((0, 0), (0, HEAD_DIM - n_experts)))
            r_hi = router.astype(BF16)
            r_lo = (router - r_hi.astype(F32)).astype(BF16)
            xt = _ffn(xt, row(g_ffn_pre[i]), row(g_ffn_post[i]), r_hi, r_lo,
                      moe_w_gate[j].astype(BF16), moe_w_up[j].astype(BF16), moe_w_down[j].astype(BF16),
                      tm_ffn, _largest_divisor(ff, 1024, HEAD_DIM), routed=True)

        xt = _ple(xt, p[i].reshape(t, -1), row(ple_g_in[i]), row(ple_g_post[i]),
                  ple_w_gate[i].astype(BF16), ple_w_proj[i].astype(BF16), tm)

    return xt.reshape(batch, seq, D_MODEL)
```

```python
import functools
import math

import jax
import jax.numpy as jnp
from jax import lax
from jax.experimental import pallas as pl
from jax.experimental.pallas import tpu as pltpu
from jax.experimental.pallas import tpu_sc as plsc

D_MODEL = 1024
EPS = 1e-6
NEG_INF = -1e30

HEAD_DIM = 128
N_HEADS = 4
BRANCH = N_HEADS * HEAD_DIM
DIL_GROUPS = ((128, 1), (512, 4), (2048, 16))
N_GROUPS = len(DIL_GROUPS)
BLK = 64
ROT_DIM = HEAD_DIM // 4
ROPE_THETA = 500000.0
TOP_K = 2

A_COLS = N_GROUPS * 3 * BRANCH
B_COLS = 5 * BRANCH
C_COLS = 3 * BRANCH
G_COLS = 3 * D_MODEL

BF16 = jnp.bfloat16
F32 = jnp.float32

VMEM_LIMIT = 56 * 1024 * 1024


def _params(sem):
    return pltpu.CompilerParams(dimension_semantics=sem, vmem_limit_bytes=VMEM_LIMIT)


def _rms(xf, g):
    return xf * lax.rsqrt(jnp.mean(xf * xf, axis=-1, keepdims=True) + EPS) * g


def _sigmoid(z):
    return 1.0 / (1.0 + jnp.exp(-z))


def _dot(a, b):
    return jnp.dot(a, b, preferred_element_type=F32)


def _dot_nt(a, b):
    return lax.dot_general(a, b, (((1,), (1,)), ((), ())), preferred_element_type=F32)


def _dot_tn(a, b):
    return lax.dot_general(a, b, (((0,), (0,)), ((), ())), preferred_element_type=F32)


def _rope_table_kernel(pos_ref, freq_ref, sign_ref, cos_ref, sin_ref):
    ang = pos_ref[...] * freq_ref[...]
    cos_ref[...] = jnp.cos(ang)
    sin_ref[...] = jnp.sin(ang) * sign_ref[...]


def _rope_tables(pos_col, tm):
    t = pos_col.shape[0]
    half = ROT_DIM // 2
    inv_freq = ROPE_THETA ** (-jnp.arange(0, ROT_DIM, 2, dtype=F32) / ROT_DIM)
    freq = jnp.concatenate([inv_freq, inv_freq, jnp.zeros((HEAD_DIM - ROT_DIM,), F32)])[None, :]
    sign = jnp.concatenate([-jnp.ones((half,), F32), jnp.ones((HEAD_DIM - half,), F32)])[None, :]
    row = pl.BlockSpec((1, HEAD_DIM), lambda i: (0, 0))
    tab = pl.BlockSpec((tm, HEAD_DIM), lambda i: (i, 0))
    return pl.pallas_call(
        _rope_table_kernel,
        out_shape=(jax.ShapeDtypeStruct((t, HEAD_DIM), F32),) * 2,
        grid=(t // tm,),
        in_specs=[pl.BlockSpec((tm, 1), lambda i: (i, 0)), row, row],
        out_specs=(tab, tab),
        compiler_params=_params(("parallel",)),
        name="rope_tables",
    )(pos_col, freq, sign)


def _rope_head(th, cos, sin):
    half = ROT_DIM // 2
    lane = lax.broadcasted_iota(jnp.int32, th.shape, 1)
    swapped = jnp.where(lane < half, pltpu.roll(th, HEAD_DIM - half, 1), pltpu.roll(th, half, 1))
    return th * cos + swapped * sin


def _inproj_attn_kernel(x_ref, g_ref, w_ref, cos_ref, sin_ref, o0_ref, o1_ref, o2_ref, stage_ref):
    h = _rms(x_ref[...], g_ref[...]).astype(BF16)
    cos = cos_ref[...]
    sin = sin_ref[...]
    scale = 1.0 / math.sqrt(HEAD_DIM)
    tm = x_ref.shape[0]
    for grp, o_ref in enumerate((o0_ref, o1_ref, o2_ref)):
        dil = DIL_GROUPS[grp][1]
        for part in range(3):
            c = grp * 3 + part
            t = _dot(h, w_ref[:, c * BRANCH:(c + 1) * BRANCH])
            if part < 2:
                heads = []
                for hd in range(N_HEADS):
                    r = _rope_head(t[:, hd * HEAD_DIM:(hd + 1) * HEAD_DIM], cos, sin)
                    heads.append(r * scale if part == 0 else r)
                t = jnp.concatenate(heads, axis=1)
            cols = slice(part * BRANCH, (part + 1) * BRANCH)
            if dil == 1:
                o_ref[0, 0, :, cols] = t.astype(BF16)
            else:
                for hd in range(N_HEADS):
                    stage_ref[hd] = t[:, hd * HEAD_DIM:(hd + 1) * HEAD_DIM]
                for r in range(dil):
                    for hd in range(N_HEADS):
                        lo = part * BRANCH + hd * HEAD_DIM
                        o_ref[0, r, :, lo:lo + HEAD_DIM] = (
                            stage_ref[hd, pl.ds(r, tm // dil, stride=dil), :].astype(BF16))


def _inproj_attention(x, g, w, cos, sin, batch, seq, tm):
    per_seq = seq // tm
    out_shape = tuple(jax.ShapeDtypeStruct((batch, dil, seq // dil, 3 * BRANCH), BF16) for _, dil in DIL_GROUPS)
    out_specs = tuple(pl.BlockSpec((1, dil, tm // dil, 3 * BRANCH), lambda b, j: (b, 0, j, 0))
                      for _, dil in DIL_GROUPS)
    rows = lambda width: pl.BlockSpec((tm, width), lambda b, j: (b * per_seq + j, 0))
    return pl.pallas_call(
        _inproj_attn_kernel, out_shape=out_shape, grid=(batch, per_seq),
        in_specs=[rows(D_MODEL), pl.BlockSpec((1, D_MODEL), lambda b, j: (0, 0)),
                  pl.BlockSpec((D_MODEL, A_COLS), lambda b, j: (0, 0)), rows(HEAD_DIM), rows(HEAD_DIM)],
        out_specs=out_specs,
        scratch_shapes=[pltpu.VMEM((N_HEADS, tm, HEAD_DIM), F32)],
        compiler_params=_params(("parallel", "parallel")), name="inproj_attention",
    )(x, g, w, cos, sin)


def _inproj_plain_kernel(x_ref, g_ref, w_ref, o_ref):
    h = _rms(x_ref[...], g_ref[...]).astype(BF16)
    for c in range(o_ref.shape[1] // BRANCH):
        cols = slice(c * BRANCH, (c + 1) * BRANCH)
        o_ref[:, cols] = _dot(h, w_ref[:, cols]).astype(BF16)


def _inproj_conv_gate_kernel(x_ref, g_ref, w_ref, cb_ref, gate_ref):
    h = _rms(x_ref[...], g_ref[...]).astype(BF16)
    u = _dot(h, w_ref[:, 0:BRANCH])
    b_gate = _dot(h, w_ref[:, BRANCH:2 * BRANCH])
    c_gate = _dot(h, w_ref[:, 2 * BRANCH:3 * BRANCH])
    cb_ref[:, 0:BRANCH] = (c_gate * u).astype(BF16)
    cb_ref[:, BRANCH:2 * BRANCH] = b_gate.astype(BF16)
    for c in range(G_COLS // BRANCH):
        t = _dot(h, w_ref[:, C_COLS + c * BRANCH:C_COLS + (c + 1) * BRANCH])
        gate_ref[:, c * BRANCH:(c + 1) * BRANCH] = _sigmoid(t).astype(BF16)


def _inproj(kernel, x, g, w, out_widths, tm, name):
    t = x.shape[0]
    n = w.shape[1]
    in_specs = [pl.BlockSpec((tm, D_MODEL), lambda i: (i, 0)),
                pl.BlockSpec((1, D_MODEL), lambda i: (0, 0)),
                pl.BlockSpec((D_MODEL, n), lambda i: (0, 0))]
    out_shape = tuple(jax.ShapeDtypeStruct((t, wd), BF16) for wd in out_widths)
    out_specs = tuple(pl.BlockSpec((tm, wd), lambda i: (i, 0)) for wd in out_widths)
    return pl.pallas_call(
        kernel, out_shape=out_shape, grid=(t // tm,), in_specs=in_specs, out_specs=out_specs,
        compiler_params=_params(("parallel",)), name=name,
    )(x, g, w)


def _attn_kernel(q_ref, k_ref, kp_ref, kn_ref, v_ref, vp_ref, vn_ref, o_ref, lse_ref, o_stage, lse_stage,
                 *, dil, nj, n_steps):
    n = pl.program_id(1)
    col_lo = jnp.where(n > 0, 0, BLK)
    col_hi = jnp.where(n < n_steps - 1, 3 * BLK, 2 * BLK)
    row = lax.broadcasted_iota(jnp.int32, (BLK, 3 * BLK), 0)
    col = lax.broadcasted_iota(jnp.int32, (BLK, 3 * BLK), 1)
    band = jnp.abs(col - BLK - row) <= BLK
    masks = {}
    for j in range(nj):
        m = band
        if j == 0:
            m = m & (col >= col_lo)
        if j == nj - 1:
            m = m & (col < col_hi)
        masks[j] = m
    pairs = [(r, j) for r in range(dil) for j in range(nj)]
    lane = lax.broadcasted_iota(jnp.int32, (len(pairs), BLK, HEAD_DIM), 2)

    def window(own_ref, prev_ref, next_ref, r, j, hs):
        parts = []
        for w in (j - 1, j, j + 1):
            if w < 0:
                parts.append(prev_ref[0, r, :, hs])
            elif w >= nj:
                parts.append(next_ref[0, r, :, hs])
            else:
                parts.append(own_ref[0, r, w * BLK:(w + 1) * BLK, hs])
        return jnp.concatenate(parts, axis=0)

    lse_tiles = jnp.zeros((len(pairs), BLK, HEAD_DIM), F32)
    for hd in range(N_HEADS):
        hs = slice(hd * HEAD_DIM, (hd + 1) * HEAD_DIM)
        scores = [_dot_nt(q_ref[0, r, j * BLK:(j + 1) * BLK, hs], window(k_ref, kp_ref, kn_ref, r, j, hs))
                  for r, j in pairs]
        s = jnp.stack([jnp.where(masks[j], sc, NEG_INF) for (r, j), sc in zip(pairs, scores)])
        m = jnp.max(s, axis=2, keepdims=True)
        p = jnp.exp(s - m)
        l = jnp.sum(p, axis=2, keepdims=True)
        pb = p.astype(BF16)
        outs = jnp.stack([_dot(pb[i], window(v_ref, vp_ref, vn_ref, r, j, hs)) for i, (r, j) in enumerate(pairs)])
        outs = outs * (1.0 / l)
        lse_tiles = jnp.where(lane == hd, m + jnp.log(l), lse_tiles)
        for i, (r, j) in enumerate(pairs):
            o_stage[hd, pl.ds(j * BLK * dil + r, BLK, stride=dil), :] = outs[i]
        o_ref[:, hs] = o_stage[hd].astype(o_ref.dtype)
    for i, (r, j) in enumerate(pairs):
        lse_stage[pl.ds(j * BLK * dil + r, BLK, stride=dil), :] = lse_tiles[i]
    lse_ref[...] = lse_stage[...]


def _attention_group(qkv, batch, seq, group):
    _, dil = DIL_GROUPS[group]
    t = batch * seq
    sub_len = seq // dil
    rows_per_step = min(seq, 16 * BLK)
    nj = rows_per_step // (dil * BLK)
    n_steps = seq // rows_per_step
    n_blk = sub_len // BLK
    sub_rows = rows_per_step // dil

    def own(part):
        return pl.BlockSpec((1, dil, sub_rows, BRANCH), lambda b, n: (b, 0, n, part))

    def prev(part):
        return pl.BlockSpec((1, dil, BLK, BRANCH), lambda b, n: (b, 0, jnp.maximum(n * nj - 1, 0), part))

    def nxt(part):
        return pl.BlockSpec((1, dil, BLK, BRANCH), lambda b, n: (b, 0, jnp.minimum((n + 1) * nj, n_blk - 1), part))

    return pl.pallas_call(
        functools.partial(_attn_kernel, dil=dil, nj=nj, n_steps=n_steps),
        out_shape=(jax.ShapeDtypeStruct((t, BRANCH), BF16), jax.ShapeDtypeStruct((t, HEAD_DIM), F32)),
        grid=(batch, n_steps),
        in_specs=[own(0), own(1), prev(1), nxt(1), own(2), prev(2), nxt(2)],
        out_specs=(pl.BlockSpec((rows_per_step, BRANCH), lambda b, n: (b * n_steps + n, 0)),
                   pl.BlockSpec((rows_per_step, HEAD_DIM), lambda b, n: (b * n_steps + n, 0))),
        scratch_shapes=[pltpu.VMEM((N_HEADS, rows_per_step, HEAD_DIM), F32), pltpu.VMEM((rows_per_step, HEAD_DIM), F32)],
        compiler_params=_params(("parallel", "parallel")),
        name=f"dilated_attention_g{group}",
    )(qkv, qkv, qkv, qkv, qkv, qkv, qkv)


def _hgrn_kernel(logit_ref, qf_ref, zf_ref, vf_ref, qb_ref, zb_ref, vb_ref, of_ref, ob_ref, state_ref,
                 *, layer, n_chunks):
    c = pl.program_id(1)

    @pl.when(c == 0)
    def _():
        state_ref[...] = jnp.zeros_like(state_ref)

    depth = logit_ref.shape[1]
    lbs = []
    for dirn in range(2):
        rows = [logit_ref[dirn, j:j + 1, :] for j in range(depth)]
        top = functools.reduce(jnp.maximum, rows)
        exps = [jnp.exp(rw - top) for rw in rows]
        lbs.append(sum(exps[1:layer + 1], jnp.zeros_like(top)) / sum(exps[1:], exps[0]))

    r = lax.broadcasted_iota(jnp.int32, (BLK, BLK), 0)
    cc = lax.broadcasted_iota(jnp.int32, (BLK, BLK), 1)

    def bcast_rows(per_chunk):
        return jnp.concatenate([jnp.broadcast_to(v, (BLK, BRANCH)) for v in per_chunk], axis=0)

    for dirn, (q_ref, z_ref, v_ref, o_ref) in enumerate(
            ((qf_ref, zf_ref, vf_ref, of_ref), (qb_ref, zb_ref, vb_ref, ob_ref))):
        tri = (cc <= r) if dirn == 0 else (cc >= r)
        tri_b = jnp.where(tri, 1.0, 0.0).astype(BF16)
        last, mid_row = (BLK - 1, BLK // 2 - 1) if dirn == 0 else (0, BLK // 2)
        lb = lbs[dirn]
        q = q_ref[...].astype(F32)
        z = z_ref[...].astype(F32)
        v = v_ref[...]
        sg = _sigmoid(z)
        logf = jnp.log(lb + (1.0 - lb) * sg)
        kk = (1.0 - lb) * (1.0 - sg)
        hi = logf.astype(BF16)
        lo = (logf - hi.astype(F32)).astype(BF16)
        chunks = [slice(g * BLK, (g + 1) * BLK) for g in range(n_chunks)]
        cums = [_dot(tri_b, hi[ch]) + _dot(tri_b, lo[ch]) for ch in chunks]
        totals = [cm[last:last + 1, :] for cm in cums]
        mids = [cm[mid_row:mid_row + 1, :] for cm in cums]
        cum = jnp.concatenate(cums, axis=0)
        mid = bcast_rows(mids)
        qa = q * jnp.exp(cum - mid)
        ka = kk * jnp.exp(mid - cum)
        q_dec = (qa * bcast_rows([jnp.exp(md) for md in mids])).astype(BF16)
        k_dec = (ka * bcast_rows([jnp.exp(tt - md) for tt, md in zip(totals, mids)])).astype(BF16)
        qa = qa.astype(BF16)
        ka = ka.astype(BF16)
        tiles = [(g, hd) for g in range(n_chunks) for hd in range(N_HEADS)]
        sl = lambda g, hd: (slice(g * BLK, (g + 1) * BLK), slice(hd * HEAD_DIM, (hd + 1) * HEAD_DIM))
        atts = [jnp.where(tri, _dot_nt(qa[sl(g, hd)], ka[sl(g, hd)]), 0.0).astype(BF16) for g, hd in tiles]
        intra = {t: _dot(att, v[sl(*t)]) for t, att in zip(tiles, atts)}
        kvs = {t: _dot_tn(v[sl(*t)], k_dec[sl(*t)]) for t in tiles}
        decays = [jnp.exp(tt) for tt in totals]
        order = range(n_chunks) if dirn == 0 else range(n_chunks - 1, -1, -1)
        for hd in range(N_HEADS):
            hs = slice(hd * HEAD_DIM, (hd + 1) * HEAD_DIM)
            state_t = state_ref[dirn, hd]
            for g in order:
                o_ref[g * BLK:(g + 1) * BLK, hs] = intra[(g, hd)] + _dot_nt(q_dec[sl(g, hd)], state_t.astype(BF16))
                state_t = state_t * decays[g][:, hs] + kvs[(g, hd)]
            state_ref[dirn, hd] = state_t


def _hgrn(proj_b, logits, batch, seq, layer):
    t = proj_b.shape[0]
    n_chunks = 4
    rows = n_chunks * BLK
    nc = seq // rows
    depth = logits.shape[1]

    def fwd(colblk):
        return pl.BlockSpec((rows, BRANCH), lambda b, c: (b * nc + c, colblk))

    def bwd(colblk):
        return pl.BlockSpec((rows, BRANCH), lambda b, c: (b * nc + nc - 1 - c, colblk))

    return pl.pallas_call(
        functools.partial(_hgrn_kernel, layer=layer, n_chunks=n_chunks),
        out_shape=(jax.ShapeDtypeStruct((t, BRANCH), F32),) * 2,
        grid=(batch, nc),
        in_specs=[pl.BlockSpec((2, depth, BRANCH), lambda b, c: (0, 0, 0)),
                  fwd(0), fwd(1), fwd(3), bwd(0), bwd(2), bwd(3)],
        out_specs=(pl.BlockSpec((rows, BRANCH), lambda b, c: (b * nc + c, 0)),
                   pl.BlockSpec((rows, BRANCH), lambda b, c: (b * nc + nc - 1 - c, 0))),
        scratch_shapes=[pltpu.VMEM((2, N_HEADS, HEAD_DIM, HEAD_DIM), F32)],
        compiler_params=_params(("parallel", "arbitrary")),
        name="hgrn2_scan",
    )(logits, proj_b, proj_b, proj_b, proj_b, proj_b, proj_b)


def _merge_kernel(x_ref, ao0_ref, ao1_ref, ao2_ref, l0_ref, l1_ref, l2_ref, of_ref, ob_ref, bg_ref,
                  cb_ref, cbp_ref, cbn_ref, gates_ref, convw_ref, ng_ref, wbr_ref, wout_ref, gpost_ref,
                  o_ref, *, tm, seq):
    i = pl.program_id(0)

    ls = (l0_ref[...], l1_ref[...], l2_ref[...])
    aos = (ao0_ref, ao1_ref, ao2_ref)
    lmax = jnp.maximum(jnp.maximum(ls[0], ls[1]), ls[2])
    es = [jnp.exp(l - lmax) for l in ls]
    inv = 1.0 / (es[0] + es[1] + es[2])
    a_parts = []
    for hd in range(N_HEADS):
        hs = slice(hd * HEAD_DIM, (hd + 1) * HEAD_DIM)
        acc = None
        for g in range(N_GROUPS):
            w = (es[g] * inv)[:, hd:hd + 1]
            term = w * aos[g][:, hs].astype(F32)
            acc = term if acc is None else acc + term
        a_parts.append(acc)
    o_a = jnp.concatenate(a_parts, axis=1).astype(BF16)

    b_parts = []
    for hd in range(N_HEADS):
        hs = slice(hd * HEAD_DIM, (hd + 1) * HEAD_DIM)
        o = of_ref[:, hs] + ob_ref[:, hs]
        b_parts.append(_rms(o, ng_ref[:, hs]))
    gate = bg_ref[...].astype(F32)
    o_b = (jnp.concatenate(b_parts, axis=1) * (gate * _sigmoid(gate))).astype(BF16)

    cu = cb_ref[:, 0:BRANCH].astype(F32)
    keep_prev = jnp.where((i * tm) % seq == 0, 0.0, 1.0)
    keep_next = jnp.where(((i + 1) * tm) % seq == 0, 0.0, 1.0)
    halo_rows = cbp_ref.shape[0]
    prev_row = keep_prev * cbp_ref[halo_rows - 1:halo_rows, 0:BRANCH].astype(F32)
    next_row = keep_next * cbn_ref[0:1, 0:BRANCH].astype(F32)
    rid = lax.broadcasted_iota(jnp.int32, cu.shape, 0)
    before = jnp.where(rid == 0, prev_row, pltpu.roll(cu, 1, 0))
    after = jnp.where(rid == tm - 1, next_row, pltpu.roll(cu, tm - 1, 0))
    conv = convw_ref[0:1, :] * before + convw_ref[1:2, :] * cu + convw_ref[2:3, :] * after
    o_c = (cb_ref[:, BRANCH:2 * BRANCH].astype(F32) * conv).astype(BF16)

    merged = None
    for n, o_n in enumerate((o_a, o_b, o_c)):
        up = _dot(o_n, wbr_ref[n])
        term = gates_ref[:, n * D_MODEL:(n + 1) * D_MODEL].astype(F32) * up
        merged = term if merged is None else merged + term
    y = _dot(merged.astype(BF16), wout_ref[...])
    o_ref[...] = x_ref[...] + _rms(y, gpost_ref[...])


def _merge(x, aos, lses, o_f, o_b, proj_b, cb, gates, conv_w, norm_g, w_branch, w_out, g_post, seq, tm):
    t = x.shape[0]
    halo = 16
    n_halo = t // halo
    per = tm // halo

    def rows(width, colblk=0):
        return pl.BlockSpec((tm, width), lambda i: (i, colblk))

    def full(shape):
        return pl.BlockSpec(shape, lambda i: (0,) * len(shape))

    in_specs = ([rows(D_MODEL)] + [rows(BRANCH)] * 3 + [rows(HEAD_DIM)] * 3 + [rows(BRANCH)] * 2
                + [rows(BRANCH, 4), rows(2 * BRANCH),
                   pl.BlockSpec((halo, 2 * BRANCH), lambda i: (jnp.maximum(i * per - 1, 0), 0)),
                   pl.BlockSpec((halo, 2 * BRANCH), lambda i: (jnp.minimum((i + 1) * per, n_halo - 1), 0)),
                   rows(G_COLS), full((3, BRANCH)), full((1, BRANCH)),
                   full((3, BRANCH, D_MODEL)), full((D_MODEL, D_MODEL)), full((1, D_MODEL))])
    return pl.pallas_call(
        functools.partial(_merge_kernel, tm=tm, seq=seq),
        out_shape=jax.ShapeDtypeStruct((t, D_MODEL), F32),
        grid=(t // tm,),
        in_specs=in_specs,
        out_specs=rows(D_MODEL),
        compiler_params=_params(("parallel",)),
        name="merge_branches",
    )(x, *aos, *lses, o_f, o_b, proj_b, cb, cb, cb, gates, conv_w, norm_g, w_branch, w_out, g_post)


def _ffn_kernel(x_ref, gpre_ref, gpost_ref, wg_ref, wu_ref, wd_ref, o_ref, h_ref, acc_ref):
    c = pl.program_id(1)

    @pl.when(c == 0)
    def _():
        h_ref[...] = _rms(x_ref[...], gpre_ref[...]).astype(BF16)
        acc_ref[...] = jnp.zeros_like(acc_ref)

    h = h_ref[...]
    g = _dot(h, wg_ref[...])
    u = _dot(h, wu_ref[...])
    acc_ref[...] += _dot((g * _sigmoid(g) * u).astype(BF16), wd_ref[...])

    @pl.when(c == pl.num_programs(1) - 1)
    def _():
        o_ref[...] = x_ref[...] + _rms(acc_ref[...], gpost_ref[...])


def _ffn(x, g_pre, g_post, wg, wu, wd, tm, fc):
    t = x.shape[0]
    ff = wg.shape[1]
    return pl.pallas_call(
        _ffn_kernel,
        out_shape=jax.ShapeDtypeStruct((t, D_MODEL), F32),
        grid=(t // tm, ff // fc),
        in_specs=[pl.BlockSpec((tm, D_MODEL), lambda i, c: (i, 0)),
                  pl.BlockSpec((1, D_MODEL), lambda i, c: (0, 0)),
                  pl.BlockSpec((1, D_MODEL), lambda i, c: (0, 0)),
                  pl.BlockSpec((D_MODEL, fc), lambda i, c: (0, c)),
                  pl.BlockSpec((D_MODEL, fc), lambda i, c: (0, c)),
                  pl.BlockSpec((fc, D_MODEL), lambda i, c: (c, 0))],
        out_specs=pl.BlockSpec((tm, D_MODEL), lambda i, c: (i, 0)),
        scratch_shapes=[pltpu.VMEM((tm, D_MODEL), BF16), pltpu.VMEM((tm, D_MODEL), F32)],
        compiler_params=_params(("parallel", "arbitrary")),
        name="dense_swiglu",
    )(x, g_pre, g_post, wg, wu, wd)


PIECES = D_MODEL // 2 // HEAD_DIM
SC_WINDOW = 128
EXPERT_TILE = 512


def _pack_rows(vals):
    bits = lax.bitcast_convert_type(vals.astype(BF16).astype(F32), jnp.uint32)
    half = D_MODEL // 2
    word = bits[:, :half] | (bits[:, half:] >> 16)
    return [lax.bitcast_convert_type(word[:, c * HEAD_DIM:(c + 1) * HEAD_DIM], jnp.int32) for c in range(PIECES)]


def _unpack_rows(piece_refs):
    words = [lax.bitcast_convert_type(r[...], jnp.uint32) for r in piece_refs]
    hi = [lax.bitcast_convert_type(w & jnp.uint32(0xFFFF0000), F32) for w in words]
    lo = [lax.bitcast_convert_type(w << 16, F32) for w in words]
    return jnp.concatenate(hi + lo, axis=1)


def _route_kernel(x_ref, gpre_ref, rhi_ref, rlo_ref, *refs, n_experts):
    piece_refs = refs[:PIECES]
    rec_ref, rank_ref, count_ref, carry_ref = refs[PIECES:]
    i = pl.program_id(0)
    tm = x_ref.shape[0]

    @pl.when(i == 0)
    def _():
        carry_ref[...] = jnp.zeros_like(carry_ref)

    hf = _rms(x_ref[...], gpre_ref[...])
    for r, piece in zip(piece_refs, _pack_rows(hf)):
        r[...] = piece
    h_hi = hf.astype(BF16)
    h_lo = (hf - h_hi.astype(F32)).astype(BF16)
    logits = _dot(h_hi, rhi_ref[...]) + _dot(h_hi, rlo_ref[...]) + _dot(h_lo, rhi_ref[...])
    lane = lax.broadcasted_iota(jnp.int32, logits.shape, 1).astype(F32)
    logits = jnp.where(lane < n_experts, logits, NEG_INF)
    m1 = jnp.max(logits, axis=1, keepdims=True)
    i1 = jnp.min(jnp.where(logits == m1, lane, 1e9), axis=1, keepdims=True)
    rest = jnp.where(lane == i1, NEG_INF, logits)
    m2 = jnp.max(rest, axis=1, keepdims=True)
    i2 = jnp.min(jnp.where(rest == m2, lane, 1e9), axis=1, keepdims=True)
    ex = jnp.exp(m2 - m1)
    w1 = 1.0 / (1.0 + ex)
    rec = jnp.where(lane == 0, i1, jnp.where(lane == 1, i2, jnp.where(lane == 2, w1, jnp.where(lane == 3, ex * w1, 0.0))))
    rec_ref[...] = rec
    rec_t = rec.T
    e1 = rec_t[0:1, :]
    e2 = rec_t[1:2, :]
    sub = lax.broadcasted_iota(jnp.int32, (8, tm), 0).astype(F32)
    oh1 = jnp.where(sub == e1, 1.0, 0.0)
    oh2 = jnp.where(sub == e2, 1.0, 0.0)
    chosen = oh1 + oh2
    src = lax.broadcasted_iota(jnp.int32, (tm, tm), 0)
    dst = lax.broadcasted_iota(jnp.int32, (tm, tm), 1)
    before = jnp.where(src < dst, 1.0, 0.0).astype(BF16)
    rank = _dot(chosen.astype(BF16), before) + carry_ref[:, 0:1]
    rank1 = jnp.sum(oh1 * rank, axis=0, keepdims=True)
    rank2 = jnp.sum(oh2 * rank, axis=0, keepdims=True)
    rank_ref[...] = jnp.where(sub == 0, rank1, jnp.where(sub == 1, rank2, jnp.where(sub == 2, e1, jnp.where(sub == 3, e2, 0.0))))
    carry_ref[...] = carry_ref[...] + jnp.sum(chosen, axis=1, keepdims=True)
    count_ref[...] = carry_ref[...]


def _route(x, g_pre, r_hi, r_lo, n_experts, tm):
    t = x.shape[0]
    assert n_experts <= 8
    piece = pl.BlockSpec((tm, HEAD_DIM), lambda i: (i, 0))
    return pl.pallas_call(
        functools.partial(_route_kernel, n_experts=n_experts),
        out_shape=(*[jax.ShapeDtypeStruct((t, HEAD_DIM), jnp.int32)] * PIECES,
                   jax.ShapeDtypeStruct((t, HEAD_DIM), F32), jax.ShapeDtypeStruct((8, t), F32),
                   jax.ShapeDtypeStruct((8, HEAD_DIM), F32)),
        grid=(t // tm,),
        in_specs=[pl.BlockSpec((tm, D_MODEL), lambda i: (i, 0)), pl.BlockSpec((1, D_MODEL), lambda i: (0, 0)),
                  pl.BlockSpec((D_MODEL, HEAD_DIM), lambda i: (0, 0)), pl.BlockSpec((D_MODEL, HEAD_DIM), lambda i: (0, 0))],
        out_specs=(*[piece] * PIECES, piece, pl.BlockSpec((8, tm), lambda i: (0, i)),
                   pl.BlockSpec((8, HEAD_DIM), lambda i: (0, 0))),
        scratch_shapes=[pltpu.VMEM((8, HEAD_DIM), F32)],
        compiler_params=_params(("arbitrary",)),
        name="moe_route",
    )(x, g_pre, r_hi, r_lo)


def _slot_kernel(rank_ref, start_ref, s1_ref, s2_ref):
    tm = rank_ref.shape[1]
    sub = lax.broadcasted_iota(jnp.int32, (8, tm), 0).astype(F32)
    start = start_ref[:, 0:1]
    rows = rank_ref[...]
    for choice, out in ((0, s1_ref), (1, s2_ref)):
        base = jnp.sum(jnp.where(sub == rows[2 + choice:3 + choice, :], start, 0.0), axis=0, keepdims=True)
        out[...] = (base + rows[choice:choice + 1, :]).astype(jnp.int32)


def _slots(ranks, starts, tm):
    t = ranks.shape[1]
    row = pl.BlockSpec((1, tm), lambda i: (0, i))
    return pl.pallas_call(
        _slot_kernel,
        out_shape=(jax.ShapeDtypeStruct((1, t), jnp.int32),) * 2,
        grid=(t // tm,),
        in_specs=[pl.BlockSpec((8, tm), lambda i: (0, i)), pl.BlockSpec((8, HEAD_DIM), lambda i: (0, 0))],
        out_specs=(row, row),
        compiler_params=_params(("parallel",)),
        name="moe_slots",
    )(ranks, starts)


def _sc_mesh():
    return plsc.VectorSubcoreMesh(core_axis_name="core", subcore_axis_name="subcore")


def _sc_scatter_rows(srcs, idxs, n_rows):
    ns, nk = len(srcs), len(idxs)
    half = srcs[0].shape[0] // SC_WINDOW // 2

    @functools.partial(pl.kernel, mesh=_sc_mesh(), scratch_types=[],
                       out_type=tuple(jax.ShapeDtypeStruct((n_rows, HEAD_DIM), srcs[0].dtype) for _ in range(ns)))
    def scatter(*refs):
        out_hbm = refs[ns + nk:]

        def body(*blocks):
            for idx in blocks[ns:]:
                for c in range(ns):
                    pltpu.sync_copy(blocks[c], out_hbm[c].at[idx.at[0]])

        pltpu.emit_pipeline(
            body, grid=(2, half),
            in_specs=[pl.BlockSpec((SC_WINDOW, HEAD_DIM), lambda i, j: (i * half + j, 0)) for _ in range(ns)]
                     + [pl.BlockSpec((1, SC_WINDOW), lambda i, j: (0, i * half + j)) for _ in range(nk)],
            out_specs=[],
            core_axis_name=("core", "subcore"),
            dimension_semantics=(pltpu.PARALLEL, pltpu.PARALLEL),
        )(*refs[:ns + nk])

    return scatter(*srcs, *idxs)


def _sc_gather_rows(tables, idx):
    nt = len(tables)
    n = idx.shape[1]
    half = n // SC_WINDOW // 2

    @functools.partial(pl.kernel, mesh=_sc_mesh(), scratch_types=[],
                       out_type=tuple(jax.ShapeDtypeStruct((n, HEAD_DIM), tables[0].dtype) for _ in range(nt)))
    def gather(*refs):
        table_hbm = refs[:nt]

        def body(idx_blk, *out_blks):
            for c in range(nt):
                pltpu.sync_copy(table_hbm[c].at[idx_blk.at[0]], out_blks[c])

        pltpu.emit_pipeline(
            body, grid=(2, half),
            in_specs=[pl.BlockSpec((1, SC_WINDOW), lambda i, j: (0, i * half + j))],
            out_specs=[pl.BlockSpec((SC_WINDOW, HEAD_DIM), lambda i, j: (i * half + j, 0)) for _ in range(nt)],
            core_axis_name=("core", "subcore"),
            dimension_semantics=(pltpu.PARALLEL, pltpu.PARALLEL),
        )(refs[nt], *refs[nt + 1:])

    return gather(*tables, idx)


def _expert_kernel(tile_expert_ref, n_used_ref, *refs):
    x_refs = refs[:PIECES]
    wg_ref, wu_ref, wd_ref = refs[PIECES:PIECES + 3]
    y_refs = refs[PIECES + 3:2 * PIECES + 3]
    x_scr, acc_ref = refs[2 * PIECES + 3:]
    i = pl.program_id(0)
    c = pl.program_id(1)

    @pl.when(i < n_used_ref[0])
    def _():
        @pl.when(c == 0)
        def _():
            x_scr[...] = _unpack_rows(x_refs).astype(BF16)
            acc_ref[...] = jnp.zeros_like(acc_ref)

        h = x_scr[...]
        g = _dot(h, wg_ref[0])
        u = _dot(h, wu_ref[0])
        acc_ref[...] += _dot((g * _sigmoid(g) * u).astype(BF16), wd_ref[0])

        @pl.when(c == pl.num_programs(1) - 1)
        def _():
            for r, piece in zip(y_refs, _pack_rows(acc_ref[...])):
                r[...] = piece


def _expert_ffn(xs, tile_expert, n_used, wg, wu, wd, fc):
    n_rows = xs[0].shape[0]
    ff = wg.shape[2]
    rows = pl.BlockSpec((EXPERT_TILE, HEAD_DIM), lambda i, c, te, nu: (jnp.minimum(i, nu[0] - 1), 0))
    out_rows = pl.BlockSpec((EXPERT_TILE, HEAD_DIM), lambda i, c, te, nu: (i, 0))
    grid_spec = pltpu.PrefetchScalarGridSpec(
        num_scalar_prefetch=2,
        grid=(n_rows // EXPERT_TILE, ff // fc),
        in_specs=[rows] * PIECES + [pl.BlockSpec((1, D_MODEL, fc), lambda i, c, te, nu: (te[i], 0, c)),
                                    pl.BlockSpec((1, D_MODEL, fc), lambda i, c, te, nu: (te[i], 0, c)),
                                    pl.BlockSpec((1, fc, D_MODEL), lambda i, c, te, nu: (te[i], c, 0))],
        out_specs=[out_rows] * PIECES,
        scratch_shapes=[pltpu.VMEM((EXPERT_TILE, D_MODEL), BF16), pltpu.VMEM((EXPERT_TILE, D_MODEL), F32)],
    )
    return pl.pallas_call(
        _expert_kernel,
        out_shape=[jax.ShapeDtypeStruct((n_rows, HEAD_DIM), jnp.int32)] * PIECES,
        grid_spec=grid_spec,
        compiler_params=_params(("arbitrary", "arbitrary")),
        name="moe_expert_swiglu",
    )(tile_expert, n_used, *xs, wg, wu, wd)


def _combine_kernel(x_ref, rec_ref, gpost_ref, *refs):
    y1 = _unpack_rows(refs[:PIECES])
    y2 = _unpack_rows(refs[PIECES:2 * PIECES])
    o_ref = refs[2 * PIECES]
    rec = rec_ref[...]
    y = rec[:, 2:3] * y1 + rec[:, 3:4] * y2
    o_ref[...] = x_ref[...] + _rms(y, gpost_ref[...])


def _combine(x, rec, g_post, y1, y2, tm):
    t = x.shape[0]
    piece = pl.BlockSpec((tm, HEAD_DIM), lambda i: (i, 0))
    return pl.pallas_call(
        _combine_kernel,
        out_shape=jax.ShapeDtypeStruct((t, D_MODEL), F32),
        grid=(t // tm,),
        in_specs=[pl.BlockSpec((tm, D_MODEL), lambda i: (i, 0)), piece, pl.BlockSpec((1, D_MODEL), lambda i: (0, 0))]
                 + [piece] * (2 * PIECES),
        out_specs=pl.BlockSpec((tm, D_MODEL), lambda i: (i, 0)),
        compiler_params=_params(("parallel",)),
        name="moe_combine",
    )(x, rec, g_post, *y1, *y2)


def _moe(x, g_pre, g_post, router, wg, wu, wd, tm):
    t = x.shape[0]
    n_experts = router.shape[1]
    router = jnp.pad(router, ((0, 0), (0, HEAD_DIM - n_experts)))
    r_hi = router.astype(BF16)
    r_lo = (router - r_hi.astype(F32)).astype(BF16)
    *h_pieces, rec, ranks, counts = _route(x, g_pre, r_hi, r_lo, n_experts, tm)

    counts = counts[:n_experts, 0].astype(jnp.int32)
    tiles = (counts + EXPERT_TILE - 1) // EXPERT_TILE
    tile_end = jnp.cumsum(tiles)
    starts = ((tile_end - tiles) * EXPERT_TILE).astype(F32)
    starts = jnp.broadcast_to(jnp.pad(starts, (0, 8 - n_experts))[:, None], (8, HEAD_DIM))
    n_rows = TOP_K * t + n_experts * EXPERT_TILE
    n_tiles = n_rows // EXPERT_TILE
    n_used = tile_end[-1:]
    tile_ids = jnp.minimum(jnp.arange(n_tiles, dtype=jnp.int32), n_used[0] - 1)
    tile_expert = jnp.sum(tile_ids[:, None] >= tile_end[None, :], axis=1).astype(jnp.int32)

    slot1, slot2 = _slots(ranks, starts, min(t, 8192))
    xs = []
    for c in range(0, PIECES, 2):
        xs += _sc_scatter_rows(h_pieces[c:c + 2], [slot1, slot2], n_rows)
    ys = _expert_ffn(xs, tile_expert, n_used, wg, wu, wd, _largest_divisor(wg.shape[2], 2048, HEAD_DIM))
    gathered = []
    for slot in (slot1, slot2):
        rows = []
        for c in range(0, PIECES, 2):
            rows += _sc_gather_rows(ys[c:c + 2], slot)
        gathered.append(rows)
    return _combine(x, rec, g_post, gathered[0], gathered[1], tm)


def _ple_kernel(x_ref, p_ref, gin_ref, gpost_ref, wgate_ref, wproj_ref, o_ref):
    x = x_ref[...]
    gate = _sigmoid(_dot(_rms(x, gin_ref[...]).astype(BF16), wgate_ref[...]))
    e = _dot(p_ref[...].astype(BF16), wproj_ref[...]) * gate
    o_ref[...] = x + _rms(e, gpost_ref[...])


def _ple(x, p, g_in, g_post, w_gate, w_proj, tm):
    t = x.shape[0]
    pd = p.shape[1]
    return pl.pallas_call(
        _ple_kernel,
        out_shape=jax.ShapeDtypeStruct((t, D_MODEL), F32),
        grid=(t // tm,),
        in_specs=[pl.BlockSpec((tm, D_MODEL), lambda i: (i, 0)),
                  pl.BlockSpec((tm, pd), lambda i: (i, 0)),
                  pl.BlockSpec((1, D_MODEL), lambda i: (0, 0)),
                  pl.BlockSpec((1, D_MODEL), lambda i: (0, 0)),
                  pl.BlockSpec((D_MODEL, D_MODEL), lambda i: (0, 0)),
                  pl.BlockSpec((pd, D_MODEL), lambda i: (0, 0))],
        out_specs=pl.BlockSpec((tm, D_MODEL), lambda i: (i, 0)),
        compiler_params=_params(("parallel",)),
        name="layer_embedding",
    )(x, p, g_in, g_post, w_gate, w_proj)


def _largest_divisor(n, cap, multiple):
    best = multiple
    for cand in range(multiple, cap + 1, multiple):
        if n % cand == 0:
            best = cand
    return best


def kernel(x, p, positions, w_in, conv_w, hgrn_lb_logits, hgrn_norm_g, w_branch, w_out, g_mix_pre, g_mix_post, g_ffn_pre, g_ffn_post, dense_w_gate, dense_w_up, dense_w_down, moe_router, moe_w_gate, moe_w_up, moe_w_down, ple_w_proj, ple_w_gate, ple_g_in, ple_g_post):
    batch, seq, _ = x.shape
    depth = w_in.shape[0]
    t = batch * seq
    tm = min(512, seq)
    tm_merge = min(256, seq)
    tm_ffn = min(1024, seq)

    xt = x.reshape(t, D_MODEL)
    cos, sin = _rope_tables(positions.astype(F32).reshape(t, 1), tm)
    row = lambda a: a.reshape(1, -1)

    for i in range(depth):
        w_i = w_in[i].astype(BF16)
        g_pre = row(g_mix_pre[i])
        qkvs = _inproj_attention(xt, g_pre, w_i[:, :A_COLS], cos, sin, batch, seq, tm)
        (proj_b,) = _inproj(_inproj_plain_kernel, xt, g_pre, w_i[:, A_COLS:A_COLS + B_COLS], (B_COLS,), tm,
                            "inproj_hgrn")
        cb, gates = _inproj(_inproj_conv_gate_kernel, xt, g_pre, w_i[:, A_COLS + B_COLS:],
                            (2 * BRANCH, G_COLS), tm, "inproj_conv_gates")

        aos, lses = zip(*[_attention_group(qkvs[g], batch, seq, g) for g in range(N_GROUPS)])
        o_f, o_b = _hgrn(proj_b, hgrn_lb_logits, batch, seq, i)
        xt = _merge(xt, aos, lses, o_f, o_b, proj_b, cb, gates, conv_w[i], row(hgrn_norm_g[i]),
                    w_branch[i].astype(BF16), w_out[i].astype(BF16), row(g_mix_post[i]), seq, tm_merge)

        j = i // 2
        if i % 2 == 0:
            xt = _ffn(xt, row(g_ffn_pre[i]), row(g_ffn_post[i]), dense_w_gate[j].astype(BF16),
                      dense_w_up[j].astype(BF16), dense_w_down[j].astype(BF16), tm_ffn,
                      _largest_divisor(dense_w_gate.shape[2], 1536, HEAD_DIM))
        else:
            xt = _moe(xt, row(g_ffn_pre[i]), row(g_ffn_post[i]), moe_router[j], moe_w_gate[j].astype(BF16),
                      moe_w_up[j].astype(BF16), moe_w_down[j].astype(BF16), tm)

        xt = _ple(xt, p[i].reshape(t, -1), row(ple_g_in[i]), row(ple_g_post[i]),
                  ple_w_gate[i].astype(BF16), ple_w_proj[i].astype(BF16), tm)

    return xt.reshape(batch, seq, D_MODEL)
```

```python
import functools
import math

import jax
import jax.numpy as jnp
from jax import lax
from jax.experimental import pallas as pl
from jax.experimental.pallas import tpu as pltpu
from jax.experimental.pallas import tpu_sc as plsc

D_MODEL = 1024
EPS = 1e-6
NEG_INF = -1e30

HEAD_DIM = 128
N_HEADS = 4
BRANCH = N_HEADS * HEAD_DIM
DIL_GROUPS = ((128, 1), (512, 4), (2048, 16))
N_GROUPS = len(DIL_GROUPS)
BLK = 64
ROT_DIM = HEAD_DIM // 4
ROPE_THETA = 500000.0
TOP_K = 2

A_COLS = N_GROUPS * 3 * BRANCH
B_COLS = 5 * BRANCH
C_COLS = 3 * BRANCH
G_COLS = 3 * D_MODEL

BF16 = jnp.bfloat16
F32 = jnp.float32

VMEM_LIMIT = 56 * 1024 * 1024


def _params(sem):
    return pltpu.CompilerParams(dimension_semantics=sem, vmem_limit_bytes=VMEM_LIMIT)


def _rms(xf, g):
    return xf * lax.rsqrt(jnp.mean(xf * xf, axis=-1, keepdims=True) + EPS) * g


def _sigmoid(z):
    return 1.0 / (1.0 + jnp.exp(-z))


def _dot(a, b):
    return jnp.dot(a, b, preferred_element_type=F32)


def _dot_nt(a, b):
    return lax.dot_general(a, b, (((1,), (1,)), ((), ())), preferred_element_type=F32)


def _dot_tn(a, b):
    return lax.dot_general(a, b, (((0,), (0,)), ((), ())), preferred_element_type=F32)


def _rope_table_kernel(pos_ref, freq_ref, sign_ref, cos_ref, sin_ref):
    ang = pos_ref[...] * freq_ref[...]
    cos_ref[...] = jnp.cos(ang)
    sin_ref[...] = jnp.sin(ang) * sign_ref[...]


def _rope_tables(pos_col, tm):
    t = pos_col.shape[0]
    half = ROT_DIM // 2
    inv_freq = ROPE_THETA ** (-jnp.arange(0, ROT_DIM, 2, dtype=F32) / ROT_DIM)
    freq = jnp.concatenate([inv_freq, inv_freq, jnp.zeros((HEAD_DIM - ROT_DIM,), F32)])[None, :]
    sign = jnp.concatenate([-jnp.ones((half,), F32), jnp.ones((HEAD_DIM - half,), F32)])[None, :]
    row = pl.BlockSpec((1, HEAD_DIM), lambda i: (0, 0))
    tab = pl.BlockSpec((tm, HEAD_DIM), lambda i: (i, 0))
    return pl.pallas_call(
        _rope_table_kernel,
        out_shape=(jax.ShapeDtypeStruct((t, HEAD_DIM), F32),) * 2,
        grid=(t // tm,),
        in_specs=[pl.BlockSpec((tm, 1), lambda i: (i, 0)), row, row],
        out_specs=(tab, tab),
        compiler_params=_params(("parallel",)),
        name="rope_tables",
    )(pos_col, freq, sign)


def _rope_head(th, cos, sin):
    half = ROT_DIM // 2
    lane = lax.broadcasted_iota(jnp.int32, th.shape, 1)
    swapped = jnp.where(lane < half, pltpu.roll(th, HEAD_DIM - half, 1), pltpu.roll(th, half, 1))
    return th * cos + swapped * sin


def _inproj_attn_kernel(x_ref, g_ref, w_ref, cos_ref, sin_ref, o0_ref, o1_ref, o2_ref, stage_ref):
    h = _rms(x_ref[...], g_ref[...]).astype(BF16)
    cos = cos_ref[...]
    sin = sin_ref[...]
    scale = 1.0 / math.sqrt(HEAD_DIM)
    tm = x_ref.shape[0]
    for grp, o_ref in enumerate((o0_ref, o1_ref, o2_ref)):
        dil = DIL_GROUPS[grp][1]
        for part in range(3):
            c = grp * 3 + part
            t = _dot(h, w_ref[:, c * BRANCH:(c + 1) * BRANCH])
            if part < 2:
                heads = []
                for hd in range(N_HEADS):
                    r = _rope_head(t[:, hd * HEAD_DIM:(hd + 1) * HEAD_DIM], cos, sin)
                    heads.append(r * scale if part == 0 else r)
                t = jnp.concatenate(heads, axis=1)
            cols = slice(part * BRANCH, (part + 1) * BRANCH)
            if dil == 1:
                o_ref[0, 0, :, cols] = t.astype(BF16)
            else:
                for hd in range(N_HEADS):
                    stage_ref[hd] = t[:, hd * HEAD_DIM:(hd + 1) * HEAD_DIM]
                for r in range(dil):
                    for hd in range(N_HEADS):
                        lo = part * BRANCH + hd * HEAD_DIM
                        o_ref[0, r, :, lo:lo + HEAD_DIM] = (
                            stage_ref[hd, pl.ds(r, tm // dil, stride=dil), :].astype(BF16))


def _inproj_attention(x, g, w, cos, sin, batch, seq, tm):
    per_seq = seq // tm
    out_shape = tuple(jax.ShapeDtypeStruct((batch, dil, seq // dil, 3 * BRANCH), BF16) for _, dil in DIL_GROUPS)
    out_specs = tuple(pl.BlockSpec((1, dil, tm // dil, 3 * BRANCH), lambda b, j: (b, 0, j, 0))
                      for _, dil in DIL_GROUPS)
    rows = lambda width: pl.BlockSpec((tm, width), lambda b, j: (b * per_seq + j, 0))
    return pl.pallas_call(
        _inproj_attn_kernel, out_shape=out_shape, grid=(batch, per_seq),
        in_specs=[rows(D_MODEL), pl.BlockSpec((1, D_MODEL), lambda b, j: (0, 0)),
                  pl.BlockSpec((D_MODEL, A_COLS), lambda b, j: (0, 0)), rows(HEAD_DIM), rows(HEAD_DIM)],
        out_specs=out_specs,
        scratch_shapes=[pltpu.VMEM((N_HEADS, tm, HEAD_DIM), F32)],
        compiler_params=_params(("parallel", "parallel")), name="inproj_attention",
    )(x, g, w, cos, sin)


def _inproj_hgrn_conv_kernel(x_ref, g_ref, w_ref, b_ref, cb_ref):
    h = _rms(x_ref[...], g_ref[...]).astype(BF16)
    for c in range(B_COLS // BRANCH):
        cols = slice(c * BRANCH, (c + 1) * BRANCH)
        b_ref[:, cols] = _dot(h, w_ref[:, cols]).astype(BF16)
    u = _dot(h, w_ref[:, B_COLS:B_COLS + BRANCH])
    b_gate = _dot(h, w_ref[:, B_COLS + BRANCH:B_COLS + 2 * BRANCH])
    c_gate = _dot(h, w_ref[:, B_COLS + 2 * BRANCH:B_COLS + 3 * BRANCH])
    cb_ref[:, 0:BRANCH] = (c_gate * u).astype(BF16)
    cb_ref[:, BRANCH:2 * BRANCH] = b_gate.astype(BF16)


def _inproj_hgrn_conv(x, g, w, tm):
    t = x.shape[0]
    widths = (B_COLS, 2 * BRANCH)
    return pl.pallas_call(
        _inproj_hgrn_conv_kernel,
        out_shape=tuple(jax.ShapeDtypeStruct((t, wd), BF16) for wd in widths),
        grid=(t // tm,),
        in_specs=[pl.BlockSpec((tm, D_MODEL), lambda i: (i, 0)),
                  pl.BlockSpec((1, D_MODEL), lambda i: (0, 0)),
                  pl.BlockSpec((D_MODEL, B_COLS + C_COLS), lambda i: (0, 0))],
        out_specs=tuple(pl.BlockSpec((tm, wd), lambda i: (i, 0)) for wd in widths),
        compiler_params=_params(("parallel",)), name="inproj_hgrn_conv",
    )(x, g, w)


def _attn_kernel(q_ref, k_ref, kp_ref, kn_ref, v_ref, vp_ref, vn_ref, o_ref, lse_ref, o_stage, lse_stage,
                 *, dil, nj, n_steps):
    n = pl.program_id(1)
    col_lo = jnp.where(n > 0, 0, BLK)
    col_hi = jnp.where(n < n_steps - 1, 3 * BLK, 2 * BLK)
    row = lax.broadcasted_iota(jnp.int32, (BLK, 3 * BLK), 0)
    col = lax.broadcasted_iota(jnp.int32, (BLK, 3 * BLK), 1)
    band = jnp.abs(col - BLK - row) <= BLK
    masks = {}
    for j in range(nj):
        m = band
        if j == 0:
            m = m & (col >= col_lo)
        if j == nj - 1:
            m = m & (col < col_hi)
        masks[j] = m
    pairs = [(r, j) for r in range(dil) for j in range(nj)]
    lane = lax.broadcasted_iota(jnp.int32, (len(pairs), BLK, HEAD_DIM), 2)

    def window(own_ref, prev_ref, next_ref, r, j, hs):
        parts = []
        for w in (j - 1, j, j + 1):
            if w < 0:
                parts.append(prev_ref[0, r, :, hs])
            elif w >= nj:
                parts.append(next_ref[0, r, :, hs])
            else:
                parts.append(own_ref[0, r, w * BLK:(w + 1) * BLK, hs])
        return jnp.concatenate(parts, axis=0)

    lse_tiles = jnp.zeros((len(pairs), BLK, HEAD_DIM), F32)
    for hd in range(N_HEADS):
        hs = slice(hd * HEAD_DIM, (hd + 1) * HEAD_DIM)
        scores = [_dot_nt(q_ref[0, r, j * BLK:(j + 1) * BLK, hs], window(k_ref, kp_ref, kn_ref, r, j, hs))
                  for r, j in pairs]
        s = jnp.stack([jnp.where(masks[j], sc, NEG_INF) for (r, j), sc in zip(pairs, scores)])
        m = jnp.max(s, axis=2, keepdims=True)
        p = jnp.exp(s - m)
        l = jnp.sum(p, axis=2, keepdims=True)
        pb = p.astype(BF16)
        outs = jnp.stack([_dot(pb[i], window(v_ref, vp_ref, vn_ref, r, j, hs)) for i, (r, j) in enumerate(pairs)])
        outs = outs * (1.0 / l)
        lse_tiles = jnp.where(lane == hd, m + jnp.log(l), lse_tiles)
        for i, (r, j) in enumerate(pairs):
            o_stage[hd, pl.ds(j * BLK * dil + r, BLK, stride=dil), :] = outs[i]
        o_ref[:, hs] = o_stage[hd].astype(o_ref.dtype)
    for i, (r, j) in enumerate(pairs):
        lse_stage[pl.ds(j * BLK * dil + r, BLK, stride=dil), :] = lse_tiles[i]
    lse_ref[...] = lse_stage[...]


def _attention_group(qkv, batch, seq, group):
    _, dil = DIL_GROUPS[group]
    t = batch * seq
    sub_len = seq // dil
    rows_per_step = min(seq, 16 * BLK)
    nj = rows_per_step // (dil * BLK)
    n_steps = seq // rows_per_step
    n_blk = sub_len // BLK
    sub_rows = rows_per_step // dil

    def own(part):
        return pl.BlockSpec((1, dil, sub_rows, BRANCH), lambda b, n: (b, 0, n, part))

    def prev(part):
        return pl.BlockSpec((1, dil, BLK, BRANCH), lambda b, n: (b, 0, jnp.maximum(n * nj - 1, 0), part))

    def nxt(part):
        return pl.BlockSpec((1, dil, BLK, BRANCH), lambda b, n: (b, 0, jnp.minimum((n + 1) * nj, n_blk - 1), part))

    return pl.pallas_call(
        functools.partial(_attn_kernel, dil=dil, nj=nj, n_steps=n_steps),
        out_shape=(jax.ShapeDtypeStruct((t, BRANCH), BF16), jax.ShapeDtypeStruct((t, HEAD_DIM), F32)),
        grid=(batch, n_steps),
        in_specs=[own(0), own(1), prev(1), nxt(1), own(2), prev(2), nxt(2)],
        out_specs=(pl.BlockSpec((rows_per_step, BRANCH), lambda b, n: (b * n_steps + n, 0)),
                   pl.BlockSpec((rows_per_step, HEAD_DIM), lambda b, n: (b * n_steps + n, 0))),
        scratch_shapes=[pltpu.VMEM((N_HEADS, rows_per_step, HEAD_DIM), F32), pltpu.VMEM((rows_per_step, HEAD_DIM), F32)],
        compiler_params=_params(("parallel", "parallel")),
        name=f"dilated_attention_g{group}",
    )(qkv, qkv, qkv, qkv, qkv, qkv, qkv)


def _hgrn_kernel(logit_ref, qf_ref, zf_ref, vf_ref, qb_ref, zb_ref, vb_ref, of_ref, ob_ref, state_ref,
                 *, layer, n_chunks):
    c = pl.program_id(1)

    @pl.when(c == 0)
    def _():
        state_ref[...] = jnp.zeros_like(state_ref)

    depth = logit_ref.shape[1]
    lbs = []
    for dirn in range(2):
        rows = [logit_ref[dirn, j:j + 1, :] for j in range(depth)]
        top = functools.reduce(jnp.maximum, rows)
        exps = [jnp.exp(rw - top) for rw in rows]
        lbs.append(sum(exps[1:layer + 1], jnp.zeros_like(top)) / sum(exps[1:], exps[0]))

    r = lax.broadcasted_iota(jnp.int32, (BLK, BLK), 0)
    cc = lax.broadcasted_iota(jnp.int32, (BLK, BLK), 1)

    def bcast_rows(per_chunk):
        return jnp.concatenate([jnp.broadcast_to(v, (BLK, BRANCH)) for v in per_chunk], axis=0)

    for dirn, (q_ref, z_ref, v_ref, o_ref) in enumerate(
            ((qf_ref, zf_ref, vf_ref, of_ref), (qb_ref, zb_ref, vb_ref, ob_ref))):
        tri = (cc <= r) if dirn == 0 else (cc >= r)
        tri_b = jnp.where(tri, 1.0, 0.0).astype(BF16)
        last, mid_row = (BLK - 1, BLK // 2 - 1) if dirn == 0 else (0, BLK // 2)
        lb = lbs[dirn]
        q = q_ref[...].astype(F32)
        z = z_ref[...].astype(F32)
        v = v_ref[...]
        sg = _sigmoid(z)
        logf = jnp.log(lb + (1.0 - lb) * sg)
        kk = (1.0 - lb) * (1.0 - sg)
        hi = logf.astype(BF16)
        lo = (logf - hi.astype(F32)).astype(BF16)
        chunks = [slice(g * BLK, (g + 1) * BLK) for g in range(n_chunks)]
        cums = [_dot(tri_b, hi[ch]) + _dot(tri_b, lo[ch]) for ch in chunks]
        totals = [cm[last:last + 1, :] for cm in cums]
        mids = [cm[mid_row:mid_row + 1, :] for cm in cums]
        cum = jnp.concatenate(cums, axis=0)
        mid = bcast_rows(mids)
        qa = q * jnp.exp(cum - mid)
        ka = kk * jnp.exp(mid - cum)
        q_dec = (qa * bcast_rows([jnp.exp(md) for md in mids])).astype(BF16)
        k_dec = (ka * bcast_rows([jnp.exp(tt - md) for tt, md in zip(totals, mids)])).astype(BF16)
        qa = qa.astype(BF16)
        ka = ka.astype(BF16)
        tiles = [(g, hd) for g in range(n_chunks) for hd in range(N_HEADS)]
        sl = lambda g, hd: (slice(g * BLK, (g + 1) * BLK), slice(hd * HEAD_DIM, (hd + 1) * HEAD_DIM))
        atts = [jnp.where(tri, _dot_nt(qa[sl(g, hd)], ka[sl(g, hd)]), 0.0).astype(BF16) for g, hd in tiles]
        intra = {t: _dot(att, v[sl(*t)]) for t, att in zip(tiles, atts)}
        kvs = {t: _dot_tn(v[sl(*t)], k_dec[sl(*t)]) for t in tiles}
        decays = [jnp.exp(tt) for tt in totals]
        order = range(n_chunks) if dirn == 0 else range(n_chunks - 1, -1, -1)
        for hd in range(N_HEADS):
            hs = slice(hd * HEAD_DIM, (hd + 1) * HEAD_DIM)
            state_t = state_ref[dirn, hd]
            for g in order:
                o_ref[g * BLK:(g + 1) * BLK, hs] = intra[(g, hd)] + _dot_nt(q_dec[sl(g, hd)], state_t.astype(BF16))
                state_t = state_t * decays[g][:, hs] + kvs[(g, hd)]
            state_ref[dirn, hd] = state_t


def _hgrn(proj_b, logits, batch, seq, layer):
    t = proj_b.shape[0]
    n_chunks = 4
    rows = n_chunks * BLK
    nc = seq // rows
    depth = logits.shape[1]

    def fwd(colblk):
        return pl.BlockSpec((rows, BRANCH), lambda b, c: (b * nc + c, colblk))

    def bwd(colblk):
        return pl.BlockSpec((rows, BRANCH), lambda b, c: (b * nc + nc - 1 - c, colblk))

    return pl.pallas_call(
        functools.partial(_hgrn_kernel, layer=layer, n_chunks=n_chunks),
        out_shape=(jax.ShapeDtypeStruct((t, BRANCH), F32),) * 2,
        grid=(batch, nc),
        in_specs=[pl.BlockSpec((2, depth, BRANCH), lambda b, c: (0, 0, 0)),
                  fwd(0), fwd(1), fwd(3), bwd(0), bwd(2), bwd(3)],
        out_specs=(pl.BlockSpec((rows, BRANCH), lambda b, c: (b * nc + c, 0)),
                   pl.BlockSpec((rows, BRANCH), lambda b, c: (b * nc + nc - 1 - c, 0))),
        scratch_shapes=[pltpu.VMEM((2, N_HEADS, HEAD_DIM, HEAD_DIM), F32)],
        compiler_params=_params(("parallel", "arbitrary")),
        name="hgrn2_scan",
    )(logits, proj_b, proj_b, proj_b, proj_b, proj_b, proj_b)


def _merge_kernel(x_ref, ao0_ref, ao1_ref, ao2_ref, l0_ref, l1_ref, l2_ref, of_ref, ob_ref, bg_ref,
                  cb_ref, cbp_ref, cbn_ref, gpre_ref, wgate_ref, convw_ref, ng_ref, wbr_ref, wout_ref, gpost_ref,
                  o_ref, *, tm, seq):
    i = pl.program_id(0)
    h = _rms(x_ref[...], gpre_ref[...]).astype(BF16)

    ls = (l0_ref[...], l1_ref[...], l2_ref[...])
    aos = (ao0_ref, ao1_ref, ao2_ref)
    lmax = jnp.maximum(jnp.maximum(ls[0], ls[1]), ls[2])
    es = [jnp.exp(l - lmax) for l in ls]
    inv = 1.0 / (es[0] + es[1] + es[2])
    a_parts = []
    for hd in range(N_HEADS):
        hs = slice(hd * HEAD_DIM, (hd + 1) * HEAD_DIM)
        acc = None
        for g in range(N_GROUPS):
            w = (es[g] * inv)[:, hd:hd + 1]
            term = w * aos[g][:, hs].astype(F32)
            acc = term if acc is None else acc + term
        a_parts.append(acc)
    o_a = jnp.concatenate(a_parts, axis=1).astype(BF16)

    b_parts = []
    for hd in range(N_HEADS):
        hs = slice(hd * HEAD_DIM, (hd + 1) * HEAD_DIM)
        o = of_ref[:, hs] + ob_ref[:, hs]
        b_parts.append(_rms(o, ng_ref[:, hs]))
    gate = bg_ref[...].astype(F32)
    o_b = (jnp.concatenate(b_parts, axis=1) * (gate * _sigmoid(gate))).astype(BF16)

    cu = cb_ref[:, 0:BRANCH].astype(F32)
    keep_prev = jnp.where((i * tm) % seq == 0, 0.0, 1.0)
    keep_next = jnp.where(((i + 1) * tm) % seq == 0, 0.0, 1.0)
    halo_rows = cbp_ref.shape[0]
    prev_row = keep_prev * cbp_ref[halo_rows - 1:halo_rows, 0:BRANCH].astype(F32)
    next_row = keep_next * cbn_ref[0:1, 0:BRANCH].astype(F32)
    rid = lax.broadcasted_iota(jnp.int32, cu.shape, 0)
    before = jnp.where(rid == 0, prev_row, pltpu.roll(cu, 1, 0))
    after = jnp.where(rid == tm - 1, next_row, pltpu.roll(cu, tm - 1, 0))
    conv = convw_ref[0:1, :] * before + convw_ref[1:2, :] * cu + convw_ref[2:3, :] * after
    o_c = (cb_ref[:, BRANCH:2 * BRANCH].astype(F32) * conv).astype(BF16)

    merged = None
    for n, o_n in enumerate((o_a, o_b, o_c)):
        up = _dot(o_n, wbr_ref[n])
        term = _sigmoid(_dot(h, wgate_ref[:, n * D_MODEL:(n + 1) * D_MODEL])) * up
        merged = term if merged is None else merged + term
    y = _dot(merged.astype(BF16), wout_ref[...])
    o_ref[...] = x_ref[...] + _rms(y, gpost_ref[...])


def _merge(x, aos, lses, o_f, o_b, proj_b, cb, g_pre, w_gate, conv_w, norm_g, w_branch, w_out, g_post, seq, tm):
    t = x.shape[0]
    halo = 16
    n_halo = t // halo
    per = tm // halo

    def rows(width, colblk=0):
        return pl.BlockSpec((tm, width), lambda i: (i, colblk))

    def full(shape):
        return pl.BlockSpec(shape, lambda i: (0,) * len(shape))

    in_specs = ([rows(D_MODEL)] + [rows(BRANCH)] * 3 + [rows(HEAD_DIM)] * 3 + [rows(BRANCH)] * 2
                + [rows(BRANCH, 4), rows(2 * BRANCH),
                   pl.BlockSpec((halo, 2 * BRANCH), lambda i: (jnp.maximum(i * per - 1, 0), 0)),
                   pl.BlockSpec((halo, 2 * BRANCH), lambda i: (jnp.minimum((i + 1) * per, n_halo - 1), 0)),
                   full((1, D_MODEL)), full((D_MODEL, G_COLS)), full((3, BRANCH)), full((1, BRANCH)),
                   full((3, BRANCH, D_MODEL)), full((D_MODEL, D_MODEL)), full((1, D_MODEL))])
    return pl.pallas_call(
        functools.partial(_merge_kernel, tm=tm, seq=seq),
        out_shape=jax.ShapeDtypeStruct((t, D_MODEL), F32),
        grid=(t // tm,),
        in_specs=in_specs,
        out_specs=rows(D_MODEL),
        compiler_params=_params(("parallel",)),
        name="merge_branches",
    )(x, *aos, *lses, o_f, o_b, proj_b, cb, cb, cb, g_pre, w_gate, conv_w, norm_g, w_branch, w_out, g_post)


def _embed_tail(x, p_ref, gin_ref, gpost_ref, wgate_ref, wproj_ref):
    gate = _sigmoid(_dot(_rms(x, gin_ref[...]).astype(BF16), wgate_ref[...]))
    e = _dot(p_ref[...].astype(BF16), wproj_ref[...]) * gate
    return x + _rms(e, gpost_ref[...])


def _embed_operands(embed, tm, index):
    pd = embed[0].shape[1]
    const = lambda *_: (0, 0)
    return [pl.BlockSpec((tm, pd), index), pl.BlockSpec((1, D_MODEL), const), pl.BlockSpec((1, D_MODEL), const),
            pl.BlockSpec((D_MODEL, D_MODEL), const), pl.BlockSpec((pd, D_MODEL), const)]


def _ffn_kernel(x_ref, gpre_ref, gpost_ref, wg_ref, wu_ref, wd_ref, p_ref, ein_ref, epost_ref, egate_ref, eproj_ref,
                o_ref, h_ref, acc_ref):
    c = pl.program_id(1)

    @pl.when(c == 0)
    def _():
        h_ref[...] = _rms(x_ref[...], gpre_ref[...]).astype(BF16)
        acc_ref[...] = jnp.zeros_like(acc_ref)

    h = h_ref[...]
    g = _dot(h, wg_ref[...])
    u = _dot(h, wu_ref[...])
    acc_ref[...] += _dot((g * _sigmoid(g) * u).astype(BF16), wd_ref[...])

    @pl.when(c == pl.num_programs(1) - 1)
    def _():
        mixed = x_ref[...] + _rms(acc_ref[...], gpost_ref[...])
        o_ref[...] = _embed_tail(mixed, p_ref, ein_ref, epost_ref, egate_ref, eproj_ref)


def _ffn(x, g_pre, g_post, wg, wu, wd, embed, tm, fc):
    t = x.shape[0]
    ff = wg.shape[1]
    return pl.pallas_call(
        _ffn_kernel,
        out_shape=jax.ShapeDtypeStruct((t, D_MODEL), F32),
        grid=(t // tm, ff // fc),
        in_specs=[pl.BlockSpec((tm, D_MODEL), lambda i, c: (i, 0)),
                  pl.BlockSpec((1, D_MODEL), lambda i, c: (0, 0)),
                  pl.BlockSpec((1, D_MODEL), lambda i, c: (0, 0)),
                  pl.BlockSpec((D_MODEL, fc), lambda i, c: (0, c)),
                  pl.BlockSpec((D_MODEL, fc), lambda i, c: (0, c)),
                  pl.BlockSpec((fc, D_MODEL), lambda i, c: (c, 0))]
                 + _embed_operands(embed, tm, lambda i, c: (i, 0)),
        out_specs=pl.BlockSpec((tm, D_MODEL), lambda i, c: (i, 0)),
        scratch_shapes=[pltpu.VMEM((tm, D_MODEL), BF16), pltpu.VMEM((tm, D_MODEL), F32)],
        compiler_params=_params(("parallel", "arbitrary")),
        name="dense_swiglu",
    )(x, g_pre, g_post, wg, wu, wd, *embed)


PIECES = D_MODEL // 2 // HEAD_DIM
SC_WINDOW = 128
EXPERT_TILE = 512


def _pack_rows(vals):
    bits = lax.bitcast_convert_type(vals.astype(BF16).astype(F32), jnp.uint32)
    half = D_MODEL // 2
    word = bits[:, :half] | (bits[:, half:] >> 16)
    return [lax.bitcast_convert_type(word[:, c * HEAD_DIM:(c + 1) * HEAD_DIM], jnp.int32) for c in range(PIECES)]


def _unpack_rows(piece_refs):
    words = [lax.bitcast_convert_type(r[...], jnp.uint32) for r in piece_refs]
    hi = [lax.bitcast_convert_type(w & jnp.uint32(0xFFFF0000), F32) for w in words]
    lo = [lax.bitcast_convert_type(w << 16, F32) for w in words]
    return jnp.concatenate(hi + lo, axis=1)


def _route_kernel(x_ref, gpre_ref, rhi_ref, rlo_ref, *refs, n_experts):
    piece_refs = refs[:PIECES]
    rec_ref, rank_ref, count_ref, carry_ref = refs[PIECES:]
    i = pl.program_id(0)
    tm = x_ref.shape[0]

    @pl.when(i == 0)
    def _():
        carry_ref[...] = jnp.zeros_like(carry_ref)

    hf = _rms(x_ref[...], gpre_ref[...])
    for r, piece in zip(piece_refs, _pack_rows(hf)):
        r[...] = piece
    h_hi = hf.astype(BF16)
    h_lo = (hf - h_hi.astype(F32)).astype(BF16)
    logits = _dot(h_hi, rhi_ref[...]) + _dot(h_hi, rlo_ref[...]) + _dot(h_lo, rhi_ref[...])
    lane = lax.broadcasted_iota(jnp.int32, logits.shape, 1).astype(F32)
    logits = jnp.where(lane < n_experts, logits, NEG_INF)
    m1 = jnp.max(logits, axis=1, keepdims=True)
    i1 = jnp.min(jnp.where(logits == m1, lane, 1e9), axis=1, keepdims=True)
    rest = jnp.where(lane == i1, NEG_INF, logits)
    m2 = jnp.max(rest, axis=1, keepdims=True)
    i2 = jnp.min(jnp.where(rest == m2, lane, 1e9), axis=1, keepdims=True)
    ex = jnp.exp(m2 - m1)
    w1 = 1.0 / (1.0 + ex)
    rec = jnp.where(lane == 0, i1, jnp.where(lane == 1, i2, jnp.where(lane == 2, w1, jnp.where(lane == 3, ex * w1, 0.0))))
    rec_ref[...] = rec
    rec_t = rec.T
    e1 = rec_t[0:1, :]
    e2 = rec_t[1:2, :]
    sub = lax.broadcasted_iota(jnp.int32, (8, tm), 0).astype(F32)
    oh1 = jnp.where(sub == e1, 1.0, 0.0)
    oh2 = jnp.where(sub == e2, 1.0, 0.0)
    chosen = oh1 + oh2
    src = lax.broadcasted_iota(jnp.int32, (tm, tm), 0)
    dst = lax.broadcasted_iota(jnp.int32, (tm, tm), 1)
    before = jnp.where(src < dst, 1.0, 0.0).astype(BF16)
    rank = _dot(chosen.astype(BF16), before) + carry_ref[:, 0:1]
    rank1 = jnp.sum(oh1 * rank, axis=0, keepdims=True)
    rank2 = jnp.sum(oh2 * rank, axis=0, keepdims=True)
    rank_ref[...] = jnp.where(sub == 0, rank1, jnp.where(sub == 1, rank2, jnp.where(sub == 2, e1, jnp.where(sub == 3, e2, 0.0))))
    carry_ref[...] = carry_ref[...] + jnp.sum(chosen, axis=1, keepdims=True)
    count_ref[...] = carry_ref[...]


def _route(x, g_pre, r_hi, r_lo, n_experts, tm):
    t = x.shape[0]
    assert n_experts <= 8
    piece = pl.BlockSpec((tm, HEAD_DIM), lambda i: (i, 0))
    return pl.pallas_call(
        functools.partial(_route_kernel, n_experts=n_experts),
        out_shape=(*[jax.ShapeDtypeStruct((t, HEAD_DIM), jnp.int32)] * PIECES,
                   jax.ShapeDtypeStruct((t, HEAD_DIM), F32), jax.ShapeDtypeStruct((8, t), F32),
                   jax.ShapeDtypeStruct((8, HEAD_DIM), F32)),
        grid=(t // tm,),
        in_specs=[pl.BlockSpec((tm, D_MODEL), lambda i: (i, 0)), pl.BlockSpec((1, D_MODEL), lambda i: (0, 0)),
                  pl.BlockSpec((D_MODEL, HEAD_DIM), lambda i: (0, 0)), pl.BlockSpec((D_MODEL, HEAD_DIM), lambda i: (0, 0))],
        out_specs=(*[piece] * PIECES, piece, pl.BlockSpec((8, tm), lambda i: (0, i)),
                   pl.BlockSpec((8, HEAD_DIM), lambda i: (0, 0))),
        scratch_shapes=[pltpu.VMEM((8, HEAD_DIM), F32)],
        compiler_params=_params(("arbitrary",)),
        name="moe_route",
    )(x, g_pre, r_hi, r_lo)


def _slot_kernel(rank_ref, start_ref, s1_ref, s2_ref):
    tm = rank_ref.shape[1]
    sub = lax.broadcasted_iota(jnp.int32, (8, tm), 0).astype(F32)
    start = start_ref[:, 0:1]
    rows = rank_ref[...]
    for choice, out in ((0, s1_ref), (1, s2_ref)):
        base = jnp.sum(jnp.where(sub == rows[2 + choice:3 + choice, :], start, 0.0), axis=0, keepdims=True)
        out[...] = (base + rows[choice:choice + 1, :]).astype(jnp.int32)


def _slots(ranks, starts, tm):
    t = ranks.shape[1]
    row = pl.BlockSpec((1, tm), lambda i: (0, i))
    return pl.pallas_call(
        _slot_kernel,
        out_shape=(jax.ShapeDtypeStruct((1, t), jnp.int32),) * 2,
        grid=(t // tm,),
        in_specs=[pl.BlockSpec((8, tm), lambda i: (0, i)), pl.BlockSpec((8, HEAD_DIM), lambda i: (0, 0))],
        out_specs=(row, row),
        compiler_params=_params(("parallel",)),
        name="moe_slots",
    )(ranks, starts)


def _sc_mesh():
    return plsc.VectorSubcoreMesh(core_axis_name="core", subcore_axis_name="subcore")


def _sc_scatter_rows(srcs, idxs, n_rows):
    ns, nk = len(srcs), len(idxs)
    half = srcs[0].shape[0] // SC_WINDOW // 2

    @functools.partial(pl.kernel, mesh=_sc_mesh(), scratch_types=[],
                       out_type=tuple(jax.ShapeDtypeStruct((n_rows, HEAD_DIM), srcs[0].dtype) for _ in range(ns)))
    def scatter(*refs):
        out_hbm = refs[ns + nk:]

        def body(*blocks):
            for idx in blocks[ns:]:
                for c in range(ns):
                    pltpu.sync_copy(blocks[c], out_hbm[c].at[idx.at[0]])

        pltpu.emit_pipeline(
            body, grid=(2, half),
            in_specs=[pl.BlockSpec((SC_WINDOW, HEAD_DIM), lambda i, j: (i * half + j, 0)) for _ in range(ns)]
                     + [pl.BlockSpec((1, SC_WINDOW), lambda i, j: (0, i * half + j)) for _ in range(nk)],
            out_specs=[],
            core_axis_name=("core", "subcore"),
            dimension_semantics=(pltpu.PARALLEL, pltpu.PARALLEL),
        )(*refs[:ns + nk])

    return scatter(*srcs, *idxs)


def _sc_gather_rows(tables, idx):
    nt = len(tables)
    n = idx.shape[1]
    half = n // SC_WINDOW // 2

    @functools.partial(pl.kernel, mesh=_sc_mesh(), scratch_types=[],
                       out_type=tuple(jax.ShapeDtypeStruct((n, HEAD_DIM), tables[0].dtype) for _ in range(nt)))
    def gather(*refs):
        table_hbm = refs[:nt]

        def body(idx_blk, *out_blks):
            for c in range(nt):
                pltpu.sync_copy(table_hbm[c].at[idx_blk.at[0]], out_blks[c])

        pltpu.emit_pipeline(
            body, grid=(2, half),
            in_specs=[pl.BlockSpec((1, SC_WINDOW), lambda i, j: (0, i * half + j))],
            out_specs=[pl.BlockSpec((SC_WINDOW, HEAD_DIM), lambda i, j: (i * half + j, 0)) for _ in range(nt)],
            core_axis_name=("core", "subcore"),
            dimension_semantics=(pltpu.PARALLEL, pltpu.PARALLEL),
        )(refs[nt], *refs[nt + 1:])

    return gather(*tables, idx)


def _expert_kernel(tile_expert_ref, n_used_ref, *refs):
    x_refs = refs[:PIECES]
    wg_ref, wu_ref, wd_ref = refs[PIECES:PIECES + 3]
    y_refs = refs[PIECES + 3:2 * PIECES + 3]
    x_scr, acc_ref = refs[2 * PIECES + 3:]
    i = pl.program_id(0)
    c = pl.program_id(1)

    @pl.when(i < n_used_ref[0])
    def _():
        @pl.when(c == 0)
        def _():
            x_scr[...] = _unpack_rows(x_refs).astype(BF16)
            acc_ref[...] = jnp.zeros_like(acc_ref)

        h = x_scr[...]
        g = _dot(h, wg_ref[0])
        u = _dot(h, wu_ref[0])
        acc_ref[...] += _dot((g * _sigmoid(g) * u).astype(BF16), wd_ref[0])

        @pl.when(c == pl.num_programs(1) - 1)
        def _():
            for r, piece in zip(y_refs, _pack_rows(acc_ref[...])):
                r[...] = piece


def _expert_ffn(xs, tile_expert, n_used, wg, wu, wd, fc):
    n_rows = xs[0].shape[0]
    ff = wg.shape[2]
    rows = pl.BlockSpec((EXPERT_TILE, HEAD_DIM), lambda i, c, te, nu: (jnp.minimum(i, nu[0] - 1), 0))
    out_rows = pl.BlockSpec((EXPERT_TILE, HEAD_DIM), lambda i, c, te, nu: (i, 0))
    grid_spec = pltpu.PrefetchScalarGridSpec(
        num_scalar_prefetch=2,
        grid=(n_rows // EXPERT_TILE, ff // fc),
        in_specs=[rows] * PIECES + [pl.BlockSpec((1, D_MODEL, fc), lambda i, c, te, nu: (te[i], 0, c)),
                                    pl.BlockSpec((1, D_MODEL, fc), lambda i, c, te, nu: (te[i], 0, c)),
                                    pl.BlockSpec((1, fc, D_MODEL), lambda i, c, te, nu: (te[i], c, 0))],
        out_specs=[out_rows] * PIECES,
        scratch_shapes=[pltpu.VMEM((EXPERT_TILE, D_MODEL), BF16), pltpu.VMEM((EXPERT_TILE, D_MODEL), F32)],
    )
    return pl.pallas_call(
        _expert_kernel,
        out_shape=[jax.ShapeDtypeStruct((n_rows, HEAD_DIM), jnp.int32)] * PIECES,
        grid_spec=grid_spec,
        compiler_params=_params(("arbitrary", "arbitrary")),
        name="moe_expert_swiglu",
    )(tile_expert, n_used, *xs, wg, wu, wd)


def _combine_kernel(x_ref, rec_ref, gpost_ref, p_ref, ein_ref, epost_ref, egate_ref, eproj_ref, *refs):
    y1 = _unpack_rows(refs[:PIECES])
    y2 = _unpack_rows(refs[PIECES:2 * PIECES])
    o_ref = refs[2 * PIECES]
    rec = rec_ref[...]
    y = rec[:, 2:3] * y1 + rec[:, 3:4] * y2
    mixed = x_ref[...] + _rms(y, gpost_ref[...])
    o_ref[...] = _embed_tail(mixed, p_ref, ein_ref, epost_ref, egate_ref, eproj_ref)


def _combine(x, rec, g_post, embed, y1, y2, tm):
    t = x.shape[0]
    piece = pl.BlockSpec((tm, HEAD_DIM), lambda i: (i, 0))
    return pl.pallas_call(
        _combine_kernel,
        out_shape=jax.ShapeDtypeStruct((t, D_MODEL), F32),
        grid=(t // tm,),
        in_specs=[pl.BlockSpec((tm, D_MODEL), lambda i: (i, 0)), piece, pl.BlockSpec((1, D_MODEL), lambda i: (0, 0))]
                 + _embed_operands(embed, tm, lambda i: (i, 0)) + [piece] * (2 * PIECES),
        out_specs=pl.BlockSpec((tm, D_MODEL), lambda i: (i, 0)),
        compiler_params=_params(("parallel",)),
        name="moe_combine",
    )(x, rec, g_post, *embed, *y1, *y2)


def _moe(x, g_pre, g_post, router, wg, wu, wd, embed, tm):
    t = x.shape[0]
    n_experts = router.shape[1]
    router = jnp.pad(router, ((0, 0), (0, HEAD_DIM - n_experts)))
    r_hi = router.astype(BF16)
    r_lo = (router - r_hi.astype(F32)).astype(BF16)
    *h_pieces, rec, ranks, counts = _route(x, g_pre, r_hi, r_lo, n_experts, tm)

    counts = counts[:n_experts, 0].astype(jnp.int32)
    tiles = (counts + EXPERT_TILE - 1) // EXPERT_TILE
    tile_end = jnp.cumsum(tiles)
    starts = ((tile_end - tiles) * EXPERT_TILE).astype(F32)
    starts = jnp.broadcast_to(jnp.pad(starts, (0, 8 - n_experts))[:, None], (8, HEAD_DIM))
    n_rows = TOP_K * t + n_experts * EXPERT_TILE
    n_tiles = n_rows // EXPERT_TILE
    n_used = tile_end[-1:]
    tile_ids = jnp.minimum(jnp.arange(n_tiles, dtype=jnp.int32), n_used[0] - 1)
    tile_expert = jnp.sum(tile_ids[:, None] >= tile_end[None, :], axis=1).astype(jnp.int32)

    slot1, slot2 = _slots(ranks, starts, min(t, 8192))
    xs = []
    for c in range(0, PIECES, 2):
        xs += _sc_scatter_rows(h_pieces[c:c + 2], [slot1, slot2], n_rows)
    ys = _expert_ffn(xs, tile_expert, n_used, wg, wu, wd, _largest_divisor(wg.shape[2], 2048, HEAD_DIM))
    gathered = []
    for slot in (slot1, slot2):
        rows = []
        for c in range(0, PIECES, 2):
            rows += _sc_gather_rows(ys[c:c + 2], slot)
        gathered.append(rows)
    return _combine(x, rec, g_post, embed, gathered[0], gathered[1], tm)


def _largest_divisor(n, cap, multiple):
    best = multiple
    for cand in range(multiple, cap + 1, multiple):
        if n % cand == 0:
            best = cand
    return best


def kernel(x, p, positions, w_in, conv_w, hgrn_lb_logits, hgrn_norm_g, w_branch, w_out, g_mix_pre, g_mix_post, g_ffn_pre, g_ffn_post, dense_w_gate, dense_w_up, dense_w_down, moe_router, moe_w_gate, moe_w_up, moe_w_down, ple_w_proj, ple_w_gate, ple_g_in, ple_g_post):
    batch, seq, _ = x.shape
    depth = w_in.shape[0]
    t = batch * seq
    tm = min(512, seq)
    tm_merge = min(256, seq)
    tm_ffn = min(512, seq)

    xt = x.reshape(t, D_MODEL)
    cos, sin = _rope_tables(positions.astype(F32).reshape(t, 1), tm)
    row = lambda a: a.reshape(1, -1)

    for i in range(depth):
        w_i = w_in[i].astype(BF16)
        g_pre = row(g_mix_pre[i])
        qkvs = _inproj_attention(xt, g_pre, w_i[:, :A_COLS], cos, sin, batch, seq, tm)
        proj_b, cb = _inproj_hgrn_conv(xt, g_pre, w_i[:, A_COLS:A_COLS + B_COLS + C_COLS], tm)

        aos, lses = zip(*[_attention_group(qkvs[g], batch, seq, g) for g in range(N_GROUPS)])
        o_f, o_b = _hgrn(proj_b, hgrn_lb_logits, batch, seq, i)
        xt = _merge(xt, aos, lses, o_f, o_b, proj_b, cb, g_pre, w_i[:, A_COLS + B_COLS + C_COLS:], conv_w[i],
                    row(hgrn_norm_g[i]), w_branch[i].astype(BF16), w_out[i].astype(BF16), row(g_mix_post[i]),
                    seq, tm_merge)

        embed = (p[i].reshape(t, -1), row(ple_g_in[i]), row(ple_g_post[i]),
                 ple_w_gate[i].astype(BF16), ple_w_proj[i].astype(BF16))
        j = i // 2
        if i % 2 == 0:
            xt = _ffn(xt, row(g_ffn_pre[i]), row(g_ffn_post[i]), dense_w_gate[j].astype(BF16),
                      dense_w_up[j].astype(BF16), dense_w_down[j].astype(BF16), embed, tm_ffn,
                      _largest_divisor(dense_w_gate.shape[2], 1536, HEAD_DIM))
        else:
            xt = _moe(xt, row(g_ffn_pre[i]), row(g_ffn_post[i]), moe_router[j], moe_w_gate[j].astype(BF16),
                      moe_w_up[j].astype(BF16), moe_w_down[j].astype(BF16), embed, tm)

    return xt.reshape(batch, seq, D_MODEL)
```

```python
import functools
import math

import jax
import jax.numpy as jnp
from jax import lax
from jax.experimental import pallas as pl
from jax.experimental.pallas import tpu as pltpu
from jax.experimental.pallas import tpu_sc as plsc

D_MODEL = 1024
EPS = 1e-6
NEG_INF = -1e30

HEAD_DIM = 128
N_HEADS = 4
BRANCH = N_HEADS * HEAD_DIM
DIL_GROUPS = ((128, 1), (512, 4), (2048, 16))
N_GROUPS = len(DIL_GROUPS)
BLK = 64
ROT_DIM = HEAD_DIM // 4
ROPE_THETA = 500000.0
TOP_K = 2

A_COLS = N_GROUPS * 3 * BRANCH
B_COLS = 5 * BRANCH
C_COLS = 3 * BRANCH
G_COLS = 3 * D_MODEL

BF16 = jnp.bfloat16
F32 = jnp.float32

VMEM_LIMIT = 56 * 1024 * 1024


def _params(sem):
    return pltpu.CompilerParams(dimension_semantics=sem, vmem_limit_bytes=VMEM_LIMIT)


def _resident(shape, index_map):
    return pl.BlockSpec(shape, index_map, pipeline_mode=pl.Buffered(1))


def _rms(xf, g):
    return xf * lax.rsqrt(jnp.mean(xf * xf, axis=-1, keepdims=True) + EPS) * g


def _sigmoid(z):
    return 0.5 * jnp.tanh(0.5 * z) + 0.5


def _sigmoid_rel(z):
    return 1.0 / (1.0 + jnp.exp(-z))


def _dot(a, b):
    return jnp.dot(a, b, preferred_element_type=F32)


def _dot_nt(a, b):
    return lax.dot_general(a, b, (((1,), (1,)), ((), ())), preferred_element_type=F32)


def _dot_tn(a, b):
    return lax.dot_general(a, b, (((0,), (0,)), ((), ())), preferred_element_type=F32)


def _rope_table_kernel(pos_ref, freq_ref, sign_ref, cos_ref, sin_ref):
    ang = pos_ref[...] * freq_ref[...]
    cos_ref[...] = jnp.cos(ang)
    sin_ref[...] = jnp.sin(ang) * sign_ref[...]


def _rope_tables(pos_col, tm):
    t = pos_col.shape[0]
    half = ROT_DIM // 2
    inv_freq = ROPE_THETA ** (-jnp.arange(0, ROT_DIM, 2, dtype=F32) / ROT_DIM)
    freq = jnp.concatenate([inv_freq, inv_freq, jnp.zeros((HEAD_DIM - ROT_DIM,), F32)])[None, :]
    sign = jnp.concatenate([-jnp.ones((half,), F32), jnp.ones((HEAD_DIM - half,), F32)])[None, :]
    row = pl.BlockSpec((1, HEAD_DIM), lambda i: (0, 0))
    tab = pl.BlockSpec((tm, HEAD_DIM), lambda i: (i, 0))
    return pl.pallas_call(
        _rope_table_kernel,
        out_shape=(jax.ShapeDtypeStruct((t, HEAD_DIM), F32),) * 2,
        grid=(t // tm,),
        in_specs=[pl.BlockSpec((tm, 1), lambda i: (i, 0)), row, row],
        out_specs=(tab, tab),
        compiler_params=_params(("parallel",)),
        name="rope_tables",
    )(pos_col, freq, sign)


def _rope_head(th, cos, sin):
    half = ROT_DIM // 2
    lane = lax.broadcasted_iota(jnp.int32, th.shape, 1)
    swapped = jnp.where(lane < half, pltpu.roll(th, HEAD_DIM - half, 1), pltpu.roll(th, half, 1))
    return th * cos + swapped * sin


def _inproj_attn_kernel(x_ref, g_ref, w_ref, cos_ref, sin_ref, o0_ref, o1_ref, o2_ref, h_ref, stage_ref):
    h = _rms(x_ref[...], g_ref[...]).astype(BF16)
    h_ref[...] = h
    cos = cos_ref[...]
    sin = sin_ref[...]
    scale = 1.0 / math.sqrt(HEAD_DIM)
    tm = x_ref.shape[0]
    for grp, o_ref in enumerate((o0_ref, o1_ref, o2_ref)):
        dil = DIL_GROUPS[grp][1]
        for part in range(3):
            c = grp * 3 + part
            t = _dot(h, w_ref[:, c * BRANCH:(c + 1) * BRANCH])
            if part < 2:
                heads = []
                for hd in range(N_HEADS):
                    r = _rope_head(t[:, hd * HEAD_DIM:(hd + 1) * HEAD_DIM], cos, sin)
                    heads.append(r * scale if part == 0 else r)
                t = jnp.concatenate(heads, axis=1)
            cols = slice(part * BRANCH, (part + 1) * BRANCH)
            if dil == 1:
                o_ref[0, 0, :, cols] = t.astype(BF16)
            else:
                for hd in range(N_HEADS):
                    stage_ref[hd] = t[:, hd * HEAD_DIM:(hd + 1) * HEAD_DIM]
                for r in range(dil):
                    for hd in range(N_HEADS):
                        lo = part * BRANCH + hd * HEAD_DIM
                        o_ref[0, r, :, lo:lo + HEAD_DIM] = (
                            stage_ref[hd, pl.ds(r, tm // dil, stride=dil), :].astype(BF16))


def _inproj_attention(x, g, w, cos, sin, batch, seq, tm):
    per_seq = seq // tm
    rows = lambda width: pl.BlockSpec((tm, width), lambda b, j: (b * per_seq + j, 0))
    out_shape = tuple(jax.ShapeDtypeStruct((batch, dil, seq // dil, 3 * BRANCH), BF16) for _, dil in DIL_GROUPS)
    out_shape += (jax.ShapeDtypeStruct((batch * seq, D_MODEL), BF16),)
    out_specs = tuple(pl.BlockSpec((1, dil, tm // dil, 3 * BRANCH), lambda b, j: (b, 0, j, 0))
                      for _, dil in DIL_GROUPS) + (rows(D_MODEL),)
    return pl.pallas_call(
        _inproj_attn_kernel, out_shape=out_shape, grid=(batch, per_seq),
        in_specs=[rows(D_MODEL), pl.BlockSpec((1, D_MODEL), lambda b, j: (0, 0)),
                  _resident((D_MODEL, A_COLS), lambda b, j: (0, 0)), rows(HEAD_DIM), rows(HEAD_DIM)],
        out_specs=out_specs,
        scratch_shapes=[pltpu.VMEM((N_HEADS, tm, HEAD_DIM), F32)],
        compiler_params=_params(("parallel", "parallel")), name="inproj_attention",
    )(x, g, w, cos, sin)


def _attn_kernel(q_ref, k_ref, kp_ref, kn_ref, v_ref, vp_ref, vn_ref, h_ref, w_ref, o_ref, lse_ref, side_ref,
                 o_stage, lse_stage, *, dil, nj, n_steps, conv_side):
    n = pl.program_id(1)
    part_rows = h_ref.shape[0] // N_HEADS

    def side_projection(part):
        rs = slice(part * part_rows, (part + 1) * part_rows)
        h = h_ref[rs, :]
        if conv_side:
            u = _dot(h, w_ref[:, 0:BRANCH])
            b_gate = _dot(h, w_ref[:, BRANCH:2 * BRANCH])
            c_gate = _dot(h, w_ref[:, 2 * BRANCH:3 * BRANCH])
            side_ref[rs, 0:BRANCH] = (c_gate * u).astype(BF16)
            side_ref[rs, BRANCH:2 * BRANCH] = b_gate.astype(BF16)
        else:
            for c in range(w_ref.shape[1] // BRANCH):
                cols = slice(c * BRANCH, (c + 1) * BRANCH)
                side_ref[rs, cols] = _dot(h, w_ref[:, cols]).astype(BF16)

    col_lo = jnp.where(n > 0, 0, BLK)
    col_hi = jnp.where(n < n_steps - 1, 3 * BLK, 2 * BLK)
    row = lax.broadcasted_iota(jnp.int32, (BLK, 3 * BLK), 0)
    col = lax.broadcasted_iota(jnp.int32, (BLK, 3 * BLK), 1)
    band = jnp.abs(col - BLK - row) <= BLK
    masks = {}
    for j in range(nj):
        m = band
        if j == 0:
            m = m & (col >= col_lo)
        if j == nj - 1:
            m = m & (col < col_hi)
        masks[j] = m
    pairs = [(r, j) for r in range(dil) for j in range(nj)]
    lane = lax.broadcasted_iota(jnp.int32, (len(pairs), BLK, HEAD_DIM), 2)

    def window(own_ref, prev_ref, next_ref, r, j, hs):
        parts = []
        for w in (j - 1, j, j + 1):
            if w < 0:
                parts.append(prev_ref[0, r, :, hs])
            elif w >= nj:
                parts.append(next_ref[0, r, :, hs])
            else:
                parts.append(own_ref[0, r, w * BLK:(w + 1) * BLK, hs])
        return jnp.concatenate(parts, axis=0)

    lse_tiles = jnp.zeros((len(pairs), BLK, HEAD_DIM), F32)
    for hd in range(N_HEADS):
        side_projection(hd)
        hs = slice(hd * HEAD_DIM, (hd + 1) * HEAD_DIM)
        scores = [_dot_nt(q_ref[0, r, j * BLK:(j + 1) * BLK, hs], window(k_ref, kp_ref, kn_ref, r, j, hs))
                  for r, j in pairs]
        s = jnp.stack([jnp.where(masks[j], sc, NEG_INF) for (r, j), sc in zip(pairs, scores)])
        m = jnp.max(s, axis=2, keepdims=True)
        p = jnp.exp(s - m)
        l = jnp.sum(p, axis=2, keepdims=True)
        pb = p.astype(BF16)
        outs = jnp.stack([_dot(pb[i], window(v_ref, vp_ref, vn_ref, r, j, hs)) for i, (r, j) in enumerate(pairs)])
        outs = outs * (1.0 / l)
        lse_tiles = jnp.where(lane == hd, m + jnp.log(l), lse_tiles)
        for i, (r, j) in enumerate(pairs):
            o_stage[hd, pl.ds(j * BLK * dil + r, BLK, stride=dil), :] = outs[i]
        o_ref[:, hs] = o_stage[hd].astype(o_ref.dtype)
    for i, (r, j) in enumerate(pairs):
        lse_stage[pl.ds(j * BLK * dil + r, BLK, stride=dil), :] = lse_tiles[i]
    lse_ref[...] = lse_stage[...]


def _attention_group(qkv, h, w_side, conv_side, batch, seq, group):
    _, dil = DIL_GROUPS[group]
    t = batch * seq
    sub_len = seq // dil
    rows_per_step = min(seq, 16 * BLK)
    nj = rows_per_step // (dil * BLK)
    n_steps = seq // rows_per_step
    n_blk = sub_len // BLK
    sub_rows = rows_per_step // dil

    def own(part):
        return pl.BlockSpec((1, dil, sub_rows, BRANCH), lambda b, n: (b, 0, n, part))

    def prev(part):
        return pl.BlockSpec((1, dil, BLK, BRANCH), lambda b, n: (b, 0, jnp.maximum(n * nj - 1, 0), part))

    def nxt(part):
        return pl.BlockSpec((1, dil, BLK, BRANCH), lambda b, n: (b, 0, jnp.minimum((n + 1) * nj, n_blk - 1), part))

    side_cols = 2 * BRANCH if conv_side else w_side.shape[1]
    step_rows = lambda width: pl.BlockSpec((rows_per_step, width), lambda b, n: (b * n_steps + n, 0))
    return pl.pallas_call(
        functools.partial(_attn_kernel, dil=dil, nj=nj, n_steps=n_steps, conv_side=conv_side),
        out_shape=(jax.ShapeDtypeStruct((t, BRANCH), BF16), jax.ShapeDtypeStruct((t, HEAD_DIM), F32),
                   jax.ShapeDtypeStruct((t, side_cols), BF16)),
        grid=(batch, n_steps),
        in_specs=[own(0), own(1), prev(1), nxt(1), own(2), prev(2), nxt(2), step_rows(D_MODEL),
                  _resident(w_side.shape, lambda b, n: (0, 0))],
        out_specs=(step_rows(BRANCH), step_rows(HEAD_DIM), step_rows(side_cols)),
        scratch_shapes=[pltpu.VMEM((N_HEADS, rows_per_step, HEAD_DIM), F32), pltpu.VMEM((rows_per_step, HEAD_DIM), F32)],
        compiler_params=_params(("parallel", "parallel")),
        name=f"dilated_attention_g{group}",
    )(qkv, qkv, qkv, qkv, qkv, qkv, qkv, h, w_side)


def _hgrn_kernel(logit_ref, qf_ref, zf_ref, vf_ref, qb_ref, zb_ref, vb_ref, of_ref, ob_ref, state_ref,
                 *, layer, n_chunks):
    c = pl.program_id(1)

    @pl.when(c == 0)
    def _():
        state_ref[...] = jnp.zeros_like(state_ref)

    depth = logit_ref.shape[1]
    lbs = []
    for dirn in range(2):
        rows = [logit_ref[dirn, j:j + 1, :] for j in range(depth)]
        top = functools.reduce(jnp.maximum, rows)
        exps = [jnp.exp(rw - top) for rw in rows]
        lbs.append(sum(exps[1:layer + 1], jnp.zeros_like(top)) / sum(exps[1:], exps[0]))

    r = lax.broadcasted_iota(jnp.int32, (BLK, BLK), 0)
    cc = lax.broadcasted_iota(jnp.int32, (BLK, BLK), 1)

    def bcast_rows(per_chunk):
        return jnp.concatenate([jnp.broadcast_to(v, (BLK, BRANCH)) for v in per_chunk], axis=0)

    for dirn, (q_ref, z_ref, v_ref, o_ref) in enumerate(
            ((qf_ref, zf_ref, vf_ref, of_ref), (qb_ref, zb_ref, vb_ref, ob_ref))):
        tri = (cc <= r) if dirn == 0 else (cc >= r)
        tri_b = jnp.where(tri, 1.0, 0.0).astype(BF16)
        last, mid_row = (BLK - 1, BLK // 2 - 1) if dirn == 0 else (0, BLK // 2)
        lb = lbs[dirn]
        q = q_ref[...].astype(F32)
        z = z_ref[...].astype(F32)
        v = v_ref[...]
        sg = _sigmoid_rel(z)
        logf = jnp.log(lb + (1.0 - lb) * sg)
        kk = (1.0 - lb) * (1.0 - sg)
        hi = logf.astype(BF16)
        lo = (logf - hi.astype(F32)).astype(BF16)
        chunks = [slice(g * BLK, (g + 1) * BLK) for g in range(n_chunks)]
        cums = [_dot(tri_b, hi[ch]) + _dot(tri_b, lo[ch]) for ch in chunks]
        totals = [cm[last:last + 1, :] for cm in cums]
        mids = [cm[mid_row:mid_row + 1, :] for cm in cums]
        cum = jnp.concatenate(cums, axis=0)
        mid = bcast_rows(mids)
        qa = q * jnp.exp(cum - mid)
        ka = kk * jnp.exp(mid - cum)
        q_dec = (qa * bcast_rows([jnp.exp(md) for md in mids])).astype(BF16)
        k_dec = (ka * bcast_rows([jnp.exp(tt - md) for tt, md in zip(totals, mids)])).astype(BF16)
        qa = qa.astype(BF16)
        ka = ka.astype(BF16)
        tiles = [(g, hd) for g in range(n_chunks) for hd in range(N_HEADS)]
        sl = lambda g, hd: (slice(g * BLK, (g + 1) * BLK), slice(hd * HEAD_DIM, (hd + 1) * HEAD_DIM))
        atts = [jnp.where(tri, _dot_nt(qa[sl(g, hd)], ka[sl(g, hd)]), 0.0).astype(BF16) for g, hd in tiles]
        intra = {t: _dot(att, v[sl(*t)]) for t, att in zip(tiles, atts)}
        kvs = {t: _dot_tn(v[sl(*t)], k_dec[sl(*t)]) for t in tiles}
        decays = [jnp.exp(tt) for tt in totals]
        order = range(n_chunks) if dirn == 0 else range(n_chunks - 1, -1, -1)
        for hd in range(N_HEADS):
            hs = slice(hd * HEAD_DIM, (hd + 1) * HEAD_DIM)
            state_t = state_ref[dirn, hd]
            for g in order:
                o_ref[g * BLK:(g + 1) * BLK, hs] = intra[(g, hd)] + _dot_nt(q_dec[sl(g, hd)], state_t.astype(BF16))
                state_t = state_t * decays[g][:, hs] + kvs[(g, hd)]
            state_ref[dirn, hd] = state_t


def _hgrn(qzz, ig, logits, batch, seq, layer):
    t = qzz.shape[0]
    n_chunks = 8
    rows = n_chunks * BLK
    nc = seq // rows
    depth = logits.shape[1]

    def fwd(colblk):
        return pl.BlockSpec((rows, BRANCH), lambda b, c: (b * nc + c, colblk))

    def bwd(colblk):
        return pl.BlockSpec((rows, BRANCH), lambda b, c: (b * nc + nc - 1 - c, colblk))

    return pl.pallas_call(
        functools.partial(_hgrn_kernel, layer=layer, n_chunks=n_chunks),
        out_shape=(jax.ShapeDtypeStruct((t, BRANCH), F32),) * 2,
        grid=(batch, nc),
        in_specs=[pl.BlockSpec((2, depth, BRANCH), lambda b, c: (0, 0, 0)),
                  fwd(0), fwd(1), fwd(0), bwd(0), bwd(2), bwd(0)],
        out_specs=(pl.BlockSpec((rows, BRANCH), lambda b, c: (b * nc + c, 0)),
                   pl.BlockSpec((rows, BRANCH), lambda b, c: (b * nc + nc - 1 - c, 0))),
        scratch_shapes=[pltpu.VMEM((2, N_HEADS, HEAD_DIM, HEAD_DIM), F32)],
        compiler_params=_params(("parallel", "arbitrary")),
        name="hgrn2_scan",
    )(logits, qzz, qzz, ig, qzz, qzz, ig)


def _merge_kernel(x_ref, ao0_ref, ao1_ref, ao2_ref, l0_ref, l1_ref, l2_ref, of_ref, ob_ref, bg_ref,
                  cb_ref, cbp_ref, cbn_ref, h_ref, wgate_ref, convw_ref, ng_ref, wbr_ref, wout_ref, gpost_ref,
                  o_ref, *, tm, seq):
    i = pl.program_id(0)
    h = h_ref[...]

    ls = (l0_ref[...], l1_ref[...], l2_ref[...])
    aos = (ao0_ref, ao1_ref, ao2_ref)
    lmax = jnp.maximum(jnp.maximum(ls[0], ls[1]), ls[2])
    es = [jnp.exp(l - lmax) for l in ls]
    inv = 1.0 / (es[0] + es[1] + es[2])
    a_parts = []
    for hd in range(N_HEADS):
        hs = slice(hd * HEAD_DIM, (hd + 1) * HEAD_DIM)
        acc = None
        for g in range(N_GROUPS):
            w = (es[g] * inv)[:, hd:hd + 1]
            term = w * aos[g][:, hs].astype(F32)
            acc = term if acc is None else acc + term
        a_parts.append(acc)
    o_a = jnp.concatenate(a_parts, axis=1).astype(BF16)

    b_parts = []
    for hd in range(N_HEADS):
        hs = slice(hd * HEAD_DIM, (hd + 1) * HEAD_DIM)
        o = of_ref[:, hs] + ob_ref[:, hs]
        b_parts.append(_rms(o, ng_ref[:, hs]))
    gate = bg_ref[...].astype(F32)
    o_b = (jnp.concatenate(b_parts, axis=1) * (gate * _sigmoid(gate))).astype(BF16)

    cu = cb_ref[:, 0:BRANCH].astype(F32)
    keep_prev = jnp.where((i * tm) % seq == 0, 0.0, 1.0)
    keep_next = jnp.where(((i + 1) * tm) % seq == 0, 0.0, 1.0)
    halo_rows = cbp_ref.shape[0]
    prev_row = keep_prev * cbp_ref[halo_rows - 1:halo_rows, 0:BRANCH].astype(F32)
    next_row = keep_next * cbn_ref[0:1, 0:BRANCH].astype(F32)
    rid = lax.broadcasted_iota(jnp.int32, cu.shape, 0)
    before = jnp.where(rid == 0, prev_row, pltpu.roll(cu, 1, 0))
    after = jnp.where(rid == tm - 1, next_row, pltpu.roll(cu, tm - 1, 0))
    conv = convw_ref[0:1, :] * before + convw_ref[1:2, :] * cu + convw_ref[2:3, :] * after
    o_c = (cb_ref[:, BRANCH:2 * BRANCH].astype(F32) * conv).astype(BF16)

    merged = None
    for n, o_n in enumerate((o_a, o_b, o_c)):
        up = _dot(o_n, wbr_ref[n])
        term = _sigmoid(_dot(h, wgate_ref[:, n * D_MODEL:(n + 1) * D_MODEL])) * up
        merged = term if merged is None else merged + term
    y = _dot(merged.astype(BF16), wout_ref[...])
    o_ref[...] = x_ref[...] + _rms(y, gpost_ref[...])


def _merge(x, aos, lses, o_f, o_b, ig, cb, h, w_gate, conv_w, norm_g, w_branch, w_out, g_post, seq, tm):
    t = x.shape[0]
    halo = 16
    n_halo = t // halo
    per = tm // halo

    def rows(width, colblk=0):
        return pl.BlockSpec((tm, width), lambda i: (i, colblk))

    def full(shape):
        return _resident(shape, lambda i: (0,) * len(shape))

    in_specs = ([rows(D_MODEL)] + [rows(BRANCH)] * 3 + [rows(HEAD_DIM)] * 3 + [rows(BRANCH)] * 2
                + [rows(BRANCH, 1), rows(2 * BRANCH),
                   pl.BlockSpec((halo, 2 * BRANCH), lambda i: (jnp.maximum(i * per - 1, 0), 0)),
                   pl.BlockSpec((halo, 2 * BRANCH), lambda i: (jnp.minimum((i + 1) * per, n_halo - 1), 0)),
                   rows(D_MODEL), full((D_MODEL, G_COLS)), full((3, BRANCH)), full((1, BRANCH)),
                   full((3, BRANCH, D_MODEL)), full((D_MODEL, D_MODEL)), full((1, D_MODEL))])
    return pl.pallas_call(
        functools.partial(_merge_kernel, tm=tm, seq=seq),
        out_shape=jax.ShapeDtypeStruct((t, D_MODEL), F32),
        grid=(t // tm,),
        in_specs=in_specs,
        out_specs=rows(D_MODEL),
        compiler_params=_params(("parallel",)),
        name="merge_branches",
    )(x, *aos, *lses, o_f, o_b, ig, cb, cb, cb, h, w_gate, conv_w, norm_g, w_branch, w_out, g_post)


def _swiglu_accumulate(h_ref, wg, wu, wd, acc_ref):
    h = h_ref[...]
    g = _dot(h, wg)
    u = _dot(h, wu)
    acc_ref[...] += _dot((g * _sigmoid(g) * u).astype(BF16), wd)


def _embed_tail(x, p_ref, gin_ref, gpost_ref, wgate_ref, wproj_ref):
    gate = _sigmoid(_dot(_rms(x, gin_ref[...]).astype(BF16), wgate_ref[...]))
    e = _dot(p_ref[...].astype(BF16), wproj_ref[...]) * gate
    return x + _rms(e, gpost_ref[...])


def _embed_operands(embed, tm, index):
    pd = embed[0].shape[1]
    const = lambda *_: (0, 0)
    return [pl.BlockSpec((tm, pd), index), pl.BlockSpec((1, D_MODEL), const), pl.BlockSpec((1, D_MODEL), const),
            _resident((D_MODEL, D_MODEL), const), _resident((pd, D_MODEL), const)]


def _ffn_kernel(x_ref, gpre_ref, gpost_ref, wg_ref, wu_ref, wd_ref, p_ref, ein_ref, epost_ref, egate_ref, eproj_ref,
                o_ref, h_ref, acc_ref):
    c = pl.program_id(1)

    @pl.when(c == 0)
    def _():
        h_ref[...] = _rms(x_ref[...], gpre_ref[...]).astype(BF16)
        acc_ref[...] = jnp.zeros_like(acc_ref)

    _swiglu_accumulate(h_ref, wg_ref[...], wu_ref[...], wd_ref[...], acc_ref)

    @pl.when(c == pl.num_programs(1) - 1)
    def _():
        mixed = x_ref[...] + _rms(acc_ref[...], gpost_ref[...])
        o_ref[...] = _embed_tail(mixed, p_ref, ein_ref, epost_ref, egate_ref, eproj_ref)


def _ffn(x, g_pre, g_post, wg, wu, wd, embed, tm, fc):
    t = x.shape[0]
    ff = wg.shape[1]
    return pl.pallas_call(
        _ffn_kernel,
        out_shape=jax.ShapeDtypeStruct((t, D_MODEL), F32),
        grid=(t // tm, ff // fc),
        in_specs=[pl.BlockSpec((tm, D_MODEL), lambda i, c: (i, 0)),
                  pl.BlockSpec((1, D_MODEL), lambda i, c: (0, 0)),
                  pl.BlockSpec((1, D_MODEL), lambda i, c: (0, 0)),
                  pl.BlockSpec((D_MODEL, fc), lambda i, c: (0, c)),
                  pl.BlockSpec((D_MODEL, fc), lambda i, c: (0, c)),
                  pl.BlockSpec((fc, D_MODEL), lambda i, c: (c, 0))]
                 + _embed_operands(embed, tm, lambda i, c: (i, 0)),
        out_specs=pl.BlockSpec((tm, D_MODEL), lambda i, c: (i, 0)),
        scratch_shapes=[pltpu.VMEM((tm, D_MODEL), BF16), pltpu.VMEM((tm, D_MODEL), F32)],
        compiler_params=_params(("parallel", "arbitrary")),
        name="dense_swiglu",
    )(x, g_pre, g_post, wg, wu, wd, *embed)


PIECES = D_MODEL // 2 // HEAD_DIM
SC_WINDOW = 128
EXPERT_TILE = 512


def _pack_rows(vals):
    bits = lax.bitcast_convert_type(vals.astype(BF16).astype(F32), jnp.uint32)
    half = D_MODEL // 2
    word = bits[:, :half] | (bits[:, half:] >> 16)
    return [lax.bitcast_convert_type(word[:, c * HEAD_DIM:(c + 1) * HEAD_DIM], jnp.int32) for c in range(PIECES)]


def _unpack_rows(piece_refs):
    words = [lax.bitcast_convert_type(r[...], jnp.uint32) for r in piece_refs]
    hi = [lax.bitcast_convert_type(w & jnp.uint32(0xFFFF0000), F32) for w in words]
    lo = [lax.bitcast_convert_type(w << 16, F32) for w in words]
    return jnp.concatenate(hi + lo, axis=1)


def _route_kernel(x_ref, gpre_ref, rhi_ref, rlo_ref, *refs, n_experts):
    piece_refs = refs[:PIECES]
    rec_ref, rank_ref, count_ref, carry_ref = refs[PIECES:]
    i = pl.program_id(0)
    tm = x_ref.shape[0]

    @pl.when(i == 0)
    def _():
        carry_ref[...] = jnp.zeros_like(carry_ref)

    hf = _rms(x_ref[...], gpre_ref[...])
    for r, piece in zip(piece_refs, _pack_rows(hf)):
        r[...] = piece
    h_hi = hf.astype(BF16)
    h_lo = (hf - h_hi.astype(F32)).astype(BF16)
    logits = _dot(h_hi, rhi_ref[...]) + _dot(h_hi, rlo_ref[...]) + _dot(h_lo, rhi_ref[...])
    lane = lax.broadcasted_iota(jnp.int32, logits.shape, 1).astype(F32)
    logits = jnp.where(lane < n_experts, logits, NEG_INF)
    m1 = jnp.max(logits, axis=1, keepdims=True)
    i1 = jnp.min(jnp.where(logits == m1, lane, 1e9), axis=1, keepdims=True)
    rest = jnp.where(lane == i1, NEG_INF, logits)
    m2 = jnp.max(rest, axis=1, keepdims=True)
    i2 = jnp.min(jnp.where(rest == m2, lane, 1e9), axis=1, keepdims=True)
    ex = jnp.exp(m2 - m1)
    w1 = 1.0 / (1.0 + ex)
    rec = jnp.where(lane == 0, i1, jnp.where(lane == 1, i2, jnp.where(lane == 2, w1, jnp.where(lane == 3, ex * w1, 0.0))))
    rec_ref[...] = rec
    rec_t = rec.T
    e1 = rec_t[0:1, :]
    e2 = rec_t[1:2, :]
    sub = lax.broadcasted_iota(jnp.int32, (8, tm), 0).astype(F32)
    oh1 = jnp.where(sub == e1, 1.0, 0.0)
    oh2 = jnp.where(sub == e2, 1.0, 0.0)
    chosen = oh1 + oh2
    src = lax.broadcasted_iota(jnp.int32, (tm, tm), 0)
    dst = lax.broadcasted_iota(jnp.int32, (tm, tm), 1)
    before = jnp.where(src < dst, 1.0, 0.0).astype(BF16)
    rank = _dot(chosen.astype(BF16), before) + carry_ref[:, 0:1]
    rank1 = jnp.sum(oh1 * rank, axis=0, keepdims=True)
    rank2 = jnp.sum(oh2 * rank, axis=0, keepdims=True)
    rank_ref[...] = jnp.where(sub == 0, rank1, jnp.where(sub == 1, rank2, jnp.where(sub == 2, e1, jnp.where(sub == 3, e2, 0.0))))
    carry_ref[...] = carry_ref[...] + jnp.sum(chosen, axis=1, keepdims=True)
    count_ref[...] = carry_ref[...]


def _route(x, g_pre, r_hi, r_lo, n_experts, tm):
    t = x.shape[0]
    assert n_experts <= 8
    piece = pl.BlockSpec((tm, HEAD_DIM), lambda i: (i, 0))
    return pl.pallas_call(
        functools.partial(_route_kernel, n_experts=n_experts),
        out_shape=(*[jax.ShapeDtypeStruct((t, HEAD_DIM), jnp.int32)] * PIECES,
                   jax.ShapeDtypeStruct((t, HEAD_DIM), F32), jax.ShapeDtypeStruct((8, t), F32),
                   jax.ShapeDtypeStruct((8, HEAD_DIM), F32)),
        grid=(t // tm,),
        in_specs=[pl.BlockSpec((tm, D_MODEL), lambda i: (i, 0)), pl.BlockSpec((1, D_MODEL), lambda i: (0, 0)),
                  pl.BlockSpec((D_MODEL, HEAD_DIM), lambda i: (0, 0)), pl.BlockSpec((D_MODEL, HEAD_DIM), lambda i: (0, 0))],
        out_specs=(*[piece] * PIECES, piece, pl.BlockSpec((8, tm), lambda i: (0, i)),
                   pl.BlockSpec((8, HEAD_DIM), lambda i: (0, 0))),
        scratch_shapes=[pltpu.VMEM((8, HEAD_DIM), F32)],
        compiler_params=_params(("arbitrary",)),
        name="moe_route",
    )(x, g_pre, r_hi, r_lo)


def _slot_kernel(rank_ref, start_ref, s1_ref, s2_ref):
    tm = rank_ref.shape[1]
    sub = lax.broadcasted_iota(jnp.int32, (8, tm), 0).astype(F32)
    start = start_ref[:, 0:1]
    rows = rank_ref[...]
    for choice, out in ((0, s1_ref), (1, s2_ref)):
        base = jnp.sum(jnp.where(sub == rows[2 + choice:3 + choice, :], start, 0.0), axis=0, keepdims=True)
        out[...] = (base + rows[choice:choice + 1, :]).astype(jnp.int32)


def _slots(ranks, starts, tm):
    t = ranks.shape[1]
    row = pl.BlockSpec((1, tm), lambda i: (0, i))
    return pl.pallas_call(
        _slot_kernel,
        out_shape=(jax.ShapeDtypeStruct((1, t), jnp.int32),) * 2,
        grid=(t // tm,),
        in_specs=[pl.BlockSpec((8, tm), lambda i: (0, i)), pl.BlockSpec((8, HEAD_DIM), lambda i: (0, 0))],
        out_specs=(row, row),
        compiler_params=_params(("parallel",)),
        name="moe_slots",
    )(ranks, starts)


def _sc_mesh():
    return plsc.VectorSubcoreMesh(core_axis_name="core", subcore_axis_name="subcore")


def _sc_scatter_rows(srcs, idxs, n_rows):
    ns, nk = len(srcs), len(idxs)
    half = srcs[0].shape[0] // SC_WINDOW // 2

    @functools.partial(pl.kernel, mesh=_sc_mesh(), scratch_types=[],
                       out_type=tuple(jax.ShapeDtypeStruct((n_rows, HEAD_DIM), srcs[0].dtype) for _ in range(ns)))
    def scatter(*refs):
        out_hbm = refs[ns + nk:]

        def body(*blocks):
            for idx in blocks[ns:]:
                for c in range(ns):
                    pltpu.sync_copy(blocks[c], out_hbm[c].at[idx.at[0]])

        pltpu.emit_pipeline(
            body, grid=(2, half),
            in_specs=[pl.BlockSpec((SC_WINDOW, HEAD_DIM), lambda i, j: (i * half + j, 0)) for _ in range(ns)]
                     + [pl.BlockSpec((1, SC_WINDOW), lambda i, j: (0, i * half + j)) for _ in range(nk)],
            out_specs=[],
            core_axis_name=("core", "subcore"),
            dimension_semantics=(pltpu.PARALLEL, pltpu.PARALLEL),
        )(*refs[:ns + nk])

    return scatter(*srcs, *idxs)


def _sc_gather_rows(tables, idx):
    nt = len(tables)
    n = idx.shape[1]
    half = n // SC_WINDOW // 2

    @functools.partial(pl.kernel, mesh=_sc_mesh(), scratch_types=[],
                       out_type=tuple(jax.ShapeDtypeStruct((n, HEAD_DIM), tables[0].dtype) for _ in range(nt)))
    def gather(*refs):
        table_hbm = refs[:nt]

        def body(idx_blk, *out_blks):
            for c in range(nt):
                pltpu.sync_copy(table_hbm[c].at[idx_blk.at[0]], out_blks[c])

        pltpu.emit_pipeline(
            body, grid=(2, half),
            in_specs=[pl.BlockSpec((1, SC_WINDOW), lambda i, j: (0, i * half + j))],
            out_specs=[pl.BlockSpec((SC_WINDOW, HEAD_DIM), lambda i, j: (i * half + j, 0)) for _ in range(nt)],
            core_axis_name=("core", "subcore"),
            dimension_semantics=(pltpu.PARALLEL, pltpu.PARALLEL),
        )(refs[nt], *refs[nt + 1:])

    return gather(*tables, idx)


def _expert_kernel(tile_expert_ref, n_used_ref, *refs):
    x_refs = refs[:PIECES]
    wg_ref, wu_ref, wd_ref = refs[PIECES:PIECES + 3]
    y_refs = refs[PIECES + 3:2 * PIECES + 3]
    x_scr, acc_ref = refs[2 * PIECES + 3:]
    i = pl.program_id(0)
    c = pl.program_id(1)

    @pl.when(i < n_used_ref[0])
    def _():
        @pl.when(c == 0)
        def _():
            x_scr[...] = _unpack_rows(x_refs).astype(BF16)
            acc_ref[...] = jnp.zeros_like(acc_ref)

        _swiglu_accumulate(x_scr, wg_ref[0], wu_ref[0], wd_ref[0], acc_ref)

        @pl.when(c == pl.num_programs(1) - 1)
        def _():
            for r, piece in zip(y_refs, _pack_rows(acc_ref[...])):
                r[...] = piece


def _expert_ffn(xs, tile_expert, n_used, wg, wu, wd, fc):
    n_rows = xs[0].shape[0]
    ff = wg.shape[2]
    rows = pl.BlockSpec((EXPERT_TILE, HEAD_DIM), lambda i, c, te, nu: (jnp.minimum(i, nu[0] - 1), 0))
    out_rows = pl.BlockSpec((EXPERT_TILE, HEAD_DIM), lambda i, c, te, nu: (i, 0))
    grid_spec = pltpu.PrefetchScalarGridSpec(
        num_scalar_prefetch=2,
        grid=(n_rows // EXPERT_TILE, ff // fc),
        in_specs=[rows] * PIECES + [pl.BlockSpec((1, D_MODEL, fc), lambda i, c, te, nu: (te[i], 0, c)),
                                    pl.BlockSpec((1, D_MODEL, fc), lambda i, c, te, nu: (te[i], 0, c)),
                                    pl.BlockSpec((1, fc, D_MODEL), lambda i, c, te, nu: (te[i], c, 0))],
        out_specs=[out_rows] * PIECES,
        scratch_shapes=[pltpu.VMEM((EXPERT_TILE, D_MODEL), BF16), pltpu.VMEM((EXPERT_TILE, D_MODEL), F32)],
    )
    return pl.pallas_call(
        _expert_kernel,
        out_shape=[jax.ShapeDtypeStruct((n_rows, HEAD_DIM), jnp.int32)] * PIECES,
        grid_spec=grid_spec,
        compiler_params=_params(("arbitrary", "arbitrary")),
        name="moe_expert_swiglu",
    )(tile_expert, n_used, *xs, wg, wu, wd)


def _combine_kernel(x_ref, rec_ref, gpost_ref, p_ref, ein_ref, epost_ref, egate_ref, eproj_ref, *refs):
    y1 = _unpack_rows(refs[:PIECES])
    y2 = _unpack_rows(refs[PIECES:2 * PIECES])
    o_ref = refs[2 * PIECES]
    rec = rec_ref[...]
    y = rec[:, 2:3] * y1 + rec[:, 3:4] * y2
    mixed = x_ref[...] + _rms(y, gpost_ref[...])
    o_ref[...] = _embed_tail(mixed, p_ref, ein_ref, epost_ref, egate_ref, eproj_ref)


def _combine(x, rec, g_post, embed, y1, y2, tm):
    t = x.shape[0]
    piece = pl.BlockSpec((tm, HEAD_DIM), lambda i: (i, 0))
    return pl.pallas_call(
        _combine_kernel,
        out_shape=jax.ShapeDtypeStruct((t, D_MODEL), F32),
        grid=(t // tm,),
        in_specs=[pl.BlockSpec((tm, D_MODEL), lambda i: (i, 0)), piece, pl.BlockSpec((1, D_MODEL), lambda i: (0, 0))]
                 + _embed_operands(embed, tm, lambda i: (i, 0)) + [piece] * (2 * PIECES),
        out_specs=pl.BlockSpec((tm, D_MODEL), lambda i: (i, 0)),
        compiler_params=_params(("parallel",)),
        name="moe_combine",
    )(x, rec, g_post, *embed, *y1, *y2)


def _moe(x, g_pre, g_post, router, wg, wu, wd, embed, tm):
    t = x.shape[0]
    n_experts = router.shape[1]
    router = jnp.pad(router, ((0, 0), (0, HEAD_DIM - n_experts)))
    r_hi = router.astype(BF16)
    r_lo = (router - r_hi.astype(F32)).astype(BF16)
    *h_pieces, rec, ranks, counts = _route(x, g_pre, r_hi, r_lo, n_experts, tm)

    counts = counts[:n_experts, 0].astype(jnp.int32)
    tiles = (counts + EXPERT_TILE - 1) // EXPERT_TILE
    tile_end = jnp.cumsum(tiles)
    starts = ((tile_end - tiles) * EXPERT_TILE).astype(F32)
    starts = jnp.broadcast_to(jnp.pad(starts, (0, 8 - n_experts))[:, None], (8, HEAD_DIM))
    n_rows = TOP_K * t + n_experts * EXPERT_TILE
    n_tiles = n_rows // EXPERT_TILE
    n_used = tile_end[-1:]
    tile_ids = jnp.minimum(jnp.arange(n_tiles, dtype=jnp.int32), n_used[0] - 1)
    tile_expert = jnp.sum(tile_ids[:, None] >= tile_end[None, :], axis=1).astype(jnp.int32)

    slot1, slot2 = _slots(ranks, starts, min(t, 8192))
    xs = []
    for c in range(0, PIECES, 2):
        xs += _sc_scatter_rows(h_pieces[c:c + 2], [slot1, slot2], n_rows)
    ys = _expert_ffn(xs, tile_expert, n_used, wg, wu, wd, _largest_divisor(wg.shape[2], 2048, HEAD_DIM))
    gathered = []
    for slot in (slot1, slot2):
        rows = []
        for c in range(0, PIECES, 2):
            rows += _sc_gather_rows(ys[c:c + 2], slot)
        gathered.append(rows)
    return _combine(x, rec, g_post, embed, gathered[0], gathered[1], tm)


def _largest_divisor(n, cap, multiple):
    best = multiple
    for cand in range(multiple, cap + 1, multiple):
        if n % cand == 0:
            best = cand
    return best


def kernel(x, p, positions, w_in, conv_w, hgrn_lb_logits, hgrn_norm_g, w_branch, w_out, g_mix_pre, g_mix_post, g_ffn_pre, g_ffn_post, dense_w_gate, dense_w_up, dense_w_down, moe_router, moe_w_gate, moe_w_up, moe_w_down, ple_w_proj, ple_w_gate, ple_g_in, ple_g_post):
    batch, seq, _ = x.shape
    depth = w_in.shape[0]
    t = batch * seq
    tm = min(512, seq)
    tm_merge = min(512, seq)
    tm_ffn = min(512, seq)

    xt = x.reshape(t, D_MODEL)
    cos, sin = _rope_tables(positions.astype(F32).reshape(t, 1), tm)
    row = lambda a: a.reshape(1, -1)

    for i in range(depth):
        w_i = w_in[i].astype(BF16)
        g_pre = row(g_mix_pre[i])
        *qkvs, h = _inproj_attention(xt, g_pre, w_i[:, :A_COLS], cos, sin, batch, seq, tm)
        side_cols = ((A_COLS, A_COLS + 3 * BRANCH), (A_COLS + 3 * BRANCH, A_COLS + B_COLS),
                     (A_COLS + B_COLS, A_COLS + B_COLS + C_COLS))
        aos, lses, sides = zip(*[_attention_group(qkvs[g], h, w_i[:, lo:hi], g == 2, batch, seq, g)
                                 for g, (lo, hi) in enumerate(side_cols)])
        qzz, ig, cb = sides
        o_f, o_b = _hgrn(qzz, ig, hgrn_lb_logits, batch, seq, i)
        xt = _merge(xt, aos, lses, o_f, o_b, ig, cb, h, w_i[:, A_COLS + B_COLS + C_COLS:], conv_w[i],
                    row(hgrn_norm_g[i]), w_branch[i].astype(BF16), w_out[i].astype(BF16), row(g_mix_post[i]),
                    seq, tm_merge)

        embed = (p[i].reshape(t, -1), row(ple_g_in[i]), row(ple_g_post[i]),
                 ple_w_gate[i].astype(BF16), ple_w_proj[i].astype(BF16))
        j = i // 2
        if i % 2 == 0:
            xt = _ffn(xt, row(g_ffn_pre[i]), row(g_ffn_post[i]), dense_w_gate[j].astype(BF16),
                      dense_w_up[j].astype(BF16), dense_w_down[j].astype(BF16), embed, tm_ffn,
                      _largest_divisor(dense_w_gate.shape[2], 1536, HEAD_DIM))
        else:
            xt = _moe(xt, row(g_ffn_pre[i]), row(g_ffn_post[i]), moe_router[j], moe_w_gate[j].astype(BF16),
                      moe_w_up[j].astype(BF16), moe_w_down[j].astype(BF16), embed, tm)

    return xt.reshape(batch, seq, D_MODEL)
```

```python
import functools
import math

import jax
import jax.numpy as jnp
from jax import lax
from jax.experimental import pallas as pl
from jax.experimental.pallas import tpu as pltpu
from jax.experimental.pallas import tpu_sc as plsc

D_MODEL = 1024
EPS = 1e-6
NEG_INF = -1e30

HEAD_DIM = 128
N_HEADS = 4
BRANCH = N_HEADS * HEAD_DIM
DIL_GROUPS = ((128, 1), (512, 4), (2048, 16))
N_GROUPS = len(DIL_GROUPS)
BLK = 64
ROT_DIM = HEAD_DIM // 4
ROPE_THETA = 500000.0
TOP_K = 2

A_COLS = N_GROUPS * 3 * BRANCH
B_COLS = 5 * BRANCH
C_COLS = 3 * BRANCH
G_COLS = 3 * D_MODEL

BF16 = jnp.bfloat16
F32 = jnp.float32

VMEM_LIMIT = 56 * 1024 * 1024


def _params(sem):
    return pltpu.CompilerParams(dimension_semantics=sem, vmem_limit_bytes=VMEM_LIMIT)


def _resident(shape, index_map):
    return pl.BlockSpec(shape, index_map, pipeline_mode=pl.Buffered(1))


def _rms(xf, g):
    return xf * lax.rsqrt(jnp.mean(xf * xf, axis=-1, keepdims=True) + EPS) * g


def _sigmoid(z):
    return 0.5 * jnp.tanh(0.5 * z) + 0.5


def _sigmoid_rel(z):
    return 1.0 / (1.0 + jnp.exp(-z))


def _dot(a, b):
    return jnp.dot(a, b, preferred_element_type=F32)


def _dot_nt(a, b):
    return lax.dot_general(a, b, (((1,), (1,)), ((), ())), preferred_element_type=F32)


def _dot_tn(a, b):
    return lax.dot_general(a, b, (((0,), (0,)), ((), ())), preferred_element_type=F32)


def _rope_table_kernel(pos_ref, freq_ref, sign_ref, cos_ref, sin_ref):
    ang = pos_ref[...] * freq_ref[...]
    cos_ref[...] = jnp.cos(ang)
    sin_ref[...] = jnp.sin(ang) * sign_ref[...]


def _rope_tables(pos_col, tm):
    t = pos_col.shape[0]
    half = ROT_DIM // 2
    inv_freq = ROPE_THETA ** (-jnp.arange(0, ROT_DIM, 2, dtype=F32) / ROT_DIM)
    freq = jnp.concatenate([inv_freq, inv_freq, jnp.zeros((HEAD_DIM - ROT_DIM,), F32)])[None, :]
    sign = jnp.concatenate([-jnp.ones((half,), F32), jnp.ones((HEAD_DIM - half,), F32)])[None, :]
    row = pl.BlockSpec((1, HEAD_DIM), lambda i: (0, 0))
    tab = pl.BlockSpec((tm, HEAD_DIM), lambda i: (i, 0))
    return pl.pallas_call(
        _rope_table_kernel,
        out_shape=(jax.ShapeDtypeStruct((t, HEAD_DIM), F32),) * 2,
        grid=(t // tm,),
        in_specs=[pl.BlockSpec((tm, 1), lambda i: (i, 0)), row, row],
        out_specs=(tab, tab),
        compiler_params=_params(("parallel",)),
        name="rope_tables",
    )(pos_col, freq, sign)


def _rope_head(th, cos, sin):
    half = ROT_DIM // 2
    lane = lax.broadcasted_iota(jnp.int32, th.shape, 1)
    swapped = jnp.where(lane < half, pltpu.roll(th, HEAD_DIM - half, 1), pltpu.roll(th, half, 1))
    return th * cos + swapped * sin


def _inproj_attn_kernel(x_ref, g_ref, w_ref, cos_ref, sin_ref, o0_ref, o1_ref, o2_ref, h_ref, stage_ref):
    h = _rms(x_ref[...], g_ref[...]).astype(BF16)
    h_ref[...] = h
    cos = cos_ref[...]
    sin = sin_ref[...]
    scale = 1.0 / math.sqrt(HEAD_DIM)
    tm = x_ref.shape[0]
    for grp, o_ref in enumerate((o0_ref, o1_ref, o2_ref)):
        dil = DIL_GROUPS[grp][1]
        for part in range(3):
            c = grp * 3 + part
            t = _dot(h, w_ref[:, c * BRANCH:(c + 1) * BRANCH])
            if part < 2:
                heads = []
                for hd in range(N_HEADS):
                    r = _rope_head(t[:, hd * HEAD_DIM:(hd + 1) * HEAD_DIM], cos, sin)
                    heads.append(r * scale if part == 0 else r)
                t = jnp.concatenate(heads, axis=1)
            cols = slice(part * BRANCH, (part + 1) * BRANCH)
            if dil == 1:
                o_ref[0, 0, :, cols] = t.astype(BF16)
            else:
                for hd in range(N_HEADS):
                    stage_ref[hd] = t[:, hd * HEAD_DIM:(hd + 1) * HEAD_DIM]
                for r in range(dil):
                    for hd in range(N_HEADS):
                        lo = part * BRANCH + hd * HEAD_DIM
                        o_ref[0, r, :, lo:lo + HEAD_DIM] = (
                            stage_ref[hd, pl.ds(r, tm // dil, stride=dil), :].astype(BF16))


def _inproj_attention(x, g, w, cos, sin, batch, seq, tm):
    per_seq = seq // tm
    rows = lambda width: pl.BlockSpec((tm, width), lambda b, j: (b * per_seq + j, 0))
    out_shape = tuple(jax.ShapeDtypeStruct((batch, dil, seq // dil, 3 * BRANCH), BF16) for _, dil in DIL_GROUPS)
    out_shape += (jax.ShapeDtypeStruct((batch * seq, D_MODEL), BF16),)
    out_specs = tuple(pl.BlockSpec((1, dil, tm // dil, 3 * BRANCH), lambda b, j: (b, 0, j, 0))
                      for _, dil in DIL_GROUPS) + (rows(D_MODEL),)
    return pl.pallas_call(
        _inproj_attn_kernel, out_shape=out_shape, grid=(batch, per_seq),
        in_specs=[rows(D_MODEL), pl.BlockSpec((1, D_MODEL), lambda b, j: (0, 0)),
                  _resident((D_MODEL, A_COLS), lambda b, j: (0, 0)), rows(HEAD_DIM), rows(HEAD_DIM)],
        out_specs=out_specs,
        scratch_shapes=[pltpu.VMEM((N_HEADS, tm, HEAD_DIM), F32)],
        compiler_params=_params(("parallel", "parallel")), name="inproj_attention",
    )(x, g, w, cos, sin)


def _attn_kernel(q_ref, k_ref, kp_ref, kn_ref, v_ref, vp_ref, vn_ref, h_ref, w_ref, o_ref, lse_ref, side_ref,
                 o_stage, lse_stage, *, dil, nj, n_steps, conv_side):
    n = pl.program_id(1)
    part_rows = h_ref.shape[0] // N_HEADS

    def side_projection(part):
        rs = slice(part * part_rows, (part + 1) * part_rows)
        h = h_ref[rs, :]
        if conv_side:
            u = _dot(h, w_ref[:, 0:BRANCH])
            b_gate = _dot(h, w_ref[:, BRANCH:2 * BRANCH])
            c_gate = _dot(h, w_ref[:, 2 * BRANCH:3 * BRANCH])
            side_ref[rs, 0:BRANCH] = (c_gate * u).astype(BF16)
            side_ref[rs, BRANCH:2 * BRANCH] = b_gate.astype(BF16)
        else:
            for c in range(w_ref.shape[1] // BRANCH):
                cols = slice(c * BRANCH, (c + 1) * BRANCH)
                side_ref[rs, cols] = _dot(h, w_ref[:, cols]).astype(BF16)

    col_lo = jnp.where(n > 0, 0, BLK)
    col_hi = jnp.where(n < n_steps - 1, 3 * BLK, 2 * BLK)
    row = lax.broadcasted_iota(jnp.int32, (BLK, 3 * BLK), 0)
    col = lax.broadcasted_iota(jnp.int32, (BLK, 3 * BLK), 1)
    band = jnp.abs(col - BLK - row) <= BLK
    masks = {}
    for j in range(nj):
        m = band
        if j == 0:
            m = m & (col >= col_lo)
        if j == nj - 1:
            m = m & (col < col_hi)
        masks[j] = m
    pairs = [(r, j) for r in range(dil) for j in range(nj)]
    lane = lax.broadcasted_iota(jnp.int32, (len(pairs), BLK, HEAD_DIM), 2)

    def window(own_ref, prev_ref, next_ref, r, j, hs):
        parts = []
        for w in (j - 1, j, j + 1):
            if w < 0:
                parts.append(prev_ref[0, r, :, hs])
            elif w >= nj:
                parts.append(next_ref[0, r, :, hs])
            else:
                parts.append(own_ref[0, r, w * BLK:(w + 1) * BLK, hs])
        return jnp.concatenate(parts, axis=0)

    lse_tiles = jnp.zeros((len(pairs), BLK, HEAD_DIM), F32)
    for hd in range(N_HEADS):
        side_projection(hd)
        hs = slice(hd * HEAD_DIM, (hd + 1) * HEAD_DIM)
        scores = [_dot_nt(q_ref[0, r, j * BLK:(j + 1) * BLK, hs], window(k_ref, kp_ref, kn_ref, r, j, hs))
                  for r, j in pairs]
        s = jnp.stack([jnp.where(masks[j], sc, NEG_INF) for (r, j), sc in zip(pairs, scores)])
        m = jnp.max(s, axis=2, keepdims=True)
        p = jnp.exp(s - m)
        l = jnp.sum(p, axis=2, keepdims=True)
        pb = p.astype(BF16)
        outs = jnp.stack([_dot(pb[i], window(v_ref, vp_ref, vn_ref, r, j, hs)) for i, (r, j) in enumerate(pairs)])
        outs = outs * (1.0 / l)
        lse_tiles = jnp.where(lane == hd, m + jnp.log(l), lse_tiles)
        for i, (r, j) in enumerate(pairs):
            o_stage[hd, pl.ds(j * BLK * dil + r, BLK, stride=dil), :] = outs[i]
        o_ref[:, hs] = o_stage[hd].astype(o_ref.dtype)
    for i, (r, j) in enumerate(pairs):
        lse_stage[pl.ds(j * BLK * dil + r, BLK, stride=dil), :] = lse_tiles[i]
    lse_ref[...] = lse_stage[...]


def _attention_group(qkv, h, w_side, conv_side, batch, seq, group):
    _, dil = DIL_GROUPS[group]
    t = batch * seq
    sub_len = seq // dil
    rows_per_step = min(seq, 16 * BLK)
    nj = rows_per_step // (dil * BLK)
    n_steps = seq // rows_per_step
    n_blk = sub_len // BLK
    sub_rows = rows_per_step // dil

    def own(part):
        return pl.BlockSpec((1, dil, sub_rows, BRANCH), lambda b, n: (b, 0, n, part))

    def prev(part):
        return pl.BlockSpec((1, dil, BLK, BRANCH), lambda b, n: (b, 0, jnp.maximum(n * nj - 1, 0), part))

    def nxt(part):
        return pl.BlockSpec((1, dil, BLK, BRANCH), lambda b, n: (b, 0, jnp.minimum((n + 1) * nj, n_blk - 1), part))

    side_cols = 2 * BRANCH if conv_side else w_side.shape[1]
    step_rows = lambda width: pl.BlockSpec((rows_per_step, width), lambda b, n: (b * n_steps + n, 0))
    return pl.pallas_call(
        functools.partial(_attn_kernel, dil=dil, nj=nj, n_steps=n_steps, conv_side=conv_side),
        out_shape=(jax.ShapeDtypeStruct((t, BRANCH), BF16), jax.ShapeDtypeStruct((t, HEAD_DIM), F32),
                   jax.ShapeDtypeStruct((t, side_cols), BF16)),
        grid=(batch, n_steps),
        in_specs=[own(0), own(1), prev(1), nxt(1), own(2), prev(2), nxt(2), step_rows(D_MODEL),
                  _resident(w_side.shape, lambda b, n: (0, 0))],
        out_specs=(step_rows(BRANCH), step_rows(HEAD_DIM), step_rows(side_cols)),
        scratch_shapes=[pltpu.VMEM((N_HEADS, rows_per_step, HEAD_DIM), F32), pltpu.VMEM((rows_per_step, HEAD_DIM), F32)],
        compiler_params=_params(("parallel", "parallel")),
        name=f"dilated_attention_g{group}",
    )(qkv, qkv, qkv, qkv, qkv, qkv, qkv, h, w_side)


def _hgrn_kernel(logit_ref, qf_ref, zf_ref, vf_ref, qb_ref, zb_ref, vb_ref, of_ref, ob_ref, state_ref,
                 *, layer, n_chunks):
    c = pl.program_id(1)

    @pl.when(c == 0)
    def _():
        state_ref[...] = jnp.zeros_like(state_ref)

    depth = logit_ref.shape[1]
    lbs = []
    for dirn in range(2):
        rows = [logit_ref[dirn, j:j + 1, :] for j in range(depth)]
        top = functools.reduce(jnp.maximum, rows)
        exps = [jnp.exp(rw - top) for rw in rows]
        lbs.append(sum(exps[1:layer + 1], jnp.zeros_like(top)) / sum(exps[1:], exps[0]))

    r = lax.broadcasted_iota(jnp.int32, (BLK, BLK), 0)
    cc = lax.broadcasted_iota(jnp.int32, (BLK, BLK), 1)

    def bcast_rows(per_chunk):
        return jnp.concatenate([jnp.broadcast_to(v, (BLK, BRANCH)) for v in per_chunk], axis=0)

    for dirn, (q_ref, z_ref, v_ref, o_ref) in enumerate(
            ((qf_ref, zf_ref, vf_ref, of_ref), (qb_ref, zb_ref, vb_ref, ob_ref))):
        tri = (cc <= r) if dirn == 0 else (cc >= r)
        tri_b = jnp.where(tri, 1.0, 0.0).astype(BF16)
        last, mid_row = (BLK - 1, BLK // 2 - 1) if dirn == 0 else (0, BLK // 2)
        lb = lbs[dirn]
        q = q_ref[...].astype(F32)
        z = z_ref[...].astype(F32)
        v = v_ref[...]
        sg = _sigmoid_rel(z)
        logf = jnp.log(lb + (1.0 - lb) * sg)
        kk = (1.0 - lb) * (1.0 - sg)
        hi = logf.astype(BF16)
        lo = (logf - hi.astype(F32)).astype(BF16)
        chunks = [slice(g * BLK, (g + 1) * BLK) for g in range(n_chunks)]
        cums = [_dot(tri_b, hi[ch]) + _dot(tri_b, lo[ch]) for ch in chunks]
        totals = [cm[last:last + 1, :] for cm in cums]
        mids = [cm[mid_row:mid_row + 1, :] for cm in cums]
        cum = jnp.concatenate(cums, axis=0)
        mid = bcast_rows(mids)
        qa = q * jnp.exp(cum - mid)
        ka = kk * jnp.exp(mid - cum)
        q_dec = (qa * bcast_rows([jnp.exp(md) for md in mids])).astype(BF16)
        k_dec = (ka * bcast_rows([jnp.exp(tt - md) for tt, md in zip(totals, mids)])).astype(BF16)
        qa = qa.astype(BF16)
        ka = ka.astype(BF16)
        tiles = [(g, hd) for g in range(n_chunks) for hd in range(N_HEADS)]
        sl = lambda g, hd: (slice(g * BLK, (g + 1) * BLK), slice(hd * HEAD_DIM, (hd + 1) * HEAD_DIM))
        atts = [jnp.where(tri, _dot_nt(qa[sl(g, hd)], ka[sl(g, hd)]), 0.0).astype(BF16) for g, hd in tiles]
        intra = {t: _dot(att, v[sl(*t)]) for t, att in zip(tiles, atts)}
        kvs = {t: _dot_tn(v[sl(*t)], k_dec[sl(*t)]) for t in tiles}
        decays = [jnp.exp(tt) for tt in totals]
        order = range(n_chunks) if dirn == 0 else range(n_chunks - 1, -1, -1)
        for hd in range(N_HEADS):
            hs = slice(hd * HEAD_DIM, (hd + 1) * HEAD_DIM)
            state_t = state_ref[dirn, hd]
            for g in order:
                o_ref[g * BLK:(g + 1) * BLK, hs] = intra[(g, hd)] + _dot_nt(q_dec[sl(g, hd)], state_t.astype(BF16))
                state_t = state_t * decays[g][:, hs] + kvs[(g, hd)]
            state_ref[dirn, hd] = state_t


def _hgrn(qzz, ig, logits, batch, seq, layer):
    t = qzz.shape[0]
    n_chunks = 4
    rows = n_chunks * BLK
    nc = seq // rows
    depth = logits.shape[1]

    def fwd(colblk):
        return pl.BlockSpec((rows, BRANCH), lambda b, c: (b * nc + c, colblk))

    def bwd(colblk):
        return pl.BlockSpec((rows, BRANCH), lambda b, c: (b * nc + nc - 1 - c, colblk))

    return pl.pallas_call(
        functools.partial(_hgrn_kernel, layer=layer, n_chunks=n_chunks),
        out_shape=(jax.ShapeDtypeStruct((t, BRANCH), F32),) * 2,
        grid=(batch, nc),
        in_specs=[pl.BlockSpec((2, depth, BRANCH), lambda b, c: (0, 0, 0)),
                  fwd(0), fwd(1), fwd(0), bwd(0), bwd(2), bwd(0)],
        out_specs=(pl.BlockSpec((rows, BRANCH), lambda b, c: (b * nc + c, 0)),
                   pl.BlockSpec((rows, BRANCH), lambda b, c: (b * nc + nc - 1 - c, 0))),
        scratch_shapes=[pltpu.VMEM((2, N_HEADS, HEAD_DIM, HEAD_DIM), F32)],
        compiler_params=_params(("parallel", "arbitrary")),
        name="hgrn2_scan",
    )(logits, qzz, qzz, ig, qzz, qzz, ig)


def _merge_kernel(x_ref, ao0_ref, ao1_ref, ao2_ref, l0_ref, l1_ref, l2_ref, of_ref, ob_ref, bg_ref,
                  cb_ref, cbp_ref, cbn_ref, h_ref, wgate_ref, convw_ref, ng_ref, wbr_ref, wout_ref, gpost_ref,
                  o_ref, *, tm, seq):
    i = pl.program_id(0)
    h = h_ref[...]

    ls = (l0_ref[...], l1_ref[...], l2_ref[...])
    aos = (ao0_ref, ao1_ref, ao2_ref)
    lmax = jnp.maximum(jnp.maximum(ls[0], ls[1]), ls[2])
    es = [jnp.exp(l - lmax) for l in ls]
    inv = 1.0 / (es[0] + es[1] + es[2])
    a_parts = []
    for hd in range(N_HEADS):
        hs = slice(hd * HEAD_DIM, (hd + 1) * HEAD_DIM)
        acc = None
        for g in range(N_GROUPS):
            w = (es[g] * inv)[:, hd:hd + 1]
            term = w * aos[g][:, hs].astype(F32)
            acc = term if acc is None else acc + term
        a_parts.append(acc)
    o_a = jnp.concatenate(a_parts, axis=1).astype(BF16)

    b_parts = []
    for hd in range(N_HEADS):
        hs = slice(hd * HEAD_DIM, (hd + 1) * HEAD_DIM)
        o = of_ref[:, hs] + ob_ref[:, hs]
        b_parts.append(_rms(o, ng_ref[:, hs]))
    gate = bg_ref[...].astype(F32)
    o_b = (jnp.concatenate(b_parts, axis=1) * (gate * _sigmoid(gate))).astype(BF16)

    cu = cb_ref[:, 0:BRANCH].astype(F32)
    keep_prev = jnp.where((i * tm) % seq == 0, 0.0, 1.0)
    keep_next = jnp.where(((i + 1) * tm) % seq == 0, 0.0, 1.0)
    halo_rows = cbp_ref.shape[0]
    prev_row = keep_prev * cbp_ref[halo_rows - 1:halo_rows, 0:BRANCH].astype(F32)
    next_row = keep_next * cbn_ref[0:1, 0:BRANCH].astype(F32)
    rid = lax.broadcasted_iota(jnp.int32, cu.shape, 0)
    before = jnp.where(rid == 0, prev_row, pltpu.roll(cu, 1, 0))
    after = jnp.where(rid == tm - 1, next_row, pltpu.roll(cu, tm - 1, 0))
    conv = convw_ref[0:1, :] * before + convw_ref[1:2, :] * cu + convw_ref[2:3, :] * after
    o_c = (cb_ref[:, BRANCH:2 * BRANCH].astype(F32) * conv).astype(BF16)

    merged = None
    for n, o_n in enumerate((o_a, o_b, o_c)):
        up = _dot(o_n, wbr_ref[n])
        term = _sigmoid(_dot(h, wgate_ref[:, n * D_MODEL:(n + 1) * D_MODEL])) * up
        merged = term if merged is None else merged + term
    y = _dot(merged.astype(BF16), wout_ref[...])
    o_ref[...] = x_ref[...] + _rms(y, gpost_ref[...])


def _merge(x, aos, lses, o_f, o_b, ig, cb, h, w_gate, conv_w, norm_g, w_branch, w_out, g_post, seq, tm):
    t = x.shape[0]
    halo = 16
    n_halo = t // halo
    per = tm // halo

    def rows(width, colblk=0):
        return pl.BlockSpec((tm, width), lambda i: (i, colblk))

    def full(shape):
        return _resident(shape, lambda i: (0,) * len(shape))

    in_specs = ([rows(D_MODEL)] + [rows(BRANCH)] * 3 + [rows(HEAD_DIM)] * 3 + [rows(BRANCH)] * 2
                + [rows(BRANCH, 1), rows(2 * BRANCH),
                   pl.BlockSpec((halo, 2 * BRANCH), lambda i: (jnp.maximum(i * per - 1, 0), 0)),
                   pl.BlockSpec((halo, 2 * BRANCH), lambda i: (jnp.minimum((i + 1) * per, n_halo - 1), 0)),
                   rows(D_MODEL), full((D_MODEL, G_COLS)), full((3, BRANCH)), full((1, BRANCH)),
                   full((3, BRANCH, D_MODEL)), full((D_MODEL, D_MODEL)), full((1, D_MODEL))])
    return pl.pallas_call(
        functools.partial(_merge_kernel, tm=tm, seq=seq),
        out_shape=jax.ShapeDtypeStruct((t, D_MODEL), F32),
        grid=(t // tm,),
        in_specs=in_specs,
        out_specs=rows(D_MODEL),
        compiler_params=_params(("parallel",)),
        name="merge_branches",
    )(x, *aos, *lses, o_f, o_b, ig, cb, cb, cb, h, w_gate, conv_w, norm_g, w_branch, w_out, g_post)


def _swiglu(h, wg, wu, wd):
    g = _dot(h, wg)
    u = _dot(h, wu)
    return _dot((g * _sigmoid(g) * u).astype(BF16), wd)


def _embed_tail(x, p_ref, gin_ref, gpost_ref, wgate_ref, wproj_ref):
    gate = _sigmoid(_dot(_rms(x, gin_ref[...]).astype(BF16), wgate_ref[...]))
    e = _dot(p_ref[...].astype(BF16), wproj_ref[...]) * gate
    return x + _rms(e, gpost_ref[...])


def _embed_operands(embed, tm, index):
    pd = embed[0].shape[1]
    const = lambda *_: (0, 0)
    return [pl.BlockSpec((tm, pd), index), pl.BlockSpec((1, D_MODEL), const), pl.BlockSpec((1, D_MODEL), const),
            _resident((D_MODEL, D_MODEL), const), _resident((pd, D_MODEL), const)]


def _ffn_kernel(x_ref, gpre_ref, gpost_ref, wg_ref, wu_ref, wd_ref, p_ref, ein_ref, epost_ref, egate_ref, eproj_ref,
                o_ref):
    x = x_ref[...]
    h = _rms(x, gpre_ref[...]).astype(BF16)
    mixed = x + _rms(_swiglu(h, wg_ref[...], wu_ref[...], wd_ref[...]), gpost_ref[...])
    o_ref[...] = _embed_tail(mixed, p_ref, ein_ref, epost_ref, egate_ref, eproj_ref)


def _ffn(x, g_pre, g_post, wg, wu, wd, embed, tm):
    t = x.shape[0]
    const = lambda i: (0, 0)
    return pl.pallas_call(
        _ffn_kernel,
        out_shape=jax.ShapeDtypeStruct((t, D_MODEL), F32),
        grid=(t // tm,),
        in_specs=[pl.BlockSpec((tm, D_MODEL), lambda i: (i, 0)),
                  pl.BlockSpec((1, D_MODEL), const), pl.BlockSpec((1, D_MODEL), const),
                  _resident(wg.shape, const), _resident(wu.shape, const), _resident(wd.shape, const)]
                 + _embed_operands(embed, tm, lambda i: (i, 0)),
        out_specs=pl.BlockSpec((tm, D_MODEL), lambda i: (i, 0)),
        compiler_params=_params(("parallel",)),
        name="dense_swiglu",
    )(x, g_pre, g_post, wg, wu, wd, *embed)


PIECES = D_MODEL // 2 // HEAD_DIM
SC_WINDOW = 128
EXPERT_TILE = 512


def _pack_rows(vals):
    bits = lax.bitcast_convert_type(vals.astype(BF16).astype(F32), jnp.uint32)
    half = D_MODEL // 2
    word = bits[:, :half] | (bits[:, half:] >> 16)
    return [lax.bitcast_convert_type(word[:, c * HEAD_DIM:(c + 1) * HEAD_DIM], jnp.int32) for c in range(PIECES)]


def _unpack_rows(piece_refs):
    words = [lax.bitcast_convert_type(r[...], jnp.uint32) for r in piece_refs]
    hi = [lax.bitcast_convert_type(w & jnp.uint32(0xFFFF0000), F32) for w in words]
    lo = [lax.bitcast_convert_type(w << 16, F32) for w in words]
    return jnp.concatenate(hi + lo, axis=1)


def _route_kernel(x_ref, gpre_ref, rhi_ref, rlo_ref, *refs, n_experts):
    piece_refs = refs[:PIECES]
    rec_ref, rank_ref, count_ref, carry_ref = refs[PIECES:]
    i = pl.program_id(0)
    tm = x_ref.shape[0]

    @pl.when(i == 0)
    def _():
        carry_ref[...] = jnp.zeros_like(carry_ref)

    hf = _rms(x_ref[...], gpre_ref[...])
    for r, piece in zip(piece_refs, _pack_rows(hf)):
        r[...] = piece
    h_hi = hf.astype(BF16)
    h_lo = (hf - h_hi.astype(F32)).astype(BF16)
    logits = _dot(h_hi, rhi_ref[...]) + _dot(h_hi, rlo_ref[...]) + _dot(h_lo, rhi_ref[...])
    lane = lax.broadcasted_iota(jnp.int32, logits.shape, 1).astype(F32)
    logits = jnp.where(lane < n_experts, logits, NEG_INF)
    m1 = jnp.max(logits, axis=1, keepdims=True)
    i1 = jnp.min(jnp.where(logits == m1, lane, 1e9), axis=1, keepdims=True)
    rest = jnp.where(lane == i1, NEG_INF, logits)
    m2 = jnp.max(rest, axis=1, keepdims=True)
    i2 = jnp.min(jnp.where(rest == m2, lane, 1e9), axis=1, keepdims=True)
    ex = jnp.exp(m2 - m1)
    w1 = 1.0 / (1.0 + ex)
    rec = jnp.where(lane == 0, i1, jnp.where(lane == 1, i2, jnp.where(lane == 2, w1, jnp.where(lane == 3, ex * w1, 0.0))))
    rec_ref[...] = rec
    rec_t = rec.T
    e1 = rec_t[0:1, :]
    e2 = rec_t[1:2, :]
    sub = lax.broadcasted_iota(jnp.int32, (8, tm), 0).astype(F32)
    oh1 = jnp.where(sub == e1, 1.0, 0.0)
    oh2 = jnp.where(sub == e2, 1.0, 0.0)
    chosen = oh1 + oh2
    src = lax.broadcasted_iota(jnp.int32, (tm, tm), 0)
    dst = lax.broadcasted_iota(jnp.int32, (tm, tm), 1)
    before = jnp.where(src < dst, 1.0, 0.0).astype(BF16)
    rank = _dot(chosen.astype(BF16), before) + carry_ref[:, 0:1]
    rank1 = jnp.sum(oh1 * rank, axis=0, keepdims=True)
    rank2 = jnp.sum(oh2 * rank, axis=0, keepdims=True)
    rank_ref[...] = jnp.where(sub == 0, rank1, jnp.where(sub == 1, rank2, jnp.where(sub == 2, e1, jnp.where(sub == 3, e2, 0.0))))
    carry_ref[...] = carry_ref[...] + jnp.sum(chosen, axis=1, keepdims=True)
    count_ref[...] = carry_ref[...]


def _route(x, g_pre, r_hi, r_lo, n_experts, tm):
    t = x.shape[0]
    assert n_experts <= 8
    piece = pl.BlockSpec((tm, HEAD_DIM), lambda i: (i, 0))
    return pl.pallas_call(
        functools.partial(_route_kernel, n_experts=n_experts),
        out_shape=(*[jax.ShapeDtypeStruct((t, HEAD_DIM), jnp.int32)] * PIECES,
                   jax.ShapeDtypeStruct((t, HEAD_DIM), F32), jax.ShapeDtypeStruct((8, t), F32),
                   jax.ShapeDtypeStruct((8, HEAD_DIM), F32)),
        grid=(t // tm,),
        in_specs=[pl.BlockSpec((tm, D_MODEL), lambda i: (i, 0)), pl.BlockSpec((1, D_MODEL), lambda i: (0, 0)),
                  pl.BlockSpec((D_MODEL, HEAD_DIM), lambda i: (0, 0)), pl.BlockSpec((D_MODEL, HEAD_DIM), lambda i: (0, 0))],
        out_specs=(*[piece] * PIECES, piece, pl.BlockSpec((8, tm), lambda i: (0, i)),
                   pl.BlockSpec((8, HEAD_DIM), lambda i: (0, 0))),
        scratch_shapes=[pltpu.VMEM((8, HEAD_DIM), F32)],
        compiler_params=_params(("arbitrary",)),
        name="moe_route",
    )(x, g_pre, r_hi, r_lo)


def _slot_kernel(rank_ref, start_ref, s1_ref, s2_ref):
    tm = rank_ref.shape[1]
    sub = lax.broadcasted_iota(jnp.int32, (8, tm), 0).astype(F32)
    start = start_ref[:, 0:1]
    rows = rank_ref[...]
    for choice, out in ((0, s1_ref), (1, s2_ref)):
        base = jnp.sum(jnp.where(sub == rows[2 + choice:3 + choice, :], start, 0.0), axis=0, keepdims=True)
        out[...] = (base + rows[choice:choice + 1, :]).astype(jnp.int32)


def _slots(ranks, starts, tm):
    t = ranks.shape[1]
    row = pl.BlockSpec((1, tm), lambda i: (0, i))
    return pl.pallas_call(
        _slot_kernel,
        out_shape=(jax.ShapeDtypeStruct((1, t), jnp.int32),) * 2,
        grid=(t // tm,),
        in_specs=[pl.BlockSpec((8, tm), lambda i: (0, i)), pl.BlockSpec((8, HEAD_DIM), lambda i: (0, 0))],
        out_specs=(row, row),
        compiler_params=_params(("parallel",)),
        name="moe_slots",
    )(ranks, starts)


def _sc_mesh():
    return plsc.VectorSubcoreMesh(core_axis_name="core", subcore_axis_name="subcore")


def _sc_scatter_rows(srcs, idxs, n_rows):
    ns, nk = len(srcs), len(idxs)
    half = srcs[0].shape[0] // SC_WINDOW // 2

    @functools.partial(pl.kernel, mesh=_sc_mesh(), scratch_types=[],
                       out_type=tuple(jax.ShapeDtypeStruct((n_rows, HEAD_DIM), srcs[0].dtype) for _ in range(ns)))
    def scatter(*refs):
        out_hbm = refs[ns + nk:]

        def body(*blocks):
            for idx in blocks[ns:]:
                for c in range(ns):
                    pltpu.sync_copy(blocks[c], out_hbm[c].at[idx.at[0]])

        pltpu.emit_pipeline(
            body, grid=(2, half),
            in_specs=[pl.BlockSpec((SC_WINDOW, HEAD_DIM), lambda i, j: (i * half + j, 0)) for _ in range(ns)]
                     + [pl.BlockSpec((1, SC_WINDOW), lambda i, j: (0, i * half + j)) for _ in range(nk)],
            out_specs=[],
            core_axis_name=("core", "subcore"),
            dimension_semantics=(pltpu.PARALLEL, pltpu.PARALLEL),
        )(*refs[:ns + nk])

    return scatter(*srcs, *idxs)


def _sc_gather_rows(tables, idx):
    nt = len(tables)
    n = idx.shape[1]
    half = n // SC_WINDOW // 2

    @functools.partial(pl.kernel, mesh=_sc_mesh(), scratch_types=[],
                       out_type=tuple(jax.ShapeDtypeStruct((n, HEAD_DIM), tables[0].dtype) for _ in range(nt)))
    def gather(*refs):
        table_hbm = refs[:nt]

        def body(idx_blk, *out_blks):
            for c in range(nt):
                pltpu.sync_copy(table_hbm[c].at[idx_blk.at[0]], out_blks[c])

        pltpu.emit_pipeline(
            body, grid=(2, half),
            in_specs=[pl.BlockSpec((1, SC_WINDOW), lambda i, j: (0, i * half + j))],
            out_specs=[pl.BlockSpec((SC_WINDOW, HEAD_DIM), lambda i, j: (i * half + j, 0)) for _ in range(nt)],
            core_axis_name=("core", "subcore"),
            dimension_semantics=(pltpu.PARALLEL, pltpu.PARALLEL),
        )(refs[nt], *refs[nt + 1:])

    return gather(*tables, idx)


def _expert_kernel(tile_expert_ref, n_used_ref, *refs):
    x_refs = refs[:PIECES]
    wg_ref, wu_ref, wd_ref = refs[PIECES:PIECES + 3]
    y_refs = refs[PIECES + 3:]

    @pl.when(pl.program_id(0) < n_used_ref[0])
    def _():
        h = _unpack_rows(x_refs).astype(BF16)
        y = _swiglu(h, wg_ref[0], wu_ref[0], wd_ref[0])
        for r, piece in zip(y_refs, _pack_rows(y)):
            r[...] = piece


def _expert_ffn(xs, tile_expert, n_used, wg, wu, wd):
    n_rows = xs[0].shape[0]
    rows = pl.BlockSpec((EXPERT_TILE, HEAD_DIM), lambda i, te, nu: (jnp.minimum(i, nu[0] - 1), 0))
    out_rows = pl.BlockSpec((EXPERT_TILE, HEAD_DIM), lambda i, te, nu: (i, 0))
    expert = lambda i, te, nu: (te[i], 0, 0)
    grid_spec = pltpu.PrefetchScalarGridSpec(
        num_scalar_prefetch=2,
        grid=(n_rows // EXPERT_TILE,),
        in_specs=[rows] * PIECES + [_resident((1,) + wg.shape[1:], expert), _resident((1,) + wu.shape[1:], expert),
                                    _resident((1,) + wd.shape[1:], expert)],
        out_specs=[out_rows] * PIECES,
    )
    return pl.pallas_call(
        _expert_kernel,
        out_shape=[jax.ShapeDtypeStruct((n_rows, HEAD_DIM), jnp.int32)] * PIECES,
        grid_spec=grid_spec,
        compiler_params=_params(("arbitrary",)),
        name="moe_expert_swiglu",
    )(tile_expert, n_used, *xs, wg, wu, wd)


def _combine_kernel(x_ref, rec_ref, gpost_ref, p_ref, ein_ref, epost_ref, egate_ref, eproj_ref, *refs):
    y1 = _unpack_rows(refs[:PIECES])
    y2 = _unpack_rows(refs[PIECES:2 * PIECES])
    o_ref = refs[2 * PIECES]
    rec = rec_ref[...]
    y = rec[:, 2:3] * y1 + rec[:, 3:4] * y2
    mixed = x_ref[...] + _rms(y, gpost_ref[...])
    o_ref[...] = _embed_tail(mixed, p_ref, ein_ref, epost_ref, egate_ref, eproj_ref)


def _combine(x, rec, g_post, embed, y1, y2, tm):
    t = x.shape[0]
    piece = pl.BlockSpec((tm, HEAD_DIM), lambda i: (i, 0))
    return pl.pallas_call(
        _combine_kernel,
        out_shape=jax.ShapeDtypeStruct((t, D_MODEL), F32),
        grid=(t // tm,),
        in_specs=[pl.BlockSpec((tm, D_MODEL), lambda i: (i, 0)), piece, pl.BlockSpec((1, D_MODEL), lambda i: (0, 0))]
                 + _embed_operands(embed, tm, lambda i: (i, 0)) + [piece] * (2 * PIECES),
        out_specs=pl.BlockSpec((tm, D_MODEL), lambda i: (i, 0)),
        compiler_params=_params(("parallel",)),
        name="moe_combine",
    )(x, rec, g_post, *embed, *y1, *y2)


def _moe(x, g_pre, g_post, router, wg, wu, wd, embed, tm):
    t = x.shape[0]
    n_experts = router.shape[1]
    router = jnp.pad(router, ((0, 0), (0, HEAD_DIM - n_experts)))
    r_hi = router.astype(BF16)
    r_lo = (router - r_hi.astype(F32)).astype(BF16)
    *h_pieces, rec, ranks, counts = _route(x, g_pre, r_hi, r_lo, n_experts, tm)

    counts = counts[:n_experts, 0].astype(jnp.int32)
    tiles = (counts + EXPERT_TILE - 1) // EXPERT_TILE
    tile_end = jnp.cumsum(tiles)
    starts = ((tile_end - tiles) * EXPERT_TILE).astype(F32)
    starts = jnp.broadcast_to(jnp.pad(starts, (0, 8 - n_experts))[:, None], (8, HEAD_DIM))
    n_rows = TOP_K * t + n_experts * EXPERT_TILE
    n_tiles = n_rows // EXPERT_TILE
    n_used = tile_end[-1:]
    tile_ids = jnp.minimum(jnp.arange(n_tiles, dtype=jnp.int32), n_used[0] - 1)
    tile_expert = jnp.sum(tile_ids[:, None] >= tile_end[None, :], axis=1).astype(jnp.int32)

    slot1, slot2 = _slots(ranks, starts, min(t, 8192))
    xs = []
    for c in range(0, PIECES, 2):
        xs += _sc_scatter_rows(h_pieces[c:c + 2], [slot1, slot2], n_rows)
    ys = _expert_ffn(xs, tile_expert, n_used, wg, wu, wd)
    gathered = []
    for slot in (slot1, slot2):
        rows = []
        for c in range(0, PIECES, 2):
            rows += _sc_gather_rows(ys[c:c + 2], slot)
        gathered.append(rows)
    return _combine(x, rec, g_post, embed, gathered[0], gathered[1], tm)


def kernel(x, p, positions, w_in, conv_w, hgrn_lb_logits, hgrn_norm_g, w_branch, w_out, g_mix_pre, g_mix_post, g_ffn_pre, g_ffn_post, dense_w_gate, dense_w_up, dense_w_down, moe_router, moe_w_gate, moe_w_up, moe_w_down, ple_w_proj, ple_w_gate, ple_g_in, ple_g_post):
    batch, seq, _ = x.shape
    depth = w_in.shape[0]
    t = batch * seq
    tm = min(512, seq)
    tm_merge = min(512, seq)
    tm_ffn = min(512, seq)

    xt = x.reshape(t, D_MODEL)
    cos, sin = _rope_tables(positions.astype(F32).reshape(t, 1), tm)
    row = lambda a: a.reshape(1, -1)

    for i in range(depth):
        w_i = w_in[i].astype(BF16)
        g_pre = row(g_mix_pre[i])
        *qkvs, h = _inproj_attention(xt, g_pre, w_i[:, :A_COLS], cos, sin, batch, seq, tm)
        side_cols = ((A_COLS, A_COLS + 3 * BRANCH), (A_COLS + 3 * BRANCH, A_COLS + B_COLS),
                     (A_COLS + B_COLS, A_COLS + B_COLS + C_COLS))
        aos, lses, sides = zip(*[_attention_group(qkvs[g], h, w_i[:, lo:hi], g == 2, batch, seq, g)
                                 for g, (lo, hi) in enumerate(side_cols)])
        qzz, ig, cb = sides
        o_f, o_b = _hgrn(qzz, ig, hgrn_lb_logits, batch, seq, i)
        xt = _merge(xt, aos, lses, o_f, o_b, ig, cb, h, w_i[:, A_COLS + B_COLS + C_COLS:], conv_w[i],
                    row(hgrn_norm_g[i]), w_branch[i].astype(BF16), w_out[i].astype(BF16), row(g_mix_post[i]),
                    seq, tm_merge)

        embed = (p[i].reshape(t, -1), row(ple_g_in[i]), row(ple_g_post[i]),
                 ple_w_gate[i].astype(BF16), ple_w_proj[i].astype(BF16))
        j = i // 2
        if i % 2 == 0:
            xt = _ffn(xt, row(g_ffn_pre[i]), row(g_ffn_post[i]), dense_w_gate[j].astype(BF16),
                      dense_w_up[j].astype(BF16), dense_w_down[j].astype(BF16), embed, tm_ffn)
        else:
            xt = _moe(xt, row(g_ffn_pre[i]), row(g_ffn_post[i]), moe_router[j], moe_w_gate[j].astype(BF16),
                      moe_w_up[j].astype(BF16), moe_w_down[j].astype(BF16), embed, tm)

    return xt.reshape(batch, seq, D_MODEL)
```

```python
import functools
import math

import jax
import jax.numpy as jnp
from jax import lax
from jax.experimental import pallas as pl
from jax.experimental.pallas import tpu as pltpu
from jax.experimental.pallas import tpu_sc as plsc

D_MODEL = 1024
EPS = 1e-6
NEG_INF = -1e30

HEAD_DIM = 128
N_HEADS = 4
BRANCH = N_HEADS * HEAD_DIM
DIL_GROUPS = ((128, 1), (512, 4), (2048, 16))
N_GROUPS = len(DIL_GROUPS)
BLK = 64
ROT_DIM = HEAD_DIM // 4
ROPE_THETA = 500000.0
TOP_K = 2

A_COLS = N_GROUPS * 3 * BRANCH
B_COLS = 5 * BRANCH
C_COLS = 3 * BRANCH
G_COLS = 3 * D_MODEL

BF16 = jnp.bfloat16
F32 = jnp.float32

VMEM_LIMIT = 56 * 1024 * 1024


def _params(sem):
    return pltpu.CompilerParams(dimension_semantics=sem, vmem_limit_bytes=VMEM_LIMIT)


def _resident(shape, index_map):
    return pl.BlockSpec(shape, index_map, pipeline_mode=pl.Buffered(1))


def _rms(xf, g):
    return xf * lax.rsqrt(jnp.mean(xf * xf, axis=-1, keepdims=True) + EPS) * g


def _sigmoid(z):
    return 0.5 * jnp.tanh(0.5 * z) + 0.5


def _sigmoid_rel(z):
    return 1.0 / (1.0 + jnp.exp(-z))


def _dot(a, b):
    return jnp.dot(a, b, preferred_element_type=F32)


def _dot_nt(a, b):
    return lax.dot_general(a, b, (((1,), (1,)), ((), ())), preferred_element_type=F32)


def _dot_tn(a, b):
    return lax.dot_general(a, b, (((0,), (0,)), ((), ())), preferred_element_type=F32)


def _rope_table_kernel(pos_ref, freq_ref, sign_ref, cos_ref, sin_ref):
    ang = pos_ref[...] * freq_ref[...]
    cos_ref[...] = jnp.cos(ang)
    sin_ref[...] = jnp.sin(ang) * sign_ref[...]


def _rope_tables(pos_col, tm):
    t = pos_col.shape[0]
    half = ROT_DIM // 2
    inv_freq = ROPE_THETA ** (-jnp.arange(0, ROT_DIM, 2, dtype=F32) / ROT_DIM)
    freq = jnp.concatenate([inv_freq, inv_freq, jnp.zeros((HEAD_DIM - ROT_DIM,), F32)])[None, :]
    sign = jnp.concatenate([-jnp.ones((half,), F32), jnp.ones((HEAD_DIM - half,), F32)])[None, :]
    row = pl.BlockSpec((1, HEAD_DIM), lambda i: (0, 0))
    tab = pl.BlockSpec((tm, HEAD_DIM), lambda i: (i, 0))
    return pl.pallas_call(
        _rope_table_kernel,
        out_shape=(jax.ShapeDtypeStruct((t, HEAD_DIM), F32),) * 2,
        grid=(t // tm,),
        in_specs=[pl.BlockSpec((tm, 1), lambda i: (i, 0)), row, row],
        out_specs=(tab, tab),
        compiler_params=_params(("parallel",)),
        name="rope_tables",
    )(pos_col, freq, sign)


def _rope_head(th, cos, sin):
    half = ROT_DIM // 2
    lane = lax.broadcasted_iota(jnp.int32, th.shape, 1)
    swapped = jnp.where(lane < half, pltpu.roll(th, HEAD_DIM - half, 1), pltpu.roll(th, half, 1))
    return th * cos + swapped * sin


def _inproj_attn_kernel(x_ref, g_ref, w_ref, cos_ref, sin_ref, o0_ref, o1_ref, o2_ref, h_ref, stage_ref):
    hf = _rms(x_ref[...], g_ref[...])
    h_ref[...] = hf.astype(BF16)
    n_pieces = D_MODEL // HEAD_DIM
    for pc in range(n_pieces):
        stage_ref[pc] = hf[:, pc * HEAD_DIM:(pc + 1) * HEAD_DIM]
    scale = 1.0 / math.sqrt(HEAD_DIM)
    tm = x_ref.shape[0]
    for grp, o_ref in enumerate((o0_ref, o1_ref, o2_ref)):
        dil = DIL_GROUPS[grp][1]
        sub = tm // dil
        if dil == 1:
            h, cos, sin = hf.astype(BF16), cos_ref[...], sin_ref[...]
        else:
            residue = lambda ref, *lead: jnp.concatenate(
                [ref[(*lead, pl.ds(r, sub, stride=dil), slice(None))] for r in range(dil)], axis=0)
            h = jnp.concatenate([residue(stage_ref, pc) for pc in range(n_pieces)], axis=1).astype(BF16)
            cos, sin = residue(cos_ref), residue(sin_ref)
        for part in range(3):
            c = grp * 3 + part
            t = _dot(h, w_ref[:, c * BRANCH:(c + 1) * BRANCH])
            if part < 2:
                heads = []
                for hd in range(N_HEADS):
                    r = _rope_head(t[:, hd * HEAD_DIM:(hd + 1) * HEAD_DIM], cos, sin)
                    heads.append(r * scale if part == 0 else r)
                t = jnp.concatenate(heads, axis=1)
            t = t.astype(BF16)
            for r in range(dil):
                o_ref[0, r, :, part * BRANCH:(part + 1) * BRANCH] = t[r * sub:(r + 1) * sub, :]


def _inproj_attention(x, g, w, cos, sin, batch, seq, tm):
    per_seq = seq // tm
    rows = lambda width: pl.BlockSpec((tm, width), lambda b, j: (b * per_seq + j, 0))
    out_shape = tuple(jax.ShapeDtypeStruct((batch, dil, seq // dil, 3 * BRANCH), BF16) for _, dil in DIL_GROUPS)
    out_shape += (jax.ShapeDtypeStruct((batch * seq, D_MODEL), BF16),)
    out_specs = tuple(pl.BlockSpec((1, dil, tm // dil, 3 * BRANCH), lambda b, j: (b, 0, j, 0))
                      for _, dil in DIL_GROUPS) + (rows(D_MODEL),)
    return pl.pallas_call(
        _inproj_attn_kernel, out_shape=out_shape, grid=(batch, per_seq),
        in_specs=[rows(D_MODEL), pl.BlockSpec((1, D_MODEL), lambda b, j: (0, 0)),
                  _resident((D_MODEL, A_COLS), lambda b, j: (0, 0)), rows(HEAD_DIM), rows(HEAD_DIM)],
        out_specs=out_specs,
        scratch_shapes=[pltpu.VMEM((D_MODEL // HEAD_DIM, tm, HEAD_DIM), F32)],
        compiler_params=_params(("parallel", "parallel")), name="inproj_attention",
    )(x, g, w, cos, sin)


def _attn_kernel(q_ref, k_ref, kp_ref, kn_ref, v_ref, vp_ref, vn_ref, h_ref, w_ref, o_ref, lse_ref, side_ref,
                 o_stage, lse_stage, *, dil, nj, n_steps, conv_side):
    n = pl.program_id(1)
    part_rows = h_ref.shape[0] // N_HEADS

    def side_projection(part):
        rs = slice(part * part_rows, (part + 1) * part_rows)
        h = h_ref[rs, :]
        if conv_side:
            u = _dot(h, w_ref[:, 0:BRANCH])
            b_gate = _dot(h, w_ref[:, BRANCH:2 * BRANCH])
            c_gate = _dot(h, w_ref[:, 2 * BRANCH:3 * BRANCH])
            side_ref[rs, 0:BRANCH] = (c_gate * u).astype(BF16)
            side_ref[rs, BRANCH:2 * BRANCH] = b_gate.astype(BF16)
        else:
            for c in range(w_ref.shape[1] // BRANCH):
                cols = slice(c * BRANCH, (c + 1) * BRANCH)
                side_ref[rs, cols] = _dot(h, w_ref[:, cols]).astype(BF16)

    col_lo = jnp.where(n > 0, 0, BLK)
    col_hi = jnp.where(n < n_steps - 1, 3 * BLK, 2 * BLK)
    row = lax.broadcasted_iota(jnp.int32, (BLK, 3 * BLK), 0)
    col = lax.broadcasted_iota(jnp.int32, (BLK, 3 * BLK), 1)
    band = jnp.abs(col - BLK - row) <= BLK
    masks = {}
    for j in range(nj):
        m = band
        if j == 0:
            m = m & (col >= col_lo)
        if j == nj - 1:
            m = m & (col < col_hi)
        masks[j] = m
    pairs = [(r, j) for r in range(dil) for j in range(nj)]
    lane = lax.broadcasted_iota(jnp.int32, (len(pairs), BLK, HEAD_DIM), 2)

    def window(own_ref, prev_ref, next_ref, r, j, hs):
        parts = []
        for w in (j - 1, j, j + 1):
            if w < 0:
                parts.append(prev_ref[0, r, :, hs])
            elif w >= nj:
                parts.append(next_ref[0, r, :, hs])
            else:
                parts.append(own_ref[0, r, w * BLK:(w + 1) * BLK, hs])
        return jnp.concatenate(parts, axis=0)

    lse_tiles = jnp.zeros((len(pairs), BLK, HEAD_DIM), F32)
    for hd in range(N_HEADS):
        side_projection(hd)
        hs = slice(hd * HEAD_DIM, (hd + 1) * HEAD_DIM)
        scores = [_dot_nt(q_ref[0, r, j * BLK:(j + 1) * BLK, hs], window(k_ref, kp_ref, kn_ref, r, j, hs))
                  for r, j in pairs]
        s = jnp.stack([jnp.where(masks[j], sc, NEG_INF) for (r, j), sc in zip(pairs, scores)])
        m = jnp.max(s, axis=2, keepdims=True)
        p = jnp.exp(s - m)
        l = jnp.sum(p, axis=2, keepdims=True)
        pb = p.astype(BF16)
        outs = jnp.stack([_dot(pb[i], window(v_ref, vp_ref, vn_ref, r, j, hs)) for i, (r, j) in enumerate(pairs)])
        outs = outs * (1.0 / l)
        lse_tiles = jnp.where(lane == hd, m + jnp.log(l), lse_tiles)
        for i, (r, j) in enumerate(pairs):
            o_stage[hd, pl.ds(j * BLK * dil + r, BLK, stride=dil), :] = outs[i]
        o_ref[:, hs] = o_stage[hd].astype(o_ref.dtype)
    for i, (r, j) in enumerate(pairs):
        lse_stage[pl.ds(j * BLK * dil + r, BLK, stride=dil), :] = lse_tiles[i]
    lse_ref[...] = lse_stage[...]


def _attention_group(qkv, h, w_side, conv_side, batch, seq, group):
    _, dil = DIL_GROUPS[group]
    t = batch * seq
    sub_len = seq // dil
    rows_per_step = min(seq, 16 * BLK)
    nj = rows_per_step // (dil * BLK)
    n_steps = seq // rows_per_step
    n_blk = sub_len // BLK
    sub_rows = rows_per_step // dil

    def own(part):
        return pl.BlockSpec((1, dil, sub_rows, BRANCH), lambda b, n: (b, 0, n, part))

    def prev(part):
        return pl.BlockSpec((1, dil, BLK, BRANCH), lambda b, n: (b, 0, jnp.maximum(n * nj - 1, 0), part))

    def nxt(part):
        return pl.BlockSpec((1, dil, BLK, BRANCH), lambda b, n: (b, 0, jnp.minimum((n + 1) * nj, n_blk - 1), part))

    side_cols = 2 * BRANCH if conv_side else w_side.shape[1]
    step_rows = lambda width: pl.BlockSpec((rows_per_step, width), lambda b, n: (b * n_steps + n, 0))
    return pl.pallas_call(
        functools.partial(_attn_kernel, dil=dil, nj=nj, n_steps=n_steps, conv_side=conv_side),
        out_shape=(jax.ShapeDtypeStruct((t, BRANCH), BF16), jax.ShapeDtypeStruct((t, HEAD_DIM), F32),
                   jax.ShapeDtypeStruct((t, side_cols), BF16)),
        grid=(batch, n_steps),
        in_specs=[own(0), own(1), prev(1), nxt(1), own(2), prev(2), nxt(2), step_rows(D_MODEL),
                  _resident(w_side.shape, lambda b, n: (0, 0))],
        out_specs=(step_rows(BRANCH), step_rows(HEAD_DIM), step_rows(side_cols)),
        scratch_shapes=[pltpu.VMEM((N_HEADS, rows_per_step, HEAD_DIM), F32), pltpu.VMEM((rows_per_step, HEAD_DIM), F32)],
        compiler_params=_params(("parallel", "parallel")),
        name=f"dilated_attention_g{group}",
    )(qkv, qkv, qkv, qkv, qkv, qkv, qkv, h, w_side)


def _hgrn_kernel(logit_ref, qf_ref, zf_ref, vf_ref, qb_ref, zb_ref, vb_ref, of_ref, ob_ref, state_ref,
                 *, layer, n_chunks):
    c = pl.program_id(1)

    @pl.when(c == 0)
    def _():
        state_ref[...] = jnp.zeros_like(state_ref)

    depth = logit_ref.shape[1]
    lbs = []
    for dirn in range(2):
        rows = [logit_ref[dirn, j:j + 1, :] for j in range(depth)]
        top = functools.reduce(jnp.maximum, rows)
        exps = [jnp.exp(rw - top) for rw in rows]
        lbs.append(sum(exps[1:layer + 1], jnp.zeros_like(top)) / sum(exps[1:], exps[0]))

    r = lax.broadcasted_iota(jnp.int32, (BLK, BLK), 0)
    cc = lax.broadcasted_iota(jnp.int32, (BLK, BLK), 1)

    def bcast_rows(per_chunk):
        return jnp.concatenate([jnp.broadcast_to(v, (BLK, BRANCH)) for v in per_chunk], axis=0)

    for dirn, (q_ref, z_ref, v_ref, o_ref) in enumerate(
            ((qf_ref, zf_ref, vf_ref, of_ref), (qb_ref, zb_ref, vb_ref, ob_ref))):
        tri = (cc <= r) if dirn == 0 else (cc >= r)
        tri_b = jnp.where(tri, 1.0, 0.0).astype(BF16)
        last, mid_row = (BLK - 1, BLK // 2 - 1) if dirn == 0 else (0, BLK // 2)
        lb = lbs[dirn]
        q = q_ref[...].astype(F32)
        z = z_ref[...].astype(F32)
        v = v_ref[...]
        sg = _sigmoid_rel(z)
        logf = jnp.log(lb + (1.0 - lb) * sg)
        kk = (1.0 - lb) * (1.0 - sg)
        hi = logf.astype(BF16)
        lo = (logf - hi.astype(F32)).astype(BF16)
        chunks = [slice(g * BLK, (g + 1) * BLK) for g in range(n_chunks)]
        cums = [_dot(tri_b, hi[ch]) + _dot(tri_b, lo[ch]) for ch in chunks]
        totals = [cm[last:last + 1, :] for cm in cums]
        mids = [cm[mid_row:mid_row + 1, :] for cm in cums]
        cum = jnp.concatenate(cums, axis=0)
        mid = bcast_rows(mids)
        qa = q * jnp.exp(cum - mid)
        ka = kk * jnp.exp(mid - cum)
        q_dec = (qa * bcast_rows([jnp.exp(md) for md in mids])).astype(BF16)
        k_dec = (ka * bcast_rows([jnp.exp(tt - md) for tt, md in zip(totals, mids)])).astype(BF16)
        qa = qa.astype(BF16)
        ka = ka.astype(BF16)
        tiles = [(g, hd) for g in range(n_chunks) for hd in range(N_HEADS)]
        sl = lambda g, hd: (slice(g * BLK, (g + 1) * BLK), slice(hd * HEAD_DIM, (hd + 1) * HEAD_DIM))
        atts = [jnp.where(tri, _dot_nt(qa[sl(g, hd)], ka[sl(g, hd)]), 0.0).astype(BF16) for g, hd in tiles]
        intra = {t: _dot(att, v[sl(*t)]) for t, att in zip(tiles, atts)}
        kvs = {t: _dot_tn(v[sl(*t)], k_dec[sl(*t)]) for t in tiles}
        decays = [jnp.exp(tt) for tt in totals]
        order = range(n_chunks) if dirn == 0 else range(n_chunks - 1, -1, -1)
        for hd in range(N_HEADS):
            hs = slice(hd * HEAD_DIM, (hd + 1) * HEAD_DIM)
            state_t = state_ref[dirn, hd]
            for g in order:
                o_ref[g * BLK:(g + 1) * BLK, hs] = intra[(g, hd)] + _dot_nt(q_dec[sl(g, hd)], state_t.astype(BF16))
                state_t = state_t * decays[g][:, hs] + kvs[(g, hd)]
            state_ref[dirn, hd] = state_t


def _hgrn(qzz, ig, logits, batch, seq, layer):
    t = qzz.shape[0]
    n_chunks = 4
    rows = n_chunks * BLK
    nc = seq // rows
    depth = logits.shape[1]

    def fwd(colblk):
        return pl.BlockSpec((rows, BRANCH), lambda b, c: (b * nc + c, colblk))

    def bwd(colblk):
        return pl.BlockSpec((rows, BRANCH), lambda b, c: (b * nc + nc - 1 - c, colblk))

    return pl.pallas_call(
        functools.partial(_hgrn_kernel, layer=layer, n_chunks=n_chunks),
        out_shape=(jax.ShapeDtypeStruct((t, BRANCH), F32),) * 2,
        grid=(batch, nc),
        in_specs=[pl.BlockSpec((2, depth, BRANCH), lambda b, c: (0, 0, 0)),
                  fwd(0), fwd(1), fwd(0), bwd(0), bwd(2), bwd(0)],
        out_specs=(pl.BlockSpec((rows, BRANCH), lambda b, c: (b * nc + c, 0)),
                   pl.BlockSpec((rows, BRANCH), lambda b, c: (b * nc + nc - 1 - c, 0))),
        scratch_shapes=[pltpu.VMEM((2, N_HEADS, HEAD_DIM, HEAD_DIM), F32)],
        compiler_params=_params(("parallel", "arbitrary")),
        name="hgrn2_scan",
    )(logits, qzz, qzz, ig, qzz, qzz, ig)


def _merge_kernel(x_ref, ao0_ref, ao1_ref, ao2_ref, l0_ref, l1_ref, l2_ref, of_ref, ob_ref, bg_ref,
                  cb_ref, cbp_ref, cbn_ref, h_ref, wgate_ref, convw_ref, ng_ref, wbr_ref, wout_ref, gpost_ref,
                  o_ref, *, tm, seq):
    i = pl.program_id(0)
    h = h_ref[...]
    gates = [_sigmoid(_dot(h, wgate_ref[:, n * D_MODEL:(n + 1) * D_MODEL])) for n in range(3)]

    ls = (l0_ref[...], l1_ref[...], l2_ref[...])
    aos = (ao0_ref, ao1_ref, ao2_ref)
    lmax = jnp.maximum(jnp.maximum(ls[0], ls[1]), ls[2])
    es = [jnp.exp(l - lmax) for l in ls]
    inv = 1.0 / (es[0] + es[1] + es[2])
    a_parts = []
    for hd in range(N_HEADS):
        hs = slice(hd * HEAD_DIM, (hd + 1) * HEAD_DIM)
        acc = None
        for g in range(N_GROUPS):
            w = (es[g] * inv)[:, hd:hd + 1]
            term = w * aos[g][:, hs].astype(F32)
            acc = term if acc is None else acc + term
        a_parts.append(acc)
    o_a = jnp.concatenate(a_parts, axis=1).astype(BF16)

    b_parts = []
    for hd in range(N_HEADS):
        hs = slice(hd * HEAD_DIM, (hd + 1) * HEAD_DIM)
        o = of_ref[:, hs] + ob_ref[:, hs]
        b_parts.append(_rms(o, ng_ref[:, hs]))
    gate = bg_ref[...].astype(F32)
    o_b = (jnp.concatenate(b_parts, axis=1) * (gate * _sigmoid(gate))).astype(BF16)

    cu = cb_ref[:, 0:BRANCH].astype(F32)
    keep_prev = jnp.where((i * tm) % seq == 0, 0.0, 1.0)
    keep_next = jnp.where(((i + 1) * tm) % seq == 0, 0.0, 1.0)
    halo_rows = cbp_ref.shape[0]
    prev_row = keep_prev * cbp_ref[halo_rows - 1:halo_rows, 0:BRANCH].astype(F32)
    next_row = keep_next * cbn_ref[0:1, 0:BRANCH].astype(F32)
    rid = lax.broadcasted_iota(jnp.int32, cu.shape, 0)
    before = jnp.where(rid == 0, prev_row, pltpu.roll(cu, 1, 0))
    after = jnp.where(rid == tm - 1, next_row, pltpu.roll(cu, tm - 1, 0))
    conv = convw_ref[0:1, :] * before + convw_ref[1:2, :] * cu + convw_ref[2:3, :] * after
    o_c = (cb_ref[:, BRANCH:2 * BRANCH].astype(F32) * conv).astype(BF16)

    merged = None
    for n, o_n in enumerate((o_a, o_b, o_c)):
        up = _dot(o_n, wbr_ref[n])
        term = gates[n] * up
        merged = term if merged is None else merged + term
    y = _dot(merged.astype(BF16), wout_ref[...])
    o_ref[...] = x_ref[...] + _rms(y, gpost_ref[...])


def _merge(x, aos, lses, o_f, o_b, ig, cb, h, w_gate, conv_w, norm_g, w_branch, w_out, g_post, seq, tm):
    t = x.shape[0]
    halo = 16
    n_halo = t // halo
    per = tm // halo

    def rows(width, colblk=0):
        return pl.BlockSpec((tm, width), lambda i: (i, colblk))

    def full(shape):
        return _resident(shape, lambda i: (0,) * len(shape))

    in_specs = ([rows(D_MODEL)] + [rows(BRANCH)] * 3 + [rows(HEAD_DIM)] * 3 + [rows(BRANCH)] * 2
                + [rows(BRANCH, 1), rows(2 * BRANCH),
                   pl.BlockSpec((halo, 2 * BRANCH), lambda i: (jnp.maximum(i * per - 1, 0), 0)),
                   pl.BlockSpec((halo, 2 * BRANCH), lambda i: (jnp.minimum((i + 1) * per, n_halo - 1), 0)),
                   rows(D_MODEL), full((D_MODEL, G_COLS)), full((3, BRANCH)), full((1, BRANCH)),
                   full((3, BRANCH, D_MODEL)), full((D_MODEL, D_MODEL)), full((1, D_MODEL))])
    return pl.pallas_call(
        functools.partial(_merge_kernel, tm=tm, seq=seq),
        out_shape=jax.ShapeDtypeStruct((t, D_MODEL), F32),
        grid=(t // tm,),
        in_specs=in_specs,
        out_specs=rows(D_MODEL),
        compiler_params=_params(("parallel",)),
        name="merge_branches",
    )(x, *aos, *lses, o_f, o_b, ig, cb, cb, cb, h, w_gate, conv_w, norm_g, w_branch, w_out, g_post)


def _swiglu(h, wg, wu, wd):
    g = _dot(h, wg)
    u = _dot(h, wu)
    return _dot((g * _sigmoid(g) * u).astype(BF16), wd)


def _embed_tail(x, proj, gin_ref, gpost_ref, wgate_ref):
    gate = _sigmoid(_dot(_rms(x, gin_ref[...]).astype(BF16), wgate_ref[...]))
    return x + _rms(proj * gate, gpost_ref[...])


def _embed_operands(embed, tm, index):
    pd = embed[0].shape[1]
    const = lambda *_: (0, 0)
    return [pl.BlockSpec((tm, pd), index), pl.BlockSpec((1, D_MODEL), const), pl.BlockSpec((1, D_MODEL), const),
            _resident((D_MODEL, D_MODEL), const), _resident((pd, D_MODEL), const)]


def _ffn_kernel(x_ref, gpre_ref, gpost_ref, wg_ref, wu_ref, wd_ref, p_ref, ein_ref, epost_ref, egate_ref, eproj_ref,
                o_ref):
    proj = _dot(p_ref[...].astype(BF16), eproj_ref[...])
    x = x_ref[...]
    h = _rms(x, gpre_ref[...]).astype(BF16)
    mixed = x + _rms(_swiglu(h, wg_ref[...], wu_ref[...], wd_ref[...]), gpost_ref[...])
    o_ref[...] = _embed_tail(mixed, proj, ein_ref, epost_ref, egate_ref)


def _ffn(x, g_pre, g_post, wg, wu, wd, embed, tm):
    t = x.shape[0]
    const = lambda i: (0, 0)
    return pl.pallas_call(
        _ffn_kernel,
        out_shape=jax.ShapeDtypeStruct((t, D_MODEL), F32),
        grid=(t // tm,),
        in_specs=[pl.BlockSpec((tm, D_MODEL), lambda i: (i, 0)),
                  pl.BlockSpec((1, D_MODEL), const), pl.BlockSpec((1, D_MODEL), const),
                  _resident(wg.shape, const), _resident(wu.shape, const), _resident(wd.shape, const)]
                 + _embed_operands(embed, tm, lambda i: (i, 0)),
        out_specs=pl.BlockSpec((tm, D_MODEL), lambda i: (i, 0)),
        compiler_params=_params(("parallel",)),
        name="dense_swiglu",
    )(x, g_pre, g_post, wg, wu, wd, *embed)


PIECES = D_MODEL // 2 // HEAD_DIM
SC_WINDOW = 128
EXPERT_TILE = 512


def _pack_rows(vals):
    bits = lax.bitcast_convert_type(vals.astype(BF16).astype(F32), jnp.uint32)
    half = D_MODEL // 2
    word = bits[:, :half] | (bits[:, half:] >> 16)
    return [lax.bitcast_convert_type(word[:, c * HEAD_DIM:(c + 1) * HEAD_DIM], jnp.int32) for c in range(PIECES)]


def _unpack_rows(piece_refs):
    words = [lax.bitcast_convert_type(r[...], jnp.uint32) for r in piece_refs]
    hi = [lax.bitcast_convert_type(w & jnp.uint32(0xFFFF0000), F32) for w in words]
    lo = [lax.bitcast_convert_type(w << 16, F32) for w in words]
    return jnp.concatenate(hi + lo, axis=1)


def _route_kernel(x_ref, gpre_ref, rhi_ref, rlo_ref, *refs, n_experts):
    piece_refs = refs[:PIECES]
    rec_ref, rank_ref, count_ref, carry_ref = refs[PIECES:]
    i = pl.program_id(0)
    tm = x_ref.shape[0]

    @pl.when(i == 0)
    def _():
        carry_ref[...] = jnp.zeros_like(carry_ref)

    hf = _rms(x_ref[...], gpre_ref[...])
    for r, piece in zip(piece_refs, _pack_rows(hf)):
        r[...] = piece
    h_hi = hf.astype(BF16)
    h_lo = (hf - h_hi.astype(F32)).astype(BF16)
    logits = _dot(h_hi, rhi_ref[...]) + _dot(h_hi, rlo_ref[...]) + _dot(h_lo, rhi_ref[...])
    lane = lax.broadcasted_iota(jnp.int32, logits.shape, 1).astype(F32)
    logits = jnp.where(lane < n_experts, logits, NEG_INF)
    m1 = jnp.max(logits, axis=1, keepdims=True)
    i1 = jnp.min(jnp.where(logits == m1, lane, 1e9), axis=1, keepdims=True)
    rest = jnp.where(lane == i1, NEG_INF, logits)
    m2 = jnp.max(rest, axis=1, keepdims=True)
    i2 = jnp.min(jnp.where(rest == m2, lane, 1e9), axis=1, keepdims=True)
    ex = jnp.exp(m2 - m1)
    w1 = 1.0 / (1.0 + ex)
    rec = jnp.where(lane == 0, i1, jnp.where(lane == 1, i2, jnp.where(lane == 2, w1, jnp.where(lane == 3, ex * w1, 0.0))))
    rec_ref[...] = rec
    rec_t = rec.T
    e1 = rec_t[0:1, :]
    e2 = rec_t[1:2, :]
    sub = lax.broadcasted_iota(jnp.int32, (8, tm), 0).astype(F32)
    oh1 = jnp.where(sub == e1, 1.0, 0.0)
    oh2 = jnp.where(sub == e2, 1.0, 0.0)
    chosen = oh1 + oh2
    src = lax.broadcasted_iota(jnp.int32, (tm, tm), 0)
    dst = lax.broadcasted_iota(jnp.int32, (tm, tm), 1)
    before = jnp.where(src < dst, 1.0, 0.0).astype(BF16)
    rank = _dot(chosen.astype(BF16), before) + carry_ref[:, 0:1]
    rank1 = jnp.sum(oh1 * rank, axis=0, keepdims=True)
    rank2 = jnp.sum(oh2 * rank, axis=0, keepdims=True)
    rank_ref[...] = jnp.where(sub == 0, rank1, jnp.where(sub == 1, rank2, jnp.where(sub == 2, e1, jnp.where(sub == 3, e2, 0.0))))
    carry_ref[...] = carry_ref[...] + jnp.sum(chosen, axis=1, keepdims=True)
    count_ref[...] = carry_ref[...]


def _route(x, g_pre, r_hi, r_lo, n_experts, tm):
    t = x.shape[0]
    assert n_experts <= 8
    piece = pl.BlockSpec((tm, HEAD_DIM), lambda i: (i, 0))
    return pl.pallas_call(
        functools.partial(_route_kernel, n_experts=n_experts),
        out_shape=(*[jax.ShapeDtypeStruct((t, HEAD_DIM), jnp.int32)] * PIECES,
                   jax.ShapeDtypeStruct((t, HEAD_DIM), F32), jax.ShapeDtypeStruct((8, t), F32),
                   jax.ShapeDtypeStruct((8, HEAD_DIM), F32)),
        grid=(t // tm,),
        in_specs=[pl.BlockSpec((tm, D_MODEL), lambda i: (i, 0)), pl.BlockSpec((1, D_MODEL), lambda i: (0, 0)),
                  pl.BlockSpec((D_MODEL, HEAD_DIM), lambda i: (0, 0)), pl.BlockSpec((D_MODEL, HEAD_DIM), lambda i: (0, 0))],
        out_specs=(*[piece] * PIECES, piece, pl.BlockSpec((8, tm), lambda i: (0, i)),
                   pl.BlockSpec((8, HEAD_DIM), lambda i: (0, 0))),
        scratch_shapes=[pltpu.VMEM((8, HEAD_DIM), F32)],
        compiler_params=_params(("arbitrary",)),
        name="moe_route",
    )(x, g_pre, r_hi, r_lo)


def _slot_kernel(rank_ref, start_ref, s1_ref, s2_ref):
    tm = rank_ref.shape[1]
    sub = lax.broadcasted_iota(jnp.int32, (8, tm), 0).astype(F32)
    start = start_ref[:, 0:1]
    rows = rank_ref[...]
    for choice, out in ((0, s1_ref), (1, s2_ref)):
        base = jnp.sum(jnp.where(sub == rows[2 + choice:3 + choice, :], start, 0.0), axis=0, keepdims=True)
        out[...] = (base + rows[choice:choice + 1, :]).astype(jnp.int32)


def _slots(ranks, starts, tm):
    t = ranks.shape[1]
    row = pl.BlockSpec((1, tm), lambda i: (0, i))
    return pl.pallas_call(
        _slot_kernel,
        out_shape=(jax.ShapeDtypeStruct((1, t), jnp.int32),) * 2,
        grid=(t // tm,),
        in_specs=[pl.BlockSpec((8, tm), lambda i: (0, i)), pl.BlockSpec((8, HEAD_DIM), lambda i: (0, 0))],
        out_specs=(row, row),
        compiler_params=_params(("parallel",)),
        name="moe_slots",
    )(ranks, starts)


def _sc_mesh():
    return plsc.VectorSubcoreMesh(core_axis_name="core", subcore_axis_name="subcore")


def _sc_scatter_rows(srcs, idxs, n_rows):
    ns, nk = len(srcs), len(idxs)
    half = srcs[0].shape[0] // SC_WINDOW // 2

    @functools.partial(pl.kernel, mesh=_sc_mesh(), scratch_types=[],
                       out_type=tuple(jax.ShapeDtypeStruct((n_rows, HEAD_DIM), srcs[0].dtype) for _ in range(ns)))
    def scatter(*refs):
        out_hbm = refs[ns + nk:]

        def body(*blocks):
            for idx in blocks[ns:]:
                for c in range(ns):
                    pltpu.sync_copy(blocks[c], out_hbm[c].at[idx.at[0]])

        pltpu.emit_pipeline(
            body, grid=(2, half),
            in_specs=[pl.BlockSpec((SC_WINDOW, HEAD_DIM), lambda i, j: (i * half + j, 0)) for _ in range(ns)]
                     + [pl.BlockSpec((1, SC_WINDOW), lambda i, j: (0, i * half + j)) for _ in range(nk)],
            out_specs=[],
            core_axis_name=("core", "subcore"),
            dimension_semantics=(pltpu.PARALLEL, pltpu.PARALLEL),
        )(*refs[:ns + nk])

    return scatter(*srcs, *idxs)


def _sc_gather_rows(tables, idx):
    nt = len(tables)
    n = idx.shape[1]
    half = n // SC_WINDOW // 2

    @functools.partial(pl.kernel, mesh=_sc_mesh(), scratch_types=[],
                       out_type=tuple(jax.ShapeDtypeStruct((n, HEAD_DIM), tables[0].dtype) for _ in range(nt)))
    def gather(*refs):
        table_hbm = refs[:nt]

        def body(idx_blk, *out_blks):
            for c in range(nt):
                pltpu.sync_copy(table_hbm[c].at[idx_blk.at[0]], out_blks[c])

        pltpu.emit_pipeline(
            body, grid=(2, half),
            in_specs=[pl.BlockSpec((1, SC_WINDOW), lambda i, j: (0, i * half + j))],
            out_specs=[pl.BlockSpec((SC_WINDOW, HEAD_DIM), lambda i, j: (i * half + j, 0)) for _ in range(nt)],
            core_axis_name=("core", "subcore"),
            dimension_semantics=(pltpu.PARALLEL, pltpu.PARALLEL),
        )(refs[nt], *refs[nt + 1:])

    return gather(*tables, idx)


def _expert_kernel(tile_expert_ref, n_used_ref, *refs):
    x_refs = refs[:PIECES]
    wg_ref, wu_ref, wd_ref = refs[PIECES:PIECES + 3]
    y_refs = refs[PIECES + 3:]

    @pl.when(pl.program_id(0) < n_used_ref[0])
    def _():
        h = _unpack_rows(x_refs).astype(BF16)
        y = _swiglu(h, wg_ref[0], wu_ref[0], wd_ref[0])
        for r, piece in zip(y_refs, _pack_rows(y)):
            r[...] = piece


def _expert_ffn(xs, tile_expert, n_used, wg, wu, wd):
    n_rows = xs[0].shape[0]
    rows = pl.BlockSpec((EXPERT_TILE, HEAD_DIM), lambda i, te, nu: (jnp.minimum(i, nu[0] - 1), 0))
    out_rows = pl.BlockSpec((EXPERT_TILE, HEAD_DIM), lambda i, te, nu: (i, 0))
    expert = lambda i, te, nu: (te[i], 0, 0)
    grid_spec = pltpu.PrefetchScalarGridSpec(
        num_scalar_prefetch=2,
        grid=(n_rows // EXPERT_TILE,),
        in_specs=[rows] * PIECES + [_resident((1,) + wg.shape[1:], expert), _resident((1,) + wu.shape[1:], expert),
                                    _resident((1,) + wd.shape[1:], expert)],
        out_specs=[out_rows] * PIECES,
    )
    return pl.pallas_call(
        _expert_kernel,
        out_shape=[jax.ShapeDtypeStruct((n_rows, HEAD_DIM), jnp.int32)] * PIECES,
        grid_spec=grid_spec,
        compiler_params=_params(("arbitrary",)),
        name="moe_expert_swiglu",
    )(tile_expert, n_used, *xs, wg, wu, wd)


def _combine_kernel(x_ref, rec_ref, gpost_ref, p_ref, ein_ref, epost_ref, egate_ref, eproj_ref, *refs):
    proj = _dot(p_ref[...].astype(BF16), eproj_ref[...])
    y1 = _unpack_rows(refs[:PIECES])
    y2 = _unpack_rows(refs[PIECES:2 * PIECES])
    o_ref = refs[2 * PIECES]
    rec = rec_ref[...]
    y = rec[:, 2:3] * y1 + rec[:, 3:4] * y2
    mixed = x_ref[...] + _rms(y, gpost_ref[...])
    o_ref[...] = _embed_tail(mixed, proj, ein_ref, epost_ref, egate_ref)


def _combine(x, rec, g_post, embed, y1, y2, tm):
    t = x.shape[0]
    piece = pl.BlockSpec((tm, HEAD_DIM), lambda i: (i, 0))
    return pl.pallas_call(
        _combine_kernel,
        out_shape=jax.ShapeDtypeStruct((t, D_MODEL), F32),
        grid=(t // tm,),
        in_specs=[pl.BlockSpec((tm, D_MODEL), lambda i: (i, 0)), piece, pl.BlockSpec((1, D_MODEL), lambda i: (0, 0))]
                 + _embed_operands(embed, tm, lambda i: (i, 0)) + [piece] * (2 * PIECES),
        out_specs=pl.BlockSpec((tm, D_MODEL), lambda i: (i, 0)),
        compiler_params=_params(("parallel",)),
        name="moe_combine",
    )(x, rec, g_post, *embed, *y1, *y2)


def _moe(x, g_pre, g_post, router, wg, wu, wd, embed, tm):
    t = x.shape[0]
    n_experts = router.shape[1]
    router = jnp.pad(router, ((0, 0), (0, HEAD_DIM - n_experts)))
    r_hi = router.astype(BF16)
    r_lo = (router - r_hi.astype(F32)).astype(BF16)
    *h_pieces, rec, ranks, counts = _route(x, g_pre, r_hi, r_lo, n_experts, tm)

    counts = counts[:n_experts, 0].astype(jnp.int32)
    tiles = (counts + EXPERT_TILE - 1) // EXPERT_TILE
    tile_end = jnp.cumsum(tiles)
    starts = ((tile_end - tiles) * EXPERT_TILE).astype(F32)
    starts = jnp.broadcast_to(jnp.pad(starts, (0, 8 - n_experts))[:, None], (8, HEAD_DIM))
    n_rows = TOP_K * t + n_experts * EXPERT_TILE
    n_tiles = n_rows // EXPERT_TILE
    n_used = tile_end[-1:]
    tile_ids = jnp.minimum(jnp.arange(n_tiles, dtype=jnp.int32), n_used[0] - 1)
    tile_expert = jnp.sum(tile_ids[:, None] >= tile_end[None, :], axis=1).astype(jnp.int32)

    slot1, slot2 = _slots(ranks, starts, min(t, 8192))
    xs = []
    for c in range(0, PIECES, 2):
        xs += _sc_scatter_rows(h_pieces[c:c + 2], [slot1, slot2], n_rows)
    ys = _expert_ffn(xs, tile_expert, n_used, wg, wu, wd)
    gathered = []
    for slot in (slot1, slot2):
        rows = []
        for c in range(0, PIECES, 2):
            rows += _sc_gather_rows(ys[c:c + 2], slot)
        gathered.append(rows)
    return _combine(x, rec, g_post, embed, gathered[0], gathered[1], tm)


def kernel(x, p, positions, w_in, conv_w, hgrn_lb_logits, hgrn_norm_g, w_branch, w_out, g_mix_pre, g_mix_post, g_ffn_pre, g_ffn_post, dense_w_gate, dense_w_up, dense_w_down, moe_router, moe_w_gate, moe_w_up, moe_w_down, ple_w_proj, ple_w_gate, ple_g_in, ple_g_post):
    batch, seq, _ = x.shape
    depth = w_in.shape[0]
    t = batch * seq
    tm = min(512, seq)
    tm_merge = min(512, seq)
    tm_ffn = min(512, seq)

    xt = x.reshape(t, D_MODEL)
    cos, sin = _rope_tables(positions.astype(F32).reshape(t, 1), tm)
    row = lambda a: a.reshape(1, -1)

    for i in range(depth):
        w_i = w_in[i].astype(BF16)
        g_pre = row(g_mix_pre[i])
        *qkvs, h = _inproj_attention(xt, g_pre, w_i[:, :A_COLS], cos, sin, batch, seq, tm)
        side_cols = ((A_COLS, A_COLS + 3 * BRANCH), (A_COLS + 3 * BRANCH, A_COLS + B_COLS),
                     (A_COLS + B_COLS, A_COLS + B_COLS + C_COLS))
        aos, lses, sides = zip(*[_attention_group(qkvs[g], h, w_i[:, lo:hi], g == 2, batch, seq, g)
                                 for g, (lo, hi) in enumerate(side_cols)])
        qzz, ig, cb = sides
        o_f, o_b = _hgrn(qzz, ig, hgrn_lb_logits, batch, seq, i)
        xt = _merge(xt, aos, lses, o_f, o_b, ig, cb, h, w_i[:, A_COLS + B_COLS + C_COLS:], conv_w[i],
                    row(hgrn_norm_g[i]), w_branch[i].astype(BF16), w_out[i].astype(BF16), row(g_mix_post[i]),
                    seq, tm_merge)

        embed = (p[i].reshape(t, -1), row(ple_g_in[i]), row(ple_g_post[i]),
                 ple_w_gate[i].astype(BF16), ple_w_proj[i].astype(BF16))
        j = i // 2
        if i % 2 == 0:
            xt = _ffn(xt, row(g_ffn_pre[i]), row(g_ffn_post[i]), dense_w_gate[j].astype(BF16),
                      dense_w_up[j].astype(BF16), dense_w_down[j].astype(BF16), embed, tm_ffn)
        else:
            xt = _moe(xt, row(g_ffn_pre[i]), row(g_ffn_post[i]), moe_router[j], moe_w_gate[j].astype(BF16),
                      moe_w_up[j].astype(BF16), moe_w_down[j].astype(BF16), embed, tm)

    return xt.reshape(batch, seq, D_MODEL)
```

```python
import functools
import math

import jax
import jax.numpy as jnp
from jax import lax
from jax.experimental import pallas as pl
from jax.experimental.pallas import tpu as pltpu
from jax.experimental.pallas import tpu_sc as plsc

D_MODEL = 1024
EPS = 1e-6
NEG_INF = -1e30

HEAD_DIM = 128
N_HEADS = 4
BRANCH = N_HEADS * HEAD_DIM
DIL_GROUPS = ((128, 1), (512, 4), (2048, 16))
N_GROUPS = len(DIL_GROUPS)
BLK = 64
ROT_DIM = HEAD_DIM // 4
ROPE_THETA = 500000.0
TOP_K = 2

A_COLS = N_GROUPS * 3 * BRANCH
B_COLS = 5 * BRANCH
C_COLS = 3 * BRANCH
G_COLS = 3 * D_MODEL

BF16 = jnp.bfloat16
F32 = jnp.float32

VMEM_LIMIT = 56 * 1024 * 1024


def _params(sem):
    return pltpu.CompilerParams(dimension_semantics=sem, vmem_limit_bytes=VMEM_LIMIT)


def _cast_kernel(w_ref, o_ref):
    o_ref[...] = w_ref[...].astype(o_ref.dtype)


def _to_bf16(w):
    cols = w.shape[-1]
    flat = w.reshape(-1, cols)
    rows = flat.shape[0]
    block = rows
    while block % 32 == 0 and block * cols * 4 > 8 * 1024 * 1024:
        block //= 2
    out = pl.pallas_call(
        _cast_kernel,
        out_shape=jax.ShapeDtypeStruct(flat.shape, BF16),
        grid=(rows // block,),
        in_specs=[pl.BlockSpec((block, cols), lambda i: (i, 0))],
        out_specs=pl.BlockSpec((block, cols), lambda i: (i, 0)),
        compiler_params=_params(("parallel",)),
        name="weights_to_bf16",
    )(flat)
    return out.reshape(w.shape)


def _resident(shape, index_map):
    return pl.BlockSpec(shape, index_map, pipeline_mode=pl.Buffered(1))


def _rms(xf, g):
    return xf * lax.rsqrt(jnp.mean(xf * xf, axis=-1, keepdims=True) + EPS) * g


def _sigmoid(z):
    return 0.5 * jnp.tanh(0.5 * z) + 0.5


def _sigmoid_rel(z):
    return 1.0 / (1.0 + jnp.exp(-z))


def _dot(a, b):
    return jnp.dot(a, b, preferred_element_type=F32)


def _dot_nt(a, b):
    return lax.dot_general(a, b, (((1,), (1,)), ((), ())), preferred_element_type=F32)


def _dot_tn(a, b):
    return lax.dot_general(a, b, (((0,), (0,)), ((), ())), preferred_element_type=F32)


def _rope_table_kernel(pos_ref, freq_ref, sign_ref, cos_ref, sin_ref):
    ang = pos_ref[...] * freq_ref[...]
    cos_ref[...] = jnp.cos(ang)
    sin_ref[...] = jnp.sin(ang) * sign_ref[...]


def _rope_tables(pos_col, tm):
    t = pos_col.shape[0]
    half = ROT_DIM // 2
    inv_freq = ROPE_THETA ** (-jnp.arange(0, ROT_DIM, 2, dtype=F32) / ROT_DIM)
    freq = jnp.concatenate([inv_freq, inv_freq, jnp.zeros((HEAD_DIM - ROT_DIM,), F32)])[None, :]
    sign = jnp.concatenate([-jnp.ones((half,), F32), jnp.ones((HEAD_DIM - half,), F32)])[None, :]
    row = pl.BlockSpec((1, HEAD_DIM), lambda i: (0, 0))
    tab = pl.BlockSpec((tm, HEAD_DIM), lambda i: (i, 0))
    return pl.pallas_call(
        _rope_table_kernel,
        out_shape=(jax.ShapeDtypeStruct((t, HEAD_DIM), F32),) * 2,
        grid=(t // tm,),
        in_specs=[pl.BlockSpec((tm, 1), lambda i: (i, 0)), row, row],
        out_specs=(tab, tab),
        compiler_params=_params(("parallel",)),
        name="rope_tables",
    )(pos_col, freq, sign)


def _rope_head(th, cos, sin):
    half = ROT_DIM // 2
    lane = lax.broadcasted_iota(jnp.int32, th.shape, 1)
    swapped = jnp.where(lane < half, pltpu.roll(th, HEAD_DIM - half, 1), pltpu.roll(th, half, 1))
    return th * cos + swapped * sin


def _inproj_attn_kernel(x_ref, g_ref, w_ref, cos_ref, sin_ref, o0_ref, o1_ref, o2_ref, h_ref, stage_ref):
    hf = _rms(x_ref[...], g_ref[...])
    h_ref[...] = hf.astype(BF16)
    n_pieces = D_MODEL // HEAD_DIM
    for pc in range(n_pieces):
        stage_ref[pc] = hf[:, pc * HEAD_DIM:(pc + 1) * HEAD_DIM]
    scale = 1.0 / math.sqrt(HEAD_DIM)
    tm = x_ref.shape[0]
    for grp, o_ref in enumerate((o0_ref, o1_ref, o2_ref)):
        dil = DIL_GROUPS[grp][1]
        sub = tm // dil
        if dil == 1:
            h, cos, sin = hf.astype(BF16), cos_ref[...], sin_ref[...]
        else:
            residue = lambda ref, *lead: jnp.concatenate(
                [ref[(*lead, pl.ds(r, sub, stride=dil), slice(None))] for r in range(dil)], axis=0)
            h = jnp.concatenate([residue(stage_ref, pc) for pc in range(n_pieces)], axis=1).astype(BF16)
            cos, sin = residue(cos_ref), residue(sin_ref)
        for part in range(3):
            c = grp * 3 + part
            t = _dot(h, w_ref[:, c * BRANCH:(c + 1) * BRANCH])
            if part < 2:
                heads = []
                for hd in range(N_HEADS):
                    r = _rope_head(t[:, hd * HEAD_DIM:(hd + 1) * HEAD_DIM], cos, sin)
                    heads.append(r * scale if part == 0 else r)
                t = jnp.concatenate(heads, axis=1)
            t = t.astype(BF16)
            for r in range(dil):
                o_ref[0, r, :, part * BRANCH:(part + 1) * BRANCH] = t[r * sub:(r + 1) * sub, :]


def _inproj_attention(x, g, w, cos, sin, batch, seq, tm):
    per_seq = seq // tm
    rows = lambda width: pl.BlockSpec((tm, width), lambda b, j: (b * per_seq + j, 0))
    out_shape = tuple(jax.ShapeDtypeStruct((batch, dil, seq // dil, 3 * BRANCH), BF16) for _, dil in DIL_GROUPS)
    out_shape += (jax.ShapeDtypeStruct((batch * seq, D_MODEL), BF16),)
    out_specs = tuple(pl.BlockSpec((1, dil, tm // dil, 3 * BRANCH), lambda b, j: (b, 0, j, 0))
                      for _, dil in DIL_GROUPS) + (rows(D_MODEL),)
    return pl.pallas_call(
        _inproj_attn_kernel, out_shape=out_shape, grid=(batch, per_seq),
        in_specs=[rows(D_MODEL), pl.BlockSpec((1, D_MODEL), lambda b, j: (0, 0)),
                  _resident((D_MODEL, A_COLS), lambda b, j: (0, 0)), rows(HEAD_DIM), rows(HEAD_DIM)],
        out_specs=out_specs,
        scratch_shapes=[pltpu.VMEM((D_MODEL // HEAD_DIM, tm, HEAD_DIM), F32)],
        compiler_params=_params(("parallel", "parallel")), name="inproj_attention",
    )(x, g, w, cos, sin)


def _attn_kernel(q_ref, k_ref, kp_ref, kn_ref, v_ref, vp_ref, vn_ref, h_ref, w_ref, o_ref, lse_ref, side_ref,
                 o_stage, lse_stage, *, dil, nj, n_steps, conv_side):
    n = pl.program_id(1)
    part_rows = h_ref.shape[0] // N_HEADS

    def side_projection(part):
        rs = slice(part * part_rows, (part + 1) * part_rows)
        h = h_ref[rs, :]
        if conv_side:
            u = _dot(h, w_ref[:, 0:BRANCH])
            b_gate = _dot(h, w_ref[:, BRANCH:2 * BRANCH])
            c_gate = _dot(h, w_ref[:, 2 * BRANCH:3 * BRANCH])
            side_ref[rs, 0:BRANCH] = (c_gate * u).astype(BF16)
            side_ref[rs, BRANCH:2 * BRANCH] = b_gate.astype(BF16)
        else:
            for c in range(w_ref.shape[1] // BRANCH):
                cols = slice(c * BRANCH, (c + 1) * BRANCH)
                side_ref[rs, cols] = _dot(h, w_ref[:, cols]).astype(BF16)

    qb = 2 if nj % 2 == 0 else 1
    q_rows, w_rows, n_tiles = qb * BLK, (qb + 2) * BLK, nj // qb
    col_lo = jnp.where(n > 0, 0, BLK)
    col_hi = jnp.where(n < n_steps - 1, w_rows, w_rows - BLK)
    row = lax.broadcasted_iota(jnp.int32, (q_rows, w_rows), 0)
    col = lax.broadcasted_iota(jnp.int32, (q_rows, w_rows), 1)
    band = jnp.abs(col - BLK - row) <= BLK
    masks = {}
    for j in range(n_tiles):
        m = band
        if j == 0:
            m = m & (col >= col_lo)
        if j == n_tiles - 1:
            m = m & (col < col_hi)
        masks[j] = m
    pairs = [(r, j) for r in range(dil) for j in range(n_tiles)]
    lane = lax.broadcasted_iota(jnp.int32, (len(pairs), q_rows, HEAD_DIM), 2)

    def natural_rows(r, j):
        return pl.ds(j * q_rows * dil + r, q_rows, stride=dil)

    def window(own_ref, prev_ref, next_ref, r, j, hs):
        parts = []
        for w in range(j * qb - 1, j * qb + qb + 1):
            if w < 0:
                parts.append(prev_ref[0, r, :, hs])
            elif w >= nj:
                parts.append(next_ref[0, r, :, hs])
            else:
                parts.append(own_ref[0, r, w * BLK:(w + 1) * BLK, hs])
        return jnp.concatenate(parts, axis=0)

    lse_tiles = jnp.zeros((len(pairs), q_rows, HEAD_DIM), F32)
    for hd in range(N_HEADS):
        side_projection(hd)
        hs = slice(hd * HEAD_DIM, (hd + 1) * HEAD_DIM)
        scores = [_dot_nt(q_ref[0, r, j * q_rows:(j + 1) * q_rows, hs], window(k_ref, kp_ref, kn_ref, r, j, hs))
                  for r, j in pairs]
        s = jnp.stack([jnp.where(masks[j], sc, NEG_INF) for (r, j), sc in zip(pairs, scores)])
        m = jnp.max(s, axis=2, keepdims=True)
        p = jnp.exp(s - m)
        l = jnp.sum(p, axis=2, keepdims=True)
        pb = p.astype(BF16)
        outs = jnp.stack([_dot(pb[i], window(v_ref, vp_ref, vn_ref, r, j, hs)) for i, (r, j) in enumerate(pairs)])
        outs = outs * (1.0 / l)
        lse_tiles = jnp.where(lane == hd, m + jnp.log(l), lse_tiles)
        for i, (r, j) in enumerate(pairs):
            o_stage[hd, natural_rows(r, j), :] = outs[i]
        o_ref[:, hs] = o_stage[hd].astype(o_ref.dtype)
    for i, (r, j) in enumerate(pairs):
        lse_stage[natural_rows(r, j), :] = lse_tiles[i]
    lse_ref[...] = lse_stage[...]


def _attention_group(qkv, h, w_side, conv_side, batch, seq, group):
    _, dil = DIL_GROUPS[group]
    t = batch * seq
    sub_len = seq // dil
    rows_per_step = min(seq, 16 * BLK)
    nj = rows_per_step // (dil * BLK)
    n_steps = seq // rows_per_step
    n_blk = sub_len // BLK
    sub_rows = rows_per_step // dil

    def own(part):
        return pl.BlockSpec((1, dil, sub_rows, BRANCH), lambda b, n: (b, 0, n, part))

    def prev(part):
        return pl.BlockSpec((1, dil, BLK, BRANCH), lambda b, n: (b, 0, jnp.maximum(n * nj - 1, 0), part))

    def nxt(part):
        return pl.BlockSpec((1, dil, BLK, BRANCH), lambda b, n: (b, 0, jnp.minimum((n + 1) * nj, n_blk - 1), part))

    side_cols = 2 * BRANCH if conv_side else w_side.shape[1]
    step_rows = lambda width: pl.BlockSpec((rows_per_step, width), lambda b, n: (b * n_steps + n, 0))
    return pl.pallas_call(
        functools.partial(_attn_kernel, dil=dil, nj=nj, n_steps=n_steps, conv_side=conv_side),
        out_shape=(jax.ShapeDtypeStruct((t, BRANCH), BF16), jax.ShapeDtypeStruct((t, HEAD_DIM), F32),
                   jax.ShapeDtypeStruct((t, side_cols), BF16)),
        grid=(batch, n_steps),
        in_specs=[own(0), own(1), prev(1), nxt(1), own(2), prev(2), nxt(2), step_rows(D_MODEL),
                  _resident(w_side.shape, lambda b, n: (0, 0))],
        out_specs=(step_rows(BRANCH), step_rows(HEAD_DIM), step_rows(side_cols)),
        scratch_shapes=[pltpu.VMEM((N_HEADS, rows_per_step, HEAD_DIM), F32), pltpu.VMEM((rows_per_step, HEAD_DIM), F32)],
        compiler_params=_params(("parallel", "parallel")),
        name=f"dilated_attention_g{group}",
    )(qkv, qkv, qkv, qkv, qkv, qkv, qkv, h, w_side)


def _hgrn_kernel(logit_ref, qf_ref, zf_ref, vf_ref, qb_ref, zb_ref, vb_ref, of_ref, ob_ref, state_ref,
                 *, layer, n_chunks):
    c = pl.program_id(1)

    @pl.when(c == 0)
    def _():
        state_ref[...] = jnp.zeros_like(state_ref)

    depth = logit_ref.shape[1]
    lbs = []
    for dirn in range(2):
        rows = [logit_ref[dirn, j:j + 1, :] for j in range(depth)]
        top = functools.reduce(jnp.maximum, rows)
        exps = [jnp.exp(rw - top) for rw in rows]
        lbs.append(sum(exps[1:layer + 1], jnp.zeros_like(top)) / sum(exps[1:], exps[0]))

    r = lax.broadcasted_iota(jnp.int32, (BLK, BLK), 0)
    cc = lax.broadcasted_iota(jnp.int32, (BLK, BLK), 1)

    def bcast_rows(per_chunk):
        return jnp.concatenate([jnp.broadcast_to(v, (BLK, BRANCH)) for v in per_chunk], axis=0)

    for dirn, (q_ref, z_ref, v_ref, o_ref) in enumerate(
            ((qf_ref, zf_ref, vf_ref, of_ref), (qb_ref, zb_ref, vb_ref, ob_ref))):
        tri = (cc <= r) if dirn == 0 else (cc >= r)
        tri_b = jnp.where(tri, 1.0, 0.0).astype(BF16)
        last, mid_row = (BLK - 1, BLK // 2 - 1) if dirn == 0 else (0, BLK // 2)
        lb = lbs[dirn]
        q = q_ref[...].astype(F32)
        z = z_ref[...].astype(F32)
        v = v_ref[...]
        sg = _sigmoid_rel(z)
        logf = jnp.log(lb + (1.0 - lb) * sg)
        kk = (1.0 - lb) * (1.0 - sg)
        hi = logf.astype(BF16)
        lo = (logf - hi.astype(F32)).astype(BF16)
        chunks = [slice(g * BLK, (g + 1) * BLK) for g in range(n_chunks)]
        cums = [_dot(tri_b, hi[ch]) + _dot(tri_b, lo[ch]) for ch in chunks]
        totals = [cm[last:last + 1, :] for cm in cums]
        mids = [cm[mid_row:mid_row + 1, :] for cm in cums]
        cum = jnp.concatenate(cums, axis=0)
        mid = bcast_rows(mids)
        qa = q * jnp.exp(cum - mid)
        ka = kk * jnp.exp(mid - cum)
        q_dec = (qa * bcast_rows([jnp.exp(md) for md in mids])).astype(BF16)
        k_dec = (ka * bcast_rows([jnp.exp(tt - md) for tt, md in zip(totals, mids)])).astype(BF16)
        qa = qa.astype(BF16)
        ka = ka.astype(BF16)
        tiles = [(g, hd) for g in range(n_chunks) for hd in range(N_HEADS)]
        sl = lambda g, hd: (slice(g * BLK, (g + 1) * BLK), slice(hd * HEAD_DIM, (hd + 1) * HEAD_DIM))
        atts = [jnp.where(tri, _dot_nt(qa[sl(g, hd)], ka[sl(g, hd)]), 0.0).astype(BF16) for g, hd in tiles]
        intra = {t: _dot(att, v[sl(*t)]) for t, att in zip(tiles, atts)}
        kvs = {t: _dot_tn(v[sl(*t)], k_dec[sl(*t)]) for t in tiles}
        decays = [jnp.exp(tt) for tt in totals]
        order = range(n_chunks) if dirn == 0 else range(n_chunks - 1, -1, -1)
        for hd in range(N_HEADS):
            hs = slice(hd * HEAD_DIM, (hd + 1) * HEAD_DIM)
            state_t = state_ref[dirn, hd]
            for g in order:
                o_ref[g * BLK:(g + 1) * BLK, hs] = intra[(g, hd)] + _dot_nt(q_dec[sl(g, hd)], state_t.astype(BF16))
                state_t = state_t * decays[g][:, hs] + kvs[(g, hd)]
            state_ref[dirn, hd] = state_t


def _hgrn(qzz, ig, logits, batch, seq, layer):
    t = qzz.shape[0]
    n_chunks = 4
    rows = n_chunks * BLK
    nc = seq // rows
    depth = logits.shape[1]

    def fwd(colblk):
        return pl.BlockSpec((rows, BRANCH), lambda b, c: (b * nc + c, colblk))

    def bwd(colblk):
        return pl.BlockSpec((rows, BRANCH), lambda b, c: (b * nc + nc - 1 - c, colblk))

    return pl.pallas_call(
        functools.partial(_hgrn_kernel, layer=layer, n_chunks=n_chunks),
        out_shape=(jax.ShapeDtypeStruct((t, BRANCH), F32),) * 2,
        grid=(batch, nc),
        in_specs=[pl.BlockSpec((2, depth, BRANCH), lambda b, c: (0, 0, 0)),
                  fwd(0), fwd(1), fwd(0), bwd(0), bwd(2), bwd(0)],
        out_specs=(pl.BlockSpec((rows, BRANCH), lambda b, c: (b * nc + c, 0)),
                   pl.BlockSpec((rows, BRANCH), lambda b, c: (b * nc + nc - 1 - c, 0))),
        scratch_shapes=[pltpu.VMEM((2, N_HEADS, HEAD_DIM, HEAD_DIM), F32)],
        compiler_params=_params(("parallel", "arbitrary")),
        name="hgrn2_scan",
    )(logits, qzz, qzz, ig, qzz, qzz, ig)


def _merge_kernel(x_ref, ao0_ref, ao1_ref, ao2_ref, l0_ref, l1_ref, l2_ref, of_ref, ob_ref, bg_ref,
                  cb_ref, cbp_ref, cbn_ref, h_ref, wgate_ref, convw_ref, ng_ref, wbr_ref, wout_ref, gpost_ref,
                  o_ref, *, tm, seq):
    i = pl.program_id(0)
    h = h_ref[...]
    gates = [_sigmoid(_dot(h, wgate_ref[:, n * D_MODEL:(n + 1) * D_MODEL])) for n in range(3)]

    ls = (l0_ref[...], l1_ref[...], l2_ref[...])
    aos = (ao0_ref, ao1_ref, ao2_ref)
    lmax = jnp.maximum(jnp.maximum(ls[0], ls[1]), ls[2])
    es = [jnp.exp(l - lmax) for l in ls]
    inv = 1.0 / (es[0] + es[1] + es[2])
    a_parts = []
    for hd in range(N_HEADS):
        hs = slice(hd * HEAD_DIM, (hd + 1) * HEAD_DIM)
        acc = None
        for g in range(N_GROUPS):
            w = (es[g] * inv)[:, hd:hd + 1]
            term = w * aos[g][:, hs].astype(F32)
            acc = term if acc is None else acc + term
        a_parts.append(acc)
    o_a = jnp.concatenate(a_parts, axis=1).astype(BF16)

    b_parts = []
    for hd in range(N_HEADS):
        hs = slice(hd * HEAD_DIM, (hd + 1) * HEAD_DIM)
        o = of_ref[:, hs] + ob_ref[:, hs]
        b_parts.append(_rms(o, ng_ref[:, hs]))
    gate = bg_ref[...].astype(F32)
    o_b = (jnp.concatenate(b_parts, axis=1) * (gate * _sigmoid(gate))).astype(BF16)

    cu = cb_ref[:, 0:BRANCH].astype(F32)
    keep_prev = jnp.where((i * tm) % seq == 0, 0.0, 1.0)
    keep_next = jnp.where(((i + 1) * tm) % seq == 0, 0.0, 1.0)
    halo_rows = cbp_ref.shape[0]
    prev_row = keep_prev * cbp_ref[halo_rows - 1:halo_rows, 0:BRANCH].astype(F32)
    next_row = keep_next * cbn_ref[0:1, 0:BRANCH].astype(F32)
    rid = lax.broadcasted_iota(jnp.int32, cu.shape, 0)
    before = jnp.where(rid == 0, prev_row, pltpu.roll(cu, 1, 0))
    after = jnp.where(rid == tm - 1, next_row, pltpu.roll(cu, tm - 1, 0))
    conv = convw_ref[0:1, :] * before + convw_ref[1:2, :] * cu + convw_ref[2:3, :] * after
    o_c = (cb_ref[:, BRANCH:2 * BRANCH].astype(F32) * conv).astype(BF16)

    merged = None
    for n, o_n in enumerate((o_a, o_b, o_c)):
        up = _dot(o_n, wbr_ref[n])
        term = gates[n] * up
        merged = term if merged is None else merged + term
    y = _dot(merged.astype(BF16), wout_ref[...])
    o_ref[...] = x_ref[...] + _rms(y, gpost_ref[...])


def _merge(x, aos, lses, o_f, o_b, ig, cb, h, w_gate, conv_w, norm_g, w_branch, w_out, g_post, seq, tm):
    t = x.shape[0]
    halo = 16
    n_halo = t // halo
    per = tm // halo

    def rows(width, colblk=0):
        return pl.BlockSpec((tm, width), lambda i: (i, colblk))

    def full(shape):
        return _resident(shape, lambda i: (0,) * len(shape))

    in_specs = ([rows(D_MODEL)] + [rows(BRANCH)] * 3 + [rows(HEAD_DIM)] * 3 + [rows(BRANCH)] * 2
                + [rows(BRANCH, 1), rows(2 * BRANCH),
                   pl.BlockSpec((halo, 2 * BRANCH), lambda i: (jnp.maximum(i * per - 1, 0), 0)),
                   pl.BlockSpec((halo, 2 * BRANCH), lambda i: (jnp.minimum((i + 1) * per, n_halo - 1), 0)),
                   rows(D_MODEL), full((D_MODEL, G_COLS)), full((3, BRANCH)), full((1, BRANCH)),
                   full((3, BRANCH, D_MODEL)), full((D_MODEL, D_MODEL)), full((1, D_MODEL))])
    return pl.pallas_call(
        functools.partial(_merge_kernel, tm=tm, seq=seq),
        out_shape=jax.ShapeDtypeStruct((t, D_MODEL), F32),
        grid=(t // tm,),
        in_specs=in_specs,
        out_specs=rows(D_MODEL),
        compiler_params=_params(("parallel",)),
        name="merge_branches",
    )(x, *aos, *lses, o_f, o_b, ig, cb, cb, cb, h, w_gate, conv_w, norm_g, w_branch, w_out, g_post)


def _swiglu(h, wg, wu, wd):
    g = _dot(h, wg)
    u = _dot(h, wu)
    return _dot((g * _sigmoid(g) * u).astype(BF16), wd)


def _embed_tail(x, proj, gin_ref, gpost_ref, wgate_ref):
    gate = _sigmoid(_dot(_rms(x, gin_ref[...]).astype(BF16), wgate_ref[...]))
    return x + _rms(proj * gate, gpost_ref[...])


def _embed_operands(embed, tm, index):
    pd = embed[0].shape[1]
    const = lambda *_: (0, 0)
    return [pl.BlockSpec((tm, pd), index), pl.BlockSpec((1, D_MODEL), const), pl.BlockSpec((1, D_MODEL), const),
            _resident((D_MODEL, D_MODEL), const), _resident((pd, D_MODEL), const)]


def _ffn_kernel(x_ref, gpre_ref, gpost_ref, wg_ref, wu_ref, wd_ref, p_ref, ein_ref, epost_ref, egate_ref, eproj_ref,
                o_ref):
    proj = _dot(p_ref[...].astype(BF16), eproj_ref[...])
    x = x_ref[...]
    h = _rms(x, gpre_ref[...]).astype(BF16)
    mixed = x + _rms(_swiglu(h, wg_ref[...], wu_ref[...], wd_ref[...]), gpost_ref[...])
    o_ref[...] = _embed_tail(mixed, proj, ein_ref, epost_ref, egate_ref)


def _ffn(x, g_pre, g_post, wg, wu, wd, embed, tm):
    t = x.shape[0]
    const = lambda i: (0, 0)
    return pl.pallas_call(
        _ffn_kernel,
        out_shape=jax.ShapeDtypeStruct((t, D_MODEL), F32),
        grid=(t // tm,),
        in_specs=[pl.BlockSpec((tm, D_MODEL), lambda i: (i, 0)),
                  pl.BlockSpec((1, D_MODEL), const), pl.BlockSpec((1, D_MODEL), const),
                  _resident(wg.shape, const), _resident(wu.shape, const), _resident(wd.shape, const)]
                 + _embed_operands(embed, tm, lambda i: (i, 0)),
        out_specs=pl.BlockSpec((tm, D_MODEL), lambda i: (i, 0)),
        compiler_params=_params(("parallel",)),
        name="dense_swiglu",
    )(x, g_pre, g_post, wg, wu, wd, *embed)


PIECES = D_MODEL // 2 // HEAD_DIM
SC_WINDOW = 128
EXPERT_TILE = 512


def _pack_rows(vals):
    bits = lax.bitcast_convert_type(vals.astype(BF16).astype(F32), jnp.uint32)
    half = D_MODEL // 2
    word = bits[:, :half] | (bits[:, half:] >> 16)
    return [lax.bitcast_convert_type(word[:, c * HEAD_DIM:(c + 1) * HEAD_DIM], jnp.int32) for c in range(PIECES)]


def _unpack_rows(piece_refs):
    words = [lax.bitcast_convert_type(r[...], jnp.uint32) for r in piece_refs]
    hi = [lax.bitcast_convert_type(w & jnp.uint32(0xFFFF0000), F32) for w in words]
    lo = [lax.bitcast_convert_type(w << 16, F32) for w in words]
    return jnp.concatenate(hi + lo, axis=1)


def _route_kernel(x_ref, gpre_ref, rhi_ref, rlo_ref, *refs, n_experts):
    piece_refs = refs[:PIECES]
    rec_ref, rank_ref, count_ref, carry_ref = refs[PIECES:]
    i = pl.program_id(0)
    tm = x_ref.shape[0]

    @pl.when(i == 0)
    def _():
        carry_ref[...] = jnp.zeros_like(carry_ref)

    hf = _rms(x_ref[...], gpre_ref[...])
    for r, piece in zip(piece_refs, _pack_rows(hf)):
        r[...] = piece
    h_hi = hf.astype(BF16)
    h_lo = (hf - h_hi.astype(F32)).astype(BF16)
    logits = _dot(h_hi, rhi_ref[...]) + _dot(h_hi, rlo_ref[...]) + _dot(h_lo, rhi_ref[...])
    lane = lax.broadcasted_iota(jnp.int32, logits.shape, 1).astype(F32)
    logits = jnp.where(lane < n_experts, logits, NEG_INF)
    m1 = jnp.max(logits, axis=1, keepdims=True)
    i1 = jnp.min(jnp.where(logits == m1, lane, 1e9), axis=1, keepdims=True)
    rest = jnp.where(lane == i1, NEG_INF, logits)
    m2 = jnp.max(rest, axis=1, keepdims=True)
    i2 = jnp.min(jnp.where(rest == m2, lane, 1e9), axis=1, keepdims=True)
    ex = jnp.exp(m2 - m1)
    w1 = 1.0 / (1.0 + ex)
    rec = jnp.where(lane == 0, i1, jnp.where(lane == 1, i2, jnp.where(lane == 2, w1, jnp.where(lane == 3, ex * w1, 0.0))))
    rec_ref[...] = rec
    rec_t = rec.T
    e1 = rec_t[0:1, :]
    e2 = rec_t[1:2, :]
    sub = lax.broadcasted_iota(jnp.int32, (8, tm), 0).astype(F32)
    oh1 = jnp.where(sub == e1, 1.0, 0.0)
    oh2 = jnp.where(sub == e2, 1.0, 0.0)
    chosen = oh1 + oh2
    src = lax.broadcasted_iota(jnp.int32, (tm, tm), 0)
    dst = lax.broadcasted_iota(jnp.int32, (tm, tm), 1)
    before = jnp.where(src < dst, 1.0, 0.0).astype(BF16)
    rank = _dot(chosen.astype(BF16), before) + carry_ref[:, 0:1]
    rank1 = jnp.sum(oh1 * rank, axis=0, keepdims=True)
    rank2 = jnp.sum(oh2 * rank, axis=0, keepdims=True)
    rank_ref[...] = jnp.where(sub == 0, rank1, jnp.where(sub == 1, rank2, jnp.where(sub == 2, e1, jnp.where(sub == 3, e2, 0.0))))
    carry_ref[...] = carry_ref[...] + jnp.sum(chosen, axis=1, keepdims=True)
    count_ref[...] = carry_ref[...]


def _route(x, g_pre, r_hi, r_lo, n_experts, tm):
    t = x.shape[0]
    assert n_experts <= 8
    piece = pl.BlockSpec((tm, HEAD_DIM), lambda i: (i, 0))
    return pl.pallas_call(
        functools.partial(_route_kernel, n_experts=n_experts),
        out_shape=(*[jax.ShapeDtypeStruct((t, HEAD_DIM), jnp.int32)] * PIECES,
                   jax.ShapeDtypeStruct((t, HEAD_DIM), F32), jax.ShapeDtypeStruct((8, t), F32),
                   jax.ShapeDtypeStruct((8, HEAD_DIM), F32)),
        grid=(t // tm,),
        in_specs=[pl.BlockSpec((tm, D_MODEL), lambda i: (i, 0)), pl.BlockSpec((1, D_MODEL), lambda i: (0, 0)),
                  pl.BlockSpec((D_MODEL, HEAD_DIM), lambda i: (0, 0)), pl.BlockSpec((D_MODEL, HEAD_DIM), lambda i: (0, 0))],
        out_specs=(*[piece] * PIECES, piece, pl.BlockSpec((8, tm), lambda i: (0, i)),
                   pl.BlockSpec((8, HEAD_DIM), lambda i: (0, 0))),
        scratch_shapes=[pltpu.VMEM((8, HEAD_DIM), F32)],
        compiler_params=_params(("arbitrary",)),
        name="moe_route",
    )(x, g_pre, r_hi, r_lo)


def _slot_kernel(rank_ref, start_ref, s1_ref, s2_ref):
    tm = rank_ref.shape[1]
    sub = lax.broadcasted_iota(jnp.int32, (8, tm), 0).astype(F32)
    start = start_ref[:, 0:1]
    rows = rank_ref[...]
    for choice, out in ((0, s1_ref), (1, s2_ref)):
        base = jnp.sum(jnp.where(sub == rows[2 + choice:3 + choice, :], start, 0.0), axis=0, keepdims=True)
        out[...] = (base + rows[choice:choice + 1, :]).astype(jnp.int32)


def _slots(ranks, starts, tm):
    t = ranks.shape[1]
    row = pl.BlockSpec((1, tm), lambda i: (0, i))
    return pl.pallas_call(
        _slot_kernel,
        out_shape=(jax.ShapeDtypeStruct((1, t), jnp.int32),) * 2,
        grid=(t // tm,),
        in_specs=[pl.BlockSpec((8, tm), lambda i: (0, i)), pl.BlockSpec((8, HEAD_DIM), lambda i: (0, 0))],
        out_specs=(row, row),
        compiler_params=_params(("parallel",)),
        name="moe_slots",
    )(ranks, starts)


def _sc_mesh():
    return plsc.VectorSubcoreMesh(core_axis_name="core", subcore_axis_name="subcore")


def _sc_scatter_rows(srcs, idxs, n_rows):
    ns, nk = len(srcs), len(idxs)
    half = srcs[0].shape[0] // SC_WINDOW // 2

    @functools.partial(pl.kernel, mesh=_sc_mesh(), scratch_types=[],
                       out_type=tuple(jax.ShapeDtypeStruct((n_rows, HEAD_DIM), srcs[0].dtype) for _ in range(ns)))
    def scatter(*refs):
        out_hbm = refs[ns + nk:]

        def body(*blocks):
            for idx in blocks[ns:]:
                for c in range(ns):
                    pltpu.sync_copy(blocks[c], out_hbm[c].at[idx.at[0]])

        pltpu.emit_pipeline(
            body, grid=(2, half),
            in_specs=[pl.BlockSpec((SC_WINDOW, HEAD_DIM), lambda i, j: (i * half + j, 0)) for _ in range(ns)]
                     + [pl.BlockSpec((1, SC_WINDOW), lambda i, j: (0, i * half + j)) for _ in range(nk)],
            out_specs=[],
            core_axis_name=("core", "subcore"),
            dimension_semantics=(pltpu.PARALLEL, pltpu.PARALLEL),
        )(*refs[:ns + nk])

    return scatter(*srcs, *idxs)


def _sc_gather_rows(tables, idx):
    nt = len(tables)
    n = idx.shape[1]
    half = n // SC_WINDOW // 2

    @functools.partial(pl.kernel, mesh=_sc_mesh(), scratch_types=[],
                       out_type=tuple(jax.ShapeDtypeStruct((n, HEAD_DIM), tables[0].dtype) for _ in range(nt)))
    def gather(*refs):
        table_hbm = refs[:nt]

        def body(idx_blk, *out_blks):
            for c in range(nt):
                pltpu.sync_copy(table_hbm[c].at[idx_blk.at[0]], out_blks[c])

        pltpu.emit_pipeline(
            body, grid=(2, half),
            in_specs=[pl.BlockSpec((1, SC_WINDOW), lambda i, j: (0, i * half + j))],
            out_specs=[pl.BlockSpec((SC_WINDOW, HEAD_DIM), lambda i, j: (i * half + j, 0)) for _ in range(nt)],
            core_axis_name=("core", "subcore"),
            dimension_semantics=(pltpu.PARALLEL, pltpu.PARALLEL),
        )(refs[nt], *refs[nt + 1:])

    return gather(*tables, idx)


def _expert_kernel(tile_expert_ref, n_used_ref, *refs):
    x_refs = refs[:PIECES]
    wg_ref, wu_ref, wd_ref = refs[PIECES:PIECES + 3]
    y_refs = refs[PIECES + 3:]

    @pl.when(pl.program_id(0) < n_used_ref[0])
    def _():
        h = _unpack_rows(x_refs).astype(BF16)
        y = _swiglu(h, wg_ref[0], wu_ref[0], wd_ref[0])
        for r, piece in zip(y_refs, _pack_rows(y)):
            r[...] = piece


def _expert_ffn(xs, tile_expert, n_used, wg, wu, wd):
    n_rows = xs[0].shape[0]
    rows = pl.BlockSpec((EXPERT_TILE, HEAD_DIM), lambda i, te, nu: (jnp.minimum(i, nu[0] - 1), 0))
    out_rows = pl.BlockSpec((EXPERT_TILE, HEAD_DIM), lambda i, te, nu: (i, 0))
    expert = lambda i, te, nu: (te[i], 0, 0)
    grid_spec = pltpu.PrefetchScalarGridSpec(
        num_scalar_prefetch=2,
        grid=(n_rows // EXPERT_TILE,),
        in_specs=[rows] * PIECES + [_resident((1,) + wg.shape[1:], expert), _resident((1,) + wu.shape[1:], expert),
                                    _resident((1,) + wd.shape[1:], expert)],
        out_specs=[out_rows] * PIECES,
    )
    return pl.pallas_call(
        _expert_kernel,
        out_shape=[jax.ShapeDtypeStruct((n_rows, HEAD_DIM), jnp.int32)] * PIECES,
        grid_spec=grid_spec,
        compiler_params=_params(("arbitrary",)),
        name="moe_expert_swiglu",
    )(tile_expert, n_used, *xs, wg, wu, wd)


def _combine_kernel(x_ref, rec_ref, gpost_ref, p_ref, ein_ref, epost_ref, egate_ref, eproj_ref, *refs):
    proj = _dot(p_ref[...].astype(BF16), eproj_ref[...])
    y1 = _unpack_rows(refs[:PIECES])
    y2 = _unpack_rows(refs[PIECES:2 * PIECES])
    o_ref = refs[2 * PIECES]
    rec = rec_ref[...]
    y = rec[:, 2:3] * y1 + rec[:, 3:4] * y2
    mixed = x_ref[...] + _rms(y, gpost_ref[...])
    o_ref[...] = _embed_tail(mixed, proj, ein_ref, epost_ref, egate_ref)


def _combine(x, rec, g_post, embed, y1, y2, tm):
    t = x.shape[0]
    piece = pl.BlockSpec((tm, HEAD_DIM), lambda i: (i, 0))
    return pl.pallas_call(
        _combine_kernel,
        out_shape=jax.ShapeDtypeStruct((t, D_MODEL), F32),
        grid=(t // tm,),
        in_specs=[pl.BlockSpec((tm, D_MODEL), lambda i: (i, 0)), piece, pl.BlockSpec((1, D_MODEL), lambda i: (0, 0))]
                 + _embed_operands(embed, tm, lambda i: (i, 0)) + [piece] * (2 * PIECES),
        out_specs=pl.BlockSpec((tm, D_MODEL), lambda i: (i, 0)),
        compiler_params=_params(("parallel",)),
        name="moe_combine",
    )(x, rec, g_post, *embed, *y1, *y2)


def _moe(x, g_pre, g_post, router, wg, wu, wd, embed, tm):
    t = x.shape[0]
    n_experts = router.shape[1]
    router = jnp.pad(router, ((0, 0), (0, HEAD_DIM - n_experts)))
    r_hi = router.astype(BF16)
    r_lo = (router - r_hi.astype(F32)).astype(BF16)
    *h_pieces, rec, ranks, counts = _route(x, g_pre, r_hi, r_lo, n_experts, tm)

    counts = counts[:n_experts, 0].astype(jnp.int32)
    tiles = (counts + EXPERT_TILE - 1) // EXPERT_TILE
    tile_end = jnp.cumsum(tiles)
    starts = ((tile_end - tiles) * EXPERT_TILE).astype(F32)
    starts = jnp.broadcast_to(jnp.pad(starts, (0, 8 - n_experts))[:, None], (8, HEAD_DIM))
    n_rows = TOP_K * t + n_experts * EXPERT_TILE
    n_tiles = n_rows // EXPERT_TILE
    n_used = tile_end[-1:]
    tile_ids = jnp.minimum(jnp.arange(n_tiles, dtype=jnp.int32), n_used[0] - 1)
    tile_expert = jnp.sum(tile_ids[:, None] >= tile_end[None, :], axis=1).astype(jnp.int32)

    slot1, slot2 = _slots(ranks, starts, min(t, 8192))
    xs = []
    for c in range(0, PIECES, 2):
        xs += _sc_scatter_rows(h_pieces[c:c + 2], [slot1, slot2], n_rows)
    ys = _expert_ffn(xs, tile_expert, n_used, wg, wu, wd)
    gathered = []
    for slot in (slot1, slot2):
        rows = []
        for c in range(0, PIECES, 2):
            rows += _sc_gather_rows(ys[c:c + 2], slot)
        gathered.append(rows)
    return _combine(x, rec, g_post, embed, gathered[0], gathered[1], tm)


def kernel(x, p, positions, w_in, conv_w, hgrn_lb_logits, hgrn_norm_g, w_branch, w_out, g_mix_pre, g_mix_post, g_ffn_pre, g_ffn_post, dense_w_gate, dense_w_up, dense_w_down, moe_router, moe_w_gate, moe_w_up, moe_w_down, ple_w_proj, ple_w_gate, ple_g_in, ple_g_post):
    batch, seq, _ = x.shape
    depth = w_in.shape[0]
    t = batch * seq
    tm = min(512, seq)
    tm_merge = min(512, seq)
    tm_ffn = min(512, seq)

    xt = x.reshape(t, D_MODEL)
    cos, sin = _rope_tables(positions.astype(F32).reshape(t, 1), tm)
    row = lambda a: a.reshape(1, -1)

    for i in range(depth):
        w_i = _to_bf16(w_in[i])
        g_pre = row(g_mix_pre[i])
        *qkvs, h = _inproj_attention(xt, g_pre, w_i[:, :A_COLS], cos, sin, batch, seq, tm)
        side_cols = ((A_COLS, A_COLS + 3 * BRANCH), (A_COLS + 3 * BRANCH, A_COLS + B_COLS),
                     (A_COLS + B_COLS, A_COLS + B_COLS + C_COLS))
        aos, lses, sides = zip(*[_attention_group(qkvs[g], h, w_i[:, lo:hi], g == 2, batch, seq, g)
                                 for g, (lo, hi) in enumerate(side_cols)])
        qzz, ig, cb = sides
        o_f, o_b = _hgrn(qzz, ig, hgrn_lb_logits, batch, seq, i)
        xt = _merge(xt, aos, lses, o_f, o_b, ig, cb, h, w_i[:, A_COLS + B_COLS + C_COLS:], conv_w[i],
                    row(hgrn_norm_g[i]), w_branch[i].astype(BF16), w_out[i].astype(BF16), row(g_mix_post[i]),
                    seq, tm_merge)

        embed = (p[i].reshape(t, -1), row(ple_g_in[i]), row(ple_g_post[i]),
                 ple_w_gate[i].astype(BF16), ple_w_proj[i].astype(BF16))
        j = i // 2
        if i % 2 == 0:
            xt = _ffn(xt, row(g_ffn_pre[i]), row(g_ffn_post[i]), dense_w_gate[j].astype(BF16),
                      dense_w_up[j].astype(BF16), dense_w_down[j].astype(BF16), embed, tm_ffn)
        else:
            xt = _moe(xt, row(g_ffn_pre[i]), row(g_ffn_post[i]), moe_router[j], _to_bf16(moe_w_gate[j]),
                      _to_bf16(moe_w_up[j]), _to_bf16(moe_w_down[j]), embed, tm)

    return xt.reshape(batch, seq, D_MODEL)
```

```python
import functools
import math

import jax
import jax.numpy as jnp
from jax import lax
from jax.experimental import pallas as pl
from jax.experimental.pallas import tpu as pltpu
from jax.experimental.pallas import tpu_sc as plsc

D_MODEL = 1024
EPS = 1e-6
NEG_INF = -1e30

HEAD_DIM = 128
N_HEADS = 4
BRANCH = N_HEADS * HEAD_DIM
DIL_GROUPS = ((128, 1), (512, 4), (2048, 16))
N_GROUPS = len(DIL_GROUPS)
BLK = 64
ROT_DIM = HEAD_DIM // 4
ROPE_THETA = 500000.0
TOP_K = 2

A_COLS = N_GROUPS * 3 * BRANCH
B_COLS = 5 * BRANCH
C_COLS = 3 * BRANCH
G_COLS = 3 * D_MODEL

BF16 = jnp.bfloat16
F32 = jnp.float32

VMEM_LIMIT = 56 * 1024 * 1024


def _params(sem):
    return pltpu.CompilerParams(dimension_semantics=sem, vmem_limit_bytes=VMEM_LIMIT)


def _cast_kernel(w_ref, o_ref):
    o_ref[...] = w_ref[...].astype(o_ref.dtype)


def _to_bf16(w):
    cols = w.shape[-1]
    flat = w.reshape(-1, cols)
    rows = flat.shape[0]
    block = rows
    while block % 32 == 0 and block * cols * 4 > 8 * 1024 * 1024:
        block //= 2
    out = pl.pallas_call(
        _cast_kernel,
        out_shape=jax.ShapeDtypeStruct(flat.shape, BF16),
        grid=(rows // block,),
        in_specs=[pl.BlockSpec((block, cols), lambda i: (i, 0))],
        out_specs=pl.BlockSpec((block, cols), lambda i: (i, 0)),
        compiler_params=_params(("parallel",)),
        name="weights_to_bf16",
    )(flat)
    return out.reshape(w.shape)


def _split_cast_kernel(w_ref, *o_refs, bounds):
    for o_ref, (lo, hi) in zip(o_refs, bounds):
        o_ref[...] = w_ref[:, lo:hi].astype(o_ref.dtype)


def _split_to_bf16(w, bounds, rows=128):
    n, cols = w.shape
    return pl.pallas_call(
        functools.partial(_split_cast_kernel, bounds=bounds),
        out_shape=tuple(jax.ShapeDtypeStruct((n, hi - lo), BF16) for lo, hi in bounds),
        grid=(n // rows,),
        in_specs=[pl.BlockSpec((rows, cols), lambda i: (i, 0))],
        out_specs=tuple(pl.BlockSpec((rows, hi - lo), lambda i: (i, 0)) for lo, hi in bounds),
        compiler_params=_params(("parallel",)),
        name="split_weights_to_bf16",
    )(w)


def _resident(shape, index_map):
    return pl.BlockSpec(shape, index_map, pipeline_mode=pl.Buffered(1))


def _rms(xf, g):
    return xf * lax.rsqrt(jnp.mean(xf * xf, axis=-1, keepdims=True) + EPS) * g


def _sigmoid(z):
    return 0.5 * jnp.tanh(0.5 * z) + 0.5


def _sigmoid_rel(z):
    return 1.0 / (1.0 + jnp.exp(-z))


def _dot(a, b):
    return jnp.dot(a, b, preferred_element_type=F32)


def _dot_nt(a, b):
    return lax.dot_general(a, b, (((1,), (1,)), ((), ())), preferred_element_type=F32)


def _dot_tn(a, b):
    return lax.dot_general(a, b, (((0,), (0,)), ((), ())), preferred_element_type=F32)


ROPE_PACK = HEAD_DIM // ROT_DIM


def _rope_table_kernel(pos_ref, freq_ref, sign_ref, cos_ref, sin_ref):
    pos = pos_ref[...]
    group = lax.broadcasted_iota(jnp.int32, cos_ref.shape, 1) // ROT_DIM
    pos_lanes = jnp.zeros(cos_ref.shape, F32)
    for k in range(ROPE_PACK):
        pos_lanes = jnp.where(group == k, pos[:, k:k + 1], pos_lanes)
    ang = pos_lanes * freq_ref[...]
    cos_ref[...] = jnp.cos(ang)
    sin_ref[...] = jnp.sin(ang) * sign_ref[...]


def _rope_tables(pos_packed, rows):
    n = pos_packed.shape[0]
    half = ROT_DIM // 2
    inv_freq = ROPE_THETA ** (-jnp.arange(0, ROT_DIM, 2, dtype=F32) / ROT_DIM)
    freq = jnp.tile(jnp.concatenate([inv_freq, inv_freq]), ROPE_PACK)[None, :]
    sign = jnp.tile(jnp.concatenate([-jnp.ones((half,), F32), jnp.ones((half,), F32)]), ROPE_PACK)[None, :]
    row = pl.BlockSpec((1, HEAD_DIM), lambda i: (0, 0))
    tab = pl.BlockSpec((rows, HEAD_DIM), lambda i: (i, 0))
    return pl.pallas_call(
        _rope_table_kernel,
        out_shape=(jax.ShapeDtypeStruct((n, HEAD_DIM), F32),) * 2,
        grid=(n // rows,),
        in_specs=[pl.BlockSpec((rows, ROPE_PACK), lambda i: (i, 0)), row, row],
        out_specs=(tab, tab),
        compiler_params=_params(("parallel",)),
        name="rope_tables",
    )(pos_packed, freq, sign)


def _unpack_rope_tables(cosp_ref, sinp_ref, tab_ref):
    rows = cosp_ref.shape[0]
    lane = lax.broadcasted_iota(jnp.int32, (rows, HEAD_DIM), 1)
    for k in range(ROPE_PACK):
        for idx, (ref, fill) in enumerate(((cosp_ref, 1.0), (sinp_ref, 0.0))):
            packed = ref[...]
            shifted = pltpu.roll(packed, HEAD_DIM - ROT_DIM * k, 1) if k else packed
            tab_ref[idx, pl.ds(k, rows, stride=ROPE_PACK), :] = jnp.where(lane < ROT_DIM, shifted, fill)


def _rope_head(th, cos, sin):
    half = ROT_DIM // 2
    lane = lax.broadcasted_iota(jnp.int32, th.shape, 1)
    swapped = jnp.where(lane < half, pltpu.roll(th, HEAD_DIM - half, 1), pltpu.roll(th, half, 1))
    return th * cos + swapped * sin


def _inproj_attn_kernel(x_ref, g_ref, w_ref, cosp_ref, sinp_ref, o0_ref, o1_ref, o2_ref, h_ref, stage_ref, tab_ref):
    _unpack_rope_tables(cosp_ref, sinp_ref, tab_ref)
    hf = _rms(x_ref[...], g_ref[...])
    h_ref[...] = hf.astype(BF16)
    n_pieces = D_MODEL // HEAD_DIM
    for pc in range(n_pieces):
        stage_ref[pc] = hf[:, pc * HEAD_DIM:(pc + 1) * HEAD_DIM]
    scale = 1.0 / math.sqrt(HEAD_DIM)
    tm = x_ref.shape[0]
    for grp, o_ref in enumerate((o0_ref, o1_ref, o2_ref)):
        dil = DIL_GROUPS[grp][1]
        sub = tm // dil
        if dil == 1:
            h, cos, sin = hf.astype(BF16), tab_ref[0], tab_ref[1]
        else:
            residue = lambda ref, *lead: jnp.concatenate(
                [ref[(*lead, pl.ds(r, sub, stride=dil), slice(None))] for r in range(dil)], axis=0)
            h = jnp.concatenate([residue(stage_ref, pc) for pc in range(n_pieces)], axis=1).astype(BF16)
            cos, sin = residue(tab_ref, 0), residue(tab_ref, 1)
        for part in range(3):
            c = grp * 3 + part
            t = _dot(h, w_ref[:, c * BRANCH:(c + 1) * BRANCH])
            if part < 2:
                heads = []
                for hd in range(N_HEADS):
                    r = _rope_head(t[:, hd * HEAD_DIM:(hd + 1) * HEAD_DIM], cos, sin)
                    heads.append(r * scale if part == 0 else r)
                t = jnp.concatenate(heads, axis=1)
            t = t.astype(BF16)
            for r in range(dil):
                o_ref[0, r, :, part * BRANCH:(part + 1) * BRANCH] = t[r * sub:(r + 1) * sub, :]


def _inproj_attention(x, g, w, cos, sin, batch, seq, tm):
    per_seq = seq // tm
    rows = lambda width: pl.BlockSpec((tm, width), lambda b, j: (b * per_seq + j, 0))
    packed = pl.BlockSpec((tm // ROPE_PACK, HEAD_DIM), lambda b, j: (b * per_seq + j, 0))
    out_shape = tuple(jax.ShapeDtypeStruct((batch, dil, seq // dil, 3 * BRANCH), BF16) for _, dil in DIL_GROUPS)
    out_shape += (jax.ShapeDtypeStruct((batch * seq, D_MODEL), BF16),)
    out_specs = tuple(pl.BlockSpec((1, dil, tm // dil, 3 * BRANCH), lambda b, j: (b, 0, j, 0))
                      for _, dil in DIL_GROUPS) + (rows(D_MODEL),)
    return pl.pallas_call(
        _inproj_attn_kernel, out_shape=out_shape, grid=(batch, per_seq),
        in_specs=[rows(D_MODEL), pl.BlockSpec((1, D_MODEL), lambda b, j: (0, 0)),
                  _resident((D_MODEL, A_COLS), lambda b, j: (0, 0)), packed, packed],
        out_specs=out_specs,
        scratch_shapes=[pltpu.VMEM((D_MODEL // HEAD_DIM, tm, HEAD_DIM), F32), pltpu.VMEM((2, tm, HEAD_DIM), F32)],
        compiler_params=_params(("parallel", "parallel")), name="inproj_attention",
    )(x, g, w, cos, sin)


def _attn_kernel(q_ref, k_ref, kp_ref, kn_ref, v_ref, vp_ref, vn_ref, h_ref, w_ref, o_ref, lse_ref, side_ref,
                 o_stage, lse_stage, *, dil, nj, n_steps, conv_side):
    n = pl.program_id(1)
    part_rows = h_ref.shape[0] // N_HEADS

    def side_projection(part):
        rs = slice(part * part_rows, (part + 1) * part_rows)
        h = h_ref[rs, :]
        if conv_side:
            u = _dot(h, w_ref[:, 0:BRANCH])
            b_gate = _dot(h, w_ref[:, BRANCH:2 * BRANCH])
            c_gate = _dot(h, w_ref[:, 2 * BRANCH:3 * BRANCH])
            side_ref[rs, 0:BRANCH] = (c_gate * u).astype(BF16)
            side_ref[rs, BRANCH:2 * BRANCH] = b_gate.astype(BF16)
        else:
            for c in range(w_ref.shape[1] // BRANCH):
                cols = slice(c * BRANCH, (c + 1) * BRANCH)
                side_ref[rs, cols] = _dot(h, w_ref[:, cols]).astype(BF16)

    qb = 2 if nj % 2 == 0 else 1
    q_rows, w_rows, n_tiles = qb * BLK, (qb + 2) * BLK, nj // qb
    col_lo = jnp.where(n > 0, 0, BLK)
    col_hi = jnp.where(n < n_steps - 1, w_rows, w_rows - BLK)
    row = lax.broadcasted_iota(jnp.int32, (q_rows, w_rows), 0)
    col = lax.broadcasted_iota(jnp.int32, (q_rows, w_rows), 1)
    band = jnp.abs(col - BLK - row) <= BLK
    masks = {}
    for j in range(n_tiles):
        m = band
        if j == 0:
            m = m & (col >= col_lo)
        if j == n_tiles - 1:
            m = m & (col < col_hi)
        masks[j] = m
    pairs = [(r, j) for r in range(dil) for j in range(n_tiles)]
    lane = lax.broadcasted_iota(jnp.int32, (len(pairs), q_rows, HEAD_DIM), 2)

    def natural_rows(r, j):
        return pl.ds(j * q_rows * dil + r, q_rows, stride=dil)

    def window(own_ref, prev_ref, next_ref, r, j, hs):
        parts = []
        for w in range(j * qb - 1, j * qb + qb + 1):
            if w < 0:
                parts.append(prev_ref[0, r, :, hs])
            elif w >= nj:
                parts.append(next_ref[0, r, :, hs])
            else:
                parts.append(own_ref[0, r, w * BLK:(w + 1) * BLK, hs])
        return jnp.concatenate(parts, axis=0)

    lse_tiles = jnp.zeros((len(pairs), q_rows, HEAD_DIM), F32)
    for hd in range(N_HEADS):
        side_projection(hd)
        hs = slice(hd * HEAD_DIM, (hd + 1) * HEAD_DIM)
        scores = [_dot_nt(q_ref[0, r, j * q_rows:(j + 1) * q_rows, hs], window(k_ref, kp_ref, kn_ref, r, j, hs))
                  for r, j in pairs]
        s = jnp.stack([jnp.where(masks[j], sc, NEG_INF) for (r, j), sc in zip(pairs, scores)])
        m = jnp.max(s, axis=2, keepdims=True)
        p = jnp.exp(s - m)
        l = jnp.sum(p, axis=2, keepdims=True)
        pb = p.astype(BF16)
        outs = jnp.stack([_dot(pb[i], window(v_ref, vp_ref, vn_ref, r, j, hs)) for i, (r, j) in enumerate(pairs)])
        outs = outs * (1.0 / l)
        lse_tiles = jnp.where(lane == hd, m + jnp.log(l), lse_tiles)
        for i, (r, j) in enumerate(pairs):
            o_stage[hd, natural_rows(r, j), :] = outs[i]
        o_ref[:, hs] = o_stage[hd].astype(o_ref.dtype)
    for i, (r, j) in enumerate(pairs):
        lse_stage[natural_rows(r, j), :] = lse_tiles[i]
    lse_ref[...] = lse_stage[...]


def _attention_group(qkv, h, w_side, conv_side, batch, seq, group):
    _, dil = DIL_GROUPS[group]
    t = batch * seq
    sub_len = seq // dil
    rows_per_step = min(seq, 16 * BLK)
    nj = rows_per_step // (dil * BLK)
    n_steps = seq // rows_per_step
    n_blk = sub_len // BLK
    sub_rows = rows_per_step // dil

    def own(part):
        return pl.BlockSpec((1, dil, sub_rows, BRANCH), lambda b, n: (b, 0, n, part))

    def prev(part):
        return pl.BlockSpec((1, dil, BLK, BRANCH), lambda b, n: (b, 0, jnp.maximum(n * nj - 1, 0), part))

    def nxt(part):
        return pl.BlockSpec((1, dil, BLK, BRANCH), lambda b, n: (b, 0, jnp.minimum((n + 1) * nj, n_blk - 1), part))

    side_cols = 2 * BRANCH if conv_side else w_side.shape[1]
    step_rows = lambda width: pl.BlockSpec((rows_per_step, width), lambda b, n: (b * n_steps + n, 0))
    return pl.pallas_call(
        functools.partial(_attn_kernel, dil=dil, nj=nj, n_steps=n_steps, conv_side=conv_side),
        out_shape=(jax.ShapeDtypeStruct((t, BRANCH), BF16), jax.ShapeDtypeStruct((t, HEAD_DIM), F32),
                   jax.ShapeDtypeStruct((t, side_cols), BF16)),
        grid=(batch, n_steps),
        in_specs=[own(0), own(1), prev(1), nxt(1), own(2), prev(2), nxt(2), step_rows(D_MODEL),
                  _resident(w_side.shape, lambda b, n: (0, 0))],
        out_specs=(step_rows(BRANCH), step_rows(HEAD_DIM), step_rows(side_cols)),
        scratch_shapes=[pltpu.VMEM((N_HEADS, rows_per_step, HEAD_DIM), F32), pltpu.VMEM((rows_per_step, HEAD_DIM), F32)],
        compiler_params=_params(("parallel", "parallel")),
        name=f"dilated_attention_g{group}",
    )(qkv, qkv, qkv, qkv, qkv, qkv, qkv, h, w_side)


def _hgrn_kernel(logit_ref, qf_ref, zf_ref, vf_ref, qb_ref, zb_ref, vb_ref, of_ref, ob_ref, state_ref,
                 *, layer, n_chunks):
    c = pl.program_id(1)

    @pl.when(c == 0)
    def _():
        state_ref[...] = jnp.zeros_like(state_ref)

    depth = logit_ref.shape[1]
    lbs = []
    for dirn in range(2):
        rows = [logit_ref[dirn, j:j + 1, :] for j in range(depth)]
        top = functools.reduce(jnp.maximum, rows)
        exps = [jnp.exp(rw - top) for rw in rows]
        lbs.append(sum(exps[1:layer + 1], jnp.zeros_like(top)) / sum(exps[1:], exps[0]))

    r = lax.broadcasted_iota(jnp.int32, (BLK, BLK), 0)
    cc = lax.broadcasted_iota(jnp.int32, (BLK, BLK), 1)

    def bcast_rows(per_chunk):
        return jnp.concatenate([jnp.broadcast_to(v, (BLK, BRANCH)) for v in per_chunk], axis=0)

    for dirn, (q_ref, z_ref, v_ref, o_ref) in enumerate(
            ((qf_ref, zf_ref, vf_ref, of_ref), (qb_ref, zb_ref, vb_ref, ob_ref))):
        tri = (cc <= r) if dirn == 0 else (cc >= r)
        tri_b = jnp.where(tri, 1.0, 0.0).astype(BF16)
        last, mid_row = (BLK - 1, BLK // 2 - 1) if dirn == 0 else (0, BLK // 2)
        lb = lbs[dirn]
        q = q_ref[...].astype(F32)
        z = z_ref[...].astype(F32)
        v = v_ref[...]
        sg = _sigmoid_rel(z)
        logf = jnp.log(lb + (1.0 - lb) * sg)
        kk = (1.0 - lb) * (1.0 - sg)
        hi = logf.astype(BF16)
        lo = (logf - hi.astype(F32)).astype(BF16)
        chunks = [slice(g * BLK, (g + 1) * BLK) for g in range(n_chunks)]
        cums = [_dot(tri_b, hi[ch]) + _dot(tri_b, lo[ch]) for ch in chunks]
        totals = [cm[last:last + 1, :] for cm in cums]
        mids = [cm[mid_row:mid_row + 1, :] for cm in cums]
        cum = jnp.concatenate(cums, axis=0)
        mid = bcast_rows(mids)
        qa = q * jnp.exp(cum - mid)
        ka = kk * jnp.exp(mid - cum)
        q_dec = (qa * bcast_rows([jnp.exp(md) for md in mids])).astype(BF16)
        k_dec = (ka * bcast_rows([jnp.exp(tt - md) for tt, md in zip(totals, mids)])).astype(BF16)
        qa = qa.astype(BF16)
        ka = ka.astype(BF16)
        tiles = [(g, hd) for g in range(n_chunks) for hd in range(N_HEADS)]
        sl = lambda g, hd: (slice(g * BLK, (g + 1) * BLK), slice(hd * HEAD_DIM, (hd + 1) * HEAD_DIM))
        atts = [jnp.where(tri, _dot_nt(qa[sl(g, hd)], ka[sl(g, hd)]), 0.0).astype(BF16) for g, hd in tiles]
        intra = {t: _dot(att, v[sl(*t)]) for t, att in zip(tiles, atts)}
        kvs = {t: _dot_tn(v[sl(*t)], k_dec[sl(*t)]) for t in tiles}
        decays = [jnp.exp(tt) for tt in totals]
        order = range(n_chunks) if dirn == 0 else range(n_chunks - 1, -1, -1)
        for hd in range(N_HEADS):
            hs = slice(hd * HEAD_DIM, (hd + 1) * HEAD_DIM)
            state_t = state_ref[dirn, hd]
            for g in order:
                o_ref[g * BLK:(g + 1) * BLK, hs] = intra[(g, hd)] + _dot_nt(q_dec[sl(g, hd)], state_t.astype(BF16))
                state_t = state_t * decays[g][:, hs] + kvs[(g, hd)]
            state_ref[dirn, hd] = state_t


def _hgrn(qzz, ig, logits, batch, seq, layer):
    t = qzz.shape[0]
    n_chunks = 4
    rows = n_chunks * BLK
    nc = seq // rows
    depth = logits.shape[1]

    def fwd(colblk):
        return pl.BlockSpec((rows, BRANCH), lambda b, c: (b * nc + c, colblk))

    def bwd(colblk):
        return pl.BlockSpec((rows, BRANCH), lambda b, c: (b * nc + nc - 1 - c, colblk))

    return pl.pallas_call(
        functools.partial(_hgrn_kernel, layer=layer, n_chunks=n_chunks),
        out_shape=(jax.ShapeDtypeStruct((t, BRANCH), F32),) * 2,
        grid=(batch, nc),
        in_specs=[pl.BlockSpec((2, depth, BRANCH), lambda b, c: (0, 0, 0)),
                  fwd(0), fwd(1), fwd(0), bwd(0), bwd(2), bwd(0)],
        out_specs=(pl.BlockSpec((rows, BRANCH), lambda b, c: (b * nc + c, 0)),
                   pl.BlockSpec((rows, BRANCH), lambda b, c: (b * nc + nc - 1 - c, 0))),
        scratch_shapes=[pltpu.VMEM((2, N_HEADS, HEAD_DIM, HEAD_DIM), F32)],
        compiler_params=_params(("parallel", "arbitrary")),
        name="hgrn2_scan",
    )(logits, qzz, qzz, ig, qzz, qzz, ig)


def _merge_kernel(x_ref, ao0_ref, ao1_ref, ao2_ref, l0_ref, l1_ref, l2_ref, of_ref, ob_ref, bg_ref,
                  cb_ref, cbp_ref, cbn_ref, h_ref, wgate_ref, convw_ref, ng_ref, wbr_ref, wout_ref, gpost_ref,
                  o_ref, *, tm, seq):
    i = pl.program_id(0)
    h = h_ref[...]
    gates = [_sigmoid(_dot(h, wgate_ref[:, n * D_MODEL:(n + 1) * D_MODEL])) for n in range(3)]

    ls = (l0_ref[...], l1_ref[...], l2_ref[...])
    aos = (ao0_ref, ao1_ref, ao2_ref)
    lmax = jnp.maximum(jnp.maximum(ls[0], ls[1]), ls[2])
    es = [jnp.exp(l - lmax) for l in ls]
    inv = 1.0 / (es[0] + es[1] + es[2])
    a_parts = []
    for hd in range(N_HEADS):
        hs = slice(hd * HEAD_DIM, (hd + 1) * HEAD_DIM)
        acc = None
        for g in range(N_GROUPS):
            w = (es[g] * inv)[:, hd:hd + 1]
            term = w * aos[g][:, hs].astype(F32)
            acc = term if acc is None else acc + term
        a_parts.append(acc)
    o_a = jnp.concatenate(a_parts, axis=1).astype(BF16)

    b_parts = []
    for hd in range(N_HEADS):
        hs = slice(hd * HEAD_DIM, (hd + 1) * HEAD_DIM)
        o = of_ref[:, hs] + ob_ref[:, hs]
        b_parts.append(_rms(o, ng_ref[:, hs]))
    gate = bg_ref[...].astype(F32)
    o_b = (jnp.concatenate(b_parts, axis=1) * (gate * _sigmoid(gate))).astype(BF16)

    cu = cb_ref[:, 0:BRANCH].astype(F32)
    keep_prev = jnp.where((i * tm) % seq == 0, 0.0, 1.0)
    keep_next = jnp.where(((i + 1) * tm) % seq == 0, 0.0, 1.0)
    halo_rows = cbp_ref.shape[0]
    prev_row = keep_prev * cbp_ref[halo_rows - 1:halo_rows, 0:BRANCH].astype(F32)
    next_row = keep_next * cbn_ref[0:1, 0:BRANCH].astype(F32)
    rid = lax.broadcasted_iota(jnp.int32, cu.shape, 0)
    before = jnp.where(rid == 0, prev_row, pltpu.roll(cu, 1, 0))
    after = jnp.where(rid == tm - 1, next_row, pltpu.roll(cu, tm - 1, 0))
    conv = convw_ref[0:1, :] * before + convw_ref[1:2, :] * cu + convw_ref[2:3, :] * after
    o_c = (cb_ref[:, BRANCH:2 * BRANCH].astype(F32) * conv).astype(BF16)

    merged = None
    for n, o_n in enumerate((o_a, o_b, o_c)):
        up = _dot(o_n, wbr_ref[n])
        term = gates[n] * up
        merged = term if merged is None else merged + term
    y = _dot(merged.astype(BF16), wout_ref[...])
    o_ref[...] = x_ref[...] + _rms(y, gpost_ref[...])


def _merge(x, aos, lses, o_f, o_b, ig, cb, h, w_gate, conv_w, norm_g, w_branch, w_out, g_post, seq, tm):
    t = x.shape[0]
    halo = 16
    n_halo = t // halo
    per = tm // halo

    def rows(width, colblk=0):
        return pl.BlockSpec((tm, width), lambda i: (i, colblk))

    def full(shape):
        return _resident(shape, lambda i: (0,) * len(shape))

    in_specs = ([rows(D_MODEL)] + [rows(BRANCH)] * 3 + [rows(HEAD_DIM)] * 3 + [rows(BRANCH)] * 2
                + [rows(BRANCH, 1), rows(2 * BRANCH),
                   pl.BlockSpec((halo, 2 * BRANCH), lambda i: (jnp.maximum(i * per - 1, 0), 0)),
                   pl.BlockSpec((halo, 2 * BRANCH), lambda i: (jnp.minimum((i + 1) * per, n_halo - 1), 0)),
                   rows(D_MODEL), full((D_MODEL, G_COLS)), full((3, BRANCH)), full((1, BRANCH)),
                   full((3, BRANCH, D_MODEL)), full((D_MODEL, D_MODEL)), full((1, D_MODEL))])
    return pl.pallas_call(
        functools.partial(_merge_kernel, tm=tm, seq=seq),
        out_shape=jax.ShapeDtypeStruct((t, D_MODEL), F32),
        grid=(t // tm,),
        in_specs=in_specs,
        out_specs=rows(D_MODEL),
        compiler_params=_params(("parallel",)),
        name="merge_branches",
    )(x, *aos, *lses, o_f, o_b, ig, cb, cb, cb, h, w_gate, conv_w, norm_g, w_branch, w_out, g_post)


def _swiglu(h, wg, wu, wd):
    g = _dot(h, wg)
    u = _dot(h, wu)
    return _dot((g * _sigmoid(g) * u).astype(BF16), wd)


def _embed_tail(x, proj, gin_ref, gpost_ref, wgate_ref):
    gate = _sigmoid(_dot(_rms(x, gin_ref[...]).astype(BF16), wgate_ref[...]))
    return x + _rms(proj * gate, gpost_ref[...])


def _embed_operands(embed, tm, index):
    pd = embed[0].shape[1]
    const = lambda *_: (0, 0)
    return [pl.BlockSpec((tm, pd), index), pl.BlockSpec((1, D_MODEL), const), pl.BlockSpec((1, D_MODEL), const),
            _resident((D_MODEL, D_MODEL), const), _resident((pd, D_MODEL), const)]


def _ffn_kernel(x_ref, gpre_ref, gpost_ref, wg_ref, wu_ref, wd_ref, p_ref, ein_ref, epost_ref, egate_ref, eproj_ref,
                o_ref):
    proj = _dot(p_ref[...].astype(BF16), eproj_ref[...])
    x = x_ref[...]
    h = _rms(x, gpre_ref[...]).astype(BF16)
    mixed = x + _rms(_swiglu(h, wg_ref[...], wu_ref[...], wd_ref[...]), gpost_ref[...])
    o_ref[...] = _embed_tail(mixed, proj, ein_ref, epost_ref, egate_ref)


def _ffn(x, g_pre, g_post, wg, wu, wd, embed, tm):
    t = x.shape[0]
    const = lambda i: (0, 0)
    return pl.pallas_call(
        _ffn_kernel,
        out_shape=jax.ShapeDtypeStruct((t, D_MODEL), F32),
        grid=(t // tm,),
        in_specs=[pl.BlockSpec((tm, D_MODEL), lambda i: (i, 0)),
                  pl.BlockSpec((1, D_MODEL), const), pl.BlockSpec((1, D_MODEL), const),
                  _resident(wg.shape, const), _resident(wu.shape, const), _resident(wd.shape, const)]
                 + _embed_operands(embed, tm, lambda i: (i, 0)),
        out_specs=pl.BlockSpec((tm, D_MODEL), lambda i: (i, 0)),
        compiler_params=_params(("parallel",)),
        name="dense_swiglu",
    )(x, g_pre, g_post, wg, wu, wd, *embed)


PIECES = D_MODEL // 2 // HEAD_DIM
SC_WINDOW = 128
EXPERT_TILE = 512


def _pack_rows(vals):
    bits = lax.bitcast_convert_type(vals.astype(BF16).astype(F32), jnp.uint32)
    half = D_MODEL // 2
    word = bits[:, :half] | (bits[:, half:] >> 16)
    return [lax.bitcast_convert_type(word[:, c * HEAD_DIM:(c + 1) * HEAD_DIM], jnp.int32) for c in range(PIECES)]


def _unpack_rows(piece_refs):
    words = [lax.bitcast_convert_type(r[...], jnp.uint32) for r in piece_refs]
    hi = [lax.bitcast_convert_type(w & jnp.uint32(0xFFFF0000), F32) for w in words]
    lo = [lax.bitcast_convert_type(w << 16, F32) for w in words]
    return jnp.concatenate(hi + lo, axis=1)


def _route_kernel(x_ref, gpre_ref, rhi_ref, rlo_ref, *refs, n_experts):
    piece_refs = refs[:PIECES]
    rec_ref, rank_ref, count_ref, carry_ref = refs[PIECES:]
    i = pl.program_id(0)
    tm = x_ref.shape[0]

    @pl.when(i == 0)
    def _():
        carry_ref[...] = jnp.zeros_like(carry_ref)

    hf = _rms(x_ref[...], gpre_ref[...])
    for r, piece in zip(piece_refs, _pack_rows(hf)):
        r[...] = piece
    h_hi = hf.astype(BF16)
    h_lo = (hf - h_hi.astype(F32)).astype(BF16)
    logits = _dot(h_hi, rhi_ref[...]) + _dot(h_hi, rlo_ref[...]) + _dot(h_lo, rhi_ref[...])
    lane = lax.broadcasted_iota(jnp.int32, logits.shape, 1).astype(F32)
    logits = jnp.where(lane < n_experts, logits, NEG_INF)
    m1 = jnp.max(logits, axis=1, keepdims=True)
    i1 = jnp.min(jnp.where(logits == m1, lane, 1e9), axis=1, keepdims=True)
    rest = jnp.where(lane == i1, NEG_INF, logits)
    m2 = jnp.max(rest, axis=1, keepdims=True)
    i2 = jnp.min(jnp.where(rest == m2, lane, 1e9), axis=1, keepdims=True)
    ex = jnp.exp(m2 - m1)
    w1 = 1.0 / (1.0 + ex)
    rec = jnp.where(lane == 0, i1, jnp.where(lane == 1, i2, jnp.where(lane == 2, w1, jnp.where(lane == 3, ex * w1, 0.0))))
    rec_ref[...] = rec
    rec_t = rec.T
    e1 = rec_t[0:1, :]
    e2 = rec_t[1:2, :]
    sub = lax.broadcasted_iota(jnp.int32, (8, tm), 0).astype(F32)
    oh1 = jnp.where(sub == e1, 1.0, 0.0)
    oh2 = jnp.where(sub == e2, 1.0, 0.0)
    chosen = oh1 + oh2
    src = lax.broadcasted_iota(jnp.int32, (tm, tm), 0)
    dst = lax.broadcasted_iota(jnp.int32, (tm, tm), 1)
    before = jnp.where(src < dst, 1.0, 0.0).astype(BF16)
    rank = _dot(chosen.astype(BF16), before) + carry_ref[:, 0:1]
    rank1 = jnp.sum(oh1 * rank, axis=0, keepdims=True)
    rank2 = jnp.sum(oh2 * rank, axis=0, keepdims=True)
    rank_ref[...] = jnp.where(sub == 0, rank1, jnp.where(sub == 1, rank2, jnp.where(sub == 2, e1, jnp.where(sub == 3, e2, 0.0))))
    carry_ref[...] = carry_ref[...] + jnp.sum(chosen, axis=1, keepdims=True)
    count_ref[...] = carry_ref[...]


def _route(x, g_pre, r_hi, r_lo, n_experts, tm):
    t = x.shape[0]
    assert n_experts <= 8
    piece = pl.BlockSpec((tm, HEAD_DIM), lambda i: (i, 0))
    return pl.pallas_call(
        functools.partial(_route_kernel, n_experts=n_experts),
        out_shape=(*[jax.ShapeDtypeStruct((t, HEAD_DIM), jnp.int32)] * PIECES,
                   jax.ShapeDtypeStruct((t, HEAD_DIM), F32), jax.ShapeDtypeStruct((8, t), F32),
                   jax.ShapeDtypeStruct((8, HEAD_DIM), F32)),
        grid=(t // tm,),
        in_specs=[pl.BlockSpec((tm, D_MODEL), lambda i: (i, 0)), pl.BlockSpec((1, D_MODEL), lambda i: (0, 0)),
                  pl.BlockSpec((D_MODEL, HEAD_DIM), lambda i: (0, 0)), pl.BlockSpec((D_MODEL, HEAD_DIM), lambda i: (0, 0))],
        out_specs=(*[piece] * PIECES, piece, pl.BlockSpec((8, tm), lambda i: (0, i)),
                   pl.BlockSpec((8, HEAD_DIM), lambda i: (0, 0))),
        scratch_shapes=[pltpu.VMEM((8, HEAD_DIM), F32)],
        compiler_params=_params(("arbitrary",)),
        name="moe_route",
    )(x, g_pre, r_hi, r_lo)


def _slot_kernel(rank_ref, start_ref, s1_ref, s2_ref):
    tm = rank_ref.shape[1]
    sub = lax.broadcasted_iota(jnp.int32, (8, tm), 0).astype(F32)
    start = start_ref[:, 0:1]
    rows = rank_ref[...]
    for choice, out in ((0, s1_ref), (1, s2_ref)):
        base = jnp.sum(jnp.where(sub == rows[2 + choice:3 + choice, :], start, 0.0), axis=0, keepdims=True)
        out[...] = (base + rows[choice:choice + 1, :]).astype(jnp.int32)


def _slots(ranks, starts, tm):
    t = ranks.shape[1]
    row = pl.BlockSpec((1, tm), lambda i: (0, i))
    return pl.pallas_call(
        _slot_kernel,
        out_shape=(jax.ShapeDtypeStruct((1, t), jnp.int32),) * 2,
        grid=(t // tm,),
        in_specs=[pl.BlockSpec((8, tm), lambda i: (0, i)), pl.BlockSpec((8, HEAD_DIM), lambda i: (0, 0))],
        out_specs=(row, row),
        compiler_params=_params(("parallel",)),
        name="moe_slots",
    )(ranks, starts)


def _sc_mesh():
    return plsc.VectorSubcoreMesh(core_axis_name="core", subcore_axis_name="subcore")


def _sc_scatter_rows(srcs, idxs, n_rows):
    ns, nk = len(srcs), len(idxs)
    half = srcs[0].shape[0] // SC_WINDOW // 2

    @functools.partial(pl.kernel, mesh=_sc_mesh(), scratch_types=[],
                       out_type=tuple(jax.ShapeDtypeStruct((n_rows, HEAD_DIM), srcs[0].dtype) for _ in range(ns)))
    def scatter(*refs):
        out_hbm = refs[ns + nk:]

        def body(*blocks):
            for idx in blocks[ns:]:
                for c in range(ns):
                    pltpu.sync_copy(blocks[c], out_hbm[c].at[idx.at[0]])

        pltpu.emit_pipeline(
            body, grid=(2, half),
            in_specs=[pl.BlockSpec((SC_WINDOW, HEAD_DIM), lambda i, j: (i * half + j, 0)) for _ in range(ns)]
                     + [pl.BlockSpec((1, SC_WINDOW), lambda i, j: (0, i * half + j)) for _ in range(nk)],
            out_specs=[],
            core_axis_name=("core", "subcore"),
            dimension_semantics=(pltpu.PARALLEL, pltpu.PARALLEL),
        )(*refs[:ns + nk])

    return scatter(*srcs, *idxs)


def _sc_gather_rows(tables, idx):
    nt = len(tables)
    n = idx.shape[1]
    half = n // SC_WINDOW // 2

    @functools.partial(pl.kernel, mesh=_sc_mesh(), scratch_types=[],
                       out_type=tuple(jax.ShapeDtypeStruct((n, HEAD_DIM), tables[0].dtype) for _ in range(nt)))
    def gather(*refs):
        table_hbm = refs[:nt]

        def body(idx_blk, *out_blks):
            for c in range(nt):
                pltpu.sync_copy(table_hbm[c].at[idx_blk.at[0]], out_blks[c])

        pltpu.emit_pipeline(
            body, grid=(2, half),
            in_specs=[pl.BlockSpec((1, SC_WINDOW), lambda i, j: (0, i * half + j))],
            out_specs=[pl.BlockSpec((SC_WINDOW, HEAD_DIM), lambda i, j: (i * half + j, 0)) for _ in range(nt)],
            core_axis_name=("core", "subcore"),
            dimension_semantics=(pltpu.PARALLEL, pltpu.PARALLEL),
        )(refs[nt], *refs[nt + 1:])

    return gather(*tables, idx)


def _expert_kernel(tile_expert_ref, n_used_ref, *refs):
    x_refs = refs[:PIECES]
    wg_ref, wu_ref, wd_ref = refs[PIECES:PIECES + 3]
    y_refs = refs[PIECES + 3:]

    @pl.when(pl.program_id(0) < n_used_ref[0])
    def _():
        h = _unpack_rows(x_refs).astype(BF16)
        y = _swiglu(h, wg_ref[0], wu_ref[0], wd_ref[0])
        for r, piece in zip(y_refs, _pack_rows(y)):
            r[...] = piece


def _expert_ffn(xs, tile_expert, n_used, wg, wu, wd):
    n_rows = xs[0].shape[0]
    rows = pl.BlockSpec((EXPERT_TILE, HEAD_DIM), lambda i, te, nu: (jnp.minimum(i, nu[0] - 1), 0))
    out_rows = pl.BlockSpec((EXPERT_TILE, HEAD_DIM), lambda i, te, nu: (i, 0))
    expert = lambda i, te, nu: (te[i], 0, 0)
    grid_spec = pltpu.PrefetchScalarGridSpec(
        num_scalar_prefetch=2,
        grid=(n_rows // EXPERT_TILE,),
        in_specs=[rows] * PIECES + [_resident((1,) + wg.shape[1:], expert), _resident((1,) + wu.shape[1:], expert),
                                    _resident((1,) + wd.shape[1:], expert)],
        out_specs=[out_rows] * PIECES,
    )
    return pl.pallas_call(
        _expert_kernel,
        out_shape=[jax.ShapeDtypeStruct((n_rows, HEAD_DIM), jnp.int32)] * PIECES,
        grid_spec=grid_spec,
        compiler_params=_params(("arbitrary",)),
        name="moe_expert_swiglu",
    )(tile_expert, n_used, *xs, wg, wu, wd)


def _combine_kernel(x_ref, rec_ref, gpost_ref, p_ref, ein_ref, epost_ref, egate_ref, eproj_ref, *refs):
    proj = _dot(p_ref[...].astype(BF16), eproj_ref[...])
    y1 = _unpack_rows(refs[:PIECES])
    y2 = _unpack_rows(refs[PIECES:2 * PIECES])
    o_ref = refs[2 * PIECES]
    rec = rec_ref[...]
    y = rec[:, 2:3] * y1 + rec[:, 3:4] * y2
    mixed = x_ref[...] + _rms(y, gpost_ref[...])
    o_ref[...] = _embed_tail(mixed, proj, ein_ref, epost_ref, egate_ref)


def _combine(x, rec, g_post, embed, y1, y2, tm):
    t = x.shape[0]
    piece = pl.BlockSpec((tm, HEAD_DIM), lambda i: (i, 0))
    return pl.pallas_call(
        _combine_kernel,
        out_shape=jax.ShapeDtypeStruct((t, D_MODEL), F32),
        grid=(t // tm,),
        in_specs=[pl.BlockSpec((tm, D_MODEL), lambda i: (i, 0)), piece, pl.BlockSpec((1, D_MODEL), lambda i: (0, 0))]
                 + _embed_operands(embed, tm, lambda i: (i, 0)) + [piece] * (2 * PIECES),
        out_specs=pl.BlockSpec((tm, D_MODEL), lambda i: (i, 0)),
        compiler_params=_params(("parallel",)),
        name="moe_combine",
    )(x, rec, g_post, *embed, *y1, *y2)


def _moe(x, g_pre, g_post, router, wg, wu, wd, embed, tm):
    t = x.shape[0]
    n_experts = router.shape[1]
    router = jnp.pad(router, ((0, 0), (0, HEAD_DIM - n_experts)))
    r_hi = router.astype(BF16)
    r_lo = (router - r_hi.astype(F32)).astype(BF16)
    *h_pieces, rec, ranks, counts = _route(x, g_pre, r_hi, r_lo, n_experts, tm)

    counts = counts[:n_experts, 0].astype(jnp.int32)
    tiles = (counts + EXPERT_TILE - 1) // EXPERT_TILE
    tile_end = jnp.cumsum(tiles)
    starts = ((tile_end - tiles) * EXPERT_TILE).astype(F32)
    starts = jnp.broadcast_to(jnp.pad(starts, (0, 8 - n_experts))[:, None], (8, HEAD_DIM))
    n_rows = TOP_K * t + n_experts * EXPERT_TILE
    n_tiles = n_rows // EXPERT_TILE
    n_used = tile_end[-1:]
    tile_ids = jnp.minimum(jnp.arange(n_tiles, dtype=jnp.int32), n_used[0] - 1)
    tile_expert = jnp.sum(tile_ids[:, None] >= tile_end[None, :], axis=1).astype(jnp.int32)

    slot1, slot2 = _slots(ranks, starts, min(t, 8192))
    xs = []
    for c in range(0, PIECES, 2):
        xs += _sc_scatter_rows(h_pieces[c:c + 2], [slot1, slot2], n_rows)
    ys = _expert_ffn(xs, tile_expert, n_used, wg, wu, wd)
    gathered = []
    for slot in (slot1, slot2):
        rows = []
        for c in range(0, PIECES, 2):
            rows += _sc_gather_rows(ys[c:c + 2], slot)
        gathered.append(rows)
    return _combine(x, rec, g_post, embed, gathered[0], gathered[1], tm)


def kernel(x, p, positions, w_in, conv_w, hgrn_lb_logits, hgrn_norm_g, w_branch, w_out, g_mix_pre, g_mix_post, g_ffn_pre, g_ffn_post, dense_w_gate, dense_w_up, dense_w_down, moe_router, moe_w_gate, moe_w_up, moe_w_down, ple_w_proj, ple_w_gate, ple_g_in, ple_g_post):
    batch, seq, _ = x.shape
    depth = w_in.shape[0]
    t = batch * seq
    tm = min(512, seq)
    tm_merge = min(512, seq)
    tm_ffn = min(512, seq)

    xt = x.reshape(t, D_MODEL)
    cos, sin = _rope_tables(positions.astype(F32).reshape(t // ROPE_PACK, ROPE_PACK), tm)
    row = lambda a: a.reshape(1, -1)
    in_edges = (0, A_COLS, A_COLS + 3 * BRANCH, A_COLS + B_COLS, A_COLS + B_COLS + C_COLS, A_COLS + B_COLS + C_COLS + G_COLS)
    in_bounds = tuple(zip(in_edges[:-1], in_edges[1:]))

    for i in range(depth):
        w_attn, w_qzz, w_ig, w_conv, w_gates = _split_to_bf16(w_in[i], in_bounds)
        g_pre = row(g_mix_pre[i])
        *qkvs, h = _inproj_attention(xt, g_pre, w_attn, cos, sin, batch, seq, tm)
        aos, lses, sides = zip(*[_attention_group(qkvs[g], h, w_side, g == 2, batch, seq, g)
                                 for g, w_side in enumerate((w_qzz, w_ig, w_conv))])
        qzz, ig, cb = sides
        o_f, o_b = _hgrn(qzz, ig, hgrn_lb_logits, batch, seq, i)
        xt = _merge(xt, aos, lses, o_f, o_b, ig, cb, h, w_gates, conv_w[i],
                    row(hgrn_norm_g[i]), w_branch[i].astype(BF16), w_out[i].astype(BF16), row(g_mix_post[i]),
                    seq, tm_merge)

        embed = (p[i].reshape(t, -1), row(ple_g_in[i]), row(ple_g_post[i]),
                 ple_w_gate[i].astype(BF16), ple_w_proj[i].astype(BF16))
        j = i // 2
        if i % 2 == 0:
            xt = _ffn(xt, row(g_ffn_pre[i]), row(g_ffn_post[i]), dense_w_gate[j].astype(BF16),
                      dense_w_up[j].astype(BF16), dense_w_down[j].astype(BF16), embed, tm_ffn)
        else:
            xt = _moe(xt, row(g_ffn_pre[i]), row(g_ffn_post[i]), moe_router[j], _to_bf16(moe_w_gate[j]),
                      _to_bf16(moe_w_up[j]), _to_bf16(moe_w_down[j]), embed, tm)

    return xt.reshape(batch, seq, D_MODEL)
```

```python
import functools
import math

import jax
import jax.numpy as jnp
from jax import lax
from jax.experimental import pallas as pl
from jax.experimental.pallas import tpu as pltpu
from jax.experimental.pallas import tpu_sc as plsc

D_MODEL = 1024
EPS = 1e-6
NEG_INF = -1e30

HEAD_DIM = 128
N_HEADS = 4
BRANCH = N_HEADS * HEAD_DIM
DIL_GROUPS = ((128, 1), (512, 4), (2048, 16))
N_GROUPS = len(DIL_GROUPS)
BLK = 64
ROT_DIM = HEAD_DIM // 4
ROPE_THETA = 500000.0
TOP_K = 2

A_COLS = N_GROUPS * 3 * BRANCH
B_COLS = 5 * BRANCH
C_COLS = 3 * BRANCH
G_COLS = 3 * D_MODEL

BF16 = jnp.bfloat16
F32 = jnp.float32

VMEM_LIMIT = 56 * 1024 * 1024


def _params(sem):
    return pltpu.CompilerParams(dimension_semantics=sem, vmem_limit_bytes=VMEM_LIMIT)


def _cast_kernel(w_ref, o_ref):
    o_ref[...] = w_ref[...].astype(o_ref.dtype)


def _to_bf16(w):
    cols = w.shape[-1]
    flat = w.reshape(-1, cols)
    rows = flat.shape[0]
    block = rows
    while block % 32 == 0 and block * cols * 4 > 8 * 1024 * 1024:
        block //= 2
    out = pl.pallas_call(
        _cast_kernel,
        out_shape=jax.ShapeDtypeStruct(flat.shape, BF16),
        grid=(rows // block,),
        in_specs=[pl.BlockSpec((block, cols), lambda i: (i, 0))],
        out_specs=pl.BlockSpec((block, cols), lambda i: (i, 0)),
        compiler_params=_params(("parallel",)),
        name="weights_to_bf16",
    )(flat)
    return out.reshape(w.shape)


def _split_cast_kernel(w_ref, *o_refs, bounds):
    for o_ref, (lo, hi) in zip(o_refs, bounds):
        o_ref[...] = w_ref[:, lo:hi].astype(o_ref.dtype)


def _split_to_bf16(w, first_row, n, bounds, rows=128):
    cols = w.shape[1]
    first = first_row // rows
    return pl.pallas_call(
        functools.partial(_split_cast_kernel, bounds=bounds),
        out_shape=tuple(jax.ShapeDtypeStruct((n, hi - lo), BF16) for lo, hi in bounds),
        grid=(n // rows,),
        in_specs=[pl.BlockSpec((rows, cols), lambda i: (first + i, 0))],
        out_specs=tuple(pl.BlockSpec((rows, hi - lo), lambda i: (i, 0)) for lo, hi in bounds),
        compiler_params=_params(("parallel",)),
        name="split_weights_to_bf16",
    )(w)


def _resident(shape, index_map):
    return pl.BlockSpec(shape, index_map, pipeline_mode=pl.Buffered(1))


def _rms(xf, g):
    return xf * lax.rsqrt(jnp.mean(xf * xf, axis=-1, keepdims=True) + EPS) * g


def _sigmoid(z):
    return 0.5 * jnp.tanh(0.5 * z) + 0.5


def _sigmoid_rel(z):
    return 1.0 / (1.0 + jnp.exp(-z))


def _dot(a, b):
    return jnp.dot(a, b, preferred_element_type=F32)


def _dot_nt(a, b):
    return lax.dot_general(a, b, (((1,), (1,)), ((), ())), preferred_element_type=F32)


def _dot_tn(a, b):
    return lax.dot_general(a, b, (((0,), (0,)), ((), ())), preferred_element_type=F32)


ROPE_PACK = HEAD_DIM // ROT_DIM


def _rope_table_kernel(pos_ref, freq_ref, sign_ref, cos_ref, sin_ref):
    pos = pos_ref[...]
    group = lax.broadcasted_iota(jnp.int32, cos_ref.shape, 1) // ROT_DIM
    pos_lanes = jnp.zeros(cos_ref.shape, F32)
    for k in range(ROPE_PACK):
        pos_lanes = jnp.where(group == k, pos[:, k:k + 1], pos_lanes)
    ang = pos_lanes * freq_ref[...]
    cos_ref[...] = jnp.cos(ang)
    sin_ref[...] = jnp.sin(ang) * sign_ref[...]


def _rope_tables(pos_packed, rows):
    n = pos_packed.shape[0]
    half = ROT_DIM // 2
    inv_freq = ROPE_THETA ** (-jnp.arange(0, ROT_DIM, 2, dtype=F32) / ROT_DIM)
    freq = jnp.tile(jnp.concatenate([inv_freq, inv_freq]), ROPE_PACK)[None, :]
    sign = jnp.tile(jnp.concatenate([-jnp.ones((half,), F32), jnp.ones((half,), F32)]), ROPE_PACK)[None, :]
    row = pl.BlockSpec((1, HEAD_DIM), lambda i: (0, 0))
    tab = pl.BlockSpec((rows, HEAD_DIM), lambda i: (i, 0))
    return pl.pallas_call(
        _rope_table_kernel,
        out_shape=(jax.ShapeDtypeStruct((n, HEAD_DIM), F32),) * 2,
        grid=(n // rows,),
        in_specs=[pl.BlockSpec((rows, ROPE_PACK), lambda i: (i, 0)), row, row],
        out_specs=(tab, tab),
        compiler_params=_params(("parallel",)),
        name="rope_tables",
    )(pos_packed, freq, sign)


def _unpack_rope_tables(cosp_ref, sinp_ref, tab_ref):
    rows = cosp_ref.shape[0]
    lane = lax.broadcasted_iota(jnp.int32, (rows, HEAD_DIM), 1)
    for k in range(ROPE_PACK):
        for idx, (ref, fill) in enumerate(((cosp_ref, 1.0), (sinp_ref, 0.0))):
            packed = ref[...]
            shifted = pltpu.roll(packed, HEAD_DIM - ROT_DIM * k, 1) if k else packed
            tab_ref[idx, pl.ds(k, rows, stride=ROPE_PACK), :] = jnp.where(lane < ROT_DIM, shifted, fill)


def _rope_head(th, cos, sin):
    half = ROT_DIM // 2
    lane = lax.broadcasted_iota(jnp.int32, th.shape, 1)
    swapped = jnp.where(lane < half, pltpu.roll(th, HEAD_DIM - half, 1), pltpu.roll(th, half, 1))
    return th * cos + swapped * sin


def _inproj_attn_kernel(x_ref, g_ref, w_ref, cosp_ref, sinp_ref, o0_ref, o1_ref, o2_ref, h_ref, stage_ref, tab_ref):
    _unpack_rope_tables(cosp_ref, sinp_ref, tab_ref)
    hf = _rms(x_ref[...], g_ref[...])
    h_ref[...] = hf.astype(BF16)
    n_pieces = D_MODEL // HEAD_DIM
    for pc in range(n_pieces):
        stage_ref[pc] = hf[:, pc * HEAD_DIM:(pc + 1) * HEAD_DIM]
    scale = 1.0 / math.sqrt(HEAD_DIM)
    tm = x_ref.shape[0]
    for grp, o_ref in enumerate((o0_ref, o1_ref, o2_ref)):
        dil = DIL_GROUPS[grp][1]
        sub = tm // dil
        if dil == 1:
            h, cos, sin = hf.astype(BF16), tab_ref[0], tab_ref[1]
        else:
            residue = lambda ref, *lead: jnp.concatenate(
                [ref[(*lead, pl.ds(r, sub, stride=dil), slice(None))] for r in range(dil)], axis=0)
            h = jnp.concatenate([residue(stage_ref, pc) for pc in range(n_pieces)], axis=1).astype(BF16)
            cos, sin = residue(tab_ref, 0), residue(tab_ref, 1)
        for part in range(3):
            c = grp * 3 + part
            t = _dot(h, w_ref[:, c * BRANCH:(c + 1) * BRANCH])
            if part < 2:
                heads = []
                for hd in range(N_HEADS):
                    r = _rope_head(t[:, hd * HEAD_DIM:(hd + 1) * HEAD_DIM], cos, sin)
                    heads.append(r * scale if part == 0 else r)
                t = jnp.concatenate(heads, axis=1)
            t = t.astype(BF16)
            for r in range(dil):
                o_ref[0, r, :, part * BRANCH:(part + 1) * BRANCH] = t[r * sub:(r + 1) * sub, :]


def _inproj_attention(x, g, w, cos, sin, batch, seq, tm):
    per_seq = seq // tm
    rows = lambda width: pl.BlockSpec((tm, width), lambda b, j: (b * per_seq + j, 0))
    packed = pl.BlockSpec((tm // ROPE_PACK, HEAD_DIM), lambda b, j: (b * per_seq + j, 0))
    out_shape = tuple(jax.ShapeDtypeStruct((batch, dil, seq // dil, 3 * BRANCH), BF16) for _, dil in DIL_GROUPS)
    out_shape += (jax.ShapeDtypeStruct((batch * seq, D_MODEL), BF16),)
    out_specs = tuple(pl.BlockSpec((1, dil, tm // dil, 3 * BRANCH), lambda b, j: (b, 0, j, 0))
                      for _, dil in DIL_GROUPS) + (rows(D_MODEL),)
    return pl.pallas_call(
        _inproj_attn_kernel, out_shape=out_shape, grid=(batch, per_seq),
        in_specs=[rows(D_MODEL), pl.BlockSpec((1, D_MODEL), lambda b, j: (0, 0)),
                  _resident((D_MODEL, A_COLS), lambda b, j: (0, 0)), packed, packed],
        out_specs=out_specs,
        scratch_shapes=[pltpu.VMEM((D_MODEL // HEAD_DIM, tm, HEAD_DIM), F32), pltpu.VMEM((2, tm, HEAD_DIM), F32)],
        compiler_params=_params(("parallel", "parallel")), name="inproj_attention",
    )(x, g, w, cos, sin)


def _attn_kernel(q_ref, k_ref, kp_ref, kn_ref, v_ref, vp_ref, vn_ref, h_ref, w_ref, o_ref, lse_ref, side_ref,
                 o_stage, lse_stage, *, dil, nj, n_steps, conv_side):
    n = pl.program_id(1)
    part_rows = h_ref.shape[0] // N_HEADS

    def side_projection(part):
        rs = slice(part * part_rows, (part + 1) * part_rows)
        h = h_ref[rs, :]
        if conv_side:
            u = _dot(h, w_ref[:, 0:BRANCH])
            b_gate = _dot(h, w_ref[:, BRANCH:2 * BRANCH])
            c_gate = _dot(h, w_ref[:, 2 * BRANCH:3 * BRANCH])
            side_ref[rs, 0:BRANCH] = (c_gate * u).astype(BF16)
            side_ref[rs, BRANCH:2 * BRANCH] = b_gate.astype(BF16)
        else:
            for c in range(w_ref.shape[1] // BRANCH):
                cols = slice(c * BRANCH, (c + 1) * BRANCH)
                side_ref[rs, cols] = _dot(h, w_ref[:, cols]).astype(BF16)

    qb = 2 if nj % 2 == 0 else 1
    q_rows, w_rows, n_tiles = qb * BLK, (qb + 2) * BLK, nj // qb
    col_lo = jnp.where(n > 0, 0, BLK)
    col_hi = jnp.where(n < n_steps - 1, w_rows, w_rows - BLK)
    row = lax.broadcasted_iota(jnp.int32, (q_rows, w_rows), 0)
    col = lax.broadcasted_iota(jnp.int32, (q_rows, w_rows), 1)
    band = jnp.abs(col - BLK - row) <= BLK
    masks = {}
    for j in range(n_tiles):
        m = band
        if j == 0:
            m = m & (col >= col_lo)
        if j == n_tiles - 1:
            m = m & (col < col_hi)
        masks[j] = m
    pairs = [(r, j) for r in range(dil) for j in range(n_tiles)]
    lane = lax.broadcasted_iota(jnp.int32, (len(pairs), q_rows, HEAD_DIM), 2)

    def natural_rows(r, j):
        return pl.ds(j * q_rows * dil + r, q_rows, stride=dil)

    def window(own_ref, prev_ref, next_ref, r, j, hs):
        parts = []
        for w in range(j * qb - 1, j * qb + qb + 1):
            if w < 0:
                parts.append(prev_ref[0, r, :, hs])
            elif w >= nj:
                parts.append(next_ref[0, r, :, hs])
            else:
                parts.append(own_ref[0, r, w * BLK:(w + 1) * BLK, hs])
        return jnp.concatenate(parts, axis=0)

    lse_tiles = jnp.zeros((len(pairs), q_rows, HEAD_DIM), F32)
    for hd in range(N_HEADS):
        side_projection(hd)
        hs = slice(hd * HEAD_DIM, (hd + 1) * HEAD_DIM)
        scores = [_dot_nt(q_ref[0, r, j * q_rows:(j + 1) * q_rows, hs], window(k_ref, kp_ref, kn_ref, r, j, hs))
                  for r, j in pairs]
        s = jnp.stack([jnp.where(masks[j], sc, NEG_INF) for (r, j), sc in zip(pairs, scores)])
        m = jnp.max(s, axis=2, keepdims=True)
        p = jnp.exp(s - m)
        l = jnp.sum(p, axis=2, keepdims=True)
        pb = p.astype(BF16)
        outs = jnp.stack([_dot(pb[i], window(v_ref, vp_ref, vn_ref, r, j, hs)) for i, (r, j) in enumerate(pairs)])
        outs = outs * (1.0 / l)
        lse_tiles = jnp.where(lane == hd, m + jnp.log(l), lse_tiles)
        for i, (r, j) in enumerate(pairs):
            o_stage[hd, natural_rows(r, j), :] = outs[i]
        o_ref[:, hs] = o_stage[hd].astype(o_ref.dtype)
    for i, (r, j) in enumerate(pairs):
        lse_stage[natural_rows(r, j), :] = lse_tiles[i]
    lse_ref[...] = lse_stage[...]


def _attention_group(qkv, h, w_side, conv_side, batch, seq, group):
    _, dil = DIL_GROUPS[group]
    t = batch * seq
    sub_len = seq // dil
    rows_per_step = min(seq, 16 * BLK)
    nj = rows_per_step // (dil * BLK)
    n_steps = seq // rows_per_step
    n_blk = sub_len // BLK
    sub_rows = rows_per_step // dil

    def own(part):
        return pl.BlockSpec((1, dil, sub_rows, BRANCH), lambda b, n: (b, 0, n, part))

    def prev(part):
        return pl.BlockSpec((1, dil, BLK, BRANCH), lambda b, n: (b, 0, jnp.maximum(n * nj - 1, 0), part))

    def nxt(part):
        return pl.BlockSpec((1, dil, BLK, BRANCH), lambda b, n: (b, 0, jnp.minimum((n + 1) * nj, n_blk - 1), part))

    side_cols = 2 * BRANCH if conv_side else w_side.shape[1]
    step_rows = lambda width: pl.BlockSpec((rows_per_step, width), lambda b, n: (b * n_steps + n, 0))
    return pl.pallas_call(
        functools.partial(_attn_kernel, dil=dil, nj=nj, n_steps=n_steps, conv_side=conv_side),
        out_shape=(jax.ShapeDtypeStruct((t, BRANCH), BF16), jax.ShapeDtypeStruct((t, HEAD_DIM), F32),
                   jax.ShapeDtypeStruct((t, side_cols), BF16)),
        grid=(batch, n_steps),
        in_specs=[own(0), own(1), prev(1), nxt(1), own(2), prev(2), nxt(2), step_rows(D_MODEL),
                  _resident(w_side.shape, lambda b, n: (0, 0))],
        out_specs=(step_rows(BRANCH), step_rows(HEAD_DIM), step_rows(side_cols)),
        scratch_shapes=[pltpu.VMEM((N_HEADS, rows_per_step, HEAD_DIM), F32), pltpu.VMEM((rows_per_step, HEAD_DIM), F32)],
        compiler_params=_params(("parallel", "parallel")),
        name=f"dilated_attention_g{group}",
    )(qkv, qkv, qkv, qkv, qkv, qkv, qkv, h, w_side)


def _hgrn_kernel(logit_ref, qf_ref, zf_ref, vf_ref, qb_ref, zb_ref, vb_ref, of_ref, ob_ref, state_ref,
                 *, layer, n_chunks):
    c = pl.program_id(1)

    @pl.when(c == 0)
    def _():
        state_ref[...] = jnp.zeros_like(state_ref)

    depth = logit_ref.shape[1]
    lbs = []
    for dirn in range(2):
        rows = [logit_ref[dirn, j:j + 1, :] for j in range(depth)]
        top = functools.reduce(jnp.maximum, rows)
        exps = [jnp.exp(rw - top) for rw in rows]
        lbs.append(sum(exps[1:layer + 1], jnp.zeros_like(top)) / sum(exps[1:], exps[0]))

    r = lax.broadcasted_iota(jnp.int32, (BLK, BLK), 0)
    cc = lax.broadcasted_iota(jnp.int32, (BLK, BLK), 1)

    def bcast_rows(per_chunk):
        return jnp.concatenate([jnp.broadcast_to(v, (BLK, BRANCH)) for v in per_chunk], axis=0)

    for dirn, (q_ref, z_ref, v_ref, o_ref) in enumerate(
            ((qf_ref, zf_ref, vf_ref, of_ref), (qb_ref, zb_ref, vb_ref, ob_ref))):
        tri = (cc <= r) if dirn == 0 else (cc >= r)
        tri_b = jnp.where(tri, 1.0, 0.0).astype(BF16)
        last, mid_row = (BLK - 1, BLK // 2 - 1) if dirn == 0 else (0, BLK // 2)
        lb = lbs[dirn]
        q = q_ref[...].astype(F32)
        z = z_ref[...].astype(F32)
        v = v_ref[...]
        sg = _sigmoid_rel(z)
        logf = jnp.log(lb + (1.0 - lb) * sg)
        kk = (1.0 - lb) * (1.0 - sg)
        hi = logf.astype(BF16)
        lo = (logf - hi.astype(F32)).astype(BF16)
        chunks = [slice(g * BLK, (g + 1) * BLK) for g in range(n_chunks)]
        cums = [_dot(tri_b, hi[ch]) + _dot(tri_b, lo[ch]) for ch in chunks]
        totals = [cm[last:last + 1, :] for cm in cums]
        mids = [cm[mid_row:mid_row + 1, :] for cm in cums]
        cum = jnp.concatenate(cums, axis=0)
        mid = bcast_rows(mids)
        qa = q * jnp.exp(cum - mid)
        ka = kk * jnp.exp(mid - cum)
        q_dec = (qa * bcast_rows([jnp.exp(md) for md in mids])).astype(BF16)
        k_dec = (ka * bcast_rows([jnp.exp(tt - md) for tt, md in zip(totals, mids)])).astype(BF16)
        qa = qa.astype(BF16)
        ka = ka.astype(BF16)
        tiles = [(g, hd) for g in range(n_chunks) for hd in range(N_HEADS)]
        sl = lambda g, hd: (slice(g * BLK, (g + 1) * BLK), slice(hd * HEAD_DIM, (hd + 1) * HEAD_DIM))
        atts = [jnp.where(tri, _dot_nt(qa[sl(g, hd)], ka[sl(g, hd)]), 0.0).astype(BF16) for g, hd in tiles]
        intra = {t: _dot(att, v[sl(*t)]) for t, att in zip(tiles, atts)}
        kvs = {t: _dot_tn(v[sl(*t)], k_dec[sl(*t)]) for t in tiles}
        decays = [jnp.exp(tt) for tt in totals]
        order = range(n_chunks) if dirn == 0 else range(n_chunks - 1, -1, -1)
        for hd in range(N_HEADS):
            hs = slice(hd * HEAD_DIM, (hd + 1) * HEAD_DIM)
            state_t = state_ref[dirn, hd]
            for g in order:
                o_ref[g * BLK:(g + 1) * BLK, hs] = intra[(g, hd)] + _dot_nt(q_dec[sl(g, hd)], state_t.astype(BF16))
                state_t = state_t * decays[g][:, hs] + kvs[(g, hd)]
            state_ref[dirn, hd] = state_t


def _hgrn(qzz, ig, logits, batch, seq, layer):
    t = qzz.shape[0]
    n_chunks = 4
    rows = n_chunks * BLK
    nc = seq // rows
    depth = logits.shape[1]

    def fwd(colblk):
        return pl.BlockSpec((rows, BRANCH), lambda b, c: (b * nc + c, colblk))

    def bwd(colblk):
        return pl.BlockSpec((rows, BRANCH), lambda b, c: (b * nc + nc - 1 - c, colblk))

    return pl.pallas_call(
        functools.partial(_hgrn_kernel, layer=layer, n_chunks=n_chunks),
        out_shape=(jax.ShapeDtypeStruct((t, BRANCH), F32),) * 2,
        grid=(batch, nc),
        in_specs=[pl.BlockSpec((2, depth, BRANCH), lambda b, c: (0, 0, 0)),
                  fwd(0), fwd(1), fwd(0), bwd(0), bwd(2), bwd(0)],
        out_specs=(pl.BlockSpec((rows, BRANCH), lambda b, c: (b * nc + c, 0)),
                   pl.BlockSpec((rows, BRANCH), lambda b, c: (b * nc + nc - 1 - c, 0))),
        scratch_shapes=[pltpu.VMEM((2, N_HEADS, HEAD_DIM, HEAD_DIM), F32)],
        compiler_params=_params(("parallel", "arbitrary")),
        name="hgrn2_scan",
    )(logits, qzz, qzz, ig, qzz, qzz, ig)


def _merge_kernel(x_ref, ao0_ref, ao1_ref, ao2_ref, l0_ref, l1_ref, l2_ref, of_ref, ob_ref, bg_ref,
                  cb_ref, cbp_ref, cbn_ref, h_ref, wgate_ref, convw_ref, ng_ref, wbr_ref, wout_ref, gpost_ref,
                  o_ref, *, tm, seq):
    i = pl.program_id(0)
    h = h_ref[...]
    gates = [_sigmoid(_dot(h, wgate_ref[:, n * D_MODEL:(n + 1) * D_MODEL])) for n in range(3)]

    ls = (l0_ref[...], l1_ref[...], l2_ref[...])
    aos = (ao0_ref, ao1_ref, ao2_ref)
    lmax = jnp.maximum(jnp.maximum(ls[0], ls[1]), ls[2])
    es = [jnp.exp(l - lmax) for l in ls]
    inv = 1.0 / (es[0] + es[1] + es[2])
    a_parts = []
    for hd in range(N_HEADS):
        hs = slice(hd * HEAD_DIM, (hd + 1) * HEAD_DIM)
        acc = None
        for g in range(N_GROUPS):
            w = (es[g] * inv)[:, hd:hd + 1]
            term = w * aos[g][:, hs].astype(F32)
            acc = term if acc is None else acc + term
        a_parts.append(acc)
    o_a = jnp.concatenate(a_parts, axis=1).astype(BF16)

    b_parts = []
    for hd in range(N_HEADS):
        hs = slice(hd * HEAD_DIM, (hd + 1) * HEAD_DIM)
        o = of_ref[:, hs] + ob_ref[:, hs]
        b_parts.append(_rms(o, ng_ref[:, hs]))
    gate = bg_ref[...].astype(F32)
    o_b = (jnp.concatenate(b_parts, axis=1) * (gate * _sigmoid(gate))).astype(BF16)

    cu = cb_ref[:, 0:BRANCH].astype(F32)
    keep_prev = jnp.where((i * tm) % seq == 0, 0.0, 1.0)
    keep_next = jnp.where(((i + 1) * tm) % seq == 0, 0.0, 1.0)
    halo_rows = cbp_ref.shape[0]
    prev_row = keep_prev * cbp_ref[halo_rows - 1:halo_rows, 0:BRANCH].astype(F32)
    next_row = keep_next * cbn_ref[0:1, 0:BRANCH].astype(F32)
    rid = lax.broadcasted_iota(jnp.int32, cu.shape, 0)
    before = jnp.where(rid == 0, prev_row, pltpu.roll(cu, 1, 0))
    after = jnp.where(rid == tm - 1, next_row, pltpu.roll(cu, tm - 1, 0))
    conv = convw_ref[0:1, :] * before + convw_ref[1:2, :] * cu + convw_ref[2:3, :] * after
    o_c = (cb_ref[:, BRANCH:2 * BRANCH].astype(F32) * conv).astype(BF16)

    merged = None
    for n, o_n in enumerate((o_a, o_b, o_c)):
        up = _dot(o_n, wbr_ref[n])
        term = gates[n] * up
        merged = term if merged is None else merged + term
    y = _dot(merged.astype(BF16), wout_ref[...])
    o_ref[...] = x_ref[...] + _rms(y, gpost_ref[...])


def _merge(x, aos, lses, o_f, o_b, ig, cb, h, w_gate, conv_w, norm_g, w_branch, w_out, g_post, seq, tm):
    t = x.shape[0]
    halo = 16
    n_halo = t // halo
    per = tm // halo

    def rows(width, colblk=0):
        return pl.BlockSpec((tm, width), lambda i: (i, colblk))

    def full(shape):
        return _resident(shape, lambda i: (0,) * len(shape))

    in_specs = ([rows(D_MODEL)] + [rows(BRANCH)] * 3 + [rows(HEAD_DIM)] * 3 + [rows(BRANCH)] * 2
                + [rows(BRANCH, 1), rows(2 * BRANCH),
                   pl.BlockSpec((halo, 2 * BRANCH), lambda i: (jnp.maximum(i * per - 1, 0), 0)),
                   pl.BlockSpec((halo, 2 * BRANCH), lambda i: (jnp.minimum((i + 1) * per, n_halo - 1), 0)),
                   rows(D_MODEL), full((D_MODEL, G_COLS)), full((3, BRANCH)), full((1, BRANCH)),
                   full((3, BRANCH, D_MODEL)), full((D_MODEL, D_MODEL)), full((1, D_MODEL))])
    return pl.pallas_call(
        functools.partial(_merge_kernel, tm=tm, seq=seq),
        out_shape=jax.ShapeDtypeStruct((t, D_MODEL), F32),
        grid=(t // tm,),
        in_specs=in_specs,
        out_specs=rows(D_MODEL),
        compiler_params=_params(("parallel",)),
        name="merge_branches",
    )(x, *aos, *lses, o_f, o_b, ig, cb, cb, cb, h, w_gate, conv_w, norm_g, w_branch, w_out, g_post)


def _swiglu(h, wg, wu, wd):
    g = _dot(h, wg)
    u = _dot(h, wu)
    return _dot((g * _sigmoid(g) * u).astype(BF16), wd)


def _embed_tail(x, proj, gin_ref, gpost_ref, wgate_ref):
    gate = _sigmoid(_dot(_rms(x, gin_ref[...]).astype(BF16), wgate_ref[...]))
    return x + _rms(proj * gate, gpost_ref[...])


def _embed_operands(embed, p_row0, tm, index):
    pd = embed[0].shape[1]
    const = lambda *_: (0, 0)
    p_index = lambda *a: (index(*a)[0] + p_row0 // tm, 0)
    return [pl.BlockSpec((tm, pd), p_index), pl.BlockSpec((1, D_MODEL), const), pl.BlockSpec((1, D_MODEL), const),
            _resident((D_MODEL, D_MODEL), const), _resident((pd, D_MODEL), const)]


def _ffn_kernel(x_ref, gpre_ref, gpost_ref, wg_ref, wu_ref, wd_ref, p_ref, ein_ref, epost_ref, egate_ref, eproj_ref,
                o_ref):
    proj = _dot(p_ref[...].astype(BF16), eproj_ref[...])
    x = x_ref[...]
    h = _rms(x, gpre_ref[...]).astype(BF16)
    mixed = x + _rms(_swiglu(h, wg_ref[...], wu_ref[...], wd_ref[...]), gpost_ref[...])
    o_ref[...] = _embed_tail(mixed, proj, ein_ref, epost_ref, egate_ref)


def _ffn(x, g_pre, g_post, wg, wu, wd, embed, p_row0, tm):
    t = x.shape[0]
    const = lambda i: (0, 0)
    return pl.pallas_call(
        _ffn_kernel,
        out_shape=jax.ShapeDtypeStruct((t, D_MODEL), F32),
        grid=(t // tm,),
        in_specs=[pl.BlockSpec((tm, D_MODEL), lambda i: (i, 0)),
                  pl.BlockSpec((1, D_MODEL), const), pl.BlockSpec((1, D_MODEL), const),
                  _resident(wg.shape, const), _resident(wu.shape, const), _resident(wd.shape, const)]
                 + _embed_operands(embed, p_row0, tm, lambda i: (i, 0)),
        out_specs=pl.BlockSpec((tm, D_MODEL), lambda i: (i, 0)),
        compiler_params=_params(("parallel",)),
        name="dense_swiglu",
    )(x, g_pre, g_post, wg, wu, wd, *embed)


PIECES = D_MODEL // 2 // HEAD_DIM
SC_WINDOW = 128
EXPERT_TILE = 512


def _pack_rows(vals):
    bits = lax.bitcast_convert_type(vals.astype(BF16).astype(F32), jnp.uint32)
    half = D_MODEL // 2
    word = bits[:, :half] | (bits[:, half:] >> 16)
    return [lax.bitcast_convert_type(word[:, c * HEAD_DIM:(c + 1) * HEAD_DIM], jnp.int32) for c in range(PIECES)]


def _unpack_rows(piece_refs):
    words = [lax.bitcast_convert_type(r[...], jnp.uint32) for r in piece_refs]
    hi = [lax.bitcast_convert_type(w & jnp.uint32(0xFFFF0000), F32) for w in words]
    lo = [lax.bitcast_convert_type(w << 16, F32) for w in words]
    return jnp.concatenate(hi + lo, axis=1)


def _route_kernel(x_ref, gpre_ref, rhi_ref, rlo_ref, *refs, n_experts):
    piece_refs = refs[:PIECES]
    rec_ref, rank_ref, count_ref, carry_ref = refs[PIECES:]
    i = pl.program_id(0)
    tm = x_ref.shape[0]

    @pl.when(i == 0)
    def _():
        carry_ref[...] = jnp.zeros_like(carry_ref)

    hf = _rms(x_ref[...], gpre_ref[...])
    for r, piece in zip(piece_refs, _pack_rows(hf)):
        r[...] = piece
    h_hi = hf.astype(BF16)
    h_lo = (hf - h_hi.astype(F32)).astype(BF16)
    logits = _dot(h_hi, rhi_ref[...]) + _dot(h_hi, rlo_ref[...]) + _dot(h_lo, rhi_ref[...])
    lane = lax.broadcasted_iota(jnp.int32, logits.shape, 1).astype(F32)
    logits = jnp.where(lane < n_experts, logits, NEG_INF)
    m1 = jnp.max(logits, axis=1, keepdims=True)
    i1 = jnp.min(jnp.where(logits == m1, lane, 1e9), axis=1, keepdims=True)
    rest = jnp.where(lane == i1, NEG_INF, logits)
    m2 = jnp.max(rest, axis=1, keepdims=True)
    i2 = jnp.min(jnp.where(rest == m2, lane, 1e9), axis=1, keepdims=True)
    ex = jnp.exp(m2 - m1)
    w1 = 1.0 / (1.0 + ex)
    rec = jnp.where(lane == 0, i1, jnp.where(lane == 1, i2, jnp.where(lane == 2, w1, jnp.where(lane == 3, ex * w1, 0.0))))
    rec_ref[...] = rec
    rec_t = rec.T
    e1 = rec_t[0:1, :]
    e2 = rec_t[1:2, :]
    sub = lax.broadcasted_iota(jnp.int32, (8, tm), 0).astype(F32)
    oh1 = jnp.where(sub == e1, 1.0, 0.0)
    oh2 = jnp.where(sub == e2, 1.0, 0.0)
    chosen = oh1 + oh2
    src = lax.broadcasted_iota(jnp.int32, (tm, tm), 0)
    dst = lax.broadcasted_iota(jnp.int32, (tm, tm), 1)
    before = jnp.where(src < dst, 1.0, 0.0).astype(BF16)
    rank = _dot(chosen.astype(BF16), before) + carry_ref[:, 0:1]
    rank1 = jnp.sum(oh1 * rank, axis=0, keepdims=True)
    rank2 = jnp.sum(oh2 * rank, axis=0, keepdims=True)
    rank_ref[...] = jnp.where(sub == 0, rank1, jnp.where(sub == 1, rank2, jnp.where(sub == 2, e1, jnp.where(sub == 3, e2, 0.0))))
    carry_ref[...] = carry_ref[...] + jnp.sum(chosen, axis=1, keepdims=True)
    count_ref[...] = carry_ref[...]


def _route(x, g_pre, r_hi, r_lo, n_experts, tm):
    t = x.shape[0]
    assert n_experts <= 8
    piece = pl.BlockSpec((tm, HEAD_DIM), lambda i: (i, 0))
    return pl.pallas_call(
        functools.partial(_route_kernel, n_experts=n_experts),
        out_shape=(*[jax.ShapeDtypeStruct((t, HEAD_DIM), jnp.int32)] * PIECES,
                   jax.ShapeDtypeStruct((t, HEAD_DIM), F32), jax.ShapeDtypeStruct((8, t), F32),
                   jax.ShapeDtypeStruct((8, HEAD_DIM), F32)),
        grid=(t // tm,),
        in_specs=[pl.BlockSpec((tm, D_MODEL), lambda i: (i, 0)), pl.BlockSpec((1, D_MODEL), lambda i: (0, 0)),
                  pl.BlockSpec((D_MODEL, HEAD_DIM), lambda i: (0, 0)), pl.BlockSpec((D_MODEL, HEAD_DIM), lambda i: (0, 0))],
        out_specs=(*[piece] * PIECES, piece, pl.BlockSpec((8, tm), lambda i: (0, i)),
                   pl.BlockSpec((8, HEAD_DIM), lambda i: (0, 0))),
        scratch_shapes=[pltpu.VMEM((8, HEAD_DIM), F32)],
        compiler_params=_params(("arbitrary",)),
        name="moe_route",
    )(x, g_pre, r_hi, r_lo)


def _slot_kernel(rank_ref, start_ref, s1_ref, s2_ref):
    tm = rank_ref.shape[1]
    sub = lax.broadcasted_iota(jnp.int32, (8, tm), 0).astype(F32)
    start = start_ref[:, 0:1]
    rows = rank_ref[...]
    for choice, out in ((0, s1_ref), (1, s2_ref)):
        base = jnp.sum(jnp.where(sub == rows[2 + choice:3 + choice, :], start, 0.0), axis=0, keepdims=True)
        out[...] = (base + rows[choice:choice + 1, :]).astype(jnp.int32)


def _slots(ranks, starts, tm):
    t = ranks.shape[1]
    row = pl.BlockSpec((1, tm), lambda i: (0, i))
    return pl.pallas_call(
        _slot_kernel,
        out_shape=(jax.ShapeDtypeStruct((1, t), jnp.int32),) * 2,
        grid=(t // tm,),
        in_specs=[pl.BlockSpec((8, tm), lambda i: (0, i)), pl.BlockSpec((8, HEAD_DIM), lambda i: (0, 0))],
        out_specs=(row, row),
        compiler_params=_params(("parallel",)),
        name="moe_slots",
    )(ranks, starts)


def _sc_mesh():
    return plsc.VectorSubcoreMesh(core_axis_name="core", subcore_axis_name="subcore")


def _sc_scatter_rows(srcs, idxs, n_rows):
    ns, nk = len(srcs), len(idxs)
    half = srcs[0].shape[0] // SC_WINDOW // 2

    @functools.partial(pl.kernel, mesh=_sc_mesh(), scratch_types=[],
                       out_type=tuple(jax.ShapeDtypeStruct((n_rows, HEAD_DIM), srcs[0].dtype) for _ in range(ns)))
    def scatter(*refs):
        out_hbm = refs[ns + nk:]

        def body(*blocks):
            for idx in blocks[ns:]:
                for c in range(ns):
                    pltpu.sync_copy(blocks[c], out_hbm[c].at[idx.at[0]])

        pltpu.emit_pipeline(
            body, grid=(2, half),
            in_specs=[pl.BlockSpec((SC_WINDOW, HEAD_DIM), lambda i, j: (i * half + j, 0)) for _ in range(ns)]
                     + [pl.BlockSpec((1, SC_WINDOW), lambda i, j: (0, i * half + j)) for _ in range(nk)],
            out_specs=[],
            core_axis_name=("core", "subcore"),
            dimension_semantics=(pltpu.PARALLEL, pltpu.PARALLEL),
        )(*refs[:ns + nk])

    return scatter(*srcs, *idxs)


def _sc_gather_rows(tables, idx):
    nt = len(tables)
    n = idx.shape[1]
    half = n // SC_WINDOW // 2

    @functools.partial(pl.kernel, mesh=_sc_mesh(), scratch_types=[],
                       out_type=tuple(jax.ShapeDtypeStruct((n, HEAD_DIM), tables[0].dtype) for _ in range(nt)))
    def gather(*refs):
        table_hbm = refs[:nt]

        def body(idx_blk, *out_blks):
            for c in range(nt):
                pltpu.sync_copy(table_hbm[c].at[idx_blk.at[0]], out_blks[c])

        pltpu.emit_pipeline(
            body, grid=(2, half),
            in_specs=[pl.BlockSpec((1, SC_WINDOW), lambda i, j: (0, i * half + j))],
            out_specs=[pl.BlockSpec((SC_WINDOW, HEAD_DIM), lambda i, j: (i * half + j, 0)) for _ in range(nt)],
            core_axis_name=("core", "subcore"),
            dimension_semantics=(pltpu.PARALLEL, pltpu.PARALLEL),
        )(refs[nt], *refs[nt + 1:])

    return gather(*tables, idx)


def _expert_kernel(tile_expert_ref, n_used_ref, *refs):
    x_refs = refs[:PIECES]
    wg_ref, wu_ref, wd_ref = refs[PIECES:PIECES + 3]
    y_refs = refs[PIECES + 3:]

    @pl.when(pl.program_id(0) < n_used_ref[0])
    def _():
        h = _unpack_rows(x_refs).astype(BF16)
        y = _swiglu(h, wg_ref[0], wu_ref[0], wd_ref[0])
        for r, piece in zip(y_refs, _pack_rows(y)):
            r[...] = piece


def _expert_ffn(xs, tile_expert, n_used, wg, wu, wd):
    n_rows = xs[0].shape[0]
    rows = pl.BlockSpec((EXPERT_TILE, HEAD_DIM), lambda i, te, nu: (jnp.minimum(i, nu[0] - 1), 0))
    out_rows = pl.BlockSpec((EXPERT_TILE, HEAD_DIM), lambda i, te, nu: (i, 0))
    expert = lambda i, te, nu: (te[i], 0, 0)
    grid_spec = pltpu.PrefetchScalarGridSpec(
        num_scalar_prefetch=2,
        grid=(n_rows // EXPERT_TILE,),
        in_specs=[rows] * PIECES + [_resident((1,) + wg.shape[1:], expert), _resident((1,) + wu.shape[1:], expert),
                                    _resident((1,) + wd.shape[1:], expert)],
        out_specs=[out_rows] * PIECES,
    )
    return pl.pallas_call(
        _expert_kernel,
        out_shape=[jax.ShapeDtypeStruct((n_rows, HEAD_DIM), jnp.int32)] * PIECES,
        grid_spec=grid_spec,
        compiler_params=_params(("arbitrary",)),
        name="moe_expert_swiglu",
    )(tile_expert, n_used, *xs, wg, wu, wd)


def _combine_kernel(x_ref, rec_ref, gpost_ref, p_ref, ein_ref, epost_ref, egate_ref, eproj_ref, *refs):
    proj = _dot(p_ref[...].astype(BF16), eproj_ref[...])
    y1 = _unpack_rows(refs[:PIECES])
    y2 = _unpack_rows(refs[PIECES:2 * PIECES])
    o_ref = refs[2 * PIECES]
    rec = rec_ref[...]
    y = rec[:, 2:3] * y1 + rec[:, 3:4] * y2
    mixed = x_ref[...] + _rms(y, gpost_ref[...])
    o_ref[...] = _embed_tail(mixed, proj, ein_ref, epost_ref, egate_ref)


def _combine(x, rec, g_post, embed, p_row0, y1, y2, tm):
    t = x.shape[0]
    piece = pl.BlockSpec((tm, HEAD_DIM), lambda i: (i, 0))
    return pl.pallas_call(
        _combine_kernel,
        out_shape=jax.ShapeDtypeStruct((t, D_MODEL), F32),
        grid=(t // tm,),
        in_specs=[pl.BlockSpec((tm, D_MODEL), lambda i: (i, 0)), piece, pl.BlockSpec((1, D_MODEL), lambda i: (0, 0))]
                 + _embed_operands(embed, p_row0, tm, lambda i: (i, 0)) + [piece] * (2 * PIECES),
        out_specs=pl.BlockSpec((tm, D_MODEL), lambda i: (i, 0)),
        compiler_params=_params(("parallel",)),
        name="moe_combine",
    )(x, rec, g_post, *embed, *y1, *y2)


def _moe(x, g_pre, g_post, router, wg, wu, wd, embed, p_row0, tm):
    t = x.shape[0]
    n_experts = router.shape[1]
    router = jnp.pad(router, ((0, 0), (0, HEAD_DIM - n_experts)))
    r_hi = router.astype(BF16)
    r_lo = (router - r_hi.astype(F32)).astype(BF16)
    *h_pieces, rec, ranks, counts = _route(x, g_pre, r_hi, r_lo, n_experts, tm)

    counts = counts[:n_experts, 0].astype(jnp.int32)
    tiles = (counts + EXPERT_TILE - 1) // EXPERT_TILE
    tile_end = jnp.cumsum(tiles)
    starts = ((tile_end - tiles) * EXPERT_TILE).astype(F32)
    starts = jnp.broadcast_to(jnp.pad(starts, (0, 8 - n_experts))[:, None], (8, HEAD_DIM))
    n_rows = TOP_K * t + n_experts * EXPERT_TILE
    n_tiles = n_rows // EXPERT_TILE
    n_used = tile_end[-1:]
    tile_ids = jnp.minimum(jnp.arange(n_tiles, dtype=jnp.int32), n_used[0] - 1)
    tile_expert = jnp.sum(tile_ids[:, None] >= tile_end[None, :], axis=1).astype(jnp.int32)

    slot1, slot2 = _slots(ranks, starts, min(t, 8192))
    xs = []
    for c in range(0, PIECES, 2):
        xs += _sc_scatter_rows(h_pieces[c:c + 2], [slot1, slot2], n_rows)
    ys = _expert_ffn(xs, tile_expert, n_used, wg, wu, wd)
    gathered = []
    for slot in (slot1, slot2):
        rows = []
        for c in range(0, PIECES, 2):
            rows += _sc_gather_rows(ys[c:c + 2], slot)
        gathered.append(rows)
    return _combine(x, rec, g_post, embed, p_row0, gathered[0], gathered[1], tm)


def kernel(x, p, positions, w_in, conv_w, hgrn_lb_logits, hgrn_norm_g, w_branch, w_out, g_mix_pre, g_mix_post, g_ffn_pre, g_ffn_post, dense_w_gate, dense_w_up, dense_w_down, moe_router, moe_w_gate, moe_w_up, moe_w_down, ple_w_proj, ple_w_gate, ple_g_in, ple_g_post):
    batch, seq, _ = x.shape
    depth = w_in.shape[0]
    t = batch * seq
    tm = min(512, seq)
    tm_merge = min(512, seq)
    tm_ffn = min(512, seq)

    xt = x.reshape(t, D_MODEL)
    cos, sin = _rope_tables(positions.astype(F32).reshape(t // ROPE_PACK, ROPE_PACK), tm)
    row = lambda a: a.reshape(1, -1)
    in_edges = (0, A_COLS, A_COLS + 3 * BRANCH, A_COLS + B_COLS, A_COLS + B_COLS + C_COLS, A_COLS + B_COLS + C_COLS + G_COLS)
    in_bounds = tuple(zip(in_edges[:-1], in_edges[1:]))
    w_in_rows = w_in.reshape(depth * D_MODEL, -1)
    p_rows = p.reshape(depth * t, -1)

    for i in range(depth):
        w_attn, w_qzz, w_ig, w_conv, w_gates = _split_to_bf16(w_in_rows, i * D_MODEL, D_MODEL, in_bounds)
        g_pre = row(g_mix_pre[i])
        *qkvs, h = _inproj_attention(xt, g_pre, w_attn, cos, sin, batch, seq, tm)
        aos, lses, sides = zip(*[_attention_group(qkvs[g], h, w_side, g == 2, batch, seq, g)
                                 for g, w_side in enumerate((w_qzz, w_ig, w_conv))])
        qzz, ig, cb = sides
        o_f, o_b = _hgrn(qzz, ig, hgrn_lb_logits, batch, seq, i)
        xt = _merge(xt, aos, lses, o_f, o_b, ig, cb, h, w_gates, conv_w[i],
                    row(hgrn_norm_g[i]), w_branch[i].astype(BF16), w_out[i].astype(BF16), row(g_mix_post[i]),
                    seq, tm_merge)

        embed = (p_rows, row(ple_g_in[i]), row(ple_g_post[i]),
                 ple_w_gate[i].astype(BF16), ple_w_proj[i].astype(BF16))
        j = i // 2
        if i % 2 == 0:
            xt = _ffn(xt, row(g_ffn_pre[i]), row(g_ffn_post[i]), dense_w_gate[j].astype(BF16),
                      dense_w_up[j].astype(BF16), dense_w_down[j].astype(BF16), embed, i * t, tm_ffn)
        else:
            xt = _moe(xt, row(g_ffn_pre[i]), row(g_ffn_post[i]), moe_router[j], _to_bf16(moe_w_gate[j]),
                      _to_bf16(moe_w_up[j]), _to_bf16(moe_w_down[j]), embed, i * t, tm)

    return xt.reshape(batch, seq, D_MODEL)
```

```python
import functools
import math

import jax
import jax.numpy as jnp
from jax import lax
from jax.experimental import pallas as pl
from jax.experimental.pallas import tpu as pltpu
from jax.experimental.pallas import tpu_sc as plsc

D_MODEL = 1024
EPS = 1e-6
NEG_INF = -1e30

HEAD_DIM = 128
N_HEADS = 4
BRANCH = N_HEADS * HEAD_DIM
DIL_GROUPS = ((128, 1), (512, 4), (2048, 16))
N_GROUPS = len(DIL_GROUPS)
BLK = 64
ROT_DIM = HEAD_DIM // 4
ROPE_THETA = 500000.0
TOP_K = 2

A_COLS = N_GROUPS * 3 * BRANCH
B_COLS = 5 * BRANCH
C_COLS = 3 * BRANCH
G_COLS = 3 * D_MODEL

BF16 = jnp.bfloat16
F32 = jnp.float32

VMEM_LIMIT = 56 * 1024 * 1024
ATTENTION_VMEM_LIMIT = 62 * 1024 * 1024


def _params(sem, vmem_limit=VMEM_LIMIT):
    return pltpu.CompilerParams(dimension_semantics=sem, vmem_limit_bytes=vmem_limit)


def _cast_kernel(w_ref, o_ref):
    o_ref[...] = w_ref[...].astype(o_ref.dtype)


def _to_bf16(w):
    cols = w.shape[-1]
    flat = w.reshape(-1, cols)
    rows = flat.shape[0]
    block = rows
    while block % 32 == 0 and block * cols * 4 > 4 * 1024 * 1024:
        block //= 2
    out = pl.pallas_call(
        _cast_kernel,
        out_shape=jax.ShapeDtypeStruct(flat.shape, BF16),
        grid=(rows // block,),
        in_specs=[pl.BlockSpec((block, cols), lambda i: (i, 0))],
        out_specs=pl.BlockSpec((block, cols), lambda i: (i, 0)),
        compiler_params=_params(("parallel",)),
        name="weights_to_bf16",
    )(flat)
    return out.reshape(w.shape)


def _split_cast_kernel(w_ref, *o_refs, bounds):
    for o_ref, (lo, hi) in zip(o_refs, bounds):
        o_ref[...] = w_ref[:, lo:hi].astype(o_ref.dtype)


def _split_to_bf16(w, first_row, n, bounds, rows=128):
    cols = w.shape[1]
    first = first_row // rows
    return pl.pallas_call(
        functools.partial(_split_cast_kernel, bounds=bounds),
        out_shape=tuple(jax.ShapeDtypeStruct((n, hi - lo), BF16) for lo, hi in bounds),
        grid=(n // rows,),
        in_specs=[pl.BlockSpec((rows, cols), lambda i: (first + i, 0))],
        out_specs=tuple(pl.BlockSpec((rows, hi - lo), lambda i: (i, 0)) for lo, hi in bounds),
        compiler_params=_params(("parallel",)),
        name="split_weights_to_bf16",
    )(w)


def _resident(shape, index_map):
    return pl.BlockSpec(shape, index_map, pipeline_mode=pl.Buffered(1))


def _rms(xf, g):
    return xf * lax.rsqrt(jnp.mean(xf * xf, axis=-1, keepdims=True) + EPS) * g


def _sigmoid(z):
    return 0.5 * jnp.tanh(0.5 * z) + 0.5


def _sigmoid_rel(z):
    return 1.0 / (1.0 + jnp.exp(-z))


def _dot(a, b):
    return jnp.dot(a, b, preferred_element_type=F32)


def _dot_nt(a, b):
    return lax.dot_general(a, b, (((1,), (1,)), ((), ())), preferred_element_type=F32)


def _dot_tn(a, b):
    return lax.dot_general(a, b, (((0,), (0,)), ((), ())), preferred_element_type=F32)


ROPE_PACK = HEAD_DIM // ROT_DIM


def _rope_table_kernel(pos_ref, freq_ref, sign_ref, cos_ref, sin_ref):
    pos = pos_ref[...]
    group = lax.broadcasted_iota(jnp.int32, cos_ref.shape, 1) // ROT_DIM
    pos_lanes = jnp.zeros(cos_ref.shape, F32)
    for k in range(ROPE_PACK):
        pos_lanes = jnp.where(group == k, pos[:, k:k + 1], pos_lanes)
    ang = pos_lanes * freq_ref[...]
    cos_ref[...] = jnp.cos(ang)
    sin_ref[...] = jnp.sin(ang) * sign_ref[...]


def _rope_tables(pos_packed, rows):
    n = pos_packed.shape[0]
    half = ROT_DIM // 2
    inv_freq = ROPE_THETA ** (-jnp.arange(0, ROT_DIM, 2, dtype=F32) / ROT_DIM)
    freq = jnp.tile(jnp.concatenate([inv_freq, inv_freq]), ROPE_PACK)[None, :]
    sign = jnp.tile(jnp.concatenate([-jnp.ones((half,), F32), jnp.ones((half,), F32)]), ROPE_PACK)[None, :]
    row = pl.BlockSpec((1, HEAD_DIM), lambda i: (0, 0))
    tab = pl.BlockSpec((rows, HEAD_DIM), lambda i: (i, 0))
    return pl.pallas_call(
        _rope_table_kernel,
        out_shape=(jax.ShapeDtypeStruct((n, HEAD_DIM), F32),) * 2,
        grid=(n // rows,),
        in_specs=[pl.BlockSpec((rows, ROPE_PACK), lambda i: (i, 0)), row, row],
        out_specs=(tab, tab),
        compiler_params=_params(("parallel",)),
        name="rope_tables",
    )(pos_packed, freq, sign)


def _unpack_rope_tables(cosp_ref, sinp_ref, tab_ref):
    rows = cosp_ref.shape[0]
    lane = lax.broadcasted_iota(jnp.int32, (rows, HEAD_DIM), 1)
    for k in range(ROPE_PACK):
        for idx, (ref, fill) in enumerate(((cosp_ref, 1.0), (sinp_ref, 0.0))):
            packed = ref[...]
            shifted = pltpu.roll(packed, HEAD_DIM - ROT_DIM * k, 1) if k else packed
            tab_ref[idx, pl.ds(k, rows, stride=ROPE_PACK), :] = jnp.where(lane < ROT_DIM, shifted, fill)


def _rope_head(th, cos, sin):
    half = ROT_DIM // 2
    lane = lax.broadcasted_iota(jnp.int32, th.shape, 1)
    swapped = jnp.where(lane < half, pltpu.roll(th, HEAD_DIM - half, 1), pltpu.roll(th, half, 1))
    return th * cos + swapped * sin


def _inproj_attn_kernel(x_ref, g_ref, w_ref, cosp_ref, sinp_ref, o0_ref, o1_ref, o2_ref, h_ref, stage_ref, tab_ref):
    _unpack_rope_tables(cosp_ref, sinp_ref, tab_ref)
    hf = _rms(x_ref[...], g_ref[...])
    h_ref[...] = hf.astype(BF16)
    n_pieces = D_MODEL // HEAD_DIM
    for pc in range(n_pieces):
        stage_ref[pc] = hf[:, pc * HEAD_DIM:(pc + 1) * HEAD_DIM]
    scale = 1.0 / math.sqrt(HEAD_DIM)
    tm = x_ref.shape[0]
    for grp, o_ref in enumerate((o0_ref, o1_ref, o2_ref)):
        dil = DIL_GROUPS[grp][1]
        sub = tm // dil
        if dil == 1:
            h, cos, sin = hf.astype(BF16), tab_ref[0], tab_ref[1]
        else:
            residue = lambda ref, *lead: jnp.concatenate(
                [ref[(*lead, pl.ds(r, sub, stride=dil), slice(None))] for r in range(dil)], axis=0)
            h = jnp.concatenate([residue(stage_ref, pc) for pc in range(n_pieces)], axis=1).astype(BF16)
            cos, sin = residue(tab_ref, 0), residue(tab_ref, 1)
        for part in range(3):
            c = grp * 3 + part
            t = _dot(h, w_ref[:, c * BRANCH:(c + 1) * BRANCH])
            if part < 2:
                heads = []
                for hd in range(N_HEADS):
                    r = _rope_head(t[:, hd * HEAD_DIM:(hd + 1) * HEAD_DIM], cos, sin)
                    heads.append(r * scale if part == 0 else r)
                t = jnp.concatenate(heads, axis=1)
            t = t.astype(BF16)
            for r in range(dil):
                o_ref[0, r, :, part * BRANCH:(part + 1) * BRANCH] = t[r * sub:(r + 1) * sub, :]


def _inproj_attention(x, g, w, cos, sin, batch, seq, tm):
    per_seq = seq // tm
    rows = lambda width: pl.BlockSpec((tm, width), lambda b, j: (b * per_seq + j, 0))
    packed = pl.BlockSpec((tm // ROPE_PACK, HEAD_DIM), lambda b, j: (b * per_seq + j, 0))
    out_shape = tuple(jax.ShapeDtypeStruct((batch, dil, seq // dil, 3 * BRANCH), BF16) for _, dil in DIL_GROUPS)
    out_shape += (jax.ShapeDtypeStruct((batch * seq, D_MODEL), BF16),)
    out_specs = tuple(pl.BlockSpec((1, dil, tm // dil, 3 * BRANCH), lambda b, j: (b, 0, j, 0))
                      for _, dil in DIL_GROUPS) + (rows(D_MODEL),)
    return pl.pallas_call(
        _inproj_attn_kernel, out_shape=out_shape, grid=(batch, per_seq),
        in_specs=[rows(D_MODEL), pl.BlockSpec((1, D_MODEL), lambda b, j: (0, 0)),
                  _resident((D_MODEL, A_COLS), lambda b, j: (0, 0)), packed, packed],
        out_specs=out_specs,
        scratch_shapes=[pltpu.VMEM((D_MODEL // HEAD_DIM, tm, HEAD_DIM), F32), pltpu.VMEM((2, tm, HEAD_DIM), F32)],
        compiler_params=_params(("parallel", "parallel")), name="inproj_attention",
    )(x, g, w, cos, sin)


def _attn_kernel(q_ref, k_ref, kp_ref, kn_ref, v_ref, vp_ref, vn_ref, h_ref, w_ref, o_ref, lse_ref, side_ref,
                 o_stage, lse_stage, *, dil, nj, n_steps, conv_side):
    n = pl.program_id(1)
    part_rows = h_ref.shape[0] // N_HEADS

    def side_projection(part):
        rs = slice(part * part_rows, (part + 1) * part_rows)
        h = h_ref[rs, :]
        if conv_side:
            u = _dot(h, w_ref[:, 0:BRANCH])
            b_gate = _dot(h, w_ref[:, BRANCH:2 * BRANCH])
            c_gate = _dot(h, w_ref[:, 2 * BRANCH:3 * BRANCH])
            side_ref[rs, 0:BRANCH] = (c_gate * u).astype(BF16)
            side_ref[rs, BRANCH:2 * BRANCH] = b_gate.astype(BF16)
        else:
            for c in range(w_ref.shape[1] // BRANCH):
                cols = slice(c * BRANCH, (c + 1) * BRANCH)
                side_ref[rs, cols] = _dot(h, w_ref[:, cols]).astype(BF16)

    qb = 2 if nj % 2 == 0 else 1
    q_rows, w_rows, n_tiles = qb * BLK, (qb + 2) * BLK, nj // qb
    col_lo = jnp.where(n > 0, 0, BLK)
    col_hi = jnp.where(n < n_steps - 1, w_rows, w_rows - BLK)
    row = lax.broadcasted_iota(jnp.int32, (q_rows, w_rows), 0)
    col = lax.broadcasted_iota(jnp.int32, (q_rows, w_rows), 1)
    band = jnp.abs(col - BLK - row) <= BLK
    masks = {}
    for j in range(n_tiles):
        m = band
        if j == 0:
            m = m & (col >= col_lo)
        if j == n_tiles - 1:
            m = m & (col < col_hi)
        masks[j] = m
    pairs = [(r, j) for r in range(dil) for j in range(n_tiles)]
    lane = lax.broadcasted_iota(jnp.int32, (len(pairs), q_rows, HEAD_DIM), 2)

    def natural_rows(r, j):
        return pl.ds(j * q_rows * dil + r, q_rows, stride=dil)

    def window(own_ref, prev_ref, next_ref, r, j, hs):
        parts = []
        for w in range(j * qb - 1, j * qb + qb + 1):
            if w < 0:
                parts.append(prev_ref[0, r, :, hs])
            elif w >= nj:
                parts.append(next_ref[0, r, :, hs])
            else:
                parts.append(own_ref[0, r, w * BLK:(w + 1) * BLK, hs])
        return jnp.concatenate(parts, axis=0)

    lse_tiles = jnp.zeros((len(pairs), q_rows, HEAD_DIM), F32)
    for hd in range(N_HEADS):
        side_projection(hd)
        hs = slice(hd * HEAD_DIM, (hd + 1) * HEAD_DIM)
        scores = [_dot_nt(q_ref[0, r, j * q_rows:(j + 1) * q_rows, hs], window(k_ref, kp_ref, kn_ref, r, j, hs))
                  for r, j in pairs]
        s = jnp.stack([jnp.where(masks[j], sc, NEG_INF) for (r, j), sc in zip(pairs, scores)])
        m = jnp.max(s, axis=2, keepdims=True)
        p = jnp.exp(s - m)
        l = jnp.sum(p, axis=2, keepdims=True)
        pb = p.astype(BF16)
        outs = jnp.stack([_dot(pb[i], window(v_ref, vp_ref, vn_ref, r, j, hs)) for i, (r, j) in enumerate(pairs)])
        outs = outs * (1.0 / l)
        lse_tiles = jnp.where(lane == hd, m + jnp.log(l), lse_tiles)
        for i, (r, j) in enumerate(pairs):
            o_stage[hd, natural_rows(r, j), :] = outs[i]
        o_ref[:, hs] = o_stage[hd].astype(o_ref.dtype)
    for i, (r, j) in enumerate(pairs):
        lse_stage[natural_rows(r, j), :] = lse_tiles[i]
    lse_ref[...] = lse_stage[...]


def _attention_group(qkv, h, w_side, conv_side, batch, seq, group):
    _, dil = DIL_GROUPS[group]
    t = batch * seq
    sub_len = seq // dil
    rows_per_step = min(seq, 16 * BLK * max(1, dil // 8))
    nj = rows_per_step // (dil * BLK)
    n_steps = seq // rows_per_step
    n_blk = sub_len // BLK
    sub_rows = rows_per_step // dil

    def own(part):
        return pl.BlockSpec((1, dil, sub_rows, BRANCH), lambda b, n: (b, 0, n, part))

    def prev(part):
        return pl.BlockSpec((1, dil, BLK, BRANCH), lambda b, n: (b, 0, jnp.maximum(n * nj - 1, 0), part))

    def nxt(part):
        return pl.BlockSpec((1, dil, BLK, BRANCH), lambda b, n: (b, 0, jnp.minimum((n + 1) * nj, n_blk - 1), part))

    side_cols = 2 * BRANCH if conv_side else w_side.shape[1]
    step_rows = lambda width: pl.BlockSpec((rows_per_step, width), lambda b, n: (b * n_steps + n, 0))
    return pl.pallas_call(
        functools.partial(_attn_kernel, dil=dil, nj=nj, n_steps=n_steps, conv_side=conv_side),
        out_shape=(jax.ShapeDtypeStruct((t, BRANCH), BF16), jax.ShapeDtypeStruct((t, HEAD_DIM), F32),
                   jax.ShapeDtypeStruct((t, side_cols), BF16)),
        grid=(batch, n_steps),
        in_specs=[own(0), own(1), prev(1), nxt(1), own(2), prev(2), nxt(2), step_rows(D_MODEL),
                  _resident(w_side.shape, lambda b, n: (0, 0))],
        out_specs=(step_rows(BRANCH), step_rows(HEAD_DIM), step_rows(side_cols)),
        scratch_shapes=[pltpu.VMEM((N_HEADS, rows_per_step, HEAD_DIM), F32), pltpu.VMEM((rows_per_step, HEAD_DIM), F32)],
        compiler_params=_params(("parallel", "parallel"), ATTENTION_VMEM_LIMIT),
        name=f"dilated_attention_g{group}",
    )(qkv, qkv, qkv, qkv, qkv, qkv, qkv, h, w_side)


def _hgrn_kernel(logit_ref, qf_ref, zf_ref, vf_ref, qb_ref, zb_ref, vb_ref, of_ref, ob_ref, state_ref,
                 *, layer, n_chunks):
    c = pl.program_id(1)

    @pl.when(c == 0)
    def _():
        state_ref[...] = jnp.zeros_like(state_ref)

    depth = logit_ref.shape[1]
    lbs = []
    for dirn in range(2):
        rows = [logit_ref[dirn, j:j + 1, :] for j in range(depth)]
        top = functools.reduce(jnp.maximum, rows)
        exps = [jnp.exp(rw - top) for rw in rows]
        lbs.append(sum(exps[1:layer + 1], jnp.zeros_like(top)) / sum(exps[1:], exps[0]))

    r = lax.broadcasted_iota(jnp.int32, (BLK, BLK), 0)
    cc = lax.broadcasted_iota(jnp.int32, (BLK, BLK), 1)

    def bcast_rows(per_chunk):
        return jnp.concatenate([jnp.broadcast_to(v, (BLK, BRANCH)) for v in per_chunk], axis=0)

    for dirn, (q_ref, z_ref, v_ref, o_ref) in enumerate(
            ((qf_ref, zf_ref, vf_ref, of_ref), (qb_ref, zb_ref, vb_ref, ob_ref))):
        tri = (cc <= r) if dirn == 0 else (cc >= r)
        tri_b = jnp.where(tri, 1.0, 0.0).astype(BF16)
        last, mid_row = (BLK - 1, BLK // 2 - 1) if dirn == 0 else (0, BLK // 2)
        lb = lbs[dirn]
        q = q_ref[...].astype(F32)
        z = z_ref[...].astype(F32)
        v = v_ref[...]
        sg = _sigmoid_rel(z)
        logf = jnp.log(lb + (1.0 - lb) * sg)
        kk = (1.0 - lb) * (1.0 - sg)
        hi = logf.astype(BF16)
        lo = (logf - hi.astype(F32)).astype(BF16)
        chunks = [slice(g * BLK, (g + 1) * BLK) for g in range(n_chunks)]
        cums = [_dot(tri_b, hi[ch]) + _dot(tri_b, lo[ch]) for ch in chunks]
        totals = [cm[last:last + 1, :] for cm in cums]
        mids = [cm[mid_row:mid_row + 1, :] for cm in cums]
        cum = jnp.concatenate(cums, axis=0)
        mid = bcast_rows(mids)
        qa = q * jnp.exp(cum - mid)
        ka = kk * jnp.exp(mid - cum)
        q_dec = (qa * bcast_rows([jnp.exp(md) for md in mids])).astype(BF16)
        k_dec = (ka * bcast_rows([jnp.exp(tt - md) for tt, md in zip(totals, mids)])).astype(BF16)
        qa = qa.astype(BF16)
        ka = ka.astype(BF16)
        tiles = [(g, hd) for g in range(n_chunks) for hd in range(N_HEADS)]
        sl = lambda g, hd: (slice(g * BLK, (g + 1) * BLK), slice(hd * HEAD_DIM, (hd + 1) * HEAD_DIM))
        atts = [jnp.where(tri, _dot_nt(qa[sl(g, hd)], ka[sl(g, hd)]), 0.0).astype(BF16) for g, hd in tiles]
        intra = {t: _dot(att, v[sl(*t)]) for t, att in zip(tiles, atts)}
        kvs = {t: _dot_tn(v[sl(*t)], k_dec[sl(*t)]) for t in tiles}
        decays = [jnp.exp(tt) for tt in totals]
        order = range(n_chunks) if dirn == 0 else range(n_chunks - 1, -1, -1)
        for hd in range(N_HEADS):
            hs = slice(hd * HEAD_DIM, (hd + 1) * HEAD_DIM)
            state_t = state_ref[dirn, hd]
            for g in order:
                o_ref[g * BLK:(g + 1) * BLK, hs] = intra[(g, hd)] + _dot_nt(q_dec[sl(g, hd)], state_t.astype(BF16))
                state_t = state_t * decays[g][:, hs] + kvs[(g, hd)]
            state_ref[dirn, hd] = state_t


def _hgrn(qzz, ig, logits, batch, seq, layer):
    t = qzz.shape[0]
    n_chunks = 4
    rows = n_chunks * BLK
    nc = seq // rows
    depth = logits.shape[1]

    def fwd(colblk):
        return pl.BlockSpec((rows, BRANCH), lambda b, c: (b * nc + c, colblk))

    def bwd(colblk):
        return pl.BlockSpec((rows, BRANCH), lambda b, c: (b * nc + nc - 1 - c, colblk))

    return pl.pallas_call(
        functools.partial(_hgrn_kernel, layer=layer, n_chunks=n_chunks),
        out_shape=(jax.ShapeDtypeStruct((t, BRANCH), F32),) * 2,
        grid=(batch, nc),
        in_specs=[pl.BlockSpec((2, depth, BRANCH), lambda b, c: (0, 0, 0)),
                  fwd(0), fwd(1), fwd(0), bwd(0), bwd(2), bwd(0)],
        out_specs=(pl.BlockSpec((rows, BRANCH), lambda b, c: (b * nc + c, 0)),
                   pl.BlockSpec((rows, BRANCH), lambda b, c: (b * nc + nc - 1 - c, 0))),
        scratch_shapes=[pltpu.VMEM((2, N_HEADS, HEAD_DIM, HEAD_DIM), F32)],
        compiler_params=_params(("parallel", "arbitrary")),
        name="hgrn2_scan",
    )(logits, qzz, qzz, ig, qzz, qzz, ig)


def _merge_kernel(x_ref, ao0_ref, ao1_ref, ao2_ref, l0_ref, l1_ref, l2_ref, of_ref, ob_ref, bg_ref,
                  cb_ref, cbp_ref, cbn_ref, h_ref, wgate_ref, convw_ref, ng_ref, wbr_ref, wout_ref, gpost_ref,
                  o_ref, *, tm, seq):
    i = pl.program_id(0)
    h = h_ref[...]
    gates = [_sigmoid(_dot(h, wgate_ref[:, n * D_MODEL:(n + 1) * D_MODEL])) for n in range(3)]

    ls = (l0_ref[...], l1_ref[...], l2_ref[...])
    aos = (ao0_ref, ao1_ref, ao2_ref)
    lmax = jnp.maximum(jnp.maximum(ls[0], ls[1]), ls[2])
    es = [jnp.exp(l - lmax) for l in ls]
    inv = 1.0 / (es[0] + es[1] + es[2])
    a_parts = []
    for hd in range(N_HEADS):
        hs = slice(hd * HEAD_DIM, (hd + 1) * HEAD_DIM)
        acc = None
        for g in range(N_GROUPS):
            w = (es[g] * inv)[:, hd:hd + 1]
            term = w * aos[g][:, hs].astype(F32)
            acc = term if acc is None else acc + term
        a_parts.append(acc)
    o_a = jnp.concatenate(a_parts, axis=1).astype(BF16)

    b_parts = []
    for hd in range(N_HEADS):
        hs = slice(hd * HEAD_DIM, (hd + 1) * HEAD_DIM)
        o = of_ref[:, hs] + ob_ref[:, hs]
        b_parts.append(_rms(o, ng_ref[:, hs]))
    gate = bg_ref[...].astype(F32)
    o_b = (jnp.concatenate(b_parts, axis=1) * (gate * _sigmoid(gate))).astype(BF16)

    cu = cb_ref[:, 0:BRANCH].astype(F32)
    keep_prev = jnp.where((i * tm) % seq == 0, 0.0, 1.0)
    keep_next = jnp.where(((i + 1) * tm) % seq == 0, 0.0, 1.0)
    halo_rows = cbp_ref.shape[0]
    prev_row = keep_prev * cbp_ref[halo_rows - 1:halo_rows, 0:BRANCH].astype(F32)
    next_row = keep_next * cbn_ref[0:1, 0:BRANCH].astype(F32)
    rid = lax.broadcasted_iota(jnp.int32, cu.shape, 0)
    before = jnp.where(rid == 0, prev_row, pltpu.roll(cu, 1, 0))
    after = jnp.where(rid == tm - 1, next_row, pltpu.roll(cu, tm - 1, 0))
    conv = convw_ref[0:1, :] * before + convw_ref[1:2, :] * cu + convw_ref[2:3, :] * after
    o_c = (cb_ref[:, BRANCH:2 * BRANCH].astype(F32) * conv).astype(BF16)

    merged = None
    for n, o_n in enumerate((o_a, o_b, o_c)):
        up = _dot(o_n, wbr_ref[n])
        term = gates[n] * up
        merged = term if merged is None else merged + term
    y = _dot(merged.astype(BF16), wout_ref[...])
    o_ref[...] = x_ref[...] + _rms(y, gpost_ref[...])


def _merge(x, aos, lses, o_f, o_b, ig, cb, h, w_gate, conv_w, norm_g, w_branch, w_out, g_post, seq, tm):
    t = x.shape[0]
    halo = 16
    n_halo = t // halo
    per = tm // halo

    def rows(width, colblk=0):
        return pl.BlockSpec((tm, width), lambda i: (i, colblk))

    def full(shape):
        return _resident(shape, lambda i: (0,) * len(shape))

    in_specs = ([rows(D_MODEL)] + [rows(BRANCH)] * 3 + [rows(HEAD_DIM)] * 3 + [rows(BRANCH)] * 2
                + [rows(BRANCH, 1), rows(2 * BRANCH),
                   pl.BlockSpec((halo, 2 * BRANCH), lambda i: (jnp.maximum(i * per - 1, 0), 0)),
                   pl.BlockSpec((halo, 2 * BRANCH), lambda i: (jnp.minimum((i + 1) * per, n_halo - 1), 0)),
                   rows(D_MODEL), full((D_MODEL, G_COLS)), full((3, BRANCH)), full((1, BRANCH)),
                   full((3, BRANCH, D_MODEL)), full((D_MODEL, D_MODEL)), full((1, D_MODEL))])
    return pl.pallas_call(
        functools.partial(_merge_kernel, tm=tm, seq=seq),
        out_shape=jax.ShapeDtypeStruct((t, D_MODEL), F32),
        grid=(t // tm,),
        in_specs=in_specs,
        out_specs=rows(D_MODEL),
        compiler_params=_params(("parallel",)),
        name="merge_branches",
    )(x, *aos, *lses, o_f, o_b, ig, cb, cb, cb, h, w_gate, conv_w, norm_g, w_branch, w_out, g_post)


def _swiglu(h, wg, wu, wd):
    g = _dot(h, wg)
    u = _dot(h, wu)
    return _dot((g * _sigmoid(g) * u).astype(BF16), wd)


def _embed_tail(x, proj, gin_ref, gpost_ref, wgate_ref):
    gate = _sigmoid(_dot(_rms(x, gin_ref[...]).astype(BF16), wgate_ref[...]))
    return x + _rms(proj * gate, gpost_ref[...])


def _embed_operands(embed, p_row0, tm, index):
    pd = embed[0].shape[1]
    const = lambda *_: (0, 0)
    p_index = lambda *a: (index(*a)[0] + p_row0 // tm, 0)
    return [pl.BlockSpec((tm, pd), p_index), pl.BlockSpec((1, D_MODEL), const), pl.BlockSpec((1, D_MODEL), const),
            _resident((D_MODEL, D_MODEL), const), _resident((pd, D_MODEL), const)]


def _ffn_kernel(x_ref, gpre_ref, gpost_ref, wg_ref, wu_ref, wd_ref, p_ref, ein_ref, epost_ref, egate_ref, eproj_ref,
                o_ref):
    proj = _dot(p_ref[...].astype(BF16), eproj_ref[...])
    x = x_ref[...]
    h = _rms(x, gpre_ref[...]).astype(BF16)
    mixed = x + _rms(_swiglu(h, wg_ref[...], wu_ref[...], wd_ref[...]), gpost_ref[...])
    o_ref[...] = _embed_tail(mixed, proj, ein_ref, epost_ref, egate_ref)


def _ffn(x, g_pre, g_post, wg, wu, wd, embed, p_row0, tm):
    t = x.shape[0]
    const = lambda i: (0, 0)
    return pl.pallas_call(
        _ffn_kernel,
        out_shape=jax.ShapeDtypeStruct((t, D_MODEL), F32),
        grid=(t // tm,),
        in_specs=[pl.BlockSpec((tm, D_MODEL), lambda i: (i, 0)),
                  pl.BlockSpec((1, D_MODEL), const), pl.BlockSpec((1, D_MODEL), const),
                  _resident(wg.shape, const), _resident(wu.shape, const), _resident(wd.shape, const)]
                 + _embed_operands(embed, p_row0, tm, lambda i: (i, 0)),
        out_specs=pl.BlockSpec((tm, D_MODEL), lambda i: (i, 0)),
        compiler_params=_params(("parallel",)),
        name="dense_swiglu",
    )(x, g_pre, g_post, wg, wu, wd, *embed)


PIECES = D_MODEL // 2 // HEAD_DIM
SC_WINDOW = 128
EXPERT_TILE = 512


def _pack_rows(vals):
    bits = lax.bitcast_convert_type(vals.astype(BF16).astype(F32), jnp.uint32)
    half = D_MODEL // 2
    word = bits[:, :half] | (bits[:, half:] >> 16)
    return [lax.bitcast_convert_type(word[:, c * HEAD_DIM:(c + 1) * HEAD_DIM], jnp.int32) for c in range(PIECES)]


def _unpack_rows(piece_refs):
    words = [lax.bitcast_convert_type(r[...], jnp.uint32) for r in piece_refs]
    hi = [lax.bitcast_convert_type(w & jnp.uint32(0xFFFF0000), F32) for w in words]
    lo = [lax.bitcast_convert_type(w << 16, F32) for w in words]
    return jnp.concatenate(hi + lo, axis=1)


def _route_kernel(x_ref, gpre_ref, rhi_ref, rlo_ref, *refs, n_experts):
    piece_refs = refs[:PIECES]
    rec_ref, rank_ref, count_ref, carry_ref = refs[PIECES:]
    i = pl.program_id(0)
    tm = x_ref.shape[0]

    @pl.when(i == 0)
    def _():
        carry_ref[...] = jnp.zeros_like(carry_ref)

    hf = _rms(x_ref[...], gpre_ref[...])
    for r, piece in zip(piece_refs, _pack_rows(hf)):
        r[...] = piece
    h_hi = hf.astype(BF16)
    h_lo = (hf - h_hi.astype(F32)).astype(BF16)
    logits = _dot(h_hi, rhi_ref[...]) + _dot(h_hi, rlo_ref[...]) + _dot(h_lo, rhi_ref[...])
    lane = lax.broadcasted_iota(jnp.int32, logits.shape, 1).astype(F32)
    logits = jnp.where(lane < n_experts, logits, NEG_INF)
    m1 = jnp.max(logits, axis=1, keepdims=True)
    i1 = jnp.min(jnp.where(logits == m1, lane, 1e9), axis=1, keepdims=True)
    rest = jnp.where(lane == i1, NEG_INF, logits)
    m2 = jnp.max(rest, axis=1, keepdims=True)
    i2 = jnp.min(jnp.where(rest == m2, lane, 1e9), axis=1, keepdims=True)
    ex = jnp.exp(m2 - m1)
    w1 = 1.0 / (1.0 + ex)
    rec = jnp.where(lane == 0, i1, jnp.where(lane == 1, i2, jnp.where(lane == 2, w1, jnp.where(lane == 3, ex * w1, 0.0))))
    rec_ref[...] = rec
    rec_t = rec.T
    e1 = rec_t[0:1, :]
    e2 = rec_t[1:2, :]
    sub = lax.broadcasted_iota(jnp.int32, (8, tm), 0).astype(F32)
    oh1 = jnp.where(sub == e1, 1.0, 0.0)
    oh2 = jnp.where(sub == e2, 1.0, 0.0)
    chosen = oh1 + oh2
    src = lax.broadcasted_iota(jnp.int32, (tm, tm), 0)
    dst = lax.broadcasted_iota(jnp.int32, (tm, tm), 1)
    before = jnp.where(src < dst, 1.0, 0.0).astype(BF16)
    rank = _dot(chosen.astype(BF16), before) + carry_ref[:, 0:1]
    rank1 = jnp.sum(oh1 * rank, axis=0, keepdims=True)
    rank2 = jnp.sum(oh2 * rank, axis=0, keepdims=True)
    rank_ref[...] = jnp.where(sub == 0, rank1, jnp.where(sub == 1, rank2, jnp.where(sub == 2, e1, jnp.where(sub == 3, e2, 0.0))))
    carry_ref[...] = carry_ref[...] + jnp.sum(chosen, axis=1, keepdims=True)
    count_ref[...] = carry_ref[...]


def _route(x, g_pre, r_hi, r_lo, n_experts, tm):
    t = x.shape[0]
    assert n_experts <= 8
    piece = pl.BlockSpec((tm, HEAD_DIM), lambda i: (i, 0))
    return pl.pallas_call(
        functools.partial(_route_kernel, n_experts=n_experts),
        out_shape=(*[jax.ShapeDtypeStruct((t, HEAD_DIM), jnp.int32)] * PIECES,
                   jax.ShapeDtypeStruct((t, HEAD_DIM), F32), jax.ShapeDtypeStruct((8, t), F32),
                   jax.ShapeDtypeStruct((8, HEAD_DIM), F32)),
        grid=(t // tm,),
        in_specs=[pl.BlockSpec((tm, D_MODEL), lambda i: (i, 0)), pl.BlockSpec((1, D_MODEL), lambda i: (0, 0)),
                  pl.BlockSpec((D_MODEL, HEAD_DIM), lambda i: (0, 0)), pl.BlockSpec((D_MODEL, HEAD_DIM), lambda i: (0, 0))],
        out_specs=(*[piece] * PIECES, piece, pl.BlockSpec((8, tm), lambda i: (0, i)),
                   pl.BlockSpec((8, HEAD_DIM), lambda i: (0, 0))),
        scratch_shapes=[pltpu.VMEM((8, HEAD_DIM), F32)],
        compiler_params=_params(("arbitrary",)),
        name="moe_route",
    )(x, g_pre, r_hi, r_lo)


def _slot_kernel(rank_ref, start_ref, s1_ref, s2_ref):
    tm = rank_ref.shape[1]
    sub = lax.broadcasted_iota(jnp.int32, (8, tm), 0).astype(F32)
    start = start_ref[:, 0:1]
    rows = rank_ref[...]
    for choice, out in ((0, s1_ref), (1, s2_ref)):
        base = jnp.sum(jnp.where(sub == rows[2 + choice:3 + choice, :], start, 0.0), axis=0, keepdims=True)
        out[...] = (base + rows[choice:choice + 1, :]).astype(jnp.int32)


def _slots(ranks, starts, tm):
    t = ranks.shape[1]
    row = pl.BlockSpec((1, tm), lambda i: (0, i))
    return pl.pallas_call(
        _slot_kernel,
        out_shape=(jax.ShapeDtypeStruct((1, t), jnp.int32),) * 2,
        grid=(t // tm,),
        in_specs=[pl.BlockSpec((8, tm), lambda i: (0, i)), pl.BlockSpec((8, HEAD_DIM), lambda i: (0, 0))],
        out_specs=(row, row),
        compiler_params=_params(("parallel",)),
        name="moe_slots",
    )(ranks, starts)


def _sc_mesh():
    return plsc.VectorSubcoreMesh(core_axis_name="core", subcore_axis_name="subcore")


def _sc_scatter_rows(srcs, idxs, n_rows):
    ns, nk = len(srcs), len(idxs)
    half = srcs[0].shape[0] // SC_WINDOW // 2

    @functools.partial(pl.kernel, mesh=_sc_mesh(), scratch_types=[],
                       out_type=tuple(jax.ShapeDtypeStruct((n_rows, HEAD_DIM), srcs[0].dtype) for _ in range(ns)))
    def scatter(*refs):
        out_hbm = refs[ns + nk:]

        def body(*blocks):
            for idx in blocks[ns:]:
                for c in range(ns):
                    pltpu.sync_copy(blocks[c], out_hbm[c].at[idx.at[0]])

        pltpu.emit_pipeline(
            body, grid=(2, half),
            in_specs=[pl.BlockSpec((SC_WINDOW, HEAD_DIM), lambda i, j: (i * half + j, 0)) for _ in range(ns)]
                     + [pl.BlockSpec((1, SC_WINDOW), lambda i, j: (0, i * half + j)) for _ in range(nk)],
            out_specs=[],
            core_axis_name=("core", "subcore"),
            dimension_semantics=(pltpu.PARALLEL, pltpu.PARALLEL),
        )(*refs[:ns + nk])

    return scatter(*srcs, *idxs)


def _sc_gather_rows(tables, idx):
    nt = len(tables)
    n = idx.shape[1]
    half = n // SC_WINDOW // 2

    @functools.partial(pl.kernel, mesh=_sc_mesh(), scratch_types=[],
                       out_type=tuple(jax.ShapeDtypeStruct((n, HEAD_DIM), tables[0].dtype) for _ in range(nt)))
    def gather(*refs):
        table_hbm = refs[:nt]

        def body(idx_blk, *out_blks):
            for c in range(nt):
                pltpu.sync_copy(table_hbm[c].at[idx_blk.at[0]], out_blks[c])

        pltpu.emit_pipeline(
            body, grid=(2, half),
            in_specs=[pl.BlockSpec((1, SC_WINDOW), lambda i, j: (0, i * half + j))],
            out_specs=[pl.BlockSpec((SC_WINDOW, HEAD_DIM), lambda i, j: (i * half + j, 0)) for _ in range(nt)],
            core_axis_name=("core", "subcore"),
            dimension_semantics=(pltpu.PARALLEL, pltpu.PARALLEL),
        )(refs[nt], *refs[nt + 1:])

    return gather(*tables, idx)


def _expert_kernel(tile_expert_ref, n_used_ref, *refs):
    x_refs = refs[:PIECES]
    wg_ref, wu_ref, wd_ref = refs[PIECES:PIECES + 3]
    y_refs = refs[PIECES + 3:]

    @pl.when(pl.program_id(0) < n_used_ref[0])
    def _():
        h = _unpack_rows(x_refs).astype(BF16)
        y = _swiglu(h, wg_ref[0], wu_ref[0], wd_ref[0])
        for r, piece in zip(y_refs, _pack_rows(y)):
            r[...] = piece


def _expert_ffn(xs, tile_expert, n_used, wg, wu, wd):
    n_rows = xs[0].shape[0]
    rows = pl.BlockSpec((EXPERT_TILE, HEAD_DIM), lambda i, te, nu: (jnp.minimum(i, nu[0] - 1), 0))
    out_rows = pl.BlockSpec((EXPERT_TILE, HEAD_DIM), lambda i, te, nu: (i, 0))
    expert = lambda i, te, nu: (te[i], 0, 0)
    grid_spec = pltpu.PrefetchScalarGridSpec(
        num_scalar_prefetch=2,
        grid=(n_rows // EXPERT_TILE,),
        in_specs=[rows] * PIECES + [_resident((1,) + wg.shape[1:], expert), _resident((1,) + wu.shape[1:], expert),
                                    _resident((1,) + wd.shape[1:], expert)],
        out_specs=[out_rows] * PIECES,
    )
    return pl.pallas_call(
        _expert_kernel,
        out_shape=[jax.ShapeDtypeStruct((n_rows, HEAD_DIM), jnp.int32)] * PIECES,
        grid_spec=grid_spec,
        compiler_params=_params(("arbitrary",)),
        name="moe_expert_swiglu",
    )(tile_expert, n_used, *xs, wg, wu, wd)


def _combine_kernel(x_ref, rec_ref, gpost_ref, p_ref, ein_ref, epost_ref, egate_ref, eproj_ref, *refs):
    proj = _dot(p_ref[...].astype(BF16), eproj_ref[...])
    y1 = _unpack_rows(refs[:PIECES])
    y2 = _unpack_rows(refs[PIECES:2 * PIECES])
    o_ref = refs[2 * PIECES]
    rec = rec_ref[...]
    y = rec[:, 2:3] * y1 + rec[:, 3:4] * y2
    mixed = x_ref[...] + _rms(y, gpost_ref[...])
    o_ref[...] = _embed_tail(mixed, proj, ein_ref, epost_ref, egate_ref)


def _combine(x, rec, g_post, embed, p_row0, y1, y2, tm):
    t = x.shape[0]
    piece = pl.BlockSpec((tm, HEAD_DIM), lambda i: (i, 0))
    return pl.pallas_call(
        _combine_kernel,
        out_shape=jax.ShapeDtypeStruct((t, D_MODEL), F32),
        grid=(t // tm,),
        in_specs=[pl.BlockSpec((tm, D_MODEL), lambda i: (i, 0)), piece, pl.BlockSpec((1, D_MODEL), lambda i: (0, 0))]
                 + _embed_operands(embed, p_row0, tm, lambda i: (i, 0)) + [piece] * (2 * PIECES),
        out_specs=pl.BlockSpec((tm, D_MODEL), lambda i: (i, 0)),
        compiler_params=_params(("parallel",)),
        name="moe_combine",
    )(x, rec, g_post, *embed, *y1, *y2)


def _moe(x, g_pre, g_post, router, wg, wu, wd, embed, p_row0, tm):
    t = x.shape[0]
    n_experts = router.shape[1]
    router = jnp.pad(router, ((0, 0), (0, HEAD_DIM - n_experts)))
    r_hi = router.astype(BF16)
    r_lo = (router - r_hi.astype(F32)).astype(BF16)
    *h_pieces, rec, ranks, counts = _route(x, g_pre, r_hi, r_lo, n_experts, tm)

    counts = counts[:n_experts, 0].astype(jnp.int32)
    tiles = (counts + EXPERT_TILE - 1) // EXPERT_TILE
    tile_end = jnp.cumsum(tiles)
    starts = ((tile_end - tiles) * EXPERT_TILE).astype(F32)
    starts = jnp.broadcast_to(jnp.pad(starts, (0, 8 - n_experts))[:, None], (8, HEAD_DIM))
    n_rows = TOP_K * t + n_experts * EXPERT_TILE
    n_tiles = n_rows // EXPERT_TILE
    n_used = tile_end[-1:]
    tile_ids = jnp.minimum(jnp.arange(n_tiles, dtype=jnp.int32), n_used[0] - 1)
    tile_expert = jnp.sum(tile_ids[:, None] >= tile_end[None, :], axis=1).astype(jnp.int32)

    slot1, slot2 = _slots(ranks, starts, min(t, 8192))
    xs = []
    for c in range(0, PIECES, 2):
        xs += _sc_scatter_rows(h_pieces[c:c + 2], [slot1, slot2], n_rows)
    ys = _expert_ffn(xs, tile_expert, n_used, wg, wu, wd)
    gathered = []
    for slot in (slot1, slot2):
        rows = []
        for c in range(0, PIECES, 2):
            rows += _sc_gather_rows(ys[c:c + 2], slot)
        gathered.append(rows)
    return _combine(x, rec, g_post, embed, p_row0, gathered[0], gathered[1], tm)


def kernel(x, p, positions, w_in, conv_w, hgrn_lb_logits, hgrn_norm_g, w_branch, w_out, g_mix_pre, g_mix_post, g_ffn_pre, g_ffn_post, dense_w_gate, dense_w_up, dense_w_down, moe_router, moe_w_gate, moe_w_up, moe_w_down, ple_w_proj, ple_w_gate, ple_g_in, ple_g_post):
    batch, seq, _ = x.shape
    depth = w_in.shape[0]
    t = batch * seq
    tm = min(512, seq)
    tm_merge = min(512, seq)
    tm_ffn = min(512, seq)

    xt = x.reshape(t, D_MODEL)
    cos, sin = _rope_tables(positions.astype(F32).reshape(t // ROPE_PACK, ROPE_PACK), tm)
    row = lambda a: a.reshape(1, -1)
    in_edges = (0, A_COLS, A_COLS + 3 * BRANCH, A_COLS + B_COLS, A_COLS + B_COLS + C_COLS, A_COLS + B_COLS + C_COLS + G_COLS)
    in_bounds = tuple(zip(in_edges[:-1], in_edges[1:]))
    w_in_rows = w_in.reshape(depth * D_MODEL, -1)
    p_rows = p.reshape(depth * t, -1)

    for i in range(depth):
        w_attn, w_qzz, w_ig, w_conv, w_gates = _split_to_bf16(w_in_rows, i * D_MODEL, D_MODEL, in_bounds)
        g_pre = row(g_mix_pre[i])
        *qkvs, h = _inproj_attention(xt, g_pre, w_attn, cos, sin, batch, seq, tm)
        aos, lses, sides = zip(*[_attention_group(qkvs[g], h, w_side, g == 2, batch, seq, g)
                                 for g, w_side in enumerate((w_qzz, w_ig, w_conv))])
        qzz, ig, cb = sides
        o_f, o_b = _hgrn(qzz, ig, hgrn_lb_logits, batch, seq, i)
        xt = _merge(xt, aos, lses, o_f, o_b, ig, cb, h, w_gates, conv_w[i],
                    row(hgrn_norm_g[i]), w_branch[i].astype(BF16), w_out[i].astype(BF16), row(g_mix_post[i]),
                    seq, tm_merge)

        embed = (p_rows, row(ple_g_in[i]), row(ple_g_post[i]),
                 ple_w_gate[i].astype(BF16), ple_w_proj[i].astype(BF16))
        j = i // 2
        if i % 2 == 0:
            xt = _ffn(xt, row(g_ffn_pre[i]), row(g_ffn_post[i]), dense_w_gate[j].astype(BF16),
                      dense_w_up[j].astype(BF16), dense_w_down[j].astype(BF16), embed, i * t, tm_ffn)
        else:
            xt = _moe(xt, row(g_ffn_pre[i]), row(g_ffn_post[i]), moe_router[j], _to_bf16(moe_w_gate[j]),
                      _to_bf16(moe_w_up[j]), _to_bf16(moe_w_down[j]), embed, i * t, tm)

    return xt.reshape(batch, seq, D_MODEL)
```

```python
import functools
import math

import jax
import jax.numpy as jnp
from jax import lax
from jax.experimental import pallas as pl
from jax.experimental.pallas import tpu as pltpu
from jax.experimental.pallas import tpu_sc as plsc

D_MODEL = 1024
EPS = 1e-6
NEG_INF = -1e30

HEAD_DIM = 128
N_HEADS = 4
BRANCH = N_HEADS * HEAD_DIM
DIL_GROUPS = ((128, 1), (512, 4), (2048, 16))
N_GROUPS = len(DIL_GROUPS)
BLK = 64
ROT_DIM = HEAD_DIM // 4
ROPE_THETA = 500000.0
TOP_K = 2

A_COLS = N_GROUPS * 3 * BRANCH
B_COLS = 5 * BRANCH
C_COLS = 3 * BRANCH
G_COLS = 3 * D_MODEL

BF16 = jnp.bfloat16
F32 = jnp.float32

VMEM_LIMIT = 56 * 1024 * 1024
ATTENTION_VMEM_LIMIT = 62 * 1024 * 1024


def _params(sem, vmem_limit=VMEM_LIMIT):
    return pltpu.CompilerParams(dimension_semantics=sem, vmem_limit_bytes=vmem_limit)


def _cast_kernel(w_ref, o_ref):
    o_ref[...] = w_ref[...].astype(o_ref.dtype)


def _to_bf16(w):
    cols = w.shape[-1]
    flat = w.reshape(-1, cols)
    rows = flat.shape[0]
    block = rows
    while block % 32 == 0 and block * cols * 4 > 4 * 1024 * 1024:
        block //= 2
    out = pl.pallas_call(
        _cast_kernel,
        out_shape=jax.ShapeDtypeStruct(flat.shape, BF16),
        grid=(rows // block,),
        in_specs=[pl.BlockSpec((block, cols), lambda i: (i, 0))],
        out_specs=pl.BlockSpec((block, cols), lambda i: (i, 0)),
        compiler_params=_params(("parallel",)),
        name="weights_to_bf16",
    )(flat)
    return out.reshape(w.shape)


def _split_cast_kernel(w_ref, *o_refs, bounds):
    for o_ref, (lo, hi) in zip(o_refs, bounds):
        o_ref[...] = w_ref[:, lo:hi].astype(o_ref.dtype)


def _split_to_bf16(w, first_row, n, bounds, rows=128):
    cols = w.shape[1]
    first = first_row // rows
    return pl.pallas_call(
        functools.partial(_split_cast_kernel, bounds=bounds),
        out_shape=tuple(jax.ShapeDtypeStruct((n, hi - lo), BF16) for lo, hi in bounds),
        grid=(n // rows,),
        in_specs=[pl.BlockSpec((rows, cols), lambda i: (first + i, 0))],
        out_specs=tuple(pl.BlockSpec((rows, hi - lo), lambda i: (i, 0)) for lo, hi in bounds),
        compiler_params=_params(("parallel",)),
        name="split_weights_to_bf16",
    )(w)


def _resident(shape, index_map):
    return pl.BlockSpec(shape, index_map, pipeline_mode=pl.Buffered(1))


def _rms(xf, g):
    return xf * lax.rsqrt(jnp.mean(xf * xf, axis=-1, keepdims=True) + EPS) * g


def _sigmoid(z):
    return 0.5 * jnp.tanh(0.5 * z) + 0.5


def _sigmoid_rel(z):
    return 1.0 / (1.0 + jnp.exp(-z))


def _dot(a, b):
    return jnp.dot(a, b, preferred_element_type=F32)


def _dot_nt(a, b):
    return lax.dot_general(a, b, (((1,), (1,)), ((), ())), preferred_element_type=F32)


def _dot_tn(a, b):
    return lax.dot_general(a, b, (((0,), (0,)), ((), ())), preferred_element_type=F32)


ROPE_PACK = HEAD_DIM // ROT_DIM


def _rope_table_kernel(pos_ref, freq_ref, sign_ref, cos_ref, sin_ref):
    pos = pos_ref[...]
    group = lax.broadcasted_iota(jnp.int32, cos_ref.shape, 1) // ROT_DIM
    pos_lanes = jnp.zeros(cos_ref.shape, F32)
    for k in range(ROPE_PACK):
        pos_lanes = jnp.where(group == k, pos[:, k:k + 1], pos_lanes)
    ang = pos_lanes * freq_ref[...]
    cos_ref[...] = jnp.cos(ang)
    sin_ref[...] = jnp.sin(ang) * sign_ref[...]


def _rope_tables(pos_packed, rows):
    n = pos_packed.shape[0]
    half = ROT_DIM // 2
    inv_freq = ROPE_THETA ** (-jnp.arange(0, ROT_DIM, 2, dtype=F32) / ROT_DIM)
    freq = jnp.tile(jnp.concatenate([inv_freq, inv_freq]), ROPE_PACK)[None, :]
    sign = jnp.tile(jnp.concatenate([-jnp.ones((half,), F32), jnp.ones((half,), F32)]), ROPE_PACK)[None, :]
    row = pl.BlockSpec((1, HEAD_DIM), lambda i: (0, 0))
    tab = pl.BlockSpec((rows, HEAD_DIM), lambda i: (i, 0))
    return pl.pallas_call(
        _rope_table_kernel,
        out_shape=(jax.ShapeDtypeStruct((n, HEAD_DIM), F32),) * 2,
        grid=(n // rows,),
        in_specs=[pl.BlockSpec((rows, ROPE_PACK), lambda i: (i, 0)), row, row],
        out_specs=(tab, tab),
        compiler_params=_params(("parallel",)),
        name="rope_tables",
    )(pos_packed, freq, sign)


def _unpack_rope_tables(cosp_ref, sinp_ref, tab_ref):
    rows = cosp_ref.shape[0]
    lane = lax.broadcasted_iota(jnp.int32, (rows, HEAD_DIM), 1)
    for k in range(ROPE_PACK):
        for idx, (ref, fill) in enumerate(((cosp_ref, 1.0), (sinp_ref, 0.0))):
            packed = ref[...]
            shifted = pltpu.roll(packed, HEAD_DIM - ROT_DIM * k, 1) if k else packed
            tab_ref[idx, pl.ds(k, rows, stride=ROPE_PACK), :] = jnp.where(lane < ROT_DIM, shifted, fill)


def _rope_head(th, cos, sin):
    half = ROT_DIM // 2
    lane = lax.broadcasted_iota(jnp.int32, th.shape, 1)
    swapped = jnp.where(lane < half, pltpu.roll(th, HEAD_DIM - half, 1), pltpu.roll(th, half, 1))
    return th * cos + swapped * sin


def _inproj_attn_kernel(x_ref, g_ref, w_ref, cosp_ref, sinp_ref, o0_ref, o1_ref, o2_ref, h_ref, stage_ref, tab_ref):
    _unpack_rope_tables(cosp_ref, sinp_ref, tab_ref)
    hf = _rms(x_ref[...], g_ref[...])
    h_ref[...] = hf.astype(BF16)
    n_pieces = D_MODEL // HEAD_DIM
    for pc in range(n_pieces):
        stage_ref[pc] = hf[:, pc * HEAD_DIM:(pc + 1) * HEAD_DIM]
    scale = 1.0 / math.sqrt(HEAD_DIM)
    tm = x_ref.shape[0]
    for grp, o_ref in enumerate((o0_ref, o1_ref, o2_ref)):
        dil = DIL_GROUPS[grp][1]
        sub = tm // dil
        if dil == 1:
            h, cos, sin = hf.astype(BF16), tab_ref[0], tab_ref[1]
        else:
            residue = lambda ref, *lead: jnp.concatenate(
                [ref[(*lead, pl.ds(r, sub, stride=dil), slice(None))] for r in range(dil)], axis=0)
            h = jnp.concatenate([residue(stage_ref, pc) for pc in range(n_pieces)], axis=1).astype(BF16)
            cos, sin = residue(tab_ref, 0), residue(tab_ref, 1)
        for part in range(3):
            c = grp * 3 + part
            t = _dot(h, w_ref[:, c * BRANCH:(c + 1) * BRANCH])
            if part < 2:
                heads = []
                for hd in range(N_HEADS):
                    r = _rope_head(t[:, hd * HEAD_DIM:(hd + 1) * HEAD_DIM], cos, sin)
                    heads.append(r * scale if part == 0 else r)
                t = jnp.concatenate(heads, axis=1)
            t = t.astype(BF16)
            for r in range(dil):
                o_ref[0, r, :, part * BRANCH:(part + 1) * BRANCH] = t[r * sub:(r + 1) * sub, :]


def _inproj_attention(x, g, w, cos, sin, batch, seq, tm):
    per_seq = seq // tm
    rows = lambda width: pl.BlockSpec((tm, width), lambda b, j: (b * per_seq + j, 0))
    packed = pl.BlockSpec((tm // ROPE_PACK, HEAD_DIM), lambda b, j: (b * per_seq + j, 0))
    out_shape = tuple(jax.ShapeDtypeStruct((batch, dil, seq // dil, 3 * BRANCH), BF16) for _, dil in DIL_GROUPS)
    out_shape += (jax.ShapeDtypeStruct((batch * seq, D_MODEL), BF16),)
    out_specs = tuple(pl.BlockSpec((1, dil, tm // dil, 3 * BRANCH), lambda b, j: (b, 0, j, 0))
                      for _, dil in DIL_GROUPS) + (rows(D_MODEL),)
    return pl.pallas_call(
        _inproj_attn_kernel, out_shape=out_shape, grid=(batch, per_seq),
        in_specs=[rows(D_MODEL), pl.BlockSpec((1, D_MODEL), lambda b, j: (0, 0)),
                  _resident((D_MODEL, A_COLS), lambda b, j: (0, 0)), packed, packed],
        out_specs=out_specs,
        scratch_shapes=[pltpu.VMEM((D_MODEL // HEAD_DIM, tm, HEAD_DIM), F32), pltpu.VMEM((2, tm, HEAD_DIM), F32)],
        compiler_params=_params(("parallel", "parallel")), name="inproj_attention",
    )(x, g, w, cos, sin)


def _attn_kernel(q_ref, k_ref, kp_ref, kn_ref, v_ref, vp_ref, vn_ref, h_ref, w_ref, o_ref, lse_ref, side_ref,
                 o_stage, lse_stage, *, dil, nj, n_steps, conv_side):
    n = pl.program_id(1)
    part_rows = h_ref.shape[0] // N_HEADS

    def side_projection(part):
        rs = slice(part * part_rows, (part + 1) * part_rows)
        h = h_ref[rs, :]
        if conv_side:
            u = _dot(h, w_ref[:, 0:BRANCH])
            b_gate = _dot(h, w_ref[:, BRANCH:2 * BRANCH])
            c_gate = _dot(h, w_ref[:, 2 * BRANCH:3 * BRANCH])
            side_ref[rs, 0:BRANCH] = (c_gate * u).astype(BF16)
            side_ref[rs, BRANCH:2 * BRANCH] = b_gate.astype(BF16)
        else:
            for c in range(w_ref.shape[1] // BRANCH):
                cols = slice(c * BRANCH, (c + 1) * BRANCH)
                side_ref[rs, cols] = _dot(h, w_ref[:, cols]).astype(BF16)

    qb = 2 if nj % 2 == 0 else 1
    q_rows, w_rows, n_tiles = qb * BLK, (qb + 2) * BLK, nj // qb
    col_lo = jnp.where(n > 0, 0, BLK)
    col_hi = jnp.where(n < n_steps - 1, w_rows, w_rows - BLK)
    row = lax.broadcasted_iota(jnp.int32, (q_rows, w_rows), 0)
    col = lax.broadcasted_iota(jnp.int32, (q_rows, w_rows), 1)
    band = jnp.abs(col - BLK - row) <= BLK
    masks = {}
    for j in range(n_tiles):
        m = band
        if j == 0:
            m = m & (col >= col_lo)
        if j == n_tiles - 1:
            m = m & (col < col_hi)
        masks[j] = m
    pairs = [(r, j) for r in range(dil) for j in range(n_tiles)]
    lane = lax.broadcasted_iota(jnp.int32, (len(pairs), q_rows, HEAD_DIM), 2)

    def natural_rows(r, j):
        return pl.ds(j * q_rows * dil + r, q_rows, stride=dil)

    def window(own_ref, prev_ref, next_ref, r, j, hs):
        parts = []
        for w in range(j * qb - 1, j * qb + qb + 1):
            if w < 0:
                parts.append(prev_ref[0, r, :, hs])
            elif w >= nj:
                parts.append(next_ref[0, r, :, hs])
            else:
                parts.append(own_ref[0, r, w * BLK:(w + 1) * BLK, hs])
        return jnp.concatenate(parts, axis=0)

    lse_tiles = jnp.zeros((len(pairs), q_rows, HEAD_DIM), F32)
    for hd in range(N_HEADS):
        side_projection(hd)
        hs = slice(hd * HEAD_DIM, (hd + 1) * HEAD_DIM)
        scores = [_dot_nt(q_ref[0, r, j * q_rows:(j + 1) * q_rows, hs], window(k_ref, kp_ref, kn_ref, r, j, hs))
                  for r, j in pairs]
        s = jnp.stack([jnp.where(masks[j], sc, NEG_INF) for (r, j), sc in zip(pairs, scores)])
        m = jnp.max(s, axis=2, keepdims=True)
        p = jnp.exp(s - m)
        l = jnp.sum(p, axis=2, keepdims=True)
        pb = p.astype(BF16)
        outs = jnp.stack([_dot(pb[i], window(v_ref, vp_ref, vn_ref, r, j, hs)) for i, (r, j) in enumerate(pairs)])
        outs = outs * (1.0 / l)
        lse_tiles = jnp.where(lane == hd, m + jnp.log(l), lse_tiles)
        for i, (r, j) in enumerate(pairs):
            o_stage[hd, natural_rows(r, j), :] = outs[i]
        o_ref[:, hs] = o_stage[hd].astype(o_ref.dtype)
    for i, (r, j) in enumerate(pairs):
        lse_stage[natural_rows(r, j), :] = lse_tiles[i]
    lse_ref[...] = lse_stage[...]


def _attention_group(qkv, h, w_side, conv_side, batch, seq, group):
    _, dil = DIL_GROUPS[group]
    t = batch * seq
    sub_len = seq // dil
    rows_per_step = min(seq, 32 * BLK)
    nj = rows_per_step // (dil * BLK)
    n_steps = seq // rows_per_step
    n_blk = sub_len // BLK
    sub_rows = rows_per_step // dil

    def own(part):
        return pl.BlockSpec((1, dil, sub_rows, BRANCH), lambda b, n: (b, 0, n, part))

    def prev(part):
        return pl.BlockSpec((1, dil, BLK, BRANCH), lambda b, n: (b, 0, jnp.maximum(n * nj - 1, 0), part))

    def nxt(part):
        return pl.BlockSpec((1, dil, BLK, BRANCH), lambda b, n: (b, 0, jnp.minimum((n + 1) * nj, n_blk - 1), part))

    side_cols = 2 * BRANCH if conv_side else w_side.shape[1]
    step_rows = lambda width: pl.BlockSpec((rows_per_step, width), lambda b, n: (b * n_steps + n, 0))
    return pl.pallas_call(
        functools.partial(_attn_kernel, dil=dil, nj=nj, n_steps=n_steps, conv_side=conv_side),
        out_shape=(jax.ShapeDtypeStruct((t, BRANCH), BF16), jax.ShapeDtypeStruct((t, HEAD_DIM), F32),
                   jax.ShapeDtypeStruct((t, side_cols), BF16)),
        grid=(batch, n_steps),
        in_specs=[own(0), own(1), prev(1), nxt(1), own(2), prev(2), nxt(2), step_rows(D_MODEL),
                  _resident(w_side.shape, lambda b, n: (0, 0))],
        out_specs=(step_rows(BRANCH), step_rows(HEAD_DIM), step_rows(side_cols)),
        scratch_shapes=[pltpu.VMEM((N_HEADS, rows_per_step, HEAD_DIM), F32), pltpu.VMEM((rows_per_step, HEAD_DIM), F32)],
        compiler_params=_params(("parallel", "parallel"), ATTENTION_VMEM_LIMIT),
        name=f"dilated_attention_g{group}",
    )(qkv, qkv, qkv, qkv, qkv, qkv, qkv, h, w_side)


def _hgrn_kernel(logit_ref, qf_ref, zf_ref, vf_ref, qb_ref, zb_ref, vb_ref, of_ref, ob_ref, state_ref,
                 *, layer, n_chunks):
    c = pl.program_id(1)

    @pl.when(c == 0)
    def _():
        state_ref[...] = jnp.zeros_like(state_ref)

    depth = logit_ref.shape[1]
    lbs = []
    for dirn in range(2):
        rows = [logit_ref[dirn, j:j + 1, :] for j in range(depth)]
        top = functools.reduce(jnp.maximum, rows)
        exps = [jnp.exp(rw - top) for rw in rows]
        lbs.append(sum(exps[1:layer + 1], jnp.zeros_like(top)) / sum(exps[1:], exps[0]))

    r = lax.broadcasted_iota(jnp.int32, (BLK, BLK), 0)
    cc = lax.broadcasted_iota(jnp.int32, (BLK, BLK), 1)

    def bcast_rows(per_chunk):
        return jnp.concatenate([jnp.broadcast_to(v, (BLK, BRANCH)) for v in per_chunk], axis=0)

    for dirn, (q_ref, z_ref, v_ref, o_ref) in enumerate(
            ((qf_ref, zf_ref, vf_ref, of_ref), (qb_ref, zb_ref, vb_ref, ob_ref))):
        tri = (cc <= r) if dirn == 0 else (cc >= r)
        tri_b = jnp.where(tri, 1.0, 0.0).astype(BF16)
        last, mid_row = (BLK - 1, BLK // 2 - 1) if dirn == 0 else (0, BLK // 2)
        lb = lbs[dirn]
        q = q_ref[...].astype(F32)
        z = z_ref[...].astype(F32)
        v = v_ref[...]
        sg = _sigmoid_rel(z)
        logf = jnp.log(lb + (1.0 - lb) * sg)
        kk = (1.0 - lb) * (1.0 - sg)
        hi = logf.astype(BF16)
        lo = (logf - hi.astype(F32)).astype(BF16)
        chunks = [slice(g * BLK, (g + 1) * BLK) for g in range(n_chunks)]
        cums = [_dot(tri_b, hi[ch]) + _dot(tri_b, lo[ch]) for ch in chunks]
        totals = [cm[last:last + 1, :] for cm in cums]
        mids = [cm[mid_row:mid_row + 1, :] for cm in cums]
        cum = jnp.concatenate(cums, axis=0)
        mid = bcast_rows(mids)
        qa = q * jnp.exp(cum - mid)
        ka = kk * jnp.exp(mid - cum)
        q_dec = (qa * bcast_rows([jnp.exp(md) for md in mids])).astype(BF16)
        k_dec = (ka * bcast_rows([jnp.exp(tt - md) for tt, md in zip(totals, mids)])).astype(BF16)
        qa = qa.astype(BF16)
        ka = ka.astype(BF16)
        tiles = [(g, hd) for g in range(n_chunks) for hd in range(N_HEADS)]
        sl = lambda g, hd: (slice(g * BLK, (g + 1) * BLK), slice(hd * HEAD_DIM, (hd + 1) * HEAD_DIM))
        atts = [jnp.where(tri, _dot_nt(qa[sl(g, hd)], ka[sl(g, hd)]), 0.0).astype(BF16) for g, hd in tiles]
        intra = {t: _dot(att, v[sl(*t)]) for t, att in zip(tiles, atts)}
        kvs = {t: _dot_tn(v[sl(*t)], k_dec[sl(*t)]) for t in tiles}
        decays = [jnp.exp(tt) for tt in totals]
        order = range(n_chunks) if dirn == 0 else range(n_chunks - 1, -1, -1)
        for hd in range(N_HEADS):
            hs = slice(hd * HEAD_DIM, (hd + 1) * HEAD_DIM)
            state_t = state_ref[dirn, hd]
            for g in order:
                o_ref[g * BLK:(g + 1) * BLK, hs] = intra[(g, hd)] + _dot_nt(q_dec[sl(g, hd)], state_t.astype(BF16))
                state_t = state_t * decays[g][:, hs] + kvs[(g, hd)]
            state_ref[dirn, hd] = state_t


def _hgrn(qzz, ig, logits, batch, seq, layer):
    t = qzz.shape[0]
    n_chunks = 4
    rows = n_chunks * BLK
    nc = seq // rows
    depth = logits.shape[1]

    def fwd(colblk):
        return pl.BlockSpec((rows, BRANCH), lambda b, c: (b * nc + c, colblk))

    def bwd(colblk):
        return pl.BlockSpec((rows, BRANCH), lambda b, c: (b * nc + nc - 1 - c, colblk))

    return pl.pallas_call(
        functools.partial(_hgrn_kernel, layer=layer, n_chunks=n_chunks),
        out_shape=(jax.ShapeDtypeStruct((t, BRANCH), F32),) * 2,
        grid=(batch, nc),
        in_specs=[pl.BlockSpec((2, depth, BRANCH), lambda b, c: (0, 0, 0)),
                  fwd(0), fwd(1), fwd(0), bwd(0), bwd(2), bwd(0)],
        out_specs=(pl.BlockSpec((rows, BRANCH), lambda b, c: (b * nc + c, 0)),
                   pl.BlockSpec((rows, BRANCH), lambda b, c: (b * nc + nc - 1 - c, 0))),
        scratch_shapes=[pltpu.VMEM((2, N_HEADS, HEAD_DIM, HEAD_DIM), F32)],
        compiler_params=_params(("parallel", "arbitrary")),
        name="hgrn2_scan",
    )(logits, qzz, qzz, ig, qzz, qzz, ig)


def _merge_kernel(x_ref, ao0_ref, ao1_ref, ao2_ref, l0_ref, l1_ref, l2_ref, of_ref, ob_ref, bg_ref,
                  cb_ref, cbp_ref, cbn_ref, h_ref, wgate_ref, convw_ref, ng_ref, wbr_ref, wout_ref, gpost_ref,
                  o_ref, *, tm, seq):
    i = pl.program_id(0)
    h = h_ref[...]
    gates = [_sigmoid(_dot(h, wgate_ref[:, n * D_MODEL:(n + 1) * D_MODEL])) for n in range(3)]

    ls = (l0_ref[...], l1_ref[...], l2_ref[...])
    aos = (ao0_ref, ao1_ref, ao2_ref)
    lmax = jnp.maximum(jnp.maximum(ls[0], ls[1]), ls[2])
    es = [jnp.exp(l - lmax) for l in ls]
    inv = 1.0 / (es[0] + es[1] + es[2])
    a_parts = []
    for hd in range(N_HEADS):
        hs = slice(hd * HEAD_DIM, (hd + 1) * HEAD_DIM)
        acc = None
        for g in range(N_GROUPS):
            w = (es[g] * inv)[:, hd:hd + 1]
            term = w * aos[g][:, hs].astype(F32)
            acc = term if acc is None else acc + term
        a_parts.append(acc)
    o_a = jnp.concatenate(a_parts, axis=1).astype(BF16)

    b_parts = []
    for hd in range(N_HEADS):
        hs = slice(hd * HEAD_DIM, (hd + 1) * HEAD_DIM)
        o = of_ref[:, hs] + ob_ref[:, hs]
        b_parts.append(_rms(o, ng_ref[:, hs]))
    gate = bg_ref[...].astype(F32)
    o_b = (jnp.concatenate(b_parts, axis=1) * (gate * _sigmoid(gate))).astype(BF16)

    cu = cb_ref[:, 0:BRANCH].astype(F32)
    keep_prev = jnp.where((i * tm) % seq == 0, 0.0, 1.0)
    keep_next = jnp.where(((i + 1) * tm) % seq == 0, 0.0, 1.0)
    halo_rows = cbp_ref.shape[0]
    prev_row = keep_prev * cbp_ref[halo_rows - 1:halo_rows, 0:BRANCH].astype(F32)
    next_row = keep_next * cbn_ref[0:1, 0:BRANCH].astype(F32)
    rid = lax.broadcasted_iota(jnp.int32, cu.shape, 0)
    before = jnp.where(rid == 0, prev_row, pltpu.roll(cu, 1, 0))
    after = jnp.where(rid == tm - 1, next_row, pltpu.roll(cu, tm - 1, 0))
    conv = convw_ref[0:1, :] * before + convw_ref[1:2, :] * cu + convw_ref[2:3, :] * after
    o_c = (cb_ref[:, BRANCH:2 * BRANCH].astype(F32) * conv).astype(BF16)

    merged = None
    for n, o_n in enumerate((o_a, o_b, o_c)):
        up = _dot(o_n, wbr_ref[n])
        term = gates[n] * up
        merged = term if merged is None else merged + term
    y = _dot(merged.astype(BF16), wout_ref[...])
    o_ref[...] = x_ref[...] + _rms(y, gpost_ref[...])


def _merge(x, aos, lses, o_f, o_b, ig, cb, h, w_gate, conv_w, norm_g, w_branch, w_out, g_post, seq, tm):
    t = x.shape[0]
    halo = 16
    n_halo = t // halo
    per = tm // halo

    def rows(width, colblk=0):
        return pl.BlockSpec((tm, width), lambda i: (i, colblk))

    def full(shape):
        return _resident(shape, lambda i: (0,) * len(shape))

    in_specs = ([rows(D_MODEL)] + [rows(BRANCH)] * 3 + [rows(HEAD_DIM)] * 3 + [rows(BRANCH)] * 2
                + [rows(BRANCH, 1), rows(2 * BRANCH),
                   pl.BlockSpec((halo, 2 * BRANCH), lambda i: (jnp.maximum(i * per - 1, 0), 0)),
                   pl.BlockSpec((halo, 2 * BRANCH), lambda i: (jnp.minimum((i + 1) * per, n_halo - 1), 0)),
                   rows(D_MODEL), full((D_MODEL, G_COLS)), full((3, BRANCH)), full((1, BRANCH)),
                   full((3, BRANCH, D_MODEL)), full((D_MODEL, D_MODEL)), full((1, D_MODEL))])
    return pl.pallas_call(
        functools.partial(_merge_kernel, tm=tm, seq=seq),
        out_shape=jax.ShapeDtypeStruct((t, D_MODEL), F32),
        grid=(t // tm,),
        in_specs=in_specs,
        out_specs=rows(D_MODEL),
        compiler_params=_params(("parallel",)),
        name="merge_branches",
    )(x, *aos, *lses, o_f, o_b, ig, cb, cb, cb, h, w_gate, conv_w, norm_g, w_branch, w_out, g_post)


def _swiglu(h, wg, wu, wd):
    g = _dot(h, wg)
    u = _dot(h, wu)
    return _dot((g * _sigmoid(g) * u).astype(BF16), wd)


def _embed_tail(x, proj, gin_ref, gpost_ref, wgate_ref):
    gate = _sigmoid(_dot(_rms(x, gin_ref[...]).astype(BF16), wgate_ref[...]))
    return x + _rms(proj * gate, gpost_ref[...])


def _embed_operands(embed, p_row0, tm, index):
    pd = embed[0].shape[1]
    const = lambda *_: (0, 0)
    p_index = lambda *a: (index(*a)[0] + p_row0 // tm, 0)
    return [pl.BlockSpec((tm, pd), p_index), pl.BlockSpec((1, D_MODEL), const), pl.BlockSpec((1, D_MODEL), const),
            _resident((D_MODEL, D_MODEL), const), _resident((pd, D_MODEL), const)]


def _ffn_kernel(x_ref, gpre_ref, gpost_ref, wg_ref, wu_ref, wd_ref, p_ref, ein_ref, epost_ref, egate_ref, eproj_ref,
                o_ref):
    proj = _dot(p_ref[...].astype(BF16), eproj_ref[...])
    x = x_ref[...]
    h = _rms(x, gpre_ref[...]).astype(BF16)
    mixed = x + _rms(_swiglu(h, wg_ref[...], wu_ref[...], wd_ref[...]), gpost_ref[...])
    o_ref[...] = _embed_tail(mixed, proj, ein_ref, epost_ref, egate_ref)


def _ffn(x, g_pre, g_post, wg, wu, wd, embed, p_row0, tm):
    t = x.shape[0]
    const = lambda i: (0, 0)
    return pl.pallas_call(
        _ffn_kernel,
        out_shape=jax.ShapeDtypeStruct((t, D_MODEL), F32),
        grid=(t // tm,),
        in_specs=[pl.BlockSpec((tm, D_MODEL), lambda i: (i, 0)),
                  pl.BlockSpec((1, D_MODEL), const), pl.BlockSpec((1, D_MODEL), const),
                  _resident(wg.shape, const), _resident(wu.shape, const), _resident(wd.shape, const)]
                 + _embed_operands(embed, p_row0, tm, lambda i: (i, 0)),
        out_specs=pl.BlockSpec((tm, D_MODEL), lambda i: (i, 0)),
        compiler_params=_params(("parallel",)),
        name="dense_swiglu",
    )(x, g_pre, g_post, wg, wu, wd, *embed)


PIECES = D_MODEL // 2 // HEAD_DIM
SC_WINDOW = 128
EXPERT_TILE = 512


def _pack_rows(vals):
    bits = lax.bitcast_convert_type(vals.astype(BF16).astype(F32), jnp.uint32)
    half = D_MODEL // 2
    word = bits[:, :half] | (bits[:, half:] >> 16)
    return [lax.bitcast_convert_type(word[:, c * HEAD_DIM:(c + 1) * HEAD_DIM], jnp.int32) for c in range(PIECES)]


def _unpack_rows(piece_refs):
    words = [lax.bitcast_convert_type(r[...], jnp.uint32) for r in piece_refs]
    hi = [lax.bitcast_convert_type(w & jnp.uint32(0xFFFF0000), F32) for w in words]
    lo = [lax.bitcast_convert_type(w << 16, F32) for w in words]
    return jnp.concatenate(hi + lo, axis=1)


def _route_kernel(x_ref, gpre_ref, rhi_ref, rlo_ref, *refs, n_experts):
    piece_refs = refs[:PIECES]
    rec_ref, rank_ref, count_ref, carry_ref = refs[PIECES:]
    i = pl.program_id(0)
    tm = x_ref.shape[0]

    @pl.when(i == 0)
    def _():
        carry_ref[...] = jnp.zeros_like(carry_ref)

    hf = _rms(x_ref[...], gpre_ref[...])
    for r, piece in zip(piece_refs, _pack_rows(hf)):
        r[...] = piece
    h_hi = hf.astype(BF16)
    h_lo = (hf - h_hi.astype(F32)).astype(BF16)
    logits = _dot(h_hi, rhi_ref[...]) + _dot(h_hi, rlo_ref[...]) + _dot(h_lo, rhi_ref[...])
    lane = lax.broadcasted_iota(jnp.int32, logits.shape, 1).astype(F32)
    logits = jnp.where(lane < n_experts, logits, NEG_INF)
    m1 = jnp.max(logits, axis=1, keepdims=True)
    i1 = jnp.min(jnp.where(logits == m1, lane, 1e9), axis=1, keepdims=True)
    rest = jnp.where(lane == i1, NEG_INF, logits)
    m2 = jnp.max(rest, axis=1, keepdims=True)
    i2 = jnp.min(jnp.where(rest == m2, lane, 1e9), axis=1, keepdims=True)
    ex = jnp.exp(m2 - m1)
    w1 = 1.0 / (1.0 + ex)
    rec = jnp.where(lane == 0, i1, jnp.where(lane == 1, i2, jnp.where(lane == 2, w1, jnp.where(lane == 3, ex * w1, 0.0))))
    rec_ref[...] = rec
    rec_t = rec.T
    e1 = rec_t[0:1, :]
    e2 = rec_t[1:2, :]
    sub = lax.broadcasted_iota(jnp.int32, (8, tm), 0).astype(F32)
    oh1 = jnp.where(sub == e1, 1.0, 0.0)
    oh2 = jnp.where(sub == e2, 1.0, 0.0)
    chosen = oh1 + oh2
    src = lax.broadcasted_iota(jnp.int32, (tm, tm), 0)
    dst = lax.broadcasted_iota(jnp.int32, (tm, tm), 1)
    before = jnp.where(src < dst, 1.0, 0.0).astype(BF16)
    rank = _dot(chosen.astype(BF16), before) + carry_ref[:, 0:1]
    rank1 = jnp.sum(oh1 * rank, axis=0, keepdims=True)
    rank2 = jnp.sum(oh2 * rank, axis=0, keepdims=True)
    rank_ref[...] = jnp.where(sub == 0, rank1, jnp.where(sub == 1, rank2, jnp.where(sub == 2, e1, jnp.where(sub == 3, e2, 0.0))))
    carry_ref[...] = carry_ref[...] + jnp.sum(chosen, axis=1, keepdims=True)
    count_ref[...] = carry_ref[...]


def _route(x, g_pre, r_hi, r_lo, n_experts, tm):
    t = x.shape[0]
    assert n_experts <= 8
    piece = pl.BlockSpec((tm, HEAD_DIM), lambda i: (i, 0))
    return pl.pallas_call(
        functools.partial(_route_kernel, n_experts=n_experts),
        out_shape=(*[jax.ShapeDtypeStruct((t, HEAD_DIM), jnp.int32)] * PIECES,
                   jax.ShapeDtypeStruct((t, HEAD_DIM), F32), jax.ShapeDtypeStruct((8, t), F32),
                   jax.ShapeDtypeStruct((8, HEAD_DIM), F32)),
        grid=(t // tm,),
        in_specs=[pl.BlockSpec((tm, D_MODEL), lambda i: (i, 0)), pl.BlockSpec((1, D_MODEL), lambda i: (0, 0)),
                  pl.BlockSpec((D_MODEL, HEAD_DIM), lambda i: (0, 0)), pl.BlockSpec((D_MODEL, HEAD_DIM), lambda i: (0, 0))],
        out_specs=(*[piece] * PIECES, piece, pl.BlockSpec((8, tm), lambda i: (0, i)),
                   pl.BlockSpec((8, HEAD_DIM), lambda i: (0, 0))),
        scratch_shapes=[pltpu.VMEM((8, HEAD_DIM), F32)],
        compiler_params=_params(("arbitrary",)),
        name="moe_route",
    )(x, g_pre, r_hi, r_lo)


def _slot_kernel(rank_ref, start_ref, s1_ref, s2_ref):
    tm = rank_ref.shape[1]
    sub = lax.broadcasted_iota(jnp.int32, (8, tm), 0).astype(F32)
    start = start_ref[:, 0:1]
    rows = rank_ref[...]
    for choice, out in ((0, s1_ref), (1, s2_ref)):
        base = jnp.sum(jnp.where(sub == rows[2 + choice:3 + choice, :], start, 0.0), axis=0, keepdims=True)
        out[...] = (base + rows[choice:choice + 1, :]).astype(jnp.int32)


def _slots(ranks, starts, tm):
    t = ranks.shape[1]
    row = pl.BlockSpec((1, tm), lambda i: (0, i))
    return pl.pallas_call(
        _slot_kernel,
        out_shape=(jax.ShapeDtypeStruct((1, t), jnp.int32),) * 2,
        grid=(t // tm,),
        in_specs=[pl.BlockSpec((8, tm), lambda i: (0, i)), pl.BlockSpec((8, HEAD_DIM), lambda i: (0, 0))],
        out_specs=(row, row),
        compiler_params=_params(("parallel",)),
        name="moe_slots",
    )(ranks, starts)


def _sc_mesh():
    return plsc.VectorSubcoreMesh(core_axis_name="core", subcore_axis_name="subcore")


def _sc_scatter_rows(srcs, idxs, n_rows):
    ns, nk = len(srcs), len(idxs)
    half = srcs[0].shape[0] // SC_WINDOW // 2

    @functools.partial(pl.kernel, mesh=_sc_mesh(), scratch_types=[],
                       out_type=tuple(jax.ShapeDtypeStruct((n_rows, HEAD_DIM), srcs[0].dtype) for _ in range(ns)))
    def scatter(*refs):
        out_hbm = refs[ns + nk:]

        def body(*blocks):
            for idx in blocks[ns:]:
                for c in range(ns):
                    pltpu.sync_copy(blocks[c], out_hbm[c].at[idx.at[0]])

        pltpu.emit_pipeline(
            body, grid=(2, half),
            in_specs=[pl.BlockSpec((SC_WINDOW, HEAD_DIM), lambda i, j: (i * half + j, 0)) for _ in range(ns)]
                     + [pl.BlockSpec((1, SC_WINDOW), lambda i, j: (0, i * half + j)) for _ in range(nk)],
            out_specs=[],
            core_axis_name=("core", "subcore"),
            dimension_semantics=(pltpu.PARALLEL, pltpu.PARALLEL),
        )(*refs[:ns + nk])

    return scatter(*srcs, *idxs)


def _sc_gather_rows(tables, idx):
    nt = len(tables)
    n = idx.shape[1]
    half = n // SC_WINDOW // 2

    @functools.partial(pl.kernel, mesh=_sc_mesh(), scratch_types=[],
                       out_type=tuple(jax.ShapeDtypeStruct((n, HEAD_DIM), tables[0].dtype) for _ in range(nt)))
    def gather(*refs):
        table_hbm = refs[:nt]

        def body(idx_blk, *out_blks):
            for c in range(nt):
                pltpu.sync_copy(table_hbm[c].at[idx_blk.at[0]], out_blks[c])

        pltpu.emit_pipeline(
            body, grid=(2, half),
            in_specs=[pl.BlockSpec((1, SC_WINDOW), lambda i, j: (0, i * half + j))],
            out_specs=[pl.BlockSpec((SC_WINDOW, HEAD_DIM), lambda i, j: (i * half + j, 0)) for _ in range(nt)],
            core_axis_name=("core", "subcore"),
            dimension_semantics=(pltpu.PARALLEL, pltpu.PARALLEL),
        )(refs[nt], *refs[nt + 1:])

    return gather(*tables, idx)


def _expert_kernel(tile_expert_ref, n_used_ref, *refs):
    x_refs = refs[:PIECES]
    wg_ref, wu_ref, wd_ref = refs[PIECES:PIECES + 3]
    y_refs = refs[PIECES + 3:]

    @pl.when(pl.program_id(0) < n_used_ref[0])
    def _():
        h = _unpack_rows(x_refs).astype(BF16)
        y = _swiglu(h, wg_ref[0], wu_ref[0], wd_ref[0])
        for r, piece in zip(y_refs, _pack_rows(y)):
            r[...] = piece


def _expert_ffn(xs, tile_expert, n_used, wg, wu, wd):
    n_rows = xs[0].shape[0]
    rows = pl.BlockSpec((EXPERT_TILE, HEAD_DIM), lambda i, te, nu: (jnp.minimum(i, nu[0] - 1), 0))
    out_rows = pl.BlockSpec((EXPERT_TILE, HEAD_DIM), lambda i, te, nu: (i, 0))
    expert = lambda i, te, nu: (te[i], 0, 0)
    grid_spec = pltpu.PrefetchScalarGridSpec(
        num_scalar_prefetch=2,
        grid=(n_rows // EXPERT_TILE,),
        in_specs=[rows] * PIECES + [_resident((1,) + wg.shape[1:], expert), _resident((1,) + wu.shape[1:], expert),
                                    _resident((1,) + wd.shape[1:], expert)],
        out_specs=[out_rows] * PIECES,
    )
    return pl.pallas_call(
        _expert_kernel,
        out_shape=[jax.ShapeDtypeStruct((n_rows, HEAD_DIM), jnp.int32)] * PIECES,
        grid_spec=grid_spec,
        compiler_params=_params(("arbitrary",)),
        name="moe_expert_swiglu",
    )(tile_expert, n_used, *xs, wg, wu, wd)


def _combine_kernel(x_ref, rec_ref, gpost_ref, p_ref, ein_ref, epost_ref, egate_ref, eproj_ref, *refs):
    proj = _dot(p_ref[...].astype(BF16), eproj_ref[...])
    y1 = _unpack_rows(refs[:PIECES])
    y2 = _unpack_rows(refs[PIECES:2 * PIECES])
    o_ref = refs[2 * PIECES]
    rec = rec_ref[...]
    y = rec[:, 2:3] * y1 + rec[:, 3:4] * y2
    mixed = x_ref[...] + _rms(y, gpost_ref[...])
    o_ref[...] = _embed_tail(mixed, proj, ein_ref, epost_ref, egate_ref)


def _combine(x, rec, g_post, embed, p_row0, y1, y2, tm):
    t = x.shape[0]
    piece = pl.BlockSpec((tm, HEAD_DIM), lambda i: (i, 0))
    return pl.pallas_call(
        _combine_kernel,
        out_shape=jax.ShapeDtypeStruct((t, D_MODEL), F32),
        grid=(t // tm,),
        in_specs=[pl.BlockSpec((tm, D_MODEL), lambda i: (i, 0)), piece, pl.BlockSpec((1, D_MODEL), lambda i: (0, 0))]
                 + _embed_operands(embed, p_row0, tm, lambda i: (i, 0)) + [piece] * (2 * PIECES),
        out_specs=pl.BlockSpec((tm, D_MODEL), lambda i: (i, 0)),
        compiler_params=_params(("parallel",)),
        name="moe_combine",
    )(x, rec, g_post, *embed, *y1, *y2)


def _moe(x, g_pre, g_post, router, wg, wu, wd, embed, p_row0, tm):
    t = x.shape[0]
    n_experts = router.shape[1]
    router = jnp.pad(router, ((0, 0), (0, HEAD_DIM - n_experts)))
    r_hi = router.astype(BF16)
    r_lo = (router - r_hi.astype(F32)).astype(BF16)
    *h_pieces, rec, ranks, counts = _route(x, g_pre, r_hi, r_lo, n_experts, tm)

    counts = counts[:n_experts, 0].astype(jnp.int32)
    tiles = (counts + EXPERT_TILE - 1) // EXPERT_TILE
    tile_end = jnp.cumsum(tiles)
    starts = ((tile_end - tiles) * EXPERT_TILE).astype(F32)
    starts = jnp.broadcast_to(jnp.pad(starts, (0, 8 - n_experts))[:, None], (8, HEAD_DIM))
    n_rows = TOP_K * t + n_experts * EXPERT_TILE
    n_tiles = n_rows // EXPERT_TILE
    n_used = tile_end[-1:]
    tile_ids = jnp.minimum(jnp.arange(n_tiles, dtype=jnp.int32), n_used[0] - 1)
    tile_expert = jnp.sum(tile_ids[:, None] >= tile_end[None, :], axis=1).astype(jnp.int32)

    slot1, slot2 = _slots(ranks, starts, min(t, 8192))
    xs = []
    for c in range(0, PIECES, 2):
        xs += _sc_scatter_rows(h_pieces[c:c + 2], [slot1, slot2], n_rows)
    ys = _expert_ffn(xs, tile_expert, n_used, wg, wu, wd)
    gathered = []
    for slot in (slot1, slot2):
        rows = []
        for c in range(0, PIECES, 2):
            rows += _sc_gather_rows(ys[c:c + 2], slot)
        gathered.append(rows)
    return _combine(x, rec, g_post, embed, p_row0, gathered[0], gathered[1], tm)


def kernel(x, p, positions, w_in, conv_w, hgrn_lb_logits, hgrn_norm_g, w_branch, w_out, g_mix_pre, g_mix_post, g_ffn_pre, g_ffn_post, dense_w_gate, dense_w_up, dense_w_down, moe_router, moe_w_gate, moe_w_up, moe_w_down, ple_w_proj, ple_w_gate, ple_g_in, ple_g_post):
    batch, seq, _ = x.shape
    depth = w_in.shape[0]
    t = batch * seq
    tm = min(512, seq)
    tm_merge = min(512, seq)
    tm_ffn = min(512, seq)

    xt = x.reshape(t, D_MODEL)
    cos, sin = _rope_tables(positions.astype(F32).reshape(t // ROPE_PACK, ROPE_PACK), tm)
    row = lambda a: a.reshape(1, -1)
    in_edges = (0, A_COLS, A_COLS + 3 * BRANCH, A_COLS + B_COLS, A_COLS + B_COLS + C_COLS, A_COLS + B_COLS + C_COLS + G_COLS)
    in_bounds = tuple(zip(in_edges[:-1], in_edges[1:]))
    w_in_rows = w_in.reshape(depth * D_MODEL, -1)
    p_rows = p.reshape(depth * t, -1)

    for i in range(depth):
        w_attn, w_qzz, w_ig, w_conv, w_gates = _split_to_bf16(w_in_rows, i * D_MODEL, D_MODEL, in_bounds)
        g_pre = row(g_mix_pre[i])
        *qkvs, h = _inproj_attention(xt, g_pre, w_attn, cos, sin, batch, seq, tm)
        aos, lses, sides = zip(*[_attention_group(qkvs[g], h, w_side, g == 2, batch, seq, g)
                                 for g, w_side in enumerate((w_qzz, w_ig, w_conv))])
        qzz, ig, cb = sides
        o_f, o_b = _hgrn(qzz, ig, hgrn_lb_logits, batch, seq, i)
        xt = _merge(xt, aos, lses, o_f, o_b, ig, cb, h, w_gates, conv_w[i],
                    row(hgrn_norm_g[i]), w_branch[i].astype(BF16), w_out[i].astype(BF16), row(g_mix_post[i]),
                    seq, tm_merge)

        embed = (p_rows, row(ple_g_in[i]), row(ple_g_post[i]),
                 ple_w_gate[i].astype(BF16), ple_w_proj[i].astype(BF16))
        j = i // 2
        if i % 2 == 0:
            xt = _ffn(xt, row(g_ffn_pre[i]), row(g_ffn_post[i]), dense_w_gate[j].astype(BF16),
                      dense_w_up[j].astype(BF16), dense_w_down[j].astype(BF16), embed, i * t, tm_ffn)
        else:
            xt = _moe(xt, row(g_ffn_pre[i]), row(g_ffn_post[i]), moe_router[j], _to_bf16(moe_w_gate[j]),
                      _to_bf16(moe_w_up[j]), _to_bf16(moe_w_down[j]), embed, i * t, tm)

    return xt.reshape(batch, seq, D_MODEL)
```

```python
import functools
import math

import jax
import jax.numpy as jnp
from jax import lax
from jax.experimental import pallas as pl
from jax.experimental.pallas import tpu as pltpu
from jax.experimental.pallas import tpu_sc as plsc

D_MODEL = 1024
EPS = 1e-6
NEG_INF = -1e30

HEAD_DIM = 128
N_HEADS = 4
BRANCH = N_HEADS * HEAD_DIM
DIL_GROUPS = ((128, 1), (512, 4), (2048, 16))
N_GROUPS = len(DIL_GROUPS)
BLK = 64
ROT_DIM = HEAD_DIM // 4
ROPE_THETA = 500000.0
TOP_K = 2

A_COLS = N_GROUPS * 3 * BRANCH
B_COLS = 5 * BRANCH
C_COLS = 3 * BRANCH
G_COLS = 3 * D_MODEL

BF16 = jnp.bfloat16
F32 = jnp.float32

TOKEN_TILE = 512
VMEM_LIMIT = 56 * 1024 * 1024
ATTENTION_VMEM_LIMIT = 62 * 1024 * 1024


def _params(sem, vmem_limit=VMEM_LIMIT):
    return pltpu.CompilerParams(dimension_semantics=sem, vmem_limit_bytes=vmem_limit)


def _cast_kernel(w_ref, o_ref):
    o_ref[...] = w_ref[...].astype(o_ref.dtype)


def _to_bf16(w):
    cols = w.shape[-1]
    flat = w.reshape(-1, cols)
    rows = flat.shape[0]
    block = rows
    while block % 32 == 0 and block * cols * 4 > 4 * 1024 * 1024:
        block //= 2
    out = pl.pallas_call(
        _cast_kernel,
        out_shape=jax.ShapeDtypeStruct(flat.shape, BF16),
        grid=(rows // block,),
        in_specs=[pl.BlockSpec((block, cols), lambda i: (i, 0))],
        out_specs=pl.BlockSpec((block, cols), lambda i: (i, 0)),
        compiler_params=_params(("parallel",)),
        name="weights_to_bf16",
    )(flat)
    return out.reshape(w.shape)


def _split_cast_kernel(w_ref, *o_refs, bounds):
    for o_ref, (lo, hi) in zip(o_refs, bounds):
        o_ref[...] = w_ref[:, lo:hi].astype(o_ref.dtype)


def _split_to_bf16(w, first_row, n, bounds, rows=128):
    cols = w.shape[1]
    first = first_row // rows
    return pl.pallas_call(
        functools.partial(_split_cast_kernel, bounds=bounds),
        out_shape=tuple(jax.ShapeDtypeStruct((n, hi - lo), BF16) for lo, hi in bounds),
        grid=(n // rows,),
        in_specs=[pl.BlockSpec((rows, cols), lambda i: (first + i, 0))],
        out_specs=tuple(pl.BlockSpec((rows, hi - lo), lambda i: (i, 0)) for lo, hi in bounds),
        compiler_params=_params(("parallel",)),
        name="split_weights_to_bf16",
    )(w)


def _resident(shape, index_map):
    return pl.BlockSpec(shape, index_map, pipeline_mode=pl.Buffered(1))


def _rms(xf, g):
    return xf * lax.rsqrt(jnp.mean(xf * xf, axis=-1, keepdims=True) + EPS) * g


def _sigmoid(z):
    return 0.5 * jnp.tanh(0.5 * z) + 0.5


def _sigmoid_rel(z):
    return 1.0 / (1.0 + jnp.exp(-z))


def _dot(a, b):
    return jnp.dot(a, b, preferred_element_type=F32)


def _dot_nt(a, b):
    return lax.dot_general(a, b, (((1,), (1,)), ((), ())), preferred_element_type=F32)


def _dot_tn(a, b):
    return lax.dot_general(a, b, (((0,), (0,)), ((), ())), preferred_element_type=F32)


ROPE_PACK = HEAD_DIM // ROT_DIM


def _rope_table_kernel(pos_ref, freq_ref, sign_ref, cos_ref, sin_ref):
    pos = pos_ref[...]
    group = lax.broadcasted_iota(jnp.int32, cos_ref.shape, 1) // ROT_DIM
    pos_lanes = jnp.zeros(cos_ref.shape, F32)
    for k in range(ROPE_PACK):
        pos_lanes = jnp.where(group == k, pos[:, k:k + 1], pos_lanes)
    ang = pos_lanes * freq_ref[...]
    cos_ref[...] = jnp.cos(ang)
    sin_ref[...] = jnp.sin(ang) * sign_ref[...]


def _rope_tables(pos_packed, rows):
    n = pos_packed.shape[0]
    half = ROT_DIM // 2
    inv_freq = ROPE_THETA ** (-jnp.arange(0, ROT_DIM, 2, dtype=F32) / ROT_DIM)
    freq = jnp.tile(jnp.concatenate([inv_freq, inv_freq]), ROPE_PACK)[None, :]
    sign = jnp.tile(jnp.concatenate([-jnp.ones((half,), F32), jnp.ones((half,), F32)]), ROPE_PACK)[None, :]
    row = pl.BlockSpec((1, HEAD_DIM), lambda i: (0, 0))
    tab = pl.BlockSpec((rows, HEAD_DIM), lambda i: (i, 0))
    return pl.pallas_call(
        _rope_table_kernel,
        out_shape=(jax.ShapeDtypeStruct((n, HEAD_DIM), F32),) * 2,
        grid=(n // rows,),
        in_specs=[pl.BlockSpec((rows, ROPE_PACK), lambda i: (i, 0)), row, row],
        out_specs=(tab, tab),
        compiler_params=_params(("parallel",)),
        name="rope_tables",
    )(pos_packed, freq, sign)


def _unpack_rope_tables(cosp_ref, sinp_ref, tab_ref):
    rows = cosp_ref.shape[0]
    lane = lax.broadcasted_iota(jnp.int32, (rows, HEAD_DIM), 1)
    for k in range(ROPE_PACK):
        for idx, (ref, fill) in enumerate(((cosp_ref, 1.0), (sinp_ref, 0.0))):
            packed = ref[...]
            shifted = pltpu.roll(packed, HEAD_DIM - ROT_DIM * k, 1) if k else packed
            tab_ref[idx, pl.ds(k, rows, stride=ROPE_PACK), :] = jnp.where(lane < ROT_DIM, shifted, fill)


def _rope_head(th, cos, sin):
    half = ROT_DIM // 2
    lane = lax.broadcasted_iota(jnp.int32, th.shape, 1)
    swapped = jnp.where(lane < half, pltpu.roll(th, HEAD_DIM - half, 1), pltpu.roll(th, half, 1))
    return th * cos + swapped * sin


def _inproj_attn_kernel(x_ref, g_ref, w_ref, cosp_ref, sinp_ref, o0_ref, o1_ref, o2_ref, h_ref, stage_ref, tab_ref):
    _unpack_rope_tables(cosp_ref, sinp_ref, tab_ref)
    hf = _rms(x_ref[...], g_ref[...])
    h_ref[...] = hf.astype(BF16)
    n_pieces = D_MODEL // HEAD_DIM
    for pc in range(n_pieces):
        stage_ref[pc] = hf[:, pc * HEAD_DIM:(pc + 1) * HEAD_DIM]
    scale = 1.0 / math.sqrt(HEAD_DIM)
    tm = x_ref.shape[0]
    for grp, o_ref in enumerate((o0_ref, o1_ref, o2_ref)):
        dil = DIL_GROUPS[grp][1]
        sub = tm // dil
        if dil == 1:
            h, cos, sin = hf.astype(BF16), tab_ref[0], tab_ref[1]
        else:
            residue = lambda ref, *lead: jnp.concatenate(
                [ref[(*lead, pl.ds(r, sub, stride=dil), slice(None))] for r in range(dil)], axis=0)
            h = jnp.concatenate([residue(stage_ref, pc) for pc in range(n_pieces)], axis=1).astype(BF16)
            cos, sin = residue(tab_ref, 0), residue(tab_ref, 1)
        for part in range(3):
            c = grp * 3 + part
            t = _dot(h, w_ref[:, c * BRANCH:(c + 1) * BRANCH])
            if part < 2:
                heads = []
                for hd in range(N_HEADS):
                    r = _rope_head(t[:, hd * HEAD_DIM:(hd + 1) * HEAD_DIM], cos, sin)
                    heads.append(r * scale if part == 0 else r)
                t = jnp.concatenate(heads, axis=1)
            t = t.astype(BF16)
            for r in range(dil):
                o_ref[0, r, :, part * BRANCH:(part + 1) * BRANCH] = t[r * sub:(r + 1) * sub, :]


def _inproj_attention(x, g, w, cos, sin, batch, seq, tm):
    per_seq = seq // tm
    rows = lambda width: pl.BlockSpec((tm, width), lambda b, j: (b * per_seq + j, 0))
    packed = pl.BlockSpec((tm // ROPE_PACK, HEAD_DIM), lambda b, j: (b * per_seq + j, 0))
    out_shape = tuple(jax.ShapeDtypeStruct((batch, dil, seq // dil, 3 * BRANCH), BF16) for _, dil in DIL_GROUPS)
    out_shape += (jax.ShapeDtypeStruct((batch * seq, D_MODEL), BF16),)
    out_specs = tuple(pl.BlockSpec((1, dil, tm // dil, 3 * BRANCH), lambda b, j: (b, 0, j, 0))
                      for _, dil in DIL_GROUPS) + (rows(D_MODEL),)
    return pl.pallas_call(
        _inproj_attn_kernel, out_shape=out_shape, grid=(batch, per_seq),
        in_specs=[rows(D_MODEL), pl.BlockSpec((1, D_MODEL), lambda b, j: (0, 0)),
                  _resident((D_MODEL, A_COLS), lambda b, j: (0, 0)), packed, packed],
        out_specs=out_specs,
        scratch_shapes=[pltpu.VMEM((D_MODEL // HEAD_DIM, tm, HEAD_DIM), F32), pltpu.VMEM((2, tm, HEAD_DIM), F32)],
        compiler_params=_params(("parallel", "parallel")), name="inproj_attention",
    )(x, g, w, cos, sin)


def _attn_kernel(q_ref, k_ref, kp_ref, kn_ref, v_ref, vp_ref, vn_ref, h_ref, w_ref, o_ref, lse_ref, side_ref,
                 o_stage, lse_stage, *, dil, nj, n_steps, conv_side):
    n = pl.program_id(1)
    part_rows = h_ref.shape[0] // N_HEADS

    def side_projection(part):
        rs = slice(part * part_rows, (part + 1) * part_rows)
        h = h_ref[rs, :]
        if conv_side:
            u = _dot(h, w_ref[:, 0:BRANCH])
            b_gate = _dot(h, w_ref[:, BRANCH:2 * BRANCH])
            c_gate = _dot(h, w_ref[:, 2 * BRANCH:3 * BRANCH])
            side_ref[rs, 0:BRANCH] = (c_gate * u).astype(BF16)
            side_ref[rs, BRANCH:2 * BRANCH] = b_gate.astype(BF16)
        else:
            for c in range(w_ref.shape[1] // BRANCH):
                cols = slice(c * BRANCH, (c + 1) * BRANCH)
                side_ref[rs, cols] = _dot(h, w_ref[:, cols]).astype(BF16)

    qb = 2 if nj % 2 == 0 else 1
    q_rows, w_rows, n_tiles = qb * BLK, (qb + 2) * BLK, nj // qb
    col_lo = jnp.where(n > 0, 0, BLK)
    col_hi = jnp.where(n < n_steps - 1, w_rows, w_rows - BLK)
    row = lax.broadcasted_iota(jnp.int32, (q_rows, w_rows), 0)
    col = lax.broadcasted_iota(jnp.int32, (q_rows, w_rows), 1)
    band = jnp.abs(col - BLK - row) <= BLK
    masks = {}
    for j in range(n_tiles):
        m = band
        if j == 0:
            m = m & (col >= col_lo)
        if j == n_tiles - 1:
            m = m & (col < col_hi)
        masks[j] = m
    pairs = [(r, j) for r in range(dil) for j in range(n_tiles)]
    lane = lax.broadcasted_iota(jnp.int32, (len(pairs), q_rows, HEAD_DIM), 2)

    def natural_rows(r, j):
        return pl.ds(j * q_rows * dil + r, q_rows, stride=dil)

    def window(own_ref, prev_ref, next_ref, r, j, hs):
        parts = []
        for w in range(j * qb - 1, j * qb + qb + 1):
            if w < 0:
                parts.append(prev_ref[0, r, :, hs])
            elif w >= nj:
                parts.append(next_ref[0, r, :, hs])
            else:
                parts.append(own_ref[0, r, w * BLK:(w + 1) * BLK, hs])
        return jnp.concatenate(parts, axis=0)

    lse_tiles = jnp.zeros((len(pairs), q_rows, HEAD_DIM), F32)
    for hd in range(N_HEADS):
        side_projection(hd)
        hs = slice(hd * HEAD_DIM, (hd + 1) * HEAD_DIM)
        scores = [_dot_nt(q_ref[0, r, j * q_rows:(j + 1) * q_rows, hs], window(k_ref, kp_ref, kn_ref, r, j, hs))
                  for r, j in pairs]
        s = jnp.stack([jnp.where(masks[j], sc, NEG_INF) for (r, j), sc in zip(pairs, scores)])
        m = jnp.max(s, axis=2, keepdims=True)
        p = jnp.exp(s - m)
        l = jnp.sum(p, axis=2, keepdims=True)
        pb = p.astype(BF16)
        outs = jnp.stack([_dot(pb[i], window(v_ref, vp_ref, vn_ref, r, j, hs)) for i, (r, j) in enumerate(pairs)])
        outs = outs * (1.0 / l)
        lse_tiles = jnp.where(lane == hd, m + jnp.log(l), lse_tiles)
        for i, (r, j) in enumerate(pairs):
            o_stage[hd, natural_rows(r, j), :] = outs[i]
        o_ref[:, hs] = o_stage[hd].astype(o_ref.dtype)
    for i, (r, j) in enumerate(pairs):
        lse_stage[natural_rows(r, j), :] = lse_tiles[i]
    lse_ref[...] = lse_stage[...]


def _attention_group(qkv, h, w_side, conv_side, batch, seq, group):
    _, dil = DIL_GROUPS[group]
    t = batch * seq
    sub_len = seq // dil
    rows_per_step = min(seq, 32 * BLK)
    nj = rows_per_step // (dil * BLK)
    n_steps = seq // rows_per_step
    n_blk = sub_len // BLK
    sub_rows = rows_per_step // dil

    def own(part):
        return pl.BlockSpec((1, dil, sub_rows, BRANCH), lambda b, n: (b, 0, n, part))

    def prev(part):
        return pl.BlockSpec((1, dil, BLK, BRANCH), lambda b, n: (b, 0, jnp.maximum(n * nj - 1, 0), part))

    def nxt(part):
        return pl.BlockSpec((1, dil, BLK, BRANCH), lambda b, n: (b, 0, jnp.minimum((n + 1) * nj, n_blk - 1), part))

    side_cols = 2 * BRANCH if conv_side else w_side.shape[1]
    step_rows = lambda width: pl.BlockSpec((rows_per_step, width), lambda b, n: (b * n_steps + n, 0))
    return pl.pallas_call(
        functools.partial(_attn_kernel, dil=dil, nj=nj, n_steps=n_steps, conv_side=conv_side),
        out_shape=(jax.ShapeDtypeStruct((t, BRANCH), BF16), jax.ShapeDtypeStruct((t, HEAD_DIM), F32),
                   jax.ShapeDtypeStruct((t, side_cols), BF16)),
        grid=(batch, n_steps),
        in_specs=[own(0), own(1), prev(1), nxt(1), own(2), prev(2), nxt(2), step_rows(D_MODEL),
                  _resident(w_side.shape, lambda b, n: (0, 0))],
        out_specs=(step_rows(BRANCH), step_rows(HEAD_DIM), step_rows(side_cols)),
        scratch_shapes=[pltpu.VMEM((N_HEADS, rows_per_step, HEAD_DIM), F32), pltpu.VMEM((rows_per_step, HEAD_DIM), F32)],
        compiler_params=_params(("parallel", "parallel"), ATTENTION_VMEM_LIMIT),
        name=f"dilated_attention_g{group}",
    )(qkv, qkv, qkv, qkv, qkv, qkv, qkv, h, w_side)


def _hgrn_kernel(logit_ref, qf_ref, zf_ref, vf_ref, qb_ref, zb_ref, vb_ref, of_ref, ob_ref, state_ref,
                 *, layer, n_chunks):
    c = pl.program_id(1)

    @pl.when(c == 0)
    def _():
        state_ref[...] = jnp.zeros_like(state_ref)

    depth = logit_ref.shape[1]
    lbs = []
    for dirn in range(2):
        rows = [logit_ref[dirn, j:j + 1, :] for j in range(depth)]
        top = functools.reduce(jnp.maximum, rows)
        exps = [jnp.exp(rw - top) for rw in rows]
        lbs.append(sum(exps[1:layer + 1], jnp.zeros_like(top)) / sum(exps[1:], exps[0]))

    r = lax.broadcasted_iota(jnp.int32, (BLK, BLK), 0)
    cc = lax.broadcasted_iota(jnp.int32, (BLK, BLK), 1)

    def bcast_rows(per_chunk):
        return jnp.concatenate([jnp.broadcast_to(v, (BLK, BRANCH)) for v in per_chunk], axis=0)

    for dirn, (q_ref, z_ref, v_ref, o_ref) in enumerate(
            ((qf_ref, zf_ref, vf_ref, of_ref), (qb_ref, zb_ref, vb_ref, ob_ref))):
        tri = (cc <= r) if dirn == 0 else (cc >= r)
        tri_b = jnp.where(tri, 1.0, 0.0).astype(BF16)
        last, mid_row = (BLK - 1, BLK // 2 - 1) if dirn == 0 else (0, BLK // 2)
        lb = lbs[dirn]
        q = q_ref[...].astype(F32)
        z = z_ref[...].astype(F32)
        v = v_ref[...]
        sg = _sigmoid_rel(z)
        logf = jnp.log(lb + (1.0 - lb) * sg)
        kk = (1.0 - lb) * (1.0 - sg)
        hi = logf.astype(BF16)
        lo = (logf - hi.astype(F32)).astype(BF16)
        chunks = [slice(g * BLK, (g + 1) * BLK) for g in range(n_chunks)]
        cums = [_dot(tri_b, hi[ch]) + _dot(tri_b, lo[ch]) for ch in chunks]
        totals = [cm[last:last + 1, :] for cm in cums]
        mids = [cm[mid_row:mid_row + 1, :] for cm in cums]
        cum = jnp.concatenate(cums, axis=0)
        mid = bcast_rows(mids)
        qa = q * jnp.exp(cum - mid)
        ka = kk * jnp.exp(mid - cum)
        q_dec = (qa * bcast_rows([jnp.exp(md) for md in mids])).astype(BF16)
        k_dec = (ka * bcast_rows([jnp.exp(tt - md) for tt, md in zip(totals, mids)])).astype(BF16)
        qa = qa.astype(BF16)
        ka = ka.astype(BF16)
        tiles = [(g, hd) for g in range(n_chunks) for hd in range(N_HEADS)]
        sl = lambda g, hd: (slice(g * BLK, (g + 1) * BLK), slice(hd * HEAD_DIM, (hd + 1) * HEAD_DIM))
        atts = [jnp.where(tri, _dot_nt(qa[sl(g, hd)], ka[sl(g, hd)]), 0.0).astype(BF16) for g, hd in tiles]
        intra = {t: _dot(att, v[sl(*t)]) for t, att in zip(tiles, atts)}
        kvs = {t: _dot_tn(v[sl(*t)], k_dec[sl(*t)]) for t in tiles}
        decays = [jnp.exp(tt) for tt in totals]
        order = range(n_chunks) if dirn == 0 else range(n_chunks - 1, -1, -1)
        for hd in range(N_HEADS):
            hs = slice(hd * HEAD_DIM, (hd + 1) * HEAD_DIM)
            state_t = state_ref[dirn, hd]
            for g in order:
                o_ref[g * BLK:(g + 1) * BLK, hs] = intra[(g, hd)] + _dot_nt(q_dec[sl(g, hd)], state_t.astype(BF16))
                state_t = state_t * decays[g][:, hs] + kvs[(g, hd)]
            state_ref[dirn, hd] = state_t


def _hgrn(qzz, ig, logits, batch, seq, layer):
    t = qzz.shape[0]
    n_chunks = 4
    rows = n_chunks * BLK
    nc = seq // rows
    depth = logits.shape[1]

    def fwd(colblk):
        return pl.BlockSpec((rows, BRANCH), lambda b, c: (b * nc + c, colblk))

    def bwd(colblk):
        return pl.BlockSpec((rows, BRANCH), lambda b, c: (b * nc + nc - 1 - c, colblk))

    return pl.pallas_call(
        functools.partial(_hgrn_kernel, layer=layer, n_chunks=n_chunks),
        out_shape=(jax.ShapeDtypeStruct((t, BRANCH), F32),) * 2,
        grid=(batch, nc),
        in_specs=[pl.BlockSpec((2, depth, BRANCH), lambda b, c: (0, 0, 0)),
                  fwd(0), fwd(1), fwd(0), bwd(0), bwd(2), bwd(0)],
        out_specs=(pl.BlockSpec((rows, BRANCH), lambda b, c: (b * nc + c, 0)),
                   pl.BlockSpec((rows, BRANCH), lambda b, c: (b * nc + nc - 1 - c, 0))),
        scratch_shapes=[pltpu.VMEM((2, N_HEADS, HEAD_DIM, HEAD_DIM), F32)],
        compiler_params=_params(("parallel", "arbitrary")),
        name="hgrn2_scan",
    )(logits, qzz, qzz, ig, qzz, qzz, ig)


def _merge_kernel(x_ref, ao0_ref, ao1_ref, ao2_ref, l0_ref, l1_ref, l2_ref, of_ref, ob_ref, bg_ref,
                  cb_ref, cbp_ref, cbn_ref, h_ref, wgate_ref, convw_ref, ng_ref, wbr_ref, wout_ref, gpost_ref,
                  o_ref, *, tm, seq):
    i = pl.program_id(0)
    h = h_ref[...]
    gates = [_sigmoid(_dot(h, wgate_ref[:, n * D_MODEL:(n + 1) * D_MODEL])) for n in range(3)]

    ls = (l0_ref[...], l1_ref[...], l2_ref[...])
    aos = (ao0_ref, ao1_ref, ao2_ref)
    lmax = jnp.maximum(jnp.maximum(ls[0], ls[1]), ls[2])
    es = [jnp.exp(l - lmax) for l in ls]
    inv = 1.0 / (es[0] + es[1] + es[2])
    a_parts = []
    for hd in range(N_HEADS):
        hs = slice(hd * HEAD_DIM, (hd + 1) * HEAD_DIM)
        acc = None
        for g in range(N_GROUPS):
            w = (es[g] * inv)[:, hd:hd + 1]
            term = w * aos[g][:, hs].astype(F32)
            acc = term if acc is None else acc + term
        a_parts.append(acc)
    o_a = jnp.concatenate(a_parts, axis=1).astype(BF16)

    b_parts = []
    for hd in range(N_HEADS):
        hs = slice(hd * HEAD_DIM, (hd + 1) * HEAD_DIM)
        o = of_ref[:, hs] + ob_ref[:, hs]
        b_parts.append(_rms(o, ng_ref[:, hs]))
    gate = bg_ref[...].astype(F32)
    o_b = (jnp.concatenate(b_parts, axis=1) * (gate * _sigmoid(gate))).astype(BF16)

    cu = cb_ref[:, 0:BRANCH].astype(F32)
    keep_prev = jnp.where((i * tm) % seq == 0, 0.0, 1.0)
    keep_next = jnp.where(((i + 1) * tm) % seq == 0, 0.0, 1.0)
    halo_rows = cbp_ref.shape[0]
    prev_row = keep_prev * cbp_ref[halo_rows - 1:halo_rows, 0:BRANCH].astype(F32)
    next_row = keep_next * cbn_ref[0:1, 0:BRANCH].astype(F32)
    rid = lax.broadcasted_iota(jnp.int32, cu.shape, 0)
    before = jnp.where(rid == 0, prev_row, pltpu.roll(cu, 1, 0))
    after = jnp.where(rid == tm - 1, next_row, pltpu.roll(cu, tm - 1, 0))
    conv = convw_ref[0:1, :] * before + convw_ref[1:2, :] * cu + convw_ref[2:3, :] * after
    o_c = (cb_ref[:, BRANCH:2 * BRANCH].astype(F32) * conv).astype(BF16)

    merged = None
    for n, o_n in enumerate((o_a, o_b, o_c)):
        up = _dot(o_n, wbr_ref[n])
        term = gates[n] * up
        merged = term if merged is None else merged + term
    y = _dot(merged.astype(BF16), wout_ref[...])
    o_ref[...] = x_ref[...] + _rms(y, gpost_ref[...])


def _merge(x, aos, lses, o_f, o_b, ig, cb, h, w_gate, conv_w, norm_g, w_branch, w_out, g_post, seq, tm):
    t = x.shape[0]
    halo = 16
    n_halo = t // halo
    per = tm // halo

    def rows(width, colblk=0):
        return pl.BlockSpec((tm, width), lambda i: (i, colblk))

    def full(shape):
        return _resident(shape, lambda i: (0,) * len(shape))

    in_specs = ([rows(D_MODEL)] + [rows(BRANCH)] * 3 + [rows(HEAD_DIM)] * 3 + [rows(BRANCH)] * 2
                + [rows(BRANCH, 1), rows(2 * BRANCH),
                   pl.BlockSpec((halo, 2 * BRANCH), lambda i: (jnp.maximum(i * per - 1, 0), 0)),
                   pl.BlockSpec((halo, 2 * BRANCH), lambda i: (jnp.minimum((i + 1) * per, n_halo - 1), 0)),
                   rows(D_MODEL), full((D_MODEL, G_COLS)), full((3, BRANCH)), full((1, BRANCH)),
                   full((3, BRANCH, D_MODEL)), full((D_MODEL, D_MODEL)), full((1, D_MODEL))])
    return pl.pallas_call(
        functools.partial(_merge_kernel, tm=tm, seq=seq),
        out_shape=jax.ShapeDtypeStruct((t, D_MODEL), F32),
        grid=(t // tm,),
        in_specs=in_specs,
        out_specs=rows(D_MODEL),
        compiler_params=_params(("parallel",)),
        name="merge_branches",
    )(x, *aos, *lses, o_f, o_b, ig, cb, cb, cb, h, w_gate, conv_w, norm_g, w_branch, w_out, g_post)


def _swiglu(h, wg, wu, wd):
    g = _dot(h, wg)
    u = _dot(h, wu)
    return _dot((g * _sigmoid(g) * u).astype(BF16), wd)


def _embed_tail(x, proj, gin_ref, gpost_ref, wgate_ref):
    gate = _sigmoid(_dot(_rms(x, gin_ref[...]).astype(BF16), wgate_ref[...]))
    return x + _rms(proj * gate, gpost_ref[...])


def _embed_operands(embed, p_row0, tm, index):
    pd = embed[0].shape[1]
    const = lambda *_: (0, 0)
    p_index = lambda *a: (index(*a)[0] + p_row0 // tm, 0)
    return [pl.BlockSpec((tm, pd), p_index), pl.BlockSpec((1, D_MODEL), const), pl.BlockSpec((1, D_MODEL), const),
            _resident((D_MODEL, D_MODEL), const), _resident((pd, D_MODEL), const)]


def _ffn_kernel(x_ref, gpre_ref, gpost_ref, wg_ref, wu_ref, wd_ref, p_ref, ein_ref, epost_ref, egate_ref, eproj_ref,
                o_ref):
    proj = _dot(p_ref[...].astype(BF16), eproj_ref[...])
    x = x_ref[...]
    h = _rms(x, gpre_ref[...]).astype(BF16)
    mixed = x + _rms(_swiglu(h, wg_ref[...], wu_ref[...], wd_ref[...]), gpost_ref[...])
    o_ref[...] = _embed_tail(mixed, proj, ein_ref, epost_ref, egate_ref)


def _ffn(x, g_pre, g_post, wg, wu, wd, embed, p_row0, tm):
    t = x.shape[0]
    const = lambda i: (0, 0)
    return pl.pallas_call(
        _ffn_kernel,
        out_shape=jax.ShapeDtypeStruct((t, D_MODEL), F32),
        grid=(t // tm,),
        in_specs=[pl.BlockSpec((tm, D_MODEL), lambda i: (i, 0)),
                  pl.BlockSpec((1, D_MODEL), const), pl.BlockSpec((1, D_MODEL), const),
                  _resident(wg.shape, const), _resident(wu.shape, const), _resident(wd.shape, const)]
                 + _embed_operands(embed, p_row0, tm, lambda i: (i, 0)),
        out_specs=pl.BlockSpec((tm, D_MODEL), lambda i: (i, 0)),
        compiler_params=_params(("parallel",)),
        name="dense_swiglu",
    )(x, g_pre, g_post, wg, wu, wd, *embed)


PIECES = D_MODEL // 2 // HEAD_DIM
SC_WINDOW = 128
EXPERT_TILE = 512


def _pack_rows(vals):
    bits = lax.bitcast_convert_type(vals.astype(BF16).astype(F32), jnp.uint32)
    half = D_MODEL // 2
    word = bits[:, :half] | (bits[:, half:] >> 16)
    return [lax.bitcast_convert_type(word[:, c * HEAD_DIM:(c + 1) * HEAD_DIM], jnp.int32) for c in range(PIECES)]


def _unpack_rows(piece_refs):
    words = [lax.bitcast_convert_type(r[...], jnp.uint32) for r in piece_refs]
    hi = [lax.bitcast_convert_type(w & jnp.uint32(0xFFFF0000), F32) for w in words]
    lo = [lax.bitcast_convert_type(w << 16, F32) for w in words]
    return jnp.concatenate(hi + lo, axis=1)


def _route_kernel(x_ref, gpre_ref, rhi_ref, rlo_ref, *refs, n_experts):
    piece_refs = refs[:PIECES]
    rec_ref, rank_ref, count_ref, carry_ref = refs[PIECES:]
    i = pl.program_id(0)
    tm = x_ref.shape[0]

    @pl.when(i == 0)
    def _():
        carry_ref[...] = jnp.zeros_like(carry_ref)

    hf = _rms(x_ref[...], gpre_ref[...])
    for r, piece in zip(piece_refs, _pack_rows(hf)):
        r[...] = piece
    h_hi = hf.astype(BF16)
    h_lo = (hf - h_hi.astype(F32)).astype(BF16)
    logits = _dot(h_hi, rhi_ref[...]) + _dot(h_hi, rlo_ref[...]) + _dot(h_lo, rhi_ref[...])
    lane = lax.broadcasted_iota(jnp.int32, logits.shape, 1).astype(F32)
    logits = jnp.where(lane < n_experts, logits, NEG_INF)
    m1 = jnp.max(logits, axis=1, keepdims=True)
    i1 = jnp.min(jnp.where(logits == m1, lane, 1e9), axis=1, keepdims=True)
    rest = jnp.where(lane == i1, NEG_INF, logits)
    m2 = jnp.max(rest, axis=1, keepdims=True)
    i2 = jnp.min(jnp.where(rest == m2, lane, 1e9), axis=1, keepdims=True)
    ex = jnp.exp(m2 - m1)
    w1 = 1.0 / (1.0 + ex)
    rec = jnp.where(lane == 0, i1, jnp.where(lane == 1, i2, jnp.where(lane == 2, w1, jnp.where(lane == 3, ex * w1, 0.0))))
    rec_ref[...] = rec
    rec_t = rec.T
    e1 = rec_t[0:1, :]
    e2 = rec_t[1:2, :]
    sub = lax.broadcasted_iota(jnp.int32, (8, tm), 0).astype(F32)
    oh1 = jnp.where(sub == e1, 1.0, 0.0)
    oh2 = jnp.where(sub == e2, 1.0, 0.0)
    chosen = oh1 + oh2
    src = lax.broadcasted_iota(jnp.int32, (tm, tm), 0)
    dst = lax.broadcasted_iota(jnp.int32, (tm, tm), 1)
    before = jnp.where(src < dst, 1.0, 0.0).astype(BF16)
    rank = _dot(chosen.astype(BF16), before) + carry_ref[:, 0:1]
    rank1 = jnp.sum(oh1 * rank, axis=0, keepdims=True)
    rank2 = jnp.sum(oh2 * rank, axis=0, keepdims=True)
    rank_ref[...] = jnp.where(sub == 0, rank1, jnp.where(sub == 1, rank2, jnp.where(sub == 2, e1, jnp.where(sub == 3, e2, 0.0))))
    carry_ref[...] = carry_ref[...] + jnp.sum(chosen, axis=1, keepdims=True)
    count_ref[...] = carry_ref[...]


def _route(x, g_pre, r_hi, r_lo, n_experts, tm):
    t = x.shape[0]
    assert n_experts <= 8
    piece = pl.BlockSpec((tm, HEAD_DIM), lambda i: (i, 0))
    return pl.pallas_call(
        functools.partial(_route_kernel, n_experts=n_experts),
        out_shape=(*[jax.ShapeDtypeStruct((t, HEAD_DIM), jnp.int32)] * PIECES,
                   jax.ShapeDtypeStruct((t, HEAD_DIM), F32), jax.ShapeDtypeStruct((8, t), F32),
                   jax.ShapeDtypeStruct((8, HEAD_DIM), F32)),
        grid=(t // tm,),
        in_specs=[pl.BlockSpec((tm, D_MODEL), lambda i: (i, 0)), pl.BlockSpec((1, D_MODEL), lambda i: (0, 0)),
                  pl.BlockSpec((D_MODEL, HEAD_DIM), lambda i: (0, 0)), pl.BlockSpec((D_MODEL, HEAD_DIM), lambda i: (0, 0))],
        out_specs=(*[piece] * PIECES, piece, pl.BlockSpec((8, tm), lambda i: (0, i)),
                   pl.BlockSpec((8, HEAD_DIM), lambda i: (0, 0))),
        scratch_shapes=[pltpu.VMEM((8, HEAD_DIM), F32)],
        compiler_params=_params(("arbitrary",)),
        name="moe_route",
    )(x, g_pre, r_hi, r_lo)


def _slot_kernel(rank_ref, start_ref, s1_ref, s2_ref):
    tm = rank_ref.shape[1]
    sub = lax.broadcasted_iota(jnp.int32, (8, tm), 0).astype(F32)
    start = start_ref[:, 0:1]
    rows = rank_ref[...]
    for choice, out in ((0, s1_ref), (1, s2_ref)):
        base = jnp.sum(jnp.where(sub == rows[2 + choice:3 + choice, :], start, 0.0), axis=0, keepdims=True)
        out[...] = (base + rows[choice:choice + 1, :]).astype(jnp.int32)


def _slots(ranks, starts, tm):
    t = ranks.shape[1]
    row = pl.BlockSpec((1, tm), lambda i: (0, i))
    return pl.pallas_call(
        _slot_kernel,
        out_shape=(jax.ShapeDtypeStruct((1, t), jnp.int32),) * 2,
        grid=(t // tm,),
        in_specs=[pl.BlockSpec((8, tm), lambda i: (0, i)), pl.BlockSpec((8, HEAD_DIM), lambda i: (0, 0))],
        out_specs=(row, row),
        compiler_params=_params(("parallel",)),
        name="moe_slots",
    )(ranks, starts)


def _sc_mesh():
    return plsc.VectorSubcoreMesh(core_axis_name="core", subcore_axis_name="subcore")


def _sc_scatter_rows(srcs, idxs, n_rows):
    ns, nk = len(srcs), len(idxs)
    half = srcs[0].shape[0] // SC_WINDOW // 2

    @functools.partial(pl.kernel, mesh=_sc_mesh(), scratch_types=[],
                       out_type=tuple(jax.ShapeDtypeStruct((n_rows, HEAD_DIM), srcs[0].dtype) for _ in range(ns)))
    def scatter(*refs):
        out_hbm = refs[ns + nk:]

        def body(*blocks):
            for idx in blocks[ns:]:
                for c in range(ns):
                    pltpu.sync_copy(blocks[c], out_hbm[c].at[idx.at[0]])

        pltpu.emit_pipeline(
            body, grid=(2, half),
            in_specs=[pl.BlockSpec((SC_WINDOW, HEAD_DIM), lambda i, j: (i * half + j, 0)) for _ in range(ns)]
                     + [pl.BlockSpec((1, SC_WINDOW), lambda i, j: (0, i * half + j)) for _ in range(nk)],
            out_specs=[],
            core_axis_name=("core", "subcore"),
            dimension_semantics=(pltpu.PARALLEL, pltpu.PARALLEL),
        )(*refs[:ns + nk])

    return scatter(*srcs, *idxs)


def _sc_gather_rows(tables, idx):
    nt = len(tables)
    n = idx.shape[1]
    half = n // SC_WINDOW // 2

    @functools.partial(pl.kernel, mesh=_sc_mesh(), scratch_types=[],
                       out_type=tuple(jax.ShapeDtypeStruct((n, HEAD_DIM), tables[0].dtype) for _ in range(nt)))
    def gather(*refs):
        table_hbm = refs[:nt]

        def body(idx_blk, *out_blks):
            for c in range(nt):
                pltpu.sync_copy(table_hbm[c].at[idx_blk.at[0]], out_blks[c])

        pltpu.emit_pipeline(
            body, grid=(2, half),
            in_specs=[pl.BlockSpec((1, SC_WINDOW), lambda i, j: (0, i * half + j))],
            out_specs=[pl.BlockSpec((SC_WINDOW, HEAD_DIM), lambda i, j: (i * half + j, 0)) for _ in range(nt)],
            core_axis_name=("core", "subcore"),
            dimension_semantics=(pltpu.PARALLEL, pltpu.PARALLEL),
        )(refs[nt], *refs[nt + 1:])

    return gather(*tables, idx)


def _expert_kernel(tile_expert_ref, n_used_ref, *refs):
    x_refs = refs[:PIECES]
    wg_ref, wu_ref, wd_ref = refs[PIECES:PIECES + 3]
    y_refs = refs[PIECES + 3:]

    @pl.when(pl.program_id(0) < n_used_ref[0])
    def _():
        h = _unpack_rows(x_refs).astype(BF16)
        y = _swiglu(h, wg_ref[0], wu_ref[0], wd_ref[0])
        for r, piece in zip(y_refs, _pack_rows(y)):
            r[...] = piece


def _expert_ffn(xs, tile_expert, n_used, wg, wu, wd):
    n_rows = xs[0].shape[0]
    rows = pl.BlockSpec((EXPERT_TILE, HEAD_DIM), lambda i, te, nu: (jnp.minimum(i, nu[0] - 1), 0))
    out_rows = pl.BlockSpec((EXPERT_TILE, HEAD_DIM), lambda i, te, nu: (i, 0))
    expert = lambda i, te, nu: (te[i], 0, 0)
    grid_spec = pltpu.PrefetchScalarGridSpec(
        num_scalar_prefetch=2,
        grid=(n_rows // EXPERT_TILE,),
        in_specs=[rows] * PIECES + [_resident((1,) + wg.shape[1:], expert), _resident((1,) + wu.shape[1:], expert),
                                    _resident((1,) + wd.shape[1:], expert)],
        out_specs=[out_rows] * PIECES,
    )
    return pl.pallas_call(
        _expert_kernel,
        out_shape=[jax.ShapeDtypeStruct((n_rows, HEAD_DIM), jnp.int32)] * PIECES,
        grid_spec=grid_spec,
        compiler_params=_params(("arbitrary",)),
        name="moe_expert_swiglu",
    )(tile_expert, n_used, *xs, wg, wu, wd)


def _combine_kernel(x_ref, rec_ref, gpost_ref, p_ref, ein_ref, epost_ref, egate_ref, eproj_ref, *refs):
    proj = _dot(p_ref[...].astype(BF16), eproj_ref[...])
    y1 = _unpack_rows(refs[:PIECES])
    y2 = _unpack_rows(refs[PIECES:2 * PIECES])
    o_ref = refs[2 * PIECES]
    rec = rec_ref[...]
    y = rec[:, 2:3] * y1 + rec[:, 3:4] * y2
    mixed = x_ref[...] + _rms(y, gpost_ref[...])
    o_ref[...] = _embed_tail(mixed, proj, ein_ref, epost_ref, egate_ref)


def _combine(x, rec, g_post, embed, p_row0, y1, y2, tm):
    t = x.shape[0]
    piece = pl.BlockSpec((tm, HEAD_DIM), lambda i: (i, 0))
    return pl.pallas_call(
        _combine_kernel,
        out_shape=jax.ShapeDtypeStruct((t, D_MODEL), F32),
        grid=(t // tm,),
        in_specs=[pl.BlockSpec((tm, D_MODEL), lambda i: (i, 0)), piece, pl.BlockSpec((1, D_MODEL), lambda i: (0, 0))]
                 + _embed_operands(embed, p_row0, tm, lambda i: (i, 0)) + [piece] * (2 * PIECES),
        out_specs=pl.BlockSpec((tm, D_MODEL), lambda i: (i, 0)),
        compiler_params=_params(("parallel",)),
        name="moe_combine",
    )(x, rec, g_post, *embed, *y1, *y2)


def _moe(x, g_pre, g_post, router, wg, wu, wd, embed, p_row0, tm):
    t = x.shape[0]
    n_experts = router.shape[1]
    router = jnp.pad(router, ((0, 0), (0, HEAD_DIM - n_experts)))
    r_hi = router.astype(BF16)
    r_lo = (router - r_hi.astype(F32)).astype(BF16)
    *h_pieces, rec, ranks, counts = _route(x, g_pre, r_hi, r_lo, n_experts, tm)

    counts = counts[:n_experts, 0].astype(jnp.int32)
    tiles = (counts + EXPERT_TILE - 1) // EXPERT_TILE
    tile_end = jnp.cumsum(tiles)
    starts = ((tile_end - tiles) * EXPERT_TILE).astype(F32)
    starts = jnp.broadcast_to(jnp.pad(starts, (0, 8 - n_experts))[:, None], (8, HEAD_DIM))
    n_rows = TOP_K * t + n_experts * EXPERT_TILE
    n_tiles = n_rows // EXPERT_TILE
    n_used = tile_end[-1:]
    tile_ids = jnp.minimum(jnp.arange(n_tiles, dtype=jnp.int32), n_used[0] - 1)
    tile_expert = jnp.sum(tile_ids[:, None] >= tile_end[None, :], axis=1).astype(jnp.int32)

    slot1, slot2 = _slots(ranks, starts, min(t, 8192))
    xs = []
    for c in range(0, PIECES, 2):
        xs += _sc_scatter_rows(h_pieces[c:c + 2], [slot1, slot2], n_rows)
    ys = _expert_ffn(xs, tile_expert, n_used, wg, wu, wd)
    gathered = []
    for slot in (slot1, slot2):
        rows = []
        for c in range(0, PIECES, 2):
            rows += _sc_gather_rows(ys[c:c + 2], slot)
        gathered.append(rows)
    return _combine(x, rec, g_post, embed, p_row0, gathered[0], gathered[1], tm)


def kernel(x, p, positions, w_in, conv_w, hgrn_lb_logits, hgrn_norm_g, w_branch, w_out, g_mix_pre, g_mix_post, g_ffn_pre, g_ffn_post, dense_w_gate, dense_w_up, dense_w_down, moe_router, moe_w_gate, moe_w_up, moe_w_down, ple_w_proj, ple_w_gate, ple_g_in, ple_g_post):
    batch, seq, _ = x.shape
    depth = w_in.shape[0]
    t = batch * seq
    tm = min(TOKEN_TILE, seq)

    xt = x.reshape(t, D_MODEL)
    cos, sin = _rope_tables(positions.astype(F32).reshape(t // ROPE_PACK, ROPE_PACK), tm)
    row = lambda a: a.reshape(1, -1)
    in_edges = (0, A_COLS, A_COLS + 3 * BRANCH, A_COLS + B_COLS, A_COLS + B_COLS + C_COLS, A_COLS + B_COLS + C_COLS + G_COLS)
    in_bounds = tuple(zip(in_edges[:-1], in_edges[1:]))
    w_in_rows = w_in.reshape(depth * D_MODEL, -1)
    p_rows = p.reshape(depth * t, -1)

    for i in range(depth):
        w_attn, w_qzz, w_ig, w_conv, w_gates = _split_to_bf16(w_in_rows, i * D_MODEL, D_MODEL, in_bounds)
        g_pre = row(g_mix_pre[i])
        *qkvs, h = _inproj_attention(xt, g_pre, w_attn, cos, sin, batch, seq, tm)
        aos, lses, sides = zip(*[_attention_group(qkvs[g], h, w_side, g == 2, batch, seq, g)
                                 for g, w_side in enumerate((w_qzz, w_ig, w_conv))])
        qzz, ig, cb = sides
        o_f, o_b = _hgrn(qzz, ig, hgrn_lb_logits, batch, seq, i)
        xt = _merge(xt, aos, lses, o_f, o_b, ig, cb, h, w_gates, conv_w[i],
                    row(hgrn_norm_g[i]), w_branch[i].astype(BF16), w_out[i].astype(BF16), row(g_mix_post[i]),
                    seq, tm)

        embed = (p_rows, row(ple_g_in[i]), row(ple_g_post[i]),
                 ple_w_gate[i].astype(BF16), ple_w_proj[i].astype(BF16))
        j = i // 2
        if i % 2 == 0:
            xt = _ffn(xt, row(g_ffn_pre[i]), row(g_ffn_post[i]), dense_w_gate[j].astype(BF16),
                      dense_w_up[j].astype(BF16), dense_w_down[j].astype(BF16), embed, i * t, tm)
        else:
            xt = _moe(xt, row(g_ffn_pre[i]), row(g_ffn_post[i]), moe_router[j], _to_bf16(moe_w_gate[j]),
                      _to_bf16(moe_w_up[j]), _to_bf16(moe_w_down[j]), embed, i * t, tm)

    return xt.reshape(batch, seq, D_MODEL)
```

```python
import functools
import math

import jax
import jax.numpy as jnp
from jax import lax
from jax.experimental import pallas as pl
from jax.experimental.pallas import tpu as pltpu
from jax.experimental.pallas import tpu_sc as plsc

D_MODEL = 1024
EPS = 1e-6
NEG_INF = -1e30

HEAD_DIM = 128
N_HEADS = 4
BRANCH = N_HEADS * HEAD_DIM
DIL_GROUPS = ((128, 1), (512, 4), (2048, 16))
N_GROUPS = len(DIL_GROUPS)
BLK = 64
ROT_DIM = HEAD_DIM // 4
ROPE_THETA = 500000.0
TOP_K = 2

A_COLS = N_GROUPS * 3 * BRANCH
B_COLS = 5 * BRANCH
C_COLS = 3 * BRANCH
G_COLS = 3 * D_MODEL

BF16 = jnp.bfloat16
F32 = jnp.float32

TOKEN_TILE = 512
VMEM_LIMIT = 56 * 1024 * 1024
ATTENTION_VMEM_LIMIT = 62 * 1024 * 1024


def _params(sem, vmem_limit=VMEM_LIMIT):
    return pltpu.CompilerParams(dimension_semantics=sem, vmem_limit_bytes=vmem_limit)


def _cast_kernel(w_ref, o_ref):
    o_ref[...] = w_ref[...].astype(o_ref.dtype)


def _to_bf16(w):
    cols = w.shape[-1]
    flat = w.reshape(-1, cols)
    rows = flat.shape[0]
    block = rows
    while block % 32 == 0 and block * cols * 4 > 4 * 1024 * 1024:
        block //= 2
    out = pl.pallas_call(
        _cast_kernel,
        out_shape=jax.ShapeDtypeStruct(flat.shape, BF16),
        grid=(rows // block,),
        in_specs=[pl.BlockSpec((block, cols), lambda i: (i, 0))],
        out_specs=pl.BlockSpec((block, cols), lambda i: (i, 0)),
        compiler_params=_params(("parallel",)),
        name="weights_to_bf16",
    )(flat)
    return out.reshape(w.shape)


def _split_cast_kernel(w_ref, *o_refs, bounds):
    for o_ref, (lo, hi) in zip(o_refs, bounds):
        o_ref[...] = w_ref[:, lo:hi].astype(o_ref.dtype)


def _split_to_bf16(w, first_row, n, bounds, rows=128):
    cols = w.shape[1]
    first = first_row // rows
    return pl.pallas_call(
        functools.partial(_split_cast_kernel, bounds=bounds),
        out_shape=tuple(jax.ShapeDtypeStruct((n, hi - lo), BF16) for lo, hi in bounds),
        grid=(n // rows,),
        in_specs=[pl.BlockSpec((rows, cols), lambda i: (first + i, 0))],
        out_specs=tuple(pl.BlockSpec((rows, hi - lo), lambda i: (i, 0)) for lo, hi in bounds),
        compiler_params=_params(("parallel",)),
        name="split_weights_to_bf16",
    )(w)


def _resident(shape, index_map):
    return pl.BlockSpec(shape, index_map, pipeline_mode=pl.Buffered(1))


def _rms(xf, g):
    return xf * lax.rsqrt(jnp.mean(xf * xf, axis=-1, keepdims=True) + EPS) * g


def _sigmoid(z):
    return 0.5 * jnp.tanh(0.5 * z) + 0.5


def _sigmoid_rel(z):
    return 1.0 / (1.0 + jnp.exp(-z))


def _dot(a, b):
    return jnp.dot(a, b, preferred_element_type=F32)


def _dot_nt(a, b):
    return lax.dot_general(a, b, (((1,), (1,)), ((), ())), preferred_element_type=F32)


def _dot_tn(a, b):
    return lax.dot_general(a, b, (((0,), (0,)), ((), ())), preferred_element_type=F32)


ROPE_PACK = HEAD_DIM // ROT_DIM


def _rope_table_kernel(pos_ref, freq_ref, sign_ref, cos_ref, sin_ref):
    pos = pos_ref[...]
    group = lax.broadcasted_iota(jnp.int32, cos_ref.shape, 1) // ROT_DIM
    pos_lanes = jnp.zeros(cos_ref.shape, F32)
    for k in range(ROPE_PACK):
        pos_lanes = jnp.where(group == k, pos[:, k:k + 1], pos_lanes)
    ang = pos_lanes * freq_ref[...]
    cos_ref[...] = jnp.cos(ang)
    sin_ref[...] = jnp.sin(ang) * sign_ref[...]


def _rope_tables(pos_packed, rows):
    n = pos_packed.shape[0]
    half = ROT_DIM // 2
    inv_freq = ROPE_THETA ** (-jnp.arange(0, ROT_DIM, 2, dtype=F32) / ROT_DIM)
    freq = jnp.tile(jnp.concatenate([inv_freq, inv_freq]), ROPE_PACK)[None, :]
    sign = jnp.tile(jnp.concatenate([-jnp.ones((half,), F32), jnp.ones((half,), F32)]), ROPE_PACK)[None, :]
    row = pl.BlockSpec((1, HEAD_DIM), lambda i: (0, 0))
    tab = pl.BlockSpec((rows, HEAD_DIM), lambda i: (i, 0))
    return pl.pallas_call(
        _rope_table_kernel,
        out_shape=(jax.ShapeDtypeStruct((n, HEAD_DIM), F32),) * 2,
        grid=(n // rows,),
        in_specs=[pl.BlockSpec((rows, ROPE_PACK), lambda i: (i, 0)), row, row],
        out_specs=(tab, tab),
        compiler_params=_params(("parallel",)),
        name="rope_tables",
    )(pos_packed, freq, sign)


def _unpack_rope_tables(cosp_ref, sinp_ref, tab_ref):
    rows = cosp_ref.shape[0]
    lane = lax.broadcasted_iota(jnp.int32, (rows, HEAD_DIM), 1)
    for k in range(ROPE_PACK):
        for idx, (ref, fill) in enumerate(((cosp_ref, 1.0), (sinp_ref, 0.0))):
            packed = ref[...]
            shifted = pltpu.roll(packed, HEAD_DIM - ROT_DIM * k, 1) if k else packed
            tab_ref[idx, pl.ds(k, rows, stride=ROPE_PACK), :] = jnp.where(lane < ROT_DIM, shifted, fill)


def _rope_head(th, cos, sin):
    half = ROT_DIM // 2
    lane = lax.broadcasted_iota(jnp.int32, th.shape, 1)
    swapped = jnp.where(lane < half, pltpu.roll(th, HEAD_DIM - half, 1), pltpu.roll(th, half, 1))
    return th * cos + swapped * sin


def _inproj_attn_kernel(x_ref, g_ref, w_ref, cosp_ref, sinp_ref, o0_ref, o1_ref, o2_ref, h_ref, stage_ref, tab_ref):
    _unpack_rope_tables(cosp_ref, sinp_ref, tab_ref)
    hf = _rms(x_ref[...], g_ref[...])
    h_ref[...] = hf.astype(BF16)
    n_pieces = D_MODEL // HEAD_DIM
    for pc in range(n_pieces):
        stage_ref[pc] = hf[:, pc * HEAD_DIM:(pc + 1) * HEAD_DIM]
    scale = 1.0 / math.sqrt(HEAD_DIM)
    tm = x_ref.shape[0]
    for grp, o_ref in enumerate((o0_ref, o1_ref, o2_ref)):
        dil = DIL_GROUPS[grp][1]
        sub = tm // dil
        if dil == 1:
            h, cos, sin = hf.astype(BF16), tab_ref[0], tab_ref[1]
        else:
            residue = lambda ref, *lead: jnp.concatenate(
                [ref[(*lead, pl.ds(r, sub, stride=dil), slice(None))] for r in range(dil)], axis=0)
            h = jnp.concatenate([residue(stage_ref, pc) for pc in range(n_pieces)], axis=1).astype(BF16)
            cos, sin = residue(tab_ref, 0), residue(tab_ref, 1)
        for part in range(3):
            c = grp * 3 + part
            t = _dot(h, w_ref[:, c * BRANCH:(c + 1) * BRANCH])
            if part < 2:
                heads = []
                for hd in range(N_HEADS):
                    r = _rope_head(t[:, hd * HEAD_DIM:(hd + 1) * HEAD_DIM], cos, sin)
                    heads.append(r * scale if part == 0 else r)
                t = jnp.concatenate(heads, axis=1)
            t = t.astype(BF16)
            for r in range(dil):
                o_ref[0, r, :, part * BRANCH:(part + 1) * BRANCH] = t[r * sub:(r + 1) * sub, :]


def _inproj_attention(x, g, w, cos, sin, batch, seq, tm):
    per_seq = seq // tm
    rows = lambda width: pl.BlockSpec((tm, width), lambda b, j: (b * per_seq + j, 0))
    packed = pl.BlockSpec((tm // ROPE_PACK, HEAD_DIM), lambda b, j: (b * per_seq + j, 0))
    out_shape = tuple(jax.ShapeDtypeStruct((batch, dil, seq // dil, 3 * BRANCH), BF16) for _, dil in DIL_GROUPS)
    out_shape += (jax.ShapeDtypeStruct((batch * seq, D_MODEL), BF16),)
    out_specs = tuple(pl.BlockSpec((1, dil, tm // dil, 3 * BRANCH), lambda b, j: (b, 0, j, 0))
                      for _, dil in DIL_GROUPS) + (rows(D_MODEL),)
    return pl.pallas_call(
        _inproj_attn_kernel, out_shape=out_shape, grid=(batch, per_seq),
        in_specs=[rows(D_MODEL), pl.BlockSpec((1, D_MODEL), lambda b, j: (0, 0)),
                  _resident((D_MODEL, A_COLS), lambda b, j: (0, 0)), packed, packed],
        out_specs=out_specs,
        scratch_shapes=[pltpu.VMEM((D_MODEL // HEAD_DIM, tm, HEAD_DIM), F32), pltpu.VMEM((2, tm, HEAD_DIM), F32)],
        compiler_params=_params(("parallel", "parallel")), name="inproj_attention",
    )(x, g, w, cos, sin)


def _attn_kernel(q_ref, k_ref, kp_ref, kn_ref, v_ref, vp_ref, vn_ref, h_ref, w_ref, o_ref, lse_ref, side_ref,
                 o_stage, lse_stage, *, dil, nj, n_steps, conv_side):
    n = pl.program_id(1)
    part_rows = h_ref.shape[0] // N_HEADS

    def side_projection(part):
        rs = slice(part * part_rows, (part + 1) * part_rows)
        h = h_ref[rs, :]
        if conv_side:
            u = _dot(h, w_ref[:, 0:BRANCH])
            b_gate = _dot(h, w_ref[:, BRANCH:2 * BRANCH])
            c_gate = _dot(h, w_ref[:, 2 * BRANCH:3 * BRANCH])
            side_ref[rs, 0:BRANCH] = (c_gate * u).astype(BF16)
            side_ref[rs, BRANCH:2 * BRANCH] = b_gate.astype(BF16)
        else:
            for c in range(w_ref.shape[1] // BRANCH):
                cols = slice(c * BRANCH, (c + 1) * BRANCH)
                side_ref[rs, cols] = _dot(h, w_ref[:, cols]).astype(BF16)

    qb = 2 if nj % 2 == 0 else 1
    q_rows, w_rows, n_tiles = qb * BLK, (qb + 2) * BLK, nj // qb
    col_lo = jnp.where(n > 0, 0, BLK)
    col_hi = jnp.where(n < n_steps - 1, w_rows, w_rows - BLK)
    row = lax.broadcasted_iota(jnp.int32, (q_rows, w_rows), 0)
    col = lax.broadcasted_iota(jnp.int32, (q_rows, w_rows), 1)
    band = jnp.abs(col - BLK - row) <= BLK
    masks = {}
    for j in range(n_tiles):
        m = band
        if j == 0:
            m = m & (col >= col_lo)
        if j == n_tiles - 1:
            m = m & (col < col_hi)
        masks[j] = m
    pairs = [(r, j) for r in range(dil) for j in range(n_tiles)]
    lane = lax.broadcasted_iota(jnp.int32, (len(pairs), q_rows, HEAD_DIM), 2)

    def natural_rows(r, j):
        return pl.ds(j * q_rows * dil + r, q_rows, stride=dil)

    def window(own_ref, prev_ref, next_ref, r, j, hs):
        parts = []
        for w in range(j * qb - 1, j * qb + qb + 1):
            if w < 0:
                parts.append(prev_ref[0, r, :, hs])
            elif w >= nj:
                parts.append(next_ref[0, r, :, hs])
            else:
                parts.append(own_ref[0, r, w * BLK:(w + 1) * BLK, hs])
        return jnp.concatenate(parts, axis=0)

    lse_tiles = jnp.zeros((len(pairs), q_rows, HEAD_DIM), F32)
    for hd in range(N_HEADS):
        side_projection(hd)
        hs = slice(hd * HEAD_DIM, (hd + 1) * HEAD_DIM)
        scores = [_dot_nt(q_ref[0, r, j * q_rows:(j + 1) * q_rows, hs], window(k_ref, kp_ref, kn_ref, r, j, hs))
                  for r, j in pairs]
        s = jnp.stack([jnp.where(masks[j], sc, NEG_INF) for (r, j), sc in zip(pairs, scores)])
        m = jnp.max(s, axis=2, keepdims=True)
        p = jnp.exp(s - m)
        l = jnp.sum(p, axis=2, keepdims=True)
        pb = p.astype(BF16)
        outs = jnp.stack([_dot(pb[i], window(v_ref, vp_ref, vn_ref, r, j, hs)) for i, (r, j) in enumerate(pairs)])
        outs = outs * (1.0 / l)
        lse_tiles = jnp.where(lane == hd, m + jnp.log(l), lse_tiles)
        for i, (r, j) in enumerate(pairs):
            o_stage[hd, natural_rows(r, j), :] = outs[i]
        o_ref[:, hs] = o_stage[hd].astype(o_ref.dtype)
    for i, (r, j) in enumerate(pairs):
        lse_stage[natural_rows(r, j), :] = lse_tiles[i]
    lse_ref[...] = lse_stage[...]


def _attention_group(qkv, h, w_side, conv_side, batch, seq, group):
    _, dil = DIL_GROUPS[group]
    t = batch * seq
    sub_len = seq // dil
    rows_per_step = min(seq, 32 * BLK)
    nj = rows_per_step // (dil * BLK)
    n_steps = seq // rows_per_step
    n_blk = sub_len // BLK
    sub_rows = rows_per_step // dil

    def own(part):
        return pl.BlockSpec((1, dil, sub_rows, BRANCH), lambda b, n: (b, 0, n, part))

    def prev(part):
        return pl.BlockSpec((1, dil, BLK, BRANCH), lambda b, n: (b, 0, jnp.maximum(n * nj - 1, 0), part))

    def nxt(part):
        return pl.BlockSpec((1, dil, BLK, BRANCH), lambda b, n: (b, 0, jnp.minimum((n + 1) * nj, n_blk - 1), part))

    side_cols = 2 * BRANCH if conv_side else w_side.shape[1]
    step_rows = lambda width: pl.BlockSpec((rows_per_step, width), lambda b, n: (b * n_steps + n, 0))
    return pl.pallas_call(
        functools.partial(_attn_kernel, dil=dil, nj=nj, n_steps=n_steps, conv_side=conv_side),
        out_shape=(jax.ShapeDtypeStruct((t, BRANCH), BF16), jax.ShapeDtypeStruct((t, HEAD_DIM), F32),
                   jax.ShapeDtypeStruct((t, side_cols), BF16)),
        grid=(batch, n_steps),
        in_specs=[own(0), own(1), prev(1), nxt(1), own(2), prev(2), nxt(2), step_rows(D_MODEL),
                  _resident(w_side.shape, lambda b, n: (0, 0))],
        out_specs=(step_rows(BRANCH), step_rows(HEAD_DIM), step_rows(side_cols)),
        scratch_shapes=[pltpu.VMEM((N_HEADS, rows_per_step, HEAD_DIM), F32), pltpu.VMEM((rows_per_step, HEAD_DIM), F32)],
        compiler_params=_params(("parallel", "parallel"), ATTENTION_VMEM_LIMIT),
        name=f"dilated_attention_g{group}",
    )(qkv, qkv, qkv, qkv, qkv, qkv, qkv, h, w_side)


def _hgrn_kernel(logit_ref, qf_ref, zf_ref, vf_ref, qb_ref, zb_ref, vb_ref, of_ref, ob_ref, state_ref,
                 *, layer, n_chunks):
    c = pl.program_id(1)

    @pl.when(c == 0)
    def _():
        state_ref[...] = jnp.zeros_like(state_ref)

    depth = logit_ref.shape[1]
    lbs = []
    for dirn in range(2):
        rows = [logit_ref[dirn, j:j + 1, :] for j in range(depth)]
        top = functools.reduce(jnp.maximum, rows)
        exps = [jnp.exp(rw - top) for rw in rows]
        lbs.append(sum(exps[1:layer + 1], jnp.zeros_like(top)) / sum(exps[1:], exps[0]))

    r = lax.broadcasted_iota(jnp.int32, (BLK, BLK), 0)
    cc = lax.broadcasted_iota(jnp.int32, (BLK, BLK), 1)

    def bcast_rows(per_chunk):
        return jnp.concatenate([jnp.broadcast_to(v, (BLK, BRANCH)) for v in per_chunk], axis=0)

    for dirn, (q_ref, z_ref, v_ref, o_ref) in enumerate(
            ((qf_ref, zf_ref, vf_ref, of_ref), (qb_ref, zb_ref, vb_ref, ob_ref))):
        tri = (cc <= r) if dirn == 0 else (cc >= r)
        tri_b = jnp.where(tri, 1.0, 0.0).astype(BF16)
        last, mid_row = (BLK - 1, BLK // 2 - 1) if dirn == 0 else (0, BLK // 2)
        lb = lbs[dirn]
        q = q_ref[...].astype(F32)
        z = z_ref[...].astype(F32)
        v = v_ref[...]
        sg = _sigmoid_rel(z)
        logf = jnp.log(lb + (1.0 - lb) * sg)
        kk = (1.0 - lb) * (1.0 - sg)
        hi = logf.astype(BF16)
        lo = (logf - hi.astype(F32)).astype(BF16)
        chunks = [slice(g * BLK, (g + 1) * BLK) for g in range(n_chunks)]
        cums = [_dot(tri_b, hi[ch]) + _dot(tri_b, lo[ch]) for ch in chunks]
        totals = [cm[last:last + 1, :] for cm in cums]
        mids = [cm[mid_row:mid_row + 1, :] for cm in cums]
        cum = jnp.concatenate(cums, axis=0)
        mid = bcast_rows(mids)
        qa = q * jnp.exp(cum - mid)
        ka = kk * jnp.exp(mid - cum)
        q_dec = (qa * bcast_rows([jnp.exp(md) for md in mids])).astype(BF16)
        k_dec = (ka * bcast_rows([jnp.exp(tt - md) for tt, md in zip(totals, mids)])).astype(BF16)
        qa = qa.astype(BF16)
        ka = ka.astype(BF16)
        tiles = [(g, hd) for g in range(n_chunks) for hd in range(N_HEADS)]
        sl = lambda g, hd: (slice(g * BLK, (g + 1) * BLK), slice(hd * HEAD_DIM, (hd + 1) * HEAD_DIM))
        atts = [jnp.where(tri, _dot_nt(qa[sl(g, hd)], ka[sl(g, hd)]), 0.0).astype(BF16) for g, hd in tiles]
        intra = {t: _dot(att, v[sl(*t)]) for t, att in zip(tiles, atts)}
        kvs = {t: _dot_tn(v[sl(*t)], k_dec[sl(*t)]) for t in tiles}
        decays = [jnp.exp(tt) for tt in totals]
        order = range(n_chunks) if dirn == 0 else range(n_chunks - 1, -1, -1)
        for hd in range(N_HEADS):
            hs = slice(hd * HEAD_DIM, (hd + 1) * HEAD_DIM)
            state_t = state_ref[dirn, hd]
            for g in order:
                o_ref[g * BLK:(g + 1) * BLK, hs] = intra[(g, hd)] + _dot_nt(q_dec[sl(g, hd)], state_t.astype(BF16))
                state_t = state_t * decays[g][:, hs] + kvs[(g, hd)]
            state_ref[dirn, hd] = state_t


def _hgrn(qzz, ig, logits, batch, seq, layer):
    t = qzz.shape[0]
    n_chunks = 4
    rows = n_chunks * BLK
    nc = seq // rows
    depth = logits.shape[1]

    def fwd(colblk):
        return pl.BlockSpec((rows, BRANCH), lambda b, c: (b * nc + c, colblk))

    def bwd(colblk):
        return pl.BlockSpec((rows, BRANCH), lambda b, c: (b * nc + nc - 1 - c, colblk))

    return pl.pallas_call(
        functools.partial(_hgrn_kernel, layer=layer, n_chunks=n_chunks),
        out_shape=(jax.ShapeDtypeStruct((t, BRANCH), F32),) * 2,
        grid=(batch, nc),
        in_specs=[pl.BlockSpec((2, depth, BRANCH), lambda b, c: (0, 0, 0)),
                  fwd(0), fwd(1), fwd(0), bwd(0), bwd(2), bwd(0)],
        out_specs=(pl.BlockSpec((rows, BRANCH), lambda b, c: (b * nc + c, 0)),
                   pl.BlockSpec((rows, BRANCH), lambda b, c: (b * nc + nc - 1 - c, 0))),
        scratch_shapes=[pltpu.VMEM((2, N_HEADS, HEAD_DIM, HEAD_DIM), F32)],
        compiler_params=_params(("parallel", "arbitrary")),
        name="hgrn2_scan",
    )(logits, qzz, qzz, ig, qzz, qzz, ig)


def _merge_kernel(x_ref, ao0_ref, ao1_ref, ao2_ref, l0_ref, l1_ref, l2_ref, of_ref, ob_ref, bg_ref,
                  cb_ref, cbp_ref, cbn_ref, h_ref, wgate_ref, convw_ref, ng_ref, wbr_ref, wout_ref, gpost_ref,
                  o_ref, *, tm, seq):
    i = pl.program_id(0)
    h = h_ref[...]
    gates = [_sigmoid(_dot(h, wgate_ref[:, n * D_MODEL:(n + 1) * D_MODEL])) for n in range(3)]

    ls = (l0_ref[...], l1_ref[...], l2_ref[...])
    aos = (ao0_ref, ao1_ref, ao2_ref)
    lmax = jnp.maximum(jnp.maximum(ls[0], ls[1]), ls[2])
    es = [jnp.exp(l - lmax) for l in ls]
    inv = 1.0 / (es[0] + es[1] + es[2])
    a_parts = []
    for hd in range(N_HEADS):
        hs = slice(hd * HEAD_DIM, (hd + 1) * HEAD_DIM)
        acc = None
        for g in range(N_GROUPS):
            w = (es[g] * inv)[:, hd:hd + 1]
            term = w * aos[g][:, hs].astype(F32)
            acc = term if acc is None else acc + term
        a_parts.append(acc)
    o_a = jnp.concatenate(a_parts, axis=1).astype(BF16)

    b_parts = []
    for hd in range(N_HEADS):
        hs = slice(hd * HEAD_DIM, (hd + 1) * HEAD_DIM)
        o = of_ref[:, hs] + ob_ref[:, hs]
        b_parts.append(_rms(o, ng_ref[:, hs]))
    gate = bg_ref[...].astype(F32)
    o_b = (jnp.concatenate(b_parts, axis=1) * (gate * _sigmoid(gate))).astype(BF16)

    cu = cb_ref[:, 0:BRANCH].astype(F32)
    keep_prev = jnp.where((i * tm) % seq == 0, 0.0, 1.0)
    keep_next = jnp.where(((i + 1) * tm) % seq == 0, 0.0, 1.0)
    halo_rows = cbp_ref.shape[0]
    prev_row = keep_prev * cbp_ref[halo_rows - 1:halo_rows, 0:BRANCH].astype(F32)
    next_row = keep_next * cbn_ref[0:1, 0:BRANCH].astype(F32)
    rid = lax.broadcasted_iota(jnp.int32, cu.shape, 0)
    before = jnp.where(rid == 0, prev_row, pltpu.roll(cu, 1, 0))
    after = jnp.where(rid == tm - 1, next_row, pltpu.roll(cu, tm - 1, 0))
    conv = convw_ref[0:1, :] * before + convw_ref[1:2, :] * cu + convw_ref[2:3, :] * after
    o_c = (cb_ref[:, BRANCH:2 * BRANCH].astype(F32) * conv).astype(BF16)

    merged = None
    for n, o_n in enumerate((o_a, o_b, o_c)):
        up = _dot(o_n, wbr_ref[n])
        term = gates[n] * up
        merged = term if merged is None else merged + term
    y = _dot(merged.astype(BF16), wout_ref[...])
    o_ref[...] = x_ref[...] + _rms(y, gpost_ref[...])


def _merge(x, aos, lses, o_f, o_b, ig, cb, h, w_gate, conv_w, norm_g, w_branch, w_out, g_post, seq, tm):
    t = x.shape[0]
    halo = 16
    n_halo = t // halo
    per = tm // halo

    def rows(width, colblk=0):
        return pl.BlockSpec((tm, width), lambda i: (i, colblk))

    def full(shape):
        return _resident(shape, lambda i: (0,) * len(shape))

    in_specs = ([rows(D_MODEL)] + [rows(BRANCH)] * 3 + [rows(HEAD_DIM)] * 3 + [rows(BRANCH)] * 2
                + [rows(BRANCH, 1), rows(2 * BRANCH),
                   pl.BlockSpec((halo, 2 * BRANCH), lambda i: (jnp.maximum(i * per - 1, 0), 0)),
                   pl.BlockSpec((halo, 2 * BRANCH), lambda i: (jnp.minimum((i + 1) * per, n_halo - 1), 0)),
                   rows(D_MODEL), full((D_MODEL, G_COLS)), full((3, BRANCH)), full((1, BRANCH)),
                   full((3, BRANCH, D_MODEL)), full((D_MODEL, D_MODEL)), full((1, D_MODEL))])
    return pl.pallas_call(
        functools.partial(_merge_kernel, tm=tm, seq=seq),
        out_shape=jax.ShapeDtypeStruct((t, D_MODEL), F32),
        grid=(t // tm,),
        in_specs=in_specs,
        out_specs=rows(D_MODEL),
        compiler_params=_params(("parallel",)),
        name="merge_branches",
    )(x, *aos, *lses, o_f, o_b, ig, cb, cb, cb, h, w_gate, conv_w, norm_g, w_branch, w_out, g_post)


def _swiglu(h, wg, wu, wd):
    g = _dot(h, wg)
    u = _dot(h, wu)
    return _dot((g * _sigmoid(g) * u).astype(BF16), wd)


def _embed_tail(x, proj, gin_ref, gpost_ref, wgate_ref):
    gate = _sigmoid(_dot(_rms(x, gin_ref[...]).astype(BF16), wgate_ref[...]))
    return x + _rms(proj * gate, gpost_ref[...])


def _embed_operands(embed, p_row0, tm, index):
    pd = embed[0].shape[1]
    const = lambda *_: (0, 0)
    p_index = lambda *a: (index(*a)[0] + p_row0 // tm, 0)
    return [pl.BlockSpec((tm, pd), p_index), pl.BlockSpec((1, D_MODEL), const), pl.BlockSpec((1, D_MODEL), const),
            _resident((D_MODEL, D_MODEL), const), _resident((pd, D_MODEL), const)]


def _ffn_kernel(x_ref, gpre_ref, gpost_ref, wg_ref, wu_ref, wd_ref, p_ref, ein_ref, epost_ref, egate_ref, eproj_ref,
                o_ref):
    proj = _dot(p_ref[...].astype(BF16), eproj_ref[...])
    x = x_ref[...]
    h = _rms(x, gpre_ref[...]).astype(BF16)
    mixed = x + _rms(_swiglu(h, wg_ref[...], wu_ref[...], wd_ref[...]), gpost_ref[...])
    o_ref[...] = _embed_tail(mixed, proj, ein_ref, epost_ref, egate_ref)


def _ffn(x, g_pre, g_post, wg, wu, wd, embed, p_row0, tm):
    t = x.shape[0]
    const = lambda i: (0, 0)
    return pl.pallas_call(
        _ffn_kernel,
        out_shape=jax.ShapeDtypeStruct((t, D_MODEL), F32),
        grid=(t // tm,),
        in_specs=[pl.BlockSpec((tm, D_MODEL), lambda i: (i, 0)),
                  pl.BlockSpec((1, D_MODEL), const), pl.BlockSpec((1, D_MODEL), const),
                  _resident(wg.shape, const), _resident(wu.shape, const), _resident(wd.shape, const)]
                 + _embed_operands(embed, p_row0, tm, lambda i: (i, 0)),
        out_specs=pl.BlockSpec((tm, D_MODEL), lambda i: (i, 0)),
        compiler_params=_params(("parallel",)),
        name="dense_swiglu",
    )(x, g_pre, g_post, wg, wu, wd, *embed)


PIECES = D_MODEL // 2 // HEAD_DIM
SC_WINDOW = 128
EXPERT_TILE = 512


def _pack_rows(vals):
    bits = lax.bitcast_convert_type(vals.astype(BF16).astype(F32), jnp.uint32)
    half = D_MODEL // 2
    word = bits[:, :half] | (bits[:, half:] >> 16)
    return [lax.bitcast_convert_type(word[:, c * HEAD_DIM:(c + 1) * HEAD_DIM], jnp.int32) for c in range(PIECES)]


def _unpack_rows(piece_refs):
    words = [lax.bitcast_convert_type(r[...], jnp.uint32) for r in piece_refs]
    hi = [lax.bitcast_convert_type(w & jnp.uint32(0xFFFF0000), F32) for w in words]
    lo = [lax.bitcast_convert_type(w << 16, F32) for w in words]
    return jnp.concatenate(hi + lo, axis=1)


def _route_kernel(x_ref, gpre_ref, rhi_ref, rlo_ref, *refs, n_experts):
    piece_refs = refs[:PIECES]
    rec_ref, rank_ref, count_ref, carry_ref = refs[PIECES:]
    i = pl.program_id(0)
    tm = x_ref.shape[0]

    @pl.when(i == 0)
    def _():
        carry_ref[...] = jnp.zeros_like(carry_ref)

    hf = _rms(x_ref[...], gpre_ref[...])
    for r, piece in zip(piece_refs, _pack_rows(hf)):
        r[...] = piece
    h_hi = hf.astype(BF16)
    h_lo = (hf - h_hi.astype(F32)).astype(BF16)
    logits = _dot(h_hi, rhi_ref[...]) + _dot(h_hi, rlo_ref[...]) + _dot(h_lo, rhi_ref[...])
    lane = lax.broadcasted_iota(jnp.int32, logits.shape, 1).astype(F32)
    logits = jnp.where(lane < n_experts, logits, NEG_INF)
    m1 = jnp.max(logits, axis=1, keepdims=True)
    i1 = jnp.min(jnp.where(logits == m1, lane, 1e9), axis=1, keepdims=True)
    rest = jnp.where(lane == i1, NEG_INF, logits)
    m2 = jnp.max(rest, axis=1, keepdims=True)
    i2 = jnp.min(jnp.where(rest == m2, lane, 1e9), axis=1, keepdims=True)
    ex = jnp.exp(m2 - m1)
    w1 = 1.0 / (1.0 + ex)
    rec = jnp.where(lane == 0, i1, jnp.where(lane == 1, i2, jnp.where(lane == 2, w1, jnp.where(lane == 3, ex * w1, 0.0))))
    rec_ref[...] = rec
    rec_t = rec.T
    e1 = rec_t[0:1, :]
    e2 = rec_t[1:2, :]
    sub = lax.broadcasted_iota(jnp.int32, (8, tm), 0).astype(F32)
    oh1 = jnp.where(sub == e1, 1.0, 0.0)
    oh2 = jnp.where(sub == e2, 1.0, 0.0)
    chosen = oh1 + oh2
    src = lax.broadcasted_iota(jnp.int32, (tm, tm), 0)
    dst = lax.broadcasted_iota(jnp.int32, (tm, tm), 1)
    before = jnp.where(src < dst, 1.0, 0.0).astype(BF16)
    rank = _dot(chosen.astype(BF16), before) + carry_ref[:, 0:1]
    rank1 = jnp.sum(oh1 * rank, axis=0, keepdims=True)
    rank2 = jnp.sum(oh2 * rank, axis=0, keepdims=True)
    rank_ref[...] = jnp.where(sub == 0, rank1, jnp.where(sub == 1, rank2, jnp.where(sub == 2, e1, jnp.where(sub == 3, e2, 0.0))))
    carry_ref[...] = carry_ref[...] + jnp.sum(chosen, axis=1, keepdims=True)
    count_ref[...] = carry_ref[...]


def _route(x, g_pre, r_hi, r_lo, n_experts, tm):
    t = x.shape[0]
    assert n_experts <= 8
    piece = pl.BlockSpec((tm, HEAD_DIM), lambda i: (i, 0))
    return pl.pallas_call(
        functools.partial(_route_kernel, n_experts=n_experts),
        out_shape=(*[jax.ShapeDtypeStruct((t, HEAD_DIM), jnp.int32)] * PIECES,
                   jax.ShapeDtypeStruct((t, HEAD_DIM), F32), jax.ShapeDtypeStruct((8, t), F32),
                   jax.ShapeDtypeStruct((8, HEAD_DIM), F32)),
        grid=(t // tm,),
        in_specs=[pl.BlockSpec((tm, D_MODEL), lambda i: (i, 0)), pl.BlockSpec((1, D_MODEL), lambda i: (0, 0)),
                  pl.BlockSpec((D_MODEL, HEAD_DIM), lambda i: (0, 0)), pl.BlockSpec((D_MODEL, HEAD_DIM), lambda i: (0, 0))],
        out_specs=(*[piece] * PIECES, piece, pl.BlockSpec((8, tm), lambda i: (0, i)),
                   pl.BlockSpec((8, HEAD_DIM), lambda i: (0, 0))),
        scratch_shapes=[pltpu.VMEM((8, HEAD_DIM), F32)],
        compiler_params=_params(("arbitrary",)),
        name="moe_route",
    )(x, g_pre, r_hi, r_lo)


def _slot_kernel(rank_ref, start_ref, s1_ref, s2_ref):
    tm = rank_ref.shape[1]
    sub = lax.broadcasted_iota(jnp.int32, (8, tm), 0).astype(F32)
    start = start_ref[:, 0:1]
    rows = rank_ref[...]
    for choice, out in ((0, s1_ref), (1, s2_ref)):
        base = jnp.sum(jnp.where(sub == rows[2 + choice:3 + choice, :], start, 0.0), axis=0, keepdims=True)
        out[...] = (base + rows[choice:choice + 1, :]).astype(jnp.int32)


def _slots(ranks, starts, tm):
    t = ranks.shape[1]
    row = pl.BlockSpec((1, tm), lambda i: (0, i))
    return pl.pallas_call(
        _slot_kernel,
        out_shape=(jax.ShapeDtypeStruct((1, t), jnp.int32),) * 2,
        grid=(t // tm,),
        in_specs=[pl.BlockSpec((8, tm), lambda i: (0, i)), pl.BlockSpec((8, HEAD_DIM), lambda i: (0, 0))],
        out_specs=(row, row),
        compiler_params=_params(("parallel",)),
        name="moe_slots",
    )(ranks, starts)


def _sc_mesh():
    return plsc.VectorSubcoreMesh(core_axis_name="core", subcore_axis_name="subcore")


def _sc_scatter_rows(srcs, idxs, n_rows):
    ns, nk = len(srcs), len(idxs)
    half = srcs[0].shape[0] // SC_WINDOW // 2

    @functools.partial(pl.kernel, mesh=_sc_mesh(), scratch_types=[],
                       out_type=tuple(jax.ShapeDtypeStruct((n_rows, HEAD_DIM), srcs[0].dtype) for _ in range(ns)))
    def scatter(*refs):
        out_hbm = refs[ns + nk:]

        def body(*blocks):
            for idx in blocks[ns:]:
                for c in range(ns):
                    pltpu.sync_copy(blocks[c], out_hbm[c].at[idx.at[0]])

        pltpu.emit_pipeline(
            body, grid=(2, half),
            in_specs=[pl.BlockSpec((SC_WINDOW, HEAD_DIM), lambda i, j: (i * half + j, 0)) for _ in range(ns)]
                     + [pl.BlockSpec((1, SC_WINDOW), lambda i, j: (0, i * half + j)) for _ in range(nk)],
            out_specs=[],
            core_axis_name=("core", "subcore"),
            dimension_semantics=(pltpu.PARALLEL, pltpu.PARALLEL),
        )(*refs[:ns + nk])

    return scatter(*srcs, *idxs)


def _sc_gather_rows(tables, idx):
    nt = len(tables)
    n = idx.shape[1]
    half = n // SC_WINDOW // 2

    @functools.partial(pl.kernel, mesh=_sc_mesh(), scratch_types=[],
                       out_type=tuple(jax.ShapeDtypeStruct((n, HEAD_DIM), tables[0].dtype) for _ in range(nt)))
    def gather(*refs):
        table_hbm = refs[:nt]

        def body(idx_blk, *out_blks):
            for c in range(nt):
                pltpu.sync_copy(table_hbm[c].at[idx_blk.at[0]], out_blks[c])

        pltpu.emit_pipeline(
            body, grid=(2, half),
            in_specs=[pl.BlockSpec((1, SC_WINDOW), lambda i, j: (0, i * half + j))],
            out_specs=[pl.BlockSpec((SC_WINDOW, HEAD_DIM), lambda i, j: (i * half + j, 0)) for _ in range(nt)],
            core_axis_name=("core", "subcore"),
            dimension_semantics=(pltpu.PARALLEL, pltpu.PARALLEL),
        )(refs[nt], *refs[nt + 1:])

    return gather(*tables, idx)


def _expert_kernel(tile_expert_ref, n_used_ref, *refs):
    x_refs = refs[:PIECES]
    wg_ref, wu_ref, wd_ref = refs[PIECES:PIECES + 3]
    y_refs = refs[PIECES + 3:]

    @pl.when(pl.program_id(0) < n_used_ref[0])
    def _():
        h = _unpack_rows(x_refs).astype(BF16)
        y = _swiglu(h, wg_ref[0], wu_ref[0], wd_ref[0])
        for r, piece in zip(y_refs, _pack_rows(y)):
            r[...] = piece


def _expert_ffn(xs, tile_expert, n_used, wg, wu, wd):
    n_rows = xs[0].shape[0]
    rows = pl.BlockSpec((EXPERT_TILE, HEAD_DIM), lambda i, te, nu: (jnp.minimum(i, nu[0] - 1), 0))
    out_rows = pl.BlockSpec((EXPERT_TILE, HEAD_DIM), lambda i, te, nu: (i, 0))
    expert = lambda i, te, nu: (te[i], 0, 0)
    grid_spec = pltpu.PrefetchScalarGridSpec(
        num_scalar_prefetch=2,
        grid=(n_rows // EXPERT_TILE,),
        in_specs=[rows] * PIECES + [_resident((1,) + wg.shape[1:], expert), _resident((1,) + wu.shape[1:], expert),
                                    _resident((1,) + wd.shape[1:], expert)],
        out_specs=[out_rows] * PIECES,
    )
    return pl.pallas_call(
        _expert_kernel,
        out_shape=[jax.ShapeDtypeStruct((n_rows, HEAD_DIM), jnp.int32)] * PIECES,
        grid_spec=grid_spec,
        compiler_params=_params(("arbitrary",)),
        name="moe_expert_swiglu",
    )(tile_expert, n_used, *xs, wg, wu, wd)


def _combine_kernel(x_ref, rec_ref, gpost_ref, p_ref, ein_ref, epost_ref, egate_ref, eproj_ref, *refs):
    proj = _dot(p_ref[...].astype(BF16), eproj_ref[...])
    y1 = _unpack_rows(refs[:PIECES])
    y2 = _unpack_rows(refs[PIECES:2 * PIECES])
    o_ref = refs[-1]
    rec = rec_ref[...]
    y = rec[:, 2:3] * y1 + rec[:, 3:4] * y2
    mixed = x_ref[...] + _rms(y, gpost_ref[...])
    o_ref[...] = _embed_tail(mixed, proj, ein_ref, epost_ref, egate_ref)


def _combine(x, rec, g_post, embed, p_row0, y1, y2, tm, row0, prev):
    t = x.shape[0]
    n = y1[0].shape[0]
    first = row0 // tm
    tok = lambda i: (first + i, 0)
    piece = pl.BlockSpec((tm, HEAD_DIM), lambda i: (i, 0))
    in_specs = ([pl.BlockSpec((tm, D_MODEL), tok), pl.BlockSpec((tm, HEAD_DIM), tok),
                 pl.BlockSpec((1, D_MODEL), lambda i: (0, 0))]
                + _embed_operands(embed, p_row0, tm, tok) + [piece] * (2 * PIECES))
    operands = [x, rec, g_post, *embed, *y1, *y2]
    aliases = {}
    if prev is not None:
        in_specs.append(pl.BlockSpec(memory_space=pl.ANY))
        aliases = {len(operands): 0}
        operands.append(prev)
    return pl.pallas_call(
        _combine_kernel,
        out_shape=jax.ShapeDtypeStruct((t, D_MODEL), F32),
        grid=(n // tm,),
        in_specs=in_specs,
        out_specs=pl.BlockSpec((tm, D_MODEL), tok),
        input_output_aliases=aliases,
        compiler_params=_params(("parallel",)),
        name="moe_combine",
    )(*operands)


def _moe(x, g_pre, g_post, router, wg, wu, wd, embed, p_row0, tm):
    t = x.shape[0]
    n_experts = router.shape[1]
    router = jnp.pad(router, ((0, 0), (0, HEAD_DIM - n_experts)))
    r_hi = router.astype(BF16)
    r_lo = (router - r_hi.astype(F32)).astype(BF16)
    *h_pieces, rec, ranks, counts = _route(x, g_pre, r_hi, r_lo, n_experts, tm)

    counts = counts[:n_experts, 0].astype(jnp.int32)
    tiles = (counts + EXPERT_TILE - 1) // EXPERT_TILE
    tile_end = jnp.cumsum(tiles)
    starts = ((tile_end - tiles) * EXPERT_TILE).astype(F32)
    starts = jnp.broadcast_to(jnp.pad(starts, (0, 8 - n_experts))[:, None], (8, HEAD_DIM))
    n_rows = TOP_K * t + n_experts * EXPERT_TILE
    n_tiles = n_rows // EXPERT_TILE
    n_used = tile_end[-1:]
    tile_ids = jnp.minimum(jnp.arange(n_tiles, dtype=jnp.int32), n_used[0] - 1)
    tile_expert = jnp.sum(tile_ids[:, None] >= tile_end[None, :], axis=1).astype(jnp.int32)

    slot1, slot2 = _slots(ranks, starts, min(t, 8192))
    xs = []
    for c in range(0, PIECES, 2):
        xs += _sc_scatter_rows(h_pieces[c:c + 2], [slot1, slot2], n_rows)
    ys = _expert_ffn(xs, tile_expert, n_used, wg, wu, wd)
    out = None
    n_part = t // 2
    for part in range(2):
        gathered = []
        for slot in (slot1, slot2):
            idx = slot[:, part * n_part:(part + 1) * n_part]
            rows = []
            for c in range(0, PIECES, 2):
                rows += _sc_gather_rows(ys[c:c + 2], idx)
            gathered.append(rows)
        out = _combine(x, rec, g_post, embed, p_row0, gathered[0], gathered[1], tm, part * n_part, out)
    return out


def kernel(x, p, positions, w_in, conv_w, hgrn_lb_logits, hgrn_norm_g, w_branch, w_out, g_mix_pre, g_mix_post, g_ffn_pre, g_ffn_post, dense_w_gate, dense_w_up, dense_w_down, moe_router, moe_w_gate, moe_w_up, moe_w_down, ple_w_proj, ple_w_gate, ple_g_in, ple_g_post):
    batch, seq, _ = x.shape
    depth = w_in.shape[0]
    t = batch * seq
    tm = min(TOKEN_TILE, seq)

    xt = x.reshape(t, D_MODEL)
    cos, sin = _rope_tables(positions.astype(F32).reshape(t // ROPE_PACK, ROPE_PACK), tm)
    row = lambda a: a.reshape(1, -1)
    in_edges = (0, A_COLS, A_COLS + 3 * BRANCH, A_COLS + B_COLS, A_COLS + B_COLS + C_COLS, A_COLS + B_COLS + C_COLS + G_COLS)
    in_bounds = tuple(zip(in_edges[:-1], in_edges[1:]))
    w_in_rows = w_in.reshape(depth * D_MODEL, -1)
    p_rows = p.reshape(depth * t, -1)

    for i in range(depth):
        w_attn, w_qzz, w_ig, w_conv, w_gates = _split_to_bf16(w_in_rows, i * D_MODEL, D_MODEL, in_bounds)
        g_pre = row(g_mix_pre[i])
        *qkvs, h = _inproj_attention(xt, g_pre, w_attn, cos, sin, batch, seq, tm)
        aos, lses, sides = zip(*[_attention_group(qkvs[g], h, w_side, g == 2, batch, seq, g)
                                 for g, w_side in enumerate((w_qzz, w_ig, w_conv))])
        qzz, ig, cb = sides
        o_f, o_b = _hgrn(qzz, ig, hgrn_lb_logits, batch, seq, i)
        xt = _merge(xt, aos, lses, o_f, o_b, ig, cb, h, w_gates, conv_w[i],
                    row(hgrn_norm_g[i]), w_branch[i].astype(BF16), w_out[i].astype(BF16), row(g_mix_post[i]),
                    seq, tm)

        embed = (p_rows, row(ple_g_in[i]), row(ple_g_post[i]),
                 ple_w_gate[i].astype(BF16), ple_w_proj[i].astype(BF16))
        j = i // 2
        if i % 2 == 0:
            xt = _ffn(xt, row(g_ffn_pre[i]), row(g_ffn_post[i]), dense_w_gate[j].astype(BF16),
                      dense_w_up[j].astype(BF16), dense_w_down[j].astype(BF16), embed, i * t, tm)
        else:
            xt = _moe(xt, row(g_ffn_pre[i]), row(g_ffn_post[i]), moe_router[j], _to_bf16(moe_w_gate[j]),
                      _to_bf16(moe_w_up[j]), _to_bf16(moe_w_down[j]), embed, i * t, tm)

    return xt.reshape(batch, seq, D_MODEL)
```

```python
import functools
import math

import jax
import jax.numpy as jnp
from jax import lax
from jax.experimental import pallas as pl
from jax.experimental.pallas import tpu as pltpu
from jax.experimental.pallas import tpu_sc as plsc

D_MODEL = 1024
EPS = 1e-6
NEG_INF = -1e30

HEAD_DIM = 128
N_HEADS = 4
BRANCH = N_HEADS * HEAD_DIM
DIL_GROUPS = ((128, 1), (512, 4), (2048, 16))
N_GROUPS = len(DIL_GROUPS)
BLK = 64
ROT_DIM = HEAD_DIM // 4
ROPE_THETA = 500000.0
TOP_K = 2

A_COLS = N_GROUPS * 3 * BRANCH
B_COLS = 5 * BRANCH
C_COLS = 3 * BRANCH
G_COLS = 3 * D_MODEL

BF16 = jnp.bfloat16
F32 = jnp.float32

TOKEN_TILE = 512
VMEM_LIMIT = 56 * 1024 * 1024
ATTENTION_VMEM_LIMIT = 62 * 1024 * 1024


def _params(sem, vmem_limit=VMEM_LIMIT):
    return pltpu.CompilerParams(dimension_semantics=sem, vmem_limit_bytes=vmem_limit)


def _cast_kernel(w_ref, o_ref):
    o_ref[...] = w_ref[...].astype(o_ref.dtype)


def _to_bf16(w):
    cols = w.shape[-1]
    flat = w.reshape(-1, cols)
    rows = flat.shape[0]
    block = rows
    while block % 32 == 0 and block * cols * 4 > 4 * 1024 * 1024:
        block //= 2
    out = pl.pallas_call(
        _cast_kernel,
        out_shape=jax.ShapeDtypeStruct(flat.shape, BF16),
        grid=(rows // block,),
        in_specs=[pl.BlockSpec((block, cols), lambda i: (i, 0))],
        out_specs=pl.BlockSpec((block, cols), lambda i: (i, 0)),
        compiler_params=_params(("parallel",)),
        name="weights_to_bf16",
    )(flat)
    return out.reshape(w.shape)


def _split_cast_kernel(w_ref, *o_refs, bounds):
    for o_ref, (lo, hi) in zip(o_refs, bounds):
        o_ref[...] = w_ref[:, lo:hi].astype(o_ref.dtype)


def _split_to_bf16(w, first_row, n, bounds, rows=128):
    cols = w.shape[1]
    first = first_row // rows
    return pl.pallas_call(
        functools.partial(_split_cast_kernel, bounds=bounds),
        out_shape=tuple(jax.ShapeDtypeStruct((n, hi - lo), BF16) for lo, hi in bounds),
        grid=(n // rows,),
        in_specs=[pl.BlockSpec((rows, cols), lambda i: (first + i, 0))],
        out_specs=tuple(pl.BlockSpec((rows, hi - lo), lambda i: (i, 0)) for lo, hi in bounds),
        compiler_params=_params(("parallel",)),
        name="split_weights_to_bf16",
    )(w)


def _resident(shape, index_map):
    return pl.BlockSpec(shape, index_map, pipeline_mode=pl.Buffered(1))


def _rms(xf, g):
    return xf * lax.rsqrt(jnp.mean(xf * xf, axis=-1, keepdims=True) + EPS) * g


def _sigmoid(z):
    return 0.5 * jnp.tanh(0.5 * z) + 0.5


def _sigmoid_rel(z):
    return 1.0 / (1.0 + jnp.exp(-z))


def _dot(a, b):
    return jnp.dot(a, b, preferred_element_type=F32)


def _dot_nt(a, b):
    return lax.dot_general(a, b, (((1,), (1,)), ((), ())), preferred_element_type=F32)


def _dot_tn(a, b):
    return lax.dot_general(a, b, (((0,), (0,)), ((), ())), preferred_element_type=F32)


ROPE_PACK = HEAD_DIM // ROT_DIM


def _rope_table_kernel(pos_ref, freq_ref, sign_ref, cos_ref, sin_ref):
    pos = pos_ref[...]
    group = lax.broadcasted_iota(jnp.int32, cos_ref.shape, 1) // ROT_DIM
    pos_lanes = jnp.zeros(cos_ref.shape, F32)
    for k in range(ROPE_PACK):
        pos_lanes = jnp.where(group == k, pos[:, k:k + 1], pos_lanes)
    ang = pos_lanes * freq_ref[...]
    cos_ref[...] = jnp.cos(ang)
    sin_ref[...] = jnp.sin(ang) * sign_ref[...]


def _rope_tables(pos_packed, rows):
    n = pos_packed.shape[0]
    half = ROT_DIM // 2
    inv_freq = ROPE_THETA ** (-jnp.arange(0, ROT_DIM, 2, dtype=F32) / ROT_DIM)
    freq = jnp.tile(jnp.concatenate([inv_freq, inv_freq]), ROPE_PACK)[None, :]
    sign = jnp.tile(jnp.concatenate([-jnp.ones((half,), F32), jnp.ones((half,), F32)]), ROPE_PACK)[None, :]
    row = pl.BlockSpec((1, HEAD_DIM), lambda i: (0, 0))
    tab = pl.BlockSpec((rows, HEAD_DIM), lambda i: (i, 0))
    return pl.pallas_call(
        _rope_table_kernel,
        out_shape=(jax.ShapeDtypeStruct((n, HEAD_DIM), F32),) * 2,
        grid=(n // rows,),
        in_specs=[pl.BlockSpec((rows, ROPE_PACK), lambda i: (i, 0)), row, row],
        out_specs=(tab, tab),
        compiler_params=_params(("parallel",)),
        name="rope_tables",
    )(pos_packed, freq, sign)


def _unpack_rope_tables(cosp_ref, sinp_ref, tab_ref):
    rows = cosp_ref.shape[0]
    lane = lax.broadcasted_iota(jnp.int32, (rows, HEAD_DIM), 1)
    for k in range(ROPE_PACK):
        for idx, (ref, fill) in enumerate(((cosp_ref, 1.0), (sinp_ref, 0.0))):
            packed = ref[...]
            shifted = pltpu.roll(packed, HEAD_DIM - ROT_DIM * k, 1) if k else packed
            tab_ref[idx, pl.ds(k, rows, stride=ROPE_PACK), :] = jnp.where(lane < ROT_DIM, shifted, fill)


def _rope_head(th, cos, sin):
    half = ROT_DIM // 2
    lane = lax.broadcasted_iota(jnp.int32, th.shape, 1)
    swapped = jnp.where(lane < half, pltpu.roll(th, HEAD_DIM - half, 1), pltpu.roll(th, half, 1))
    return th * cos + swapped * sin


def _inproj_attn_kernel(x_ref, g_ref, w_ref, cosp_ref, sinp_ref, o0_ref, o1_ref, o2_ref, h_ref, stage_ref, tab_ref):
    _unpack_rope_tables(cosp_ref, sinp_ref, tab_ref)
    hf = _rms(x_ref[...], g_ref[...])
    h_ref[...] = hf.astype(BF16)
    n_pieces = D_MODEL // HEAD_DIM
    for pc in range(n_pieces):
        stage_ref[pc] = hf[:, pc * HEAD_DIM:(pc + 1) * HEAD_DIM]
    scale = 1.0 / math.sqrt(HEAD_DIM)
    tm = x_ref.shape[0]
    for grp, o_ref in enumerate((o0_ref, o1_ref, o2_ref)):
        dil = DIL_GROUPS[grp][1]
        sub = tm // dil
        if dil == 1:
            h, cos, sin = hf.astype(BF16), tab_ref[0], tab_ref[1]
        else:
            residue = lambda ref, *lead: jnp.concatenate(
                [ref[(*lead, pl.ds(r, sub, stride=dil), slice(None))] for r in range(dil)], axis=0)
            h = jnp.concatenate([residue(stage_ref, pc) for pc in range(n_pieces)], axis=1).astype(BF16)
            cos, sin = residue(tab_ref, 0), residue(tab_ref, 1)
        for part in range(3):
            c = grp * 3 + part
            t = _dot(h, w_ref[:, c * BRANCH:(c + 1) * BRANCH])
            if part < 2:
                heads = []
                for hd in range(N_HEADS):
                    r = _rope_head(t[:, hd * HEAD_DIM:(hd + 1) * HEAD_DIM], cos, sin)
                    heads.append(r * scale if part == 0 else r)
                t = jnp.concatenate(heads, axis=1)
            t = t.astype(BF16)
            for r in range(dil):
                o_ref[0, r, :, part * BRANCH:(part + 1) * BRANCH] = t[r * sub:(r + 1) * sub, :]


def _inproj_attention(x, g, w, cos, sin, batch, seq, tm):
    per_seq = seq // tm
    rows = lambda width: pl.BlockSpec((tm, width), lambda b, j: (b * per_seq + j, 0))
    packed = pl.BlockSpec((tm // ROPE_PACK, HEAD_DIM), lambda b, j: (b * per_seq + j, 0))
    out_shape = tuple(jax.ShapeDtypeStruct((batch, dil, seq // dil, 3 * BRANCH), BF16) for _, dil in DIL_GROUPS)
    out_shape += (jax.ShapeDtypeStruct((batch * seq, D_MODEL), BF16),)
    out_specs = tuple(pl.BlockSpec((1, dil, tm // dil, 3 * BRANCH), lambda b, j: (b, 0, j, 0))
                      for _, dil in DIL_GROUPS) + (rows(D_MODEL),)
    return pl.pallas_call(
        _inproj_attn_kernel, out_shape=out_shape, grid=(batch, per_seq),
        in_specs=[rows(D_MODEL), pl.BlockSpec((1, D_MODEL), lambda b, j: (0, 0)),
                  _resident((D_MODEL, A_COLS), lambda b, j: (0, 0)), packed, packed],
        out_specs=out_specs,
        scratch_shapes=[pltpu.VMEM((D_MODEL // HEAD_DIM, tm, HEAD_DIM), F32), pltpu.VMEM((2, tm, HEAD_DIM), F32)],
        compiler_params=_params(("parallel", "parallel")), name="inproj_attention",
    )(x, g, w, cos, sin)


def _attn_kernel(q_ref, k_ref, kp_ref, kn_ref, v_ref, vp_ref, vn_ref, h_ref, w_ref, o_ref, lse_ref, side_ref,
                 o_stage, lse_stage, *, dil, nj, n_steps, conv_side):
    n = pl.program_id(1)
    part_rows = h_ref.shape[0] // N_HEADS

    def side_projection(part):
        rs = slice(part * part_rows, (part + 1) * part_rows)
        h = h_ref[rs, :]
        if conv_side:
            u = _dot(h, w_ref[:, 0:BRANCH])
            b_gate = _dot(h, w_ref[:, BRANCH:2 * BRANCH])
            c_gate = _dot(h, w_ref[:, 2 * BRANCH:3 * BRANCH])
            side_ref[rs, 0:BRANCH] = (c_gate * u).astype(BF16)
            side_ref[rs, BRANCH:2 * BRANCH] = b_gate.astype(BF16)
        else:
            for c in range(w_ref.shape[1] // BRANCH):
                cols = slice(c * BRANCH, (c + 1) * BRANCH)
                side_ref[rs, cols] = _dot(h, w_ref[:, cols]).astype(BF16)

    qb = 2 if nj % 2 == 0 else 1
    q_rows, w_rows, n_tiles = qb * BLK, (qb + 2) * BLK, nj // qb
    col_lo = jnp.where(n > 0, 0, BLK)
    col_hi = jnp.where(n < n_steps - 1, w_rows, w_rows - BLK)
    row = lax.broadcasted_iota(jnp.int32, (q_rows, w_rows), 0)
    col = lax.broadcasted_iota(jnp.int32, (q_rows, w_rows), 1)
    band = jnp.abs(col - BLK - row) <= BLK
    masks = {}
    for j in range(n_tiles):
        m = band
        if j == 0:
            m = m & (col >= col_lo)
        if j == n_tiles - 1:
            m = m & (col < col_hi)
        masks[j] = m
    pairs = [(r, j) for r in range(dil) for j in range(n_tiles)]
    lane = lax.broadcasted_iota(jnp.int32, (len(pairs), q_rows, HEAD_DIM), 2)

    def natural_rows(r, j):
        return pl.ds(j * q_rows * dil + r, q_rows, stride=dil)

    def window(own_ref, prev_ref, next_ref, r, j, hs):
        parts = []
        for w in range(j * qb - 1, j * qb + qb + 1):
            if w < 0:
                parts.append(prev_ref[0, r, :, hs])
            elif w >= nj:
                parts.append(next_ref[0, r, :, hs])
            else:
                parts.append(own_ref[0, r, w * BLK:(w + 1) * BLK, hs])
        return jnp.concatenate(parts, axis=0)

    lse_tiles = jnp.zeros((len(pairs), q_rows, HEAD_DIM), F32)
    for hd in range(N_HEADS):
        side_projection(hd)
        hs = slice(hd * HEAD_DIM, (hd + 1) * HEAD_DIM)
        scores = [_dot_nt(q_ref[0, r, j * q_rows:(j + 1) * q_rows, hs], window(k_ref, kp_ref, kn_ref, r, j, hs))
                  for r, j in pairs]
        s = jnp.stack([jnp.where(masks[j], sc, NEG_INF) for (r, j), sc in zip(pairs, scores)])
        m = jnp.max(s, axis=2, keepdims=True)
        p = jnp.exp(s - m)
        l = jnp.sum(p, axis=2, keepdims=True)
        pb = p.astype(BF16)
        outs = jnp.stack([_dot(pb[i], window(v_ref, vp_ref, vn_ref, r, j, hs)) for i, (r, j) in enumerate(pairs)])
        outs = outs * (1.0 / l)
        lse_tiles = jnp.where(lane == hd, m + jnp.log(l), lse_tiles)
        for i, (r, j) in enumerate(pairs):
            o_stage[hd, natural_rows(r, j), :] = outs[i]
        o_ref[:, hs] = o_stage[hd].astype(o_ref.dtype)
    for i, (r, j) in enumerate(pairs):
        lse_stage[natural_rows(r, j), :] = lse_tiles[i]
    lse_ref[...] = lse_stage[...]


def _attention_group(qkv, h, w_side, conv_side, batch, seq, group):
    _, dil = DIL_GROUPS[group]
    t = batch * seq
    sub_len = seq // dil
    rows_per_step = min(seq, 32 * BLK)
    nj = rows_per_step // (dil * BLK)
    n_steps = seq // rows_per_step
    n_blk = sub_len // BLK
    sub_rows = rows_per_step // dil

    def own(part):
        return pl.BlockSpec((1, dil, sub_rows, BRANCH), lambda b, n: (b, 0, n, part))

    def prev(part):
        return pl.BlockSpec((1, dil, BLK, BRANCH), lambda b, n: (b, 0, jnp.maximum(n * nj - 1, 0), part))

    def nxt(part):
        return pl.BlockSpec((1, dil, BLK, BRANCH), lambda b, n: (b, 0, jnp.minimum((n + 1) * nj, n_blk - 1), part))

    side_cols = 2 * BRANCH if conv_side else w_side.shape[1]
    step_rows = lambda width: pl.BlockSpec((rows_per_step, width), lambda b, n: (b * n_steps + n, 0))
    return pl.pallas_call(
        functools.partial(_attn_kernel, dil=dil, nj=nj, n_steps=n_steps, conv_side=conv_side),
        out_shape=(jax.ShapeDtypeStruct((t, BRANCH), BF16), jax.ShapeDtypeStruct((t, HEAD_DIM), F32),
                   jax.ShapeDtypeStruct((t, side_cols), BF16)),
        grid=(batch, n_steps),
        in_specs=[own(0), own(1), prev(1), nxt(1), own(2), prev(2), nxt(2), step_rows(D_MODEL),
                  _resident(w_side.shape, lambda b, n: (0, 0))],
        out_specs=(step_rows(BRANCH), step_rows(HEAD_DIM), step_rows(side_cols)),
        scratch_shapes=[pltpu.VMEM((N_HEADS, rows_per_step, HEAD_DIM), F32), pltpu.VMEM((rows_per_step, HEAD_DIM), F32)],
        compiler_params=_params(("parallel", "parallel"), ATTENTION_VMEM_LIMIT),
        name=f"dilated_attention_g{group}",
    )(qkv, qkv, qkv, qkv, qkv, qkv, qkv, h, w_side)


def _hgrn_kernel(logit_ref, qf_ref, zf_ref, vf_ref, qb_ref, zb_ref, vb_ref, of_ref, ob_ref, state_ref,
                 *, layer, n_chunks):
    c = pl.program_id(1)

    @pl.when(c == 0)
    def _():
        state_ref[...] = jnp.zeros_like(state_ref)

    depth = logit_ref.shape[1]
    lbs = []
    for dirn in range(2):
        rows = [logit_ref[dirn, j:j + 1, :] for j in range(depth)]
        top = functools.reduce(jnp.maximum, rows)
        exps = [jnp.exp(rw - top) for rw in rows]
        lbs.append(sum(exps[1:layer + 1], jnp.zeros_like(top)) / sum(exps[1:], exps[0]))

    r = lax.broadcasted_iota(jnp.int32, (BLK, BLK), 0)
    cc = lax.broadcasted_iota(jnp.int32, (BLK, BLK), 1)

    def bcast_rows(per_chunk):
        return jnp.concatenate([jnp.broadcast_to(v, (BLK, BRANCH)) for v in per_chunk], axis=0)

    for dirn, (q_ref, z_ref, v_ref, o_ref) in enumerate(
            ((qf_ref, zf_ref, vf_ref, of_ref), (qb_ref, zb_ref, vb_ref, ob_ref))):
        tri = (cc <= r) if dirn == 0 else (cc >= r)
        tri_b = jnp.where(tri, 1.0, 0.0).astype(BF16)
        last, mid_row = (BLK - 1, BLK // 2 - 1) if dirn == 0 else (0, BLK // 2)
        lb = lbs[dirn]
        q = q_ref[...].astype(F32)
        z = z_ref[...].astype(F32)
        v = v_ref[...]
        sg = _sigmoid_rel(z)
        logf = jnp.log(lb + (1.0 - lb) * sg)
        kk = (1.0 - lb) * (1.0 - sg)
        hi = logf.astype(BF16)
        lo = (logf - hi.astype(F32)).astype(BF16)
        chunks = [slice(g * BLK, (g + 1) * BLK) for g in range(n_chunks)]
        cums = [_dot(tri_b, hi[ch]) + _dot(tri_b, lo[ch]) for ch in chunks]
        totals = [cm[last:last + 1, :] for cm in cums]
        mids = [cm[mid_row:mid_row + 1, :] for cm in cums]
        cum = jnp.concatenate(cums, axis=0)
        mid = bcast_rows(mids)
        qa = q * jnp.exp(cum - mid)
        ka = kk * jnp.exp(mid - cum)
        q_dec = (qa * bcast_rows([jnp.exp(md) for md in mids])).astype(BF16)
        k_dec = (ka * bcast_rows([jnp.exp(tt - md) for tt, md in zip(totals, mids)])).astype(BF16)
        qa = qa.astype(BF16)
        ka = ka.astype(BF16)
        tiles = [(g, hd) for g in range(n_chunks) for hd in range(N_HEADS)]
        sl = lambda g, hd: (slice(g * BLK, (g + 1) * BLK), slice(hd * HEAD_DIM, (hd + 1) * HEAD_DIM))
        atts = [jnp.where(tri, _dot_nt(qa[sl(g, hd)], ka[sl(g, hd)]), 0.0).astype(BF16) for g, hd in tiles]
        intra = {t: _dot(att, v[sl(*t)]) for t, att in zip(tiles, atts)}
        kvs = {t: _dot_tn(v[sl(*t)], k_dec[sl(*t)]) for t in tiles}
        decays = [jnp.exp(tt) for tt in totals]
        order = range(n_chunks) if dirn == 0 else range(n_chunks - 1, -1, -1)
        for hd in range(N_HEADS):
            hs = slice(hd * HEAD_DIM, (hd + 1) * HEAD_DIM)
            state_t = state_ref[dirn, hd]
            for g in order:
                o_ref[g * BLK:(g + 1) * BLK, hs] = intra[(g, hd)] + _dot_nt(q_dec[sl(g, hd)], state_t.astype(BF16))
                state_t = state_t * decays[g][:, hs] + kvs[(g, hd)]
            state_ref[dirn, hd] = state_t


def _hgrn(qzz, ig, logits, batch, seq, layer):
    t = qzz.shape[0]
    n_chunks = 4
    rows = n_chunks * BLK
    nc = seq // rows
    depth = logits.shape[1]

    def fwd(colblk):
        return pl.BlockSpec((rows, BRANCH), lambda b, c: (b * nc + c, colblk))

    def bwd(colblk):
        return pl.BlockSpec((rows, BRANCH), lambda b, c: (b * nc + nc - 1 - c, colblk))

    return pl.pallas_call(
        functools.partial(_hgrn_kernel, layer=layer, n_chunks=n_chunks),
        out_shape=(jax.ShapeDtypeStruct((t, BRANCH), F32),) * 2,
        grid=(batch, nc),
        in_specs=[pl.BlockSpec((2, depth, BRANCH), lambda b, c: (0, 0, 0)),
                  fwd(0), fwd(1), fwd(0), bwd(0), bwd(2), bwd(0)],
        out_specs=(pl.BlockSpec((rows, BRANCH), lambda b, c: (b * nc + c, 0)),
                   pl.BlockSpec((rows, BRANCH), lambda b, c: (b * nc + nc - 1 - c, 0))),
        scratch_shapes=[pltpu.VMEM((2, N_HEADS, HEAD_DIM, HEAD_DIM), F32)],
        compiler_params=_params(("parallel", "arbitrary")),
        name="hgrn2_scan",
    )(logits, qzz, qzz, ig, qzz, qzz, ig)


def _merge_kernel(x_ref, ao0_ref, ao1_ref, ao2_ref, l0_ref, l1_ref, l2_ref, of_ref, ob_ref, bg_ref,
                  cb_ref, cbp_ref, cbn_ref, h_ref, wgate_ref, convw_ref, ng_ref, wbr_ref, wout_ref, gpost_ref,
                  o_ref, *, tm, seq):
    i = pl.program_id(0)
    h = h_ref[...]
    gates = [_sigmoid(_dot(h, wgate_ref[:, n * D_MODEL:(n + 1) * D_MODEL])) for n in range(3)]

    ls = (l0_ref[...], l1_ref[...], l2_ref[...])
    aos = (ao0_ref, ao1_ref, ao2_ref)
    lmax = jnp.maximum(jnp.maximum(ls[0], ls[1]), ls[2])
    es = [jnp.exp(l - lmax) for l in ls]
    inv = 1.0 / (es[0] + es[1] + es[2])
    a_parts = []
    for hd in range(N_HEADS):
        hs = slice(hd * HEAD_DIM, (hd + 1) * HEAD_DIM)
        acc = None
        for g in range(N_GROUPS):
            w = (es[g] * inv)[:, hd:hd + 1]
            term = w * aos[g][:, hs].astype(F32)
            acc = term if acc is None else acc + term
        a_parts.append(acc)
    o_a = jnp.concatenate(a_parts, axis=1).astype(BF16)

    b_parts = []
    for hd in range(N_HEADS):
        hs = slice(hd * HEAD_DIM, (hd + 1) * HEAD_DIM)
        o = of_ref[:, hs] + ob_ref[:, hs]
        b_parts.append(_rms(o, ng_ref[:, hs]))
    gate = bg_ref[...].astype(F32)
    o_b = (jnp.concatenate(b_parts, axis=1) * (gate * _sigmoid(gate))).astype(BF16)

    cu = cb_ref[:, 0:BRANCH].astype(F32)
    keep_prev = jnp.where((i * tm) % seq == 0, 0.0, 1.0)
    keep_next = jnp.where(((i + 1) * tm) % seq == 0, 0.0, 1.0)
    halo_rows = cbp_ref.shape[0]
    prev_row = keep_prev * cbp_ref[halo_rows - 1:halo_rows, 0:BRANCH].astype(F32)
    next_row = keep_next * cbn_ref[0:1, 0:BRANCH].astype(F32)
    rid = lax.broadcasted_iota(jnp.int32, cu.shape, 0)
    before = jnp.where(rid == 0, prev_row, pltpu.roll(cu, 1, 0))
    after = jnp.where(rid == tm - 1, next_row, pltpu.roll(cu, tm - 1, 0))
    conv = convw_ref[0:1, :] * before + convw_ref[1:2, :] * cu + convw_ref[2:3, :] * after
    o_c = (cb_ref[:, BRANCH:2 * BRANCH].astype(F32) * conv).astype(BF16)

    merged = None
    for n, o_n in enumerate((o_a, o_b, o_c)):
        up = _dot(o_n, wbr_ref[n])
        term = gates[n] * up
        merged = term if merged is None else merged + term
    y = _dot(merged.astype(BF16), wout_ref[...])
    o_ref[...] = x_ref[...] + _rms(y, gpost_ref[...])


def _merge(x, aos, lses, o_f, o_b, ig, cb, h, w_gate, conv_w, norm_g, w_branch, w_out, g_post, seq, tm):
    t = x.shape[0]
    halo = 16
    n_halo = t // halo
    per = tm // halo

    def rows(width, colblk=0):
        return pl.BlockSpec((tm, width), lambda i: (i, colblk))

    def full(shape):
        return _resident(shape, lambda i: (0,) * len(shape))

    in_specs = ([rows(D_MODEL)] + [rows(BRANCH)] * 3 + [rows(HEAD_DIM)] * 3 + [rows(BRANCH)] * 2
                + [rows(BRANCH, 1), rows(2 * BRANCH),
                   pl.BlockSpec((halo, 2 * BRANCH), lambda i: (jnp.maximum(i * per - 1, 0), 0)),
                   pl.BlockSpec((halo, 2 * BRANCH), lambda i: (jnp.minimum((i + 1) * per, n_halo - 1), 0)),
                   rows(D_MODEL), full((D_MODEL, G_COLS)), full((3, BRANCH)), full((1, BRANCH)),
                   full((3, BRANCH, D_MODEL)), full((D_MODEL, D_MODEL)), full((1, D_MODEL))])
    return pl.pallas_call(
        functools.partial(_merge_kernel, tm=tm, seq=seq),
        out_shape=jax.ShapeDtypeStruct((t, D_MODEL), F32),
        grid=(t // tm,),
        in_specs=in_specs,
        out_specs=rows(D_MODEL),
        compiler_params=_params(("parallel",)),
        name="merge_branches",
    )(x, *aos, *lses, o_f, o_b, ig, cb, cb, cb, h, w_gate, conv_w, norm_g, w_branch, w_out, g_post)


def _swiglu(h, wg, wu, wd):
    g = _dot(h, wg)
    u = _dot(h, wu)
    return _dot((g * _sigmoid(g) * u).astype(BF16), wd)


def _embed_tail(x, proj, gin_ref, gpost_ref, wgate_ref):
    gate = _sigmoid(_dot(_rms(x, gin_ref[...]).astype(BF16), wgate_ref[...]))
    return x + _rms(proj * gate, gpost_ref[...])


def _embed_operands(embed, p_row0, tm, index):
    pd = embed[0].shape[1]
    const = lambda *_: (0, 0)
    p_index = lambda *a: (index(*a)[0] + p_row0 // tm, 0)
    return [pl.BlockSpec((tm, pd), p_index), pl.BlockSpec((1, D_MODEL), const), pl.BlockSpec((1, D_MODEL), const),
            _resident((D_MODEL, D_MODEL), const), _resident((pd, D_MODEL), const)]


def _ffn_kernel(x_ref, gpre_ref, gpost_ref, wg_ref, wu_ref, wd_ref, p_ref, ein_ref, epost_ref, egate_ref, eproj_ref,
                o_ref):
    proj = _dot(p_ref[...].astype(BF16), eproj_ref[...])
    x = x_ref[...]
    h = _rms(x, gpre_ref[...]).astype(BF16)
    mixed = x + _rms(_swiglu(h, wg_ref[...], wu_ref[...], wd_ref[...]), gpost_ref[...])
    o_ref[...] = _embed_tail(mixed, proj, ein_ref, epost_ref, egate_ref)


def _ffn(x, g_pre, g_post, wg, wu, wd, embed, p_row0, tm):
    t = x.shape[0]
    const = lambda i: (0, 0)
    return pl.pallas_call(
        _ffn_kernel,
        out_shape=jax.ShapeDtypeStruct((t, D_MODEL), F32),
        grid=(t // tm,),
        in_specs=[pl.BlockSpec((tm, D_MODEL), lambda i: (i, 0)),
                  pl.BlockSpec((1, D_MODEL), const), pl.BlockSpec((1, D_MODEL), const),
                  _resident(wg.shape, const), _resident(wu.shape, const), _resident(wd.shape, const)]
                 + _embed_operands(embed, p_row0, tm, lambda i: (i, 0)),
        out_specs=pl.BlockSpec((tm, D_MODEL), lambda i: (i, 0)),
        compiler_params=_params(("parallel",)),
        name="dense_swiglu",
    )(x, g_pre, g_post, wg, wu, wd, *embed)


PIECES = D_MODEL // 2 // HEAD_DIM
SC_WINDOW = 128
EXPERT_TILE = 512
COMBINE_PARTS = 4


def _pack_rows(vals):
    bits = lax.bitcast_convert_type(vals.astype(BF16).astype(F32), jnp.uint32)
    half = D_MODEL // 2
    word = bits[:, :half] | (bits[:, half:] >> 16)
    return [lax.bitcast_convert_type(word[:, c * HEAD_DIM:(c + 1) * HEAD_DIM], jnp.int32) for c in range(PIECES)]


def _unpack_rows(piece_refs):
    words = [lax.bitcast_convert_type(r[...], jnp.uint32) for r in piece_refs]
    hi = [lax.bitcast_convert_type(w & jnp.uint32(0xFFFF0000), F32) for w in words]
    lo = [lax.bitcast_convert_type(w << 16, F32) for w in words]
    return jnp.concatenate(hi + lo, axis=1)


def _route_kernel(x_ref, gpre_ref, rhi_ref, rlo_ref, *refs, n_experts):
    piece_refs = refs[:PIECES]
    rec_ref, rank_ref, count_ref, carry_ref = refs[PIECES:]
    i = pl.program_id(0)
    tm = x_ref.shape[0]

    @pl.when(i == 0)
    def _():
        carry_ref[...] = jnp.zeros_like(carry_ref)

    hf = _rms(x_ref[...], gpre_ref[...])
    for r, piece in zip(piece_refs, _pack_rows(hf)):
        r[...] = piece
    h_hi = hf.astype(BF16)
    h_lo = (hf - h_hi.astype(F32)).astype(BF16)
    logits = _dot(h_hi, rhi_ref[...]) + _dot(h_hi, rlo_ref[...]) + _dot(h_lo, rhi_ref[...])
    lane = lax.broadcasted_iota(jnp.int32, logits.shape, 1).astype(F32)
    logits = jnp.where(lane < n_experts, logits, NEG_INF)
    m1 = jnp.max(logits, axis=1, keepdims=True)
    i1 = jnp.min(jnp.where(logits == m1, lane, 1e9), axis=1, keepdims=True)
    rest = jnp.where(lane == i1, NEG_INF, logits)
    m2 = jnp.max(rest, axis=1, keepdims=True)
    i2 = jnp.min(jnp.where(rest == m2, lane, 1e9), axis=1, keepdims=True)
    ex = jnp.exp(m2 - m1)
    w1 = 1.0 / (1.0 + ex)
    rec = jnp.where(lane == 0, i1, jnp.where(lane == 1, i2, jnp.where(lane == 2, w1, jnp.where(lane == 3, ex * w1, 0.0))))
    rec_ref[...] = rec
    rec_t = rec.T
    e1 = rec_t[0:1, :]
    e2 = rec_t[1:2, :]
    sub = lax.broadcasted_iota(jnp.int32, (8, tm), 0).astype(F32)
    oh1 = jnp.where(sub == e1, 1.0, 0.0)
    oh2 = jnp.where(sub == e2, 1.0, 0.0)
    chosen = oh1 + oh2
    src = lax.broadcasted_iota(jnp.int32, (tm, tm), 0)
    dst = lax.broadcasted_iota(jnp.int32, (tm, tm), 1)
    before = jnp.where(src < dst, 1.0, 0.0).astype(BF16)
    rank = _dot(chosen.astype(BF16), before) + carry_ref[:, 0:1]
    rank1 = jnp.sum(oh1 * rank, axis=0, keepdims=True)
    rank2 = jnp.sum(oh2 * rank, axis=0, keepdims=True)
    rank_ref[...] = jnp.where(sub == 0, rank1, jnp.where(sub == 1, rank2, jnp.where(sub == 2, e1, jnp.where(sub == 3, e2, 0.0))))
    carry_ref[...] = carry_ref[...] + jnp.sum(chosen, axis=1, keepdims=True)
    count_ref[...] = carry_ref[...]


def _route(x, g_pre, r_hi, r_lo, n_experts, tm):
    t = x.shape[0]
    assert n_experts <= 8
    piece = pl.BlockSpec((tm, HEAD_DIM), lambda i: (i, 0))
    return pl.pallas_call(
        functools.partial(_route_kernel, n_experts=n_experts),
        out_shape=(*[jax.ShapeDtypeStruct((t, HEAD_DIM), jnp.int32)] * PIECES,
                   jax.ShapeDtypeStruct((t, HEAD_DIM), F32), jax.ShapeDtypeStruct((8, t), F32),
                   jax.ShapeDtypeStruct((8, HEAD_DIM), F32)),
        grid=(t // tm,),
        in_specs=[pl.BlockSpec((tm, D_MODEL), lambda i: (i, 0)), pl.BlockSpec((1, D_MODEL), lambda i: (0, 0)),
                  pl.BlockSpec((D_MODEL, HEAD_DIM), lambda i: (0, 0)), pl.BlockSpec((D_MODEL, HEAD_DIM), lambda i: (0, 0))],
        out_specs=(*[piece] * PIECES, piece, pl.BlockSpec((8, tm), lambda i: (0, i)),
                   pl.BlockSpec((8, HEAD_DIM), lambda i: (0, 0))),
        scratch_shapes=[pltpu.VMEM((8, HEAD_DIM), F32)],
        compiler_params=_params(("arbitrary",)),
        name="moe_route",
    )(x, g_pre, r_hi, r_lo)


def _slot_kernel(rank_ref, start_ref, s1_ref, s2_ref):
    tm = rank_ref.shape[1]
    sub = lax.broadcasted_iota(jnp.int32, (8, tm), 0).astype(F32)
    start = start_ref[:, 0:1]
    rows = rank_ref[...]
    for choice, out in ((0, s1_ref), (1, s2_ref)):
        base = jnp.sum(jnp.where(sub == rows[2 + choice:3 + choice, :], start, 0.0), axis=0, keepdims=True)
        out[...] = (base + rows[choice:choice + 1, :]).astype(jnp.int32)


def _slots(ranks, starts, tm):
    t = ranks.shape[1]
    row = pl.BlockSpec((1, tm), lambda i: (0, i))
    return pl.pallas_call(
        _slot_kernel,
        out_shape=(jax.ShapeDtypeStruct((1, t), jnp.int32),) * 2,
        grid=(t // tm,),
        in_specs=[pl.BlockSpec((8, tm), lambda i: (0, i)), pl.BlockSpec((8, HEAD_DIM), lambda i: (0, 0))],
        out_specs=(row, row),
        compiler_params=_params(("parallel",)),
        name="moe_slots",
    )(ranks, starts)


def _sc_mesh():
    return plsc.VectorSubcoreMesh(core_axis_name="core", subcore_axis_name="subcore")


def _sc_scatter_rows(srcs, idxs, n_rows):
    ns, nk = len(srcs), len(idxs)
    half = srcs[0].shape[0] // SC_WINDOW // 2

    @functools.partial(pl.kernel, mesh=_sc_mesh(), scratch_types=[],
                       out_type=tuple(jax.ShapeDtypeStruct((n_rows, HEAD_DIM), srcs[0].dtype) for _ in range(ns)))
    def scatter(*refs):
        out_hbm = refs[ns + nk:]

        def body(*blocks):
            for idx in blocks[ns:]:
                for c in range(ns):
                    pltpu.sync_copy(blocks[c], out_hbm[c].at[idx.at[0]])

        pltpu.emit_pipeline(
            body, grid=(2, half),
            in_specs=[pl.BlockSpec((SC_WINDOW, HEAD_DIM), lambda i, j: (i * half + j, 0)) for _ in range(ns)]
                     + [pl.BlockSpec((1, SC_WINDOW), lambda i, j: (0, i * half + j)) for _ in range(nk)],
            out_specs=[],
            core_axis_name=("core", "subcore"),
            dimension_semantics=(pltpu.PARALLEL, pltpu.PARALLEL),
        )(*refs[:ns + nk])

    return scatter(*srcs, *idxs)


def _sc_gather_rows(tables, idx):
    nt = len(tables)
    n = idx.shape[1]
    half = n // SC_WINDOW // 2

    @functools.partial(pl.kernel, mesh=_sc_mesh(), scratch_types=[],
                       out_type=tuple(jax.ShapeDtypeStruct((n, HEAD_DIM), tables[0].dtype) for _ in range(nt)))
    def gather(*refs):
        table_hbm = refs[:nt]

        def body(idx_blk, *out_blks):
            for c in range(nt):
                pltpu.sync_copy(table_hbm[c].at[idx_blk.at[0]], out_blks[c])

        pltpu.emit_pipeline(
            body, grid=(2, half),
            in_specs=[pl.BlockSpec((1, SC_WINDOW), lambda i, j: (0, i * half + j))],
            out_specs=[pl.BlockSpec((SC_WINDOW, HEAD_DIM), lambda i, j: (i * half + j, 0)) for _ in range(nt)],
            core_axis_name=("core", "subcore"),
            dimension_semantics=(pltpu.PARALLEL, pltpu.PARALLEL),
        )(refs[nt], *refs[nt + 1:])

    return gather(*tables, idx)


def _expert_kernel(tile_expert_ref, n_used_ref, *refs):
    x_refs = refs[:PIECES]
    wg_ref, wu_ref, wd_ref = refs[PIECES:PIECES + 3]
    y_refs = refs[PIECES + 3:]

    @pl.when(pl.program_id(0) < n_used_ref[0])
    def _():
        h = _unpack_rows(x_refs).astype(BF16)
        y = _swiglu(h, wg_ref[0], wu_ref[0], wd_ref[0])
        for r, piece in zip(y_refs, _pack_rows(y)):
            r[...] = piece


def _expert_ffn(xs, tile_expert, n_used, wg, wu, wd):
    n_rows = xs[0].shape[0]
    rows = pl.BlockSpec((EXPERT_TILE, HEAD_DIM), lambda i, te, nu: (jnp.minimum(i, nu[0] - 1), 0))
    out_rows = pl.BlockSpec((EXPERT_TILE, HEAD_DIM), lambda i, te, nu: (i, 0))
    expert = lambda i, te, nu: (te[i], 0, 0)
    grid_spec = pltpu.PrefetchScalarGridSpec(
        num_scalar_prefetch=2,
        grid=(n_rows // EXPERT_TILE,),
        in_specs=[rows] * PIECES + [_resident((1,) + wg.shape[1:], expert), _resident((1,) + wu.shape[1:], expert),
                                    _resident((1,) + wd.shape[1:], expert)],
        out_specs=[out_rows] * PIECES,
    )
    return pl.pallas_call(
        _expert_kernel,
        out_shape=[jax.ShapeDtypeStruct((n_rows, HEAD_DIM), jnp.int32)] * PIECES,
        grid_spec=grid_spec,
        compiler_params=_params(("arbitrary",)),
        name="moe_expert_swiglu",
    )(tile_expert, n_used, *xs, wg, wu, wd)


def _combine_kernel(x_ref, rec_ref, gpost_ref, p_ref, ein_ref, epost_ref, egate_ref, eproj_ref, *refs):
    proj = _dot(p_ref[...].astype(BF16), eproj_ref[...])
    y1 = _unpack_rows(refs[:PIECES])
    y2 = _unpack_rows(refs[PIECES:2 * PIECES])
    o_ref = refs[-1]
    rec = rec_ref[...]
    y = rec[:, 2:3] * y1 + rec[:, 3:4] * y2
    mixed = x_ref[...] + _rms(y, gpost_ref[...])
    o_ref[...] = _embed_tail(mixed, proj, ein_ref, epost_ref, egate_ref)


def _combine(x, rec, g_post, embed, p_row0, y1, y2, tm, row0, prev):
    t = x.shape[0]
    n = y1[0].shape[0]
    first = row0 // tm
    tok = lambda i: (first + i, 0)
    piece = pl.BlockSpec((tm, HEAD_DIM), lambda i: (i, 0))
    in_specs = ([pl.BlockSpec((tm, D_MODEL), tok), pl.BlockSpec((tm, HEAD_DIM), tok),
                 pl.BlockSpec((1, D_MODEL), lambda i: (0, 0))]
                + _embed_operands(embed, p_row0, tm, tok) + [piece] * (2 * PIECES))
    operands = [x, rec, g_post, *embed, *y1, *y2]
    aliases = {}
    if prev is not None:
        in_specs.append(pl.BlockSpec(memory_space=pl.ANY))
        aliases = {len(operands): 0}
        operands.append(prev)
    return pl.pallas_call(
        _combine_kernel,
        out_shape=jax.ShapeDtypeStruct((t, D_MODEL), F32),
        grid=(n // tm,),
        in_specs=in_specs,
        out_specs=pl.BlockSpec((tm, D_MODEL), tok),
        input_output_aliases=aliases,
        compiler_params=_params(("parallel",)),
        name="moe_combine",
    )(*operands)


def _moe(x, g_pre, g_post, router, wg, wu, wd, embed, p_row0, tm):
    t = x.shape[0]
    n_experts = router.shape[1]
    router = jnp.pad(router, ((0, 0), (0, HEAD_DIM - n_experts)))
    r_hi = router.astype(BF16)
    r_lo = (router - r_hi.astype(F32)).astype(BF16)
    *h_pieces, rec, ranks, counts = _route(x, g_pre, r_hi, r_lo, n_experts, tm)

    counts = counts[:n_experts, 0].astype(jnp.int32)
    tiles = (counts + EXPERT_TILE - 1) // EXPERT_TILE
    tile_end = jnp.cumsum(tiles)
    starts = ((tile_end - tiles) * EXPERT_TILE).astype(F32)
    starts = jnp.broadcast_to(jnp.pad(starts, (0, 8 - n_experts))[:, None], (8, HEAD_DIM))
    n_rows = TOP_K * t + n_experts * EXPERT_TILE
    n_tiles = n_rows // EXPERT_TILE
    n_used = tile_end[-1:]
    tile_ids = jnp.minimum(jnp.arange(n_tiles, dtype=jnp.int32), n_used[0] - 1)
    tile_expert = jnp.sum(tile_ids[:, None] >= tile_end[None, :], axis=1).astype(jnp.int32)

    slot1, slot2 = _slots(ranks, starts, min(t, 8192))
    xs = []
    for c in range(0, PIECES, 2):
        xs += _sc_scatter_rows(h_pieces[c:c + 2], [slot1, slot2], n_rows)
    ys = _expert_ffn(xs, tile_expert, n_used, wg, wu, wd)
    out = None
    n_part = t // COMBINE_PARTS
    for part in range(COMBINE_PARTS):
        gathered = []
        for slot in (slot1, slot2):
            idx = slot[:, part * n_part:(part + 1) * n_part]
            rows = []
            for c in range(0, PIECES, 2):
                rows += _sc_gather_rows(ys[c:c + 2], idx)
            gathered.append(rows)
        out = _combine(x, rec, g_post, embed, p_row0, gathered[0], gathered[1], tm, part * n_part, out)
    return out


def kernel(x, p, positions, w_in, conv_w, hgrn_lb_logits, hgrn_norm_g, w_branch, w_out, g_mix_pre, g_mix_post, g_ffn_pre, g_ffn_post, dense_w_gate, dense_w_up, dense_w_down, moe_router, moe_w_gate, moe_w_up, moe_w_down, ple_w_proj, ple_w_gate, ple_g_in, ple_g_post):
    batch, seq, _ = x.shape
    depth = w_in.shape[0]
    t = batch * seq
    tm = min(TOKEN_TILE, seq)

    xt = x.reshape(t, D_MODEL)
    cos, sin = _rope_tables(positions.astype(F32).reshape(t // ROPE_PACK, ROPE_PACK), tm)
    row = lambda a: a.reshape(1, -1)
    in_edges = (0, A_COLS, A_COLS + 3 * BRANCH, A_COLS + B_COLS, A_COLS + B_COLS + C_COLS, A_COLS + B_COLS + C_COLS + G_COLS)
    in_bounds = tuple(zip(in_edges[:-1], in_edges[1:]))
    w_in_rows = w_in.reshape(depth * D_MODEL, -1)
    p_rows = p.reshape(depth * t, -1)

    for i in range(depth):
        w_attn, w_qzz, w_ig, w_conv, w_gates = _split_to_bf16(w_in_rows, i * D_MODEL, D_MODEL, in_bounds)
        g_pre = row(g_mix_pre[i])
        *qkvs, h = _inproj_attention(xt, g_pre, w_attn, cos, sin, batch, seq, tm)
        aos, lses, sides = zip(*[_attention_group(qkvs[g], h, w_side, g == 2, batch, seq, g)
                                 for g, w_side in enumerate((w_qzz, w_ig, w_conv))])
        qzz, ig, cb = sides
        o_f, o_b = _hgrn(qzz, ig, hgrn_lb_logits, batch, seq, i)
        xt = _merge(xt, aos, lses, o_f, o_b, ig, cb, h, w_gates, conv_w[i],
                    row(hgrn_norm_g[i]), w_branch[i].astype(BF16), w_out[i].astype(BF16), row(g_mix_post[i]),
                    seq, tm)

        embed = (p_rows, row(ple_g_in[i]), row(ple_g_post[i]),
                 ple_w_gate[i].astype(BF16), ple_w_proj[i].astype(BF16))
        j = i // 2
        if i % 2 == 0:
            xt = _ffn(xt, row(g_ffn_pre[i]), row(g_ffn_post[i]), dense_w_gate[j].astype(BF16),
                      dense_w_up[j].astype(BF16), dense_w_down[j].astype(BF16), embed, i * t, tm)
        else:
            xt = _moe(xt, row(g_ffn_pre[i]), row(g_ffn_post[i]), moe_router[j], _to_bf16(moe_w_gate[j]),
                      _to_bf16(moe_w_up[j]), _to_bf16(moe_w_down[j]), embed, i * t, tm)

    return xt.reshape(batch, seq, D_MODEL)
```
